```python
import jax, jax.numpy as jnp
from jax import lax
import numpy as np

D_MODEL = 2048
BATCH = 8
SEQ = 2048
DEPTH = 2

N_MEM = 256
MIX_WIDTH = D_MODEL
GM_WIDTH = MIX_WIDTH // 2
GM_HEAD_DIM = 128
GM_HEADS = GM_WIDTH // GM_HEAD_DIM
GM_CHUNK = 128
DN_WIDTH = MIX_WIDTH - GM_WIDTH
DN_HEAD_DIM = 128
DN_HEADS = DN_WIDTH // DN_HEAD_DIM
DN_CHUNK = 64
CONV_WIDTH = 4
XA_HEADS = 4
XA_HEAD_DIM = D_MODEL // XA_HEADS
D_FF = 5632
NORM_EPS = 1e-6
IN_COLS = 2 * GM_WIDTH + 4 * DN_WIDTH + 2 * DN_HEADS

kernel_name = "hybrid_sgu_deltanet_macaron_block"


def rmsnorm(x, g):
    xf = x.astype(jnp.float32)
    y = xf * lax.rsqrt(jnp.mean(xf * xf, axis=-1, keepdims=True) + NORM_EPS)
    return (y * g.astype(jnp.float32)).astype(x.dtype)


def l2norm(x):
    xf = x.astype(jnp.float32)
    return xf * lax.rsqrt(jnp.sum(xf * xf, axis=-1, keepdims=True) + NORM_EPS)


def swiglu_ffn(h, w_gate_up, w_down):
    gate, up = jnp.split(h @ w_gate_up, 2, axis=-1)
    return (jax.nn.silu(gate) * up) @ w_down


def causal_short_conv(x, w):
    k_width = w.shape[0]
    t_len = x.shape[1]
    xp = jnp.pad(x, ((0, 0), (k_width - 1, 0), (0, 0)))
    out = xp[:, 0:t_len] * w[0]
    for i in range(1, k_width):
        out = out + xp[:, i:i + t_len] * w[i]
    return out


def chunk_spatial_gating(u, v, sm_w, sm_b, ln_g, ln_b):
    b_, t_len, _ = u.shape
    n_chunks = t_len // GM_CHUNK
    u = jax.nn.gelu(u, approximate=False)
    vf = jax.nn.gelu(v, approximate=False).astype(jnp.float32)
    mu = jnp.mean(vf, axis=-1, keepdims=True)
    var = jnp.mean(jnp.square(vf - mu), axis=-1, keepdims=True)
    vn = ((vf - mu) * lax.rsqrt(var + NORM_EPS) * ln_g + ln_b).astype(u.dtype)
    vc = vn.reshape(b_, n_chunks, GM_CHUNK, GM_HEADS, GM_HEAD_DIM)
    causal = jnp.tril(jnp.ones((GM_CHUNK, GM_CHUNK), dtype=bool))
    w_masked = jnp.where(causal, sm_w, 0)
    mixed = jnp.einsum('hts,bnshd->bnthd', w_masked, vc) + sm_b.T[None, None, :, :, None]
    uc = u.reshape(b_, n_chunks, GM_CHUNK, GM_HEADS, GM_HEAD_DIM)
    return (uc * mixed).reshape(b_, t_len, GM_WIDTH)


def gated_delta_rule(q, k, v, g, beta):
    b_, t_len, h_, dk = q.shape
    dv = v.shape[-1]
    n_chunks = t_len // DN_CHUNK
    c = DN_CHUNK

    def to_chunks(a):
        return a.reshape(b_, n_chunks, c, h_, -1).transpose(1, 0, 3, 2, 4)

    qc = to_chunks(l2norm(q) * (dk ** -0.5))
    kc = to_chunks(l2norm(k))
    vc = to_chunks(v.astype(jnp.float32))
    gc = g.astype(jnp.float32).reshape(b_, n_chunks, c, h_).transpose(1, 0, 3, 2)
    bc = beta.astype(jnp.float32).reshape(b_, n_chunks, c, h_).transpose(1, 0, 3, 2)
    gcum = jnp.cumsum(gc, axis=-1)

    causal = jnp.tril(jnp.ones((c, c), dtype=bool))
    strict = jnp.tril(jnp.ones((c, c), dtype=bool), k=-1)
    decay = jnp.exp(jnp.where(causal, gcum[..., :, None] - gcum[..., None, :], -jnp.inf))

    k_beta = kc * bc[..., None]
    kkt = jnp.einsum('nbhtd,nbhsd->nbhts', k_beta, kc) * decay
    a_mat = jnp.eye(c, dtype=jnp.float32) + jnp.where(strict, kkt, 0.0)
    rhs = jnp.concatenate([vc * bc[..., None], k_beta * jnp.exp(gcum)[..., None]], axis=-1)
    sol = lax.linalg.triangular_solve(a_mat, rhs, left_side=True, lower=True, unit_diagonal=True)
    u_c, w_c = sol[..., :dv], sol[..., dv:]
    qk_intra = jnp.where(causal, jnp.einsum('nbhtd,nbhsd->nbhts', qc, kc) * decay, 0.0)

    def chunk_step(state, inp):
        q_i, k_i, u_i, w_i, g_i, a_i = inp
        v_new = u_i - jnp.einsum('bhck,bhkv->bhcv', w_i, state)
        o_i = (jnp.einsum('bhck,bhkv->bhcv', q_i * jnp.exp(g_i)[..., None], state)
               + jnp.einsum('bhts,bhsv->bhtv', a_i, v_new))
        g_last = g_i[..., -1]
        k_dec = k_i * jnp.exp(g_last[..., None] - g_i)[..., None]
        state = state * jnp.exp(g_last)[..., None, None] + jnp.einsum('bhck,bhcv->bhkv', k_dec, v_new)
        return state, o_i

    s0 = jnp.zeros((b_, h_, dk, dv), jnp.float32)
    _, o = lax.scan(chunk_step, s0, (qc, kc, u_c, w_c, gcum, qk_intra))
    return o.transpose(1, 0, 3, 2, 4).reshape(b_, t_len, h_, dv)


def token_mix(h, w_in, conv_w, a_log, dt_bias, sm_w, sm_b, sm_ln_g, sm_ln_b, dn_norm_w, w_out):
    b_, t_len, _ = h.shape
    proj = h @ w_in
    splits = [GM_WIDTH, 2 * GM_WIDTH, 2 * GM_WIDTH + 3 * DN_WIDTH,
              2 * GM_WIDTH + 4 * DN_WIDTH, 2 * GM_WIDTH + 4 * DN_WIDTH + DN_HEADS]
    u_a, v_a, qkv, z, b_raw, a_raw = jnp.split(proj, splits, axis=-1)

    y_a = chunk_spatial_gating(u_a, v_a, sm_w, sm_b, sm_ln_g, sm_ln_b)

    qkv = jax.nn.silu(causal_short_conv(qkv, conv_w))
    q, k, v = jnp.split(qkv, 3, axis=-1)
    q = q.reshape(b_, t_len, DN_HEADS, DN_HEAD_DIM)
    k = k.reshape(b_, t_len, DN_HEADS, DN_HEAD_DIM)
    v = v.reshape(b_, t_len, DN_HEADS, DN_HEAD_DIM)
    beta = jax.nn.sigmoid(b_raw.astype(jnp.float32))
    g = -jnp.exp(a_log.astype(jnp.float32)) * jax.nn.softplus(a_raw.astype(jnp.float32) + dt_bias.astype(jnp.float32))
    o = gated_delta_rule(q, k, v, g, beta)
    zf = z.astype(jnp.float32).reshape(b_, t_len, DN_HEADS, DN_HEAD_DIM)
    o = rmsnorm(o, dn_norm_w) * jax.nn.silu(zf)
    y_b = o.reshape(b_, t_len, DN_WIDTH).astype(h.dtype)

    return jnp.concatenate([y_a, y_b], axis=-1) @ w_out


def cross_attend(h, mem_h, w_xq, w_xkv, w_xo):
    b_, t_len, _ = h.shape
    m_len = mem_h.shape[1]
    q = (h @ w_xq).reshape(b_, t_len, XA_HEADS, XA_HEAD_DIM)
    kv = (mem_h @ w_xkv).reshape(b_, m_len, 2, XA_HEADS, XA_HEAD_DIM)
    k, v = kv[:, :, 0], kv[:, :, 1]
    s = jnp.einsum('bthd,bmhd->bhtm', q, k).astype(jnp.float32) * (XA_HEAD_DIM ** -0.5)
    p = jax.nn.softmax(s, axis=-1).astype(v.dtype)
    o = jnp.einsum('bhtm,bmhd->bthd', p, v).reshape(b_, t_len, XA_HEADS * XA_HEAD_DIM)
    return o @ w_xo


def _fwd_setup_inputs(seed: int = 0) -> dict:
    key = jax.random.key(seed)
    ks = iter(jax.random.split(key, 64))
    L = DEPTH

    def dense(shape, fan_in):
        return jax.random.normal(next(ks), shape, jnp.float32) * (fan_in ** -0.5)

    def gain(shape):
        return 1.0 + 0.05 * jax.random.normal(next(ks), shape, jnp.float32)

    def small(shape):
        return 0.02 * jax.random.normal(next(ks), shape, jnp.float32)

    x = jax.random.normal(next(ks), (BATCH, SEQ, D_MODEL), jnp.float32)
    mem = jax.random.normal(next(ks), (BATCH, N_MEM, D_MODEL), jnp.float32)
    a_log = jnp.log(jax.random.uniform(next(ks), (L, DN_HEADS), jnp.float32, minval=1.0, maxval=16.0))
    dt = jnp.exp(jax.random.uniform(next(ks), (L, DN_HEADS), jnp.float32,
                                    minval=float(np.log(1e-3)), maxval=float(np.log(1e-1))))
    dt_bias = dt + jnp.log(-jnp.expm1(-dt))
    return {
        "x": x,
        "mem": mem,
        "ffn1_norm_pre": gain((L, D_MODEL)),
        "ffn1_w_gate_up": dense((L, D_MODEL, 2 * D_FF), D_MODEL),
        "ffn1_w_down": dense((L, D_FF, D_MODEL), D_FF),
        "ffn1_norm_post": gain((L, D_MODEL)),
        "mix_norm_pre": gain((L, D_MODEL)),
        "w_in": dense((L, D_MODEL, IN_COLS), D_MODEL),
        "conv_w": dense((L, CONV_WIDTH, 3 * DN_WIDTH), CONV_WIDTH),
        "a_log": a_log,
        "dt_bias": dt_bias,
        "sm_w": dense((L, GM_HEADS, GM_CHUNK, GM_CHUNK), GM_CHUNK),
        "sm_b": gain((L, GM_HEADS, GM_CHUNK)),
        "sm_ln_g": gain((L, GM_WIDTH)),
        "sm_ln_b": small((L, GM_WIDTH)),
        "dn_norm_w": gain((L, DN_HEAD_DIM)),
        "w_out": dense((L, MIX_WIDTH, D_MODEL), MIX_WIDTH),
        "mix_norm_post": gain((L, D_MODEL)),
        "xa_norm_pre": gain((L, D_MODEL)),
        "mem_norm": gain((L, D_MODEL)),
        "w_xq": dense((L, D_MODEL, D_MODEL), D_MODEL),
        "w_xkv": dense((L, D_MODEL, 2 * D_MODEL), D_MODEL),
        "w_xo": dense((L, D_MODEL, D_MODEL), D_MODEL),
        "xa_norm_post": gain((L, D_MODEL)),
        "ffn2_norm_pre": gain((L, D_MODEL)),
        "ffn2_w_gate_up": dense((L, D_MODEL, 2 * D_FF), D_MODEL),
        "ffn2_w_down": dense((L, D_FF, D_MODEL), D_FF),
        "ffn2_norm_post": gain((L, D_MODEL)),
    }


def _fwd_reference(x, mem, ffn1_norm_pre, ffn1_w_gate_up, ffn1_w_down, ffn1_norm_post,
              mix_norm_pre, w_in, conv_w, a_log, dt_bias, sm_w, sm_b, sm_ln_g, sm_ln_b,
              dn_norm_w, w_out, mix_norm_post, xa_norm_pre, mem_norm, w_xq, w_xkv, w_xo,
              xa_norm_post, ffn2_norm_pre, ffn2_w_gate_up, ffn2_w_down, ffn2_norm_post):
    for l in range(DEPTH):
        f = swiglu_ffn(rmsnorm(x, ffn1_norm_pre[l]), ffn1_w_gate_up[l], ffn1_w_down[l])
        x = x + 0.5 * rmsnorm(f, ffn1_norm_post[l])
        m = token_mix(rmsnorm(x, mix_norm_pre[l]), w_in[l], conv_w[l], a_log[l], dt_bias[l],
                      sm_w[l], sm_b[l], sm_ln_g[l], sm_ln_b[l], dn_norm_w[l], w_out[l])
        x = x + rmsnorm(m, mix_norm_post[l])
        c = cross_attend(rmsnorm(x, xa_norm_pre[l]), rmsnorm(mem, mem_norm[l]), w_xq[l], w_xkv[l], w_xo[l])
        x = x + rmsnorm(c, xa_norm_post[l])
        f = swiglu_ffn(rmsnorm(x, ffn2_norm_pre[l]), ffn2_w_gate_up[l], ffn2_w_down[l])
        x = x + 0.5 * rmsnorm(f, ffn2_norm_post[l])
    return x


import jax as _jax
import jax.numpy as _jnp

TWIN_FORMAT = 'train_step'
FWD_PARAMS = ['x', 'mem', 'ffn1_norm_pre', 'ffn1_w_gate_up', 'ffn1_w_down', 'ffn1_norm_post', 'mix_norm_pre', 'w_in', 'conv_w', 'a_log', 'dt_bias', 'sm_w', 'sm_b', 'sm_ln_g', 'sm_ln_b', 'dn_norm_w', 'w_out', 'mix_norm_post', 'xa_norm_pre', 'mem_norm', 'w_xq', 'w_xkv', 'w_xo', 'xa_norm_post', 'ffn2_norm_pre', 'ffn2_w_gate_up', 'ffn2_w_down', 'ffn2_norm_post']
TWIN_WEIGHTS = ['ffn1_norm_pre', 'ffn1_w_gate_up', 'ffn1_w_down', 'ffn1_norm_post', 'mix_norm_pre', 'w_in', 'conv_w', 'a_log', 'dt_bias', 'sm_w', 'sm_b', 'sm_ln_g', 'sm_ln_b', 'dn_norm_w', 'w_out', 'mix_norm_post', 'xa_norm_pre', 'mem_norm', 'w_xq', 'w_xkv', 'w_xo', 'xa_norm_post', 'ffn2_norm_pre', 'ffn2_w_gate_up', 'ffn2_w_down', 'ffn2_norm_post']
TWIN_DIFF_INPUT = 'x'
TWIN_INPUTS = ['x', 'mem', 'ffn1_norm_pre', 'ffn1_w_gate_up', 'ffn1_w_down', 'ffn1_norm_post', 'mix_norm_pre', 'w_in', 'conv_w', 'a_log', 'dt_bias', 'sm_w', 'sm_b', 'sm_ln_g', 'sm_ln_b', 'dn_norm_w', 'w_out', 'mix_norm_post', 'xa_norm_pre', 'mem_norm', 'w_xq', 'w_xkv', 'w_xo', 'xa_norm_post', 'ffn2_norm_pre', 'ffn2_w_gate_up', 'ffn2_w_down', 'ffn2_norm_post', 'loss_target', 'm_ffn1_norm_pre', 'm_ffn1_w_gate_up', 'm_ffn1_w_down', 'm_ffn1_norm_post', 'm_mix_norm_pre', 'm_w_in', 'm_conv_w', 'm_a_log', 'm_dt_bias', 'm_sm_w', 'm_sm_b', 'm_sm_ln_g', 'm_sm_ln_b', 'm_dn_norm_w', 'm_w_out', 'm_mix_norm_post', 'm_xa_norm_pre', 'm_mem_norm', 'm_w_xq', 'm_w_xkv', 'm_w_xo', 'm_xa_norm_post', 'm_ffn2_norm_pre', 'm_ffn2_w_gate_up', 'm_ffn2_w_down', 'm_ffn2_norm_post', 'v_ffn1_norm_pre', 'v_ffn1_w_gate_up', 'v_ffn1_w_down', 'v_ffn1_norm_post', 'v_mix_norm_pre', 'v_w_in', 'v_conv_w', 'v_a_log', 'v_dt_bias', 'v_sm_w', 'v_sm_b', 'v_sm_ln_g', 'v_sm_ln_b', 'v_dn_norm_w', 'v_w_out', 'v_mix_norm_post', 'v_xa_norm_pre', 'v_mem_norm', 'v_w_xq', 'v_w_xkv', 'v_w_xo', 'v_xa_norm_post', 'v_ffn2_norm_pre', 'v_ffn2_w_gate_up', 'v_ffn2_w_down', 'v_ffn2_norm_post']
TWIN_OUTPUTS = ['loss', 'grad_x', 'grad_ffn1_norm_pre', 'grad_ffn1_w_gate_up', 'grad_ffn1_w_down', 'grad_ffn1_norm_post', 'grad_mix_norm_pre', 'grad_w_in', 'grad_conv_w', 'grad_a_log', 'grad_dt_bias', 'grad_sm_w', 'grad_sm_b', 'grad_sm_ln_g', 'grad_sm_ln_b', 'grad_dn_norm_w', 'grad_w_out', 'grad_mix_norm_post', 'grad_xa_norm_pre', 'grad_mem_norm', 'grad_w_xq', 'grad_w_xkv', 'grad_w_xo', 'grad_xa_norm_post', 'grad_ffn2_norm_pre', 'grad_ffn2_w_gate_up', 'grad_ffn2_w_down', 'grad_ffn2_norm_post', 'delta_ffn1_norm_pre', 'delta_ffn1_w_gate_up', 'delta_ffn1_w_down', 'delta_ffn1_norm_post', 'delta_mix_norm_pre', 'delta_w_in', 'delta_conv_w', 'delta_a_log', 'delta_dt_bias', 'delta_sm_w', 'delta_sm_b', 'delta_sm_ln_g', 'delta_sm_ln_b', 'delta_dn_norm_w', 'delta_w_out', 'delta_mix_norm_post', 'delta_xa_norm_pre', 'delta_mem_norm', 'delta_w_xq', 'delta_w_xkv', 'delta_w_xo', 'delta_xa_norm_post', 'delta_ffn2_norm_pre', 'delta_ffn2_w_gate_up', 'delta_ffn2_w_down', 'delta_ffn2_norm_post', 'new_m_ffn1_norm_pre', 'new_m_ffn1_w_gate_up', 'new_m_ffn1_w_down', 'new_m_ffn1_norm_post', 'new_m_mix_norm_pre', 'new_m_w_in', 'new_m_conv_w', 'new_m_a_log', 'new_m_dt_bias', 'new_m_sm_w', 'new_m_sm_b', 'new_m_sm_ln_g', 'new_m_sm_ln_b', 'new_m_dn_norm_w', 'new_m_w_out', 'new_m_mix_norm_post', 'new_m_xa_norm_pre', 'new_m_mem_norm', 'new_m_w_xq', 'new_m_w_xkv', 'new_m_w_xo', 'new_m_xa_norm_post', 'new_m_ffn2_norm_pre', 'new_m_ffn2_w_gate_up', 'new_m_ffn2_w_down', 'new_m_ffn2_norm_post', 'new_v_ffn1_norm_pre', 'new_v_ffn1_w_gate_up', 'new_v_ffn1_w_down', 'new_v_ffn1_norm_post', 'new_v_mix_norm_pre', 'new_v_w_in', 'new_v_conv_w', 'new_v_a_log', 'new_v_dt_bias', 'new_v_sm_w', 'new_v_sm_b', 'new_v_sm_ln_g', 'new_v_sm_ln_b', 'new_v_dn_norm_w', 'new_v_w_out', 'new_v_mix_norm_post', 'new_v_xa_norm_pre', 'new_v_mem_norm', 'new_v_w_xq', 'new_v_w_xkv', 'new_v_w_xo', 'new_v_xa_norm_post', 'new_v_ffn2_norm_pre', 'new_v_ffn2_w_gate_up', 'new_v_ffn2_w_down', 'new_v_ffn2_norm_post']
TWIN_LEAF_KINDS = {'loss': 'loss', 'grad_x': 'grad_x', 'grad_ffn1_norm_pre': 'grad_w', 'grad_ffn1_w_gate_up': 'grad_w', 'grad_ffn1_w_down': 'grad_w', 'grad_ffn1_norm_post': 'grad_w', 'grad_mix_norm_pre': 'grad_w', 'grad_w_in': 'grad_w', 'grad_conv_w': 'grad_w', 'grad_a_log': 'grad_w', 'grad_dt_bias': 'grad_w', 'grad_sm_w': 'grad_w', 'grad_sm_b': 'grad_w', 'grad_sm_ln_g': 'grad_w', 'grad_sm_ln_b': 'grad_w', 'grad_dn_norm_w': 'grad_w', 'grad_w_out': 'grad_w', 'grad_mix_norm_post': 'grad_w', 'grad_xa_norm_pre': 'grad_w', 'grad_mem_norm': 'grad_w', 'grad_w_xq': 'grad_w', 'grad_w_xkv': 'grad_w', 'grad_w_xo': 'grad_w', 'grad_xa_norm_post': 'grad_w', 'grad_ffn2_norm_pre': 'grad_w', 'grad_ffn2_w_gate_up': 'grad_w', 'grad_ffn2_w_down': 'grad_w', 'grad_ffn2_norm_post': 'grad_w', 'delta_ffn1_norm_pre': 'delta_w', 'delta_ffn1_w_gate_up': 'delta_w', 'delta_ffn1_w_down': 'delta_w', 'delta_ffn1_norm_post': 'delta_w', 'delta_mix_norm_pre': 'delta_w', 'delta_w_in': 'delta_w', 'delta_conv_w': 'delta_w', 'delta_a_log': 'delta_w', 'delta_dt_bias': 'delta_w', 'delta_sm_w': 'delta_w', 'delta_sm_b': 'delta_w', 'delta_sm_ln_g': 'delta_w', 'delta_sm_ln_b': 'delta_w', 'delta_dn_norm_w': 'delta_w', 'delta_w_out': 'delta_w', 'delta_mix_norm_post': 'delta_w', 'delta_xa_norm_pre': 'delta_w', 'delta_mem_norm': 'delta_w', 'delta_w_xq': 'delta_w', 'delta_w_xkv': 'delta_w', 'delta_w_xo': 'delta_w', 'delta_xa_norm_post': 'delta_w', 'delta_ffn2_norm_pre': 'delta_w', 'delta_ffn2_w_gate_up': 'delta_w', 'delta_ffn2_w_down': 'delta_w', 'delta_ffn2_norm_post': 'delta_w', 'new_m_ffn1_norm_pre': 'new_m', 'new_m_ffn1_w_gate_up': 'new_m', 'new_m_ffn1_w_down': 'new_m', 'new_m_ffn1_norm_post': 'new_m', 'new_m_mix_norm_pre': 'new_m', 'new_m_w_in': 'new_m', 'new_m_conv_w': 'new_m', 'new_m_a_log': 'new_m', 'new_m_dt_bias': 'new_m', 'new_m_sm_w': 'new_m', 'new_m_sm_b': 'new_m', 'new_m_sm_ln_g': 'new_m', 'new_m_sm_ln_b': 'new_m', 'new_m_dn_norm_w': 'new_m', 'new_m_w_out': 'new_m', 'new_m_mix_norm_post': 'new_m', 'new_m_xa_norm_pre': 'new_m', 'new_m_mem_norm': 'new_m', 'new_m_w_xq': 'new_m', 'new_m_w_xkv': 'new_m', 'new_m_w_xo': 'new_m', 'new_m_xa_norm_post': 'new_m', 'new_m_ffn2_norm_pre': 'new_m', 'new_m_ffn2_w_gate_up': 'new_m', 'new_m_ffn2_w_down': 'new_m', 'new_m_ffn2_norm_post': 'new_m', 'new_v_ffn1_norm_pre': 'new_v', 'new_v_ffn1_w_gate_up': 'new_v', 'new_v_ffn1_w_down': 'new_v', 'new_v_ffn1_norm_post': 'new_v', 'new_v_mix_norm_pre': 'new_v', 'new_v_w_in': 'new_v', 'new_v_conv_w': 'new_v', 'new_v_a_log': 'new_v', 'new_v_dt_bias': 'new_v', 'new_v_sm_w': 'new_v', 'new_v_sm_b': 'new_v', 'new_v_sm_ln_g': 'new_v', 'new_v_sm_ln_b': 'new_v', 'new_v_dn_norm_w': 'new_v', 'new_v_w_out': 'new_v', 'new_v_mix_norm_post': 'new_v', 'new_v_xa_norm_pre': 'new_v', 'new_v_mem_norm': 'new_v', 'new_v_w_xq': 'new_v', 'new_v_w_xkv': 'new_v', 'new_v_w_xo': 'new_v', 'new_v_xa_norm_post': 'new_v', 'new_v_ffn2_norm_pre': 'new_v', 'new_v_ffn2_w_gate_up': 'new_v', 'new_v_ffn2_w_down': 'new_v', 'new_v_ffn2_norm_post': 'new_v'}


def _forward(args):
    return _fwd_reference(*[args[k] for k in FWD_PARAMS])


def _output_shape():
    out = _jax.eval_shape(lambda: _forward(_fwd_setup_inputs(0)))
    return out.shape, out.dtype

N_MICROBATCH = 1
ADAM_LR = 0.001
ADAM_B1 = 0.9
ADAM_B2 = 0.999
ADAM_EPS = 1e-08
ADAM_WD = 0.01
ADAM_STEP = 10
PER_EXAMPLE_BATCH_AXIS = {'x': 0, 'mem': 0, 'loss_target': 0}
SHARED_INPUTS = []
_WEIGHT_DTYPES = {'ffn1_norm_pre': _jnp.float32, 'ffn1_w_gate_up': _jnp.float32, 'ffn1_w_down': _jnp.float32, 'ffn1_norm_post': _jnp.float32, 'mix_norm_pre': _jnp.float32, 'w_in': _jnp.float32, 'conv_w': _jnp.float32, 'a_log': _jnp.float32, 'dt_bias': _jnp.float32, 'sm_w': _jnp.float32, 'sm_b': _jnp.float32, 'sm_ln_g': _jnp.float32, 'sm_ln_b': _jnp.float32, 'dn_norm_w': _jnp.float32, 'w_out': _jnp.float32, 'mix_norm_post': _jnp.float32, 'xa_norm_pre': _jnp.float32, 'mem_norm': _jnp.float32, 'w_xq': _jnp.float32, 'w_xkv': _jnp.float32, 'w_xo': _jnp.float32, 'xa_norm_post': _jnp.float32, 'ffn2_norm_pre': _jnp.float32, 'ffn2_w_gate_up': _jnp.float32, 'ffn2_w_down': _jnp.float32, 'ffn2_norm_post': _jnp.float32}
MOMENT_SCALE = {'ffn1_norm_pre': 3.843341e-01, 'ffn1_w_gate_up': 1.682957e-01, 'ffn1_w_down': 3.035947e-01, 'ffn1_norm_post': 1.938048e+00, 'mix_norm_pre': 7.174119e-01, 'w_in': 4.104892e-01, 'conv_w': 4.199642e-01, 'a_log': 1.141037e+00, 'dt_bias': 1.117082e+00, 'sm_w': 1.638298e-01, 'sm_b': 2.654553e-01, 'sm_ln_g': 1.643457e-01, 'sm_ln_b': 1.799876e-01, 'dn_norm_w': 2.610845e+00, 'w_out': 1.687108e+00, 'mix_norm_post': 8.305838e+00, 'xa_norm_pre': 6.643606e-01, 'mem_norm': 2.451731e+00, 'w_xq': 6.564080e-01, 'w_xkv': 1.653895e+00, 'w_xo': 2.258140e+00, 'xa_norm_post': 8.631944e+00, 'ffn2_norm_pre': 5.215232e-01, 'ffn2_w_gate_up': 2.241001e-01, 'ffn2_w_down': 4.337476e-01, 'ffn2_norm_post': 2.040706e+00}


def _to_microbatches(a, axis):
    t = _jnp.moveaxis(a, axis, 0)
    t = t.reshape((N_MICROBATCH, t.shape[0] // N_MICROBATCH) + t.shape[1:])
    return _jnp.moveaxis(t, 1, axis + 1)


def setup_inputs(seed: int = 0) -> dict:
    inp = _fwd_setup_inputs(seed)
    key = _jax.random.fold_in(_jax.random.key(seed), 7919)
    shape, _ = _output_shape()
    out = dict(inp)
    out["loss_target"] = _jax.random.normal(_jax.random.fold_in(key, 0), shape, _jnp.float32)
    for i, name in enumerate(TWIN_WEIGHTS):
        w = inp[name].astype(_jnp.float32)
        if MOMENT_SCALE is None:
            s = _jnp.sqrt(_jnp.mean(_jnp.square(w)) + 1e-30)
        else:
            s = MOMENT_SCALE[name]
        km, kv = _jax.random.split(_jax.random.fold_in(key, i + 1))
        out[name] = w
        out["m_" + name] = s * _jax.random.normal(km, w.shape, _jnp.float32)
        out["v_" + name] = (s * s) * _jax.random.uniform(kv, w.shape, _jnp.float32, 0.5, 1.5)
    if N_MICROBATCH > 1:
        for name, axis in PER_EXAMPLE_BATCH_AXIS.items():
            out[name] = _to_microbatches(out[name], axis)
    return {'x': out['x'], 'mem': out['mem'], 'ffn1_norm_pre': out['ffn1_norm_pre'], 'ffn1_w_gate_up': out['ffn1_w_gate_up'], 'ffn1_w_down': out['ffn1_w_down'], 'ffn1_norm_post': out['ffn1_norm_post'], 'mix_norm_pre': out['mix_norm_pre'], 'w_in': out['w_in'], 'conv_w': out['conv_w'], 'a_log': out['a_log'], 'dt_bias': out['dt_bias'], 'sm_w': out['sm_w'], 'sm_b': out['sm_b'], 'sm_ln_g': out['sm_ln_g'], 'sm_ln_b': out['sm_ln_b'], 'dn_norm_w': out['dn_norm_w'], 'w_out': out['w_out'], 'mix_norm_post': out['mix_norm_post'], 'xa_norm_pre': out['xa_norm_pre'], 'mem_norm': out['mem_norm'], 'w_xq': out['w_xq'], 'w_xkv': out['w_xkv'], 'w_xo': out['w_xo'], 'xa_norm_post': out['xa_norm_post'], 'ffn2_norm_pre': out['ffn2_norm_pre'], 'ffn2_w_gate_up': out['ffn2_w_gate_up'], 'ffn2_w_down': out['ffn2_w_down'], 'ffn2_norm_post': out['ffn2_norm_post'], 'loss_target': out['loss_target'], 'm_ffn1_norm_pre': out['m_ffn1_norm_pre'], 'm_ffn1_w_gate_up': out['m_ffn1_w_gate_up'], 'm_ffn1_w_down': out['m_ffn1_w_down'], 'm_ffn1_norm_post': out['m_ffn1_norm_post'], 'm_mix_norm_pre': out['m_mix_norm_pre'], 'm_w_in': out['m_w_in'], 'm_conv_w': out['m_conv_w'], 'm_a_log': out['m_a_log'], 'm_dt_bias': out['m_dt_bias'], 'm_sm_w': out['m_sm_w'], 'm_sm_b': out['m_sm_b'], 'm_sm_ln_g': out['m_sm_ln_g'], 'm_sm_ln_b': out['m_sm_ln_b'], 'm_dn_norm_w': out['m_dn_norm_w'], 'm_w_out': out['m_w_out'], 'm_mix_norm_post': out['m_mix_norm_post'], 'm_xa_norm_pre': out['m_xa_norm_pre'], 'm_mem_norm': out['m_mem_norm'], 'm_w_xq': out['m_w_xq'], 'm_w_xkv': out['m_w_xkv'], 'm_w_xo': out['m_w_xo'], 'm_xa_norm_post': out['m_xa_norm_post'], 'm_ffn2_norm_pre': out['m_ffn2_norm_pre'], 'm_ffn2_w_gate_up': out['m_ffn2_w_gate_up'], 'm_ffn2_w_down': out['m_ffn2_w_down'], 'm_ffn2_norm_post': out['m_ffn2_norm_post'], 'v_ffn1_norm_pre': out['v_ffn1_norm_pre'], 'v_ffn1_w_gate_up': out['v_ffn1_w_gate_up'], 'v_ffn1_w_down': out['v_ffn1_w_down'], 'v_ffn1_norm_post': out['v_ffn1_norm_post'], 'v_mix_norm_pre': out['v_mix_norm_pre'], 'v_w_in': out['v_w_in'], 'v_conv_w': out['v_conv_w'], 'v_a_log': out['v_a_log'], 'v_dt_bias': out['v_dt_bias'], 'v_sm_w': out['v_sm_w'], 'v_sm_b': out['v_sm_b'], 'v_sm_ln_g': out['v_sm_ln_g'], 'v_sm_ln_b': out['v_sm_ln_b'], 'v_dn_norm_w': out['v_dn_norm_w'], 'v_w_out': out['v_w_out'], 'v_mix_norm_post': out['v_mix_norm_post'], 'v_xa_norm_pre': out['v_xa_norm_pre'], 'v_mem_norm': out['v_mem_norm'], 'v_w_xq': out['v_w_xq'], 'v_w_xkv': out['v_w_xkv'], 'v_w_xo': out['v_w_xo'], 'v_xa_norm_post': out['v_xa_norm_post'], 'v_ffn2_norm_pre': out['v_ffn2_norm_pre'], 'v_ffn2_w_gate_up': out['v_ffn2_w_gate_up'], 'v_ffn2_w_down': out['v_ffn2_w_down'], 'v_ffn2_norm_post': out['v_ffn2_norm_post']}


def _loss(weights, diff, rest, loss_target):
    with _jax.named_scope("forward"):
        args = {**rest, TWIN_DIFF_INPUT: diff, **{k: w.astype(_WEIGHT_DTYPES[k]) for k, w in weights.items()}}
        y = _forward(args)
    with _jax.named_scope("loss_head"):
        err = _jnp.square(y.astype(_jnp.float32) - loss_target)
        return 0.5 * _jnp.sum(_jnp.mean(err, axis=-1)) if err.ndim else 0.5 * err


def _adamw(w, g, m, v):
    m = ADAM_B1 * m + (1.0 - ADAM_B1) * g
    v = ADAM_B2 * v + (1.0 - ADAM_B2) * _jnp.square(g)
    m_hat = m / (1.0 - ADAM_B1 ** ADAM_STEP)
    v_hat = v / (1.0 - ADAM_B2 ** ADAM_STEP)
    delta = -ADAM_LR * (m_hat / (_jnp.sqrt(v_hat) + ADAM_EPS) + ADAM_WD * w)
    return delta, m, v


def reference(x, mem, ffn1_norm_pre, ffn1_w_gate_up, ffn1_w_down, ffn1_norm_post, mix_norm_pre, w_in, conv_w, a_log, dt_bias, sm_w, sm_b, sm_ln_g, sm_ln_b, dn_norm_w, w_out, mix_norm_post, xa_norm_pre, mem_norm, w_xq, w_xkv, w_xo, xa_norm_post, ffn2_norm_pre, ffn2_w_gate_up, ffn2_w_down, ffn2_norm_post, loss_target, m_ffn1_norm_pre, m_ffn1_w_gate_up, m_ffn1_w_down, m_ffn1_norm_post, m_mix_norm_pre, m_w_in, m_conv_w, m_a_log, m_dt_bias, m_sm_w, m_sm_b, m_sm_ln_g, m_sm_ln_b, m_dn_norm_w, m_w_out, m_mix_norm_post, m_xa_norm_pre, m_mem_norm, m_w_xq, m_w_xkv, m_w_xo, m_xa_norm_post, m_ffn2_norm_pre, m_ffn2_w_gate_up, m_ffn2_w_down, m_ffn2_norm_post, v_ffn1_norm_pre, v_ffn1_w_gate_up, v_ffn1_w_down, v_ffn1_norm_post, v_mix_norm_pre, v_w_in, v_conv_w, v_a_log, v_dt_bias, v_sm_w, v_sm_b, v_sm_ln_g, v_sm_ln_b, v_dn_norm_w, v_w_out, v_mix_norm_post, v_xa_norm_pre, v_mem_norm, v_w_xq, v_w_xkv, v_w_xo, v_xa_norm_post, v_ffn2_norm_pre, v_ffn2_w_gate_up, v_ffn2_w_down, v_ffn2_norm_post):
    given = dict(x=x, mem=mem, ffn1_norm_pre=ffn1_norm_pre, ffn1_w_gate_up=ffn1_w_gate_up, ffn1_w_down=ffn1_w_down, ffn1_norm_post=ffn1_norm_post, mix_norm_pre=mix_norm_pre, w_in=w_in, conv_w=conv_w, a_log=a_log, dt_bias=dt_bias, sm_w=sm_w, sm_b=sm_b, sm_ln_g=sm_ln_g, sm_ln_b=sm_ln_b, dn_norm_w=dn_norm_w, w_out=w_out, mix_norm_post=mix_norm_post, xa_norm_pre=xa_norm_pre, mem_norm=mem_norm, w_xq=w_xq, w_xkv=w_xkv, w_xo=w_xo, xa_norm_post=xa_norm_post, ffn2_norm_pre=ffn2_norm_pre, ffn2_w_gate_up=ffn2_w_gate_up, ffn2_w_down=ffn2_w_down, ffn2_norm_post=ffn2_norm_post, loss_target=loss_target, m_ffn1_norm_pre=m_ffn1_norm_pre, m_ffn1_w_gate_up=m_ffn1_w_gate_up, m_ffn1_w_down=m_ffn1_w_down, m_ffn1_norm_post=m_ffn1_norm_post, m_mix_norm_pre=m_mix_norm_pre, m_w_in=m_w_in, m_conv_w=m_conv_w, m_a_log=m_a_log, m_dt_bias=m_dt_bias, m_sm_w=m_sm_w, m_sm_b=m_sm_b, m_sm_ln_g=m_sm_ln_g, m_sm_ln_b=m_sm_ln_b, m_dn_norm_w=m_dn_norm_w, m_w_out=m_w_out, m_mix_norm_post=m_mix_norm_post, m_xa_norm_pre=m_xa_norm_pre, m_mem_norm=m_mem_norm, m_w_xq=m_w_xq, m_w_xkv=m_w_xkv, m_w_xo=m_w_xo, m_xa_norm_post=m_xa_norm_post, m_ffn2_norm_pre=m_ffn2_norm_pre, m_ffn2_w_gate_up=m_ffn2_w_gate_up, m_ffn2_w_down=m_ffn2_w_down, m_ffn2_norm_post=m_ffn2_norm_post, v_ffn1_norm_pre=v_ffn1_norm_pre, v_ffn1_w_gate_up=v_ffn1_w_gate_up, v_ffn1_w_down=v_ffn1_w_down, v_ffn1_norm_post=v_ffn1_norm_post, v_mix_norm_pre=v_mix_norm_pre, v_w_in=v_w_in, v_conv_w=v_conv_w, v_a_log=v_a_log, v_dt_bias=v_dt_bias, v_sm_w=v_sm_w, v_sm_b=v_sm_b, v_sm_ln_g=v_sm_ln_g, v_sm_ln_b=v_sm_ln_b, v_dn_norm_w=v_dn_norm_w, v_w_out=v_w_out, v_mix_norm_post=v_mix_norm_post, v_xa_norm_pre=v_xa_norm_pre, v_mem_norm=v_mem_norm, v_w_xq=v_w_xq, v_w_xkv=v_w_xkv, v_w_xo=v_w_xo, v_xa_norm_post=v_xa_norm_post, v_ffn2_norm_pre=v_ffn2_norm_pre, v_ffn2_w_gate_up=v_ffn2_w_gate_up, v_ffn2_w_down=v_ffn2_w_down, v_ffn2_norm_post=v_ffn2_norm_post)
    weights = {n: given[n] for n in TWIN_WEIGHTS}
    shared = {n: given[n] for n in SHARED_INPUTS}
    per_example = {n: given[n] for n in ['x', 'mem']}
    grad_fn = _jax.value_and_grad(_loss, argnums=(0, 1))

    def one_microbatch(ex, loss_target):
        ex = dict(ex)
        diff = ex.pop(TWIN_DIFF_INPUT)
        return grad_fn(weights, diff, {**shared, **ex}, loss_target)

    if N_MICROBATCH == 1:
        loss, (grad_w, grad_x) = one_microbatch(per_example, given["loss_target"])
    else:
        def body(carry, xs):
            loss_sum, grad_sum = carry
            l_k, (gw_k, gx_k) = one_microbatch(xs[0], xs[1])
            with _jax.named_scope("update"):
                return (loss_sum + l_k, _jax.tree.map(_jnp.add, grad_sum, gw_k)), gx_k

        init = (_jnp.zeros((), _jnp.float32), _jax.tree.map(_jnp.zeros_like, weights))
        (loss, grad_w), grad_x = _jax.lax.scan(body, init, (per_example, given["loss_target"]))
    with _jax.named_scope("update"):
        delta_w, new_m, new_v = {}, {}, {}
        for n in TWIN_WEIGHTS:
            delta_w[n], new_m[n], new_v[n] = _adamw(weights[n], grad_w[n], given["m_" + n], given["v_" + n])
    return (loss, grad_x, *[grad_w[n] for n in TWIN_WEIGHTS], *[delta_w[n] for n in TWIN_WEIGHTS],
            *[new_m[n] for n in TWIN_WEIGHTS], *[new_v[n] for n in TWIN_WEIGHTS])
```

```python
import functools
import math

import jax
import jax.numpy as jnp
from jax import lax
from jax.experimental import pallas as pl
from jax.experimental.pallas import tpu as pltpu

F32, BF16 = jnp.float32, jnp.bfloat16
HI = lax.Precision.HIGHEST
MESH = pl.DeviceIdType.MESH

N_DEV = 8
NORM_EPS = 1e-6
GM_HEADS, GM_CHUNK = 8, 128
DN_HEADS, DN_CHUNK, DN_DIM = 8, 64, 128
XA_HEADS, XA_DIM = 4, 512
CONV_W = 4
ADAM_LR, ADAM_B1, ADAM_B2, ADAM_EPS, ADAM_WD, ADAM_STEP = 0.001, 0.9, 0.999, 1e-08, 0.01, 10

V7X_VMEM_LIMIT = 56 * 1024 * 1024
WHOLE_TILE_MAX = 1536
ROW_TILE = 256


def _cparams(n_grid):
    return pltpu.CompilerParams(dimension_semantics=("arbitrary",) * n_grid, vmem_limit_bytes=V7X_VMEM_LIMIT)


def stepk(name, fn, grid, ins, outs, carries=(), prefetch=None, fill=()):
    n_in, n_out, n_c, n_fill = len(ins), len(outs), len(carries), len(fill)
    n_pre = 0 if prefetch is None else 1

    def body(*refs):
        refs = refs[n_pre:]
        in_refs, refs = refs[:n_in], refs[n_in + n_fill:]
        out_refs, c_refs = refs[:n_out], refs[n_out:]
        if n_c:
            first = functools.reduce(jnp.logical_and, [pl.program_id(a) == 0 for a in range(len(grid))])

            @pl.when(first)
            def _():
                for r in c_refs:
                    r[...] = jnp.zeros(r.shape, r.dtype)
        res = fn(*[r[...] for r in in_refs], *[r[...] for r in c_refs])
        for r, v in zip(tuple(out_refs) + tuple(c_refs), res):
            r[...] = v.astype(r.dtype)

    in_specs = [pl.BlockSpec(bs, im) for _, bs, im in ins] + [pl.BlockSpec(memory_space=pl.ANY)] * n_fill
    aliases = {n_pre + n_in + i: k for i, (_, k) in enumerate(fill)}
    out_specs = [pl.BlockSpec(bs, im) for _, _, bs, im in outs]
    out_shape = [jax.ShapeDtypeStruct(s, d) for s, d, _, _ in outs]
    for s, d in carries:
        zeros = (0,) * len(s)
        out_specs.append(pl.BlockSpec(s, lambda *a, _z=zeros: _z))
        out_shape.append(jax.ShapeDtypeStruct(s, d))
    args = [a for a, _, _ in ins] + [a for a, _ in fill]
    if prefetch is None:
        call = pl.pallas_call(body, name=name, grid=grid, in_specs=in_specs, out_specs=out_specs, out_shape=out_shape,
                              input_output_aliases=aliases, compiler_params=_cparams(len(grid)))
        return call(*args)
    spec = pltpu.PrefetchScalarGridSpec(num_scalar_prefetch=1, grid=grid, in_specs=in_specs, out_specs=out_specs)
    call = pl.pallas_call(body, name=name, grid_spec=spec, out_shape=out_shape, input_output_aliases=aliases,
                          compiler_params=_cparams(len(grid)))
    return call(prefetch, *args)


def _tile(n, pref, align=128):
    if n <= WHOLE_TILE_MAX:
        return n
    t = (pref // align) * align
    while t > align and n % t:
        t -= align
    assert n % t == 0, (n, pref)
    return t


def mm(name, a, b, mode, out_dtype, nbo=1):
    nba, ra, ca = a.shape
    nbb, rb, cb = b.shape
    if mode == "nn":
        m, k, n = ra, nba * ca, nbb * cb
        assert rb == k
    elif mode == "nt":
        m, k, n = ra, nba * ca, rb
        assert nbb * cb == k
    else:
        k, m, n = ra, nba * ca, nbb * cb
        assert rb == k
    co = n // nbo
    if mode == "nn":
        tm, tk, tn = _tile(m, 1024, 8), _tile(ca, 512), _tile(math.gcd(cb, co), 1024)
        a_spec = pl.BlockSpec((None, tm, tk), lambda i, j, kk: (kk // (ca // tk), i, kk % (ca // tk)))
        b_spec = pl.BlockSpec((None, tk, tn), lambda i, j, kk: (j // (cb // tn), kk, j % (cb // tn)))
        dims = (((1,), (0,)), ((), ()))
    elif mode == "nt":
        tm, tk, tn = _tile(m, 1024, 8), _tile(math.gcd(ca, cb), 512), _tile(co, 1024)
        a_spec = pl.BlockSpec((None, tm, tk), lambda i, j, kk: (kk // (ca // tk), i, kk % (ca // tk)))
        b_spec = pl.BlockSpec((None, tn, tk), lambda i, j, kk: (kk // (cb // tk), j, kk % (cb // tk)))
        dims = (((1,), (1,)), ((), ()))
    else:
        tm, tk, tn = _tile(ca, 1024), _tile(k, 512), _tile(math.gcd(cb, co), 1024)
        a_spec = pl.BlockSpec((None, tk, tm), lambda i, j, kk: (i // (ca // tm), kk, i % (ca // tm)))
        b_spec = pl.BlockSpec((None, tk, tn), lambda i, j, kk: (j // (cb // tn), kk, j % (cb // tn)))
        dims = (((0,), (0,)), ((), ()))
    o_spec = pl.BlockSpec((None, tm, tn), lambda i, j, kk: (j // (co // tn), i, j % (co // tn)))
    nk = k // tk

    def body(a_ref, b_ref, o_ref, acc_ref):
        kk = pl.program_id(2)

        @pl.when(kk == 0)
        def _():
            acc_ref[...] = jnp.zeros(acc_ref.shape, F32)

        acc_ref[...] += lax.dot_general(a_ref[...].astype(BF16), b_ref[...].astype(BF16), dims,
                                        preferred_element_type=F32)

        @pl.when(kk == nk - 1)
        def _():
            o_ref[...] = acc_ref[...].astype(o_ref.dtype)

    return pl.pallas_call(
        body, name=name, grid=(m // tm, n // tn, nk), in_specs=[a_spec, b_spec], out_specs=o_spec,
        out_shape=jax.ShapeDtypeStruct((nbo, m, co), out_dtype), scratch_shapes=[pltpu.VMEM((tm, tn), F32)],
        compiler_params=_cparams(3))(a, b)


def _rms(x, g):
    return x * lax.rsqrt(jnp.mean(x * x, axis=-1, keepdims=True) + NORM_EPS) * g


def _sigmoid(x):
    return 1.0 / (1.0 + jnp.exp(-x))


def _silu(x):
    return x * _sigmoid(x)


def _gelu(x):
    return 0.5 * x * (1.0 + lax.erf(x * 0.7071067811865476))


def _softplus(x):
    return jnp.maximum(x, 0.0) + jnp.log(1.0 + jnp.exp(-jnp.abs(x)))


def _split(x, n):
    w = x.shape[-1] // n
    return [x[..., h * w:(h + 1) * w] for h in range(n)]


def _sgu(ua, va, smw, smb, lng, lnb):
    c = ua[0].shape[0]
    width = len(va) * va[0].shape[-1]
    vf = [_gelu(v) for v in va]
    mu = sum(jnp.sum(v, axis=-1, keepdims=True) for v in vf) / width
    var = sum(jnp.sum(jnp.square(v - mu), axis=-1, keepdims=True) for v in vf) / width
    r = lax.rsqrt(var + NORM_EPS)
    causal = lax.broadcasted_iota(jnp.int32, (c, c), 0) >= lax.broadcasted_iota(jnp.int32, (c, c), 1)
    outs = []
    for h in range(len(ua)):
        vn = (vf[h] - mu) * r * lng[h] + lnb[h]
        w = jnp.where(causal, smw[h], 0.0)
        mixed = jnp.dot(w.astype(BF16), vn.astype(BF16), preferred_element_type=F32) + smb[h]
        outs.append(_gelu(ua[h]) * mixed)
    return outs


def _shift_rows(x, k, up):
    rows = x.shape[0]
    row = lax.broadcasted_iota(jnp.int32, x.shape, 0)
    if up:
        return jnp.where(row < rows - k, pltpu.roll(x, rows - k, 0), 0.0)
    return jnp.where(row >= k, pltpu.roll(x, k, 0), 0.0)


@functools.partial(jax.custom_vjp, nondiff_argnums=(1,))
def _delay(x, k):
    return _shift_rows(x, k, False)


def _delay_fwd(x, k):
    return _shift_rows(x, k, False), None


def _delay_bwd(k, _, g):
    return (_shift_rows(g, k, True),)


_delay.defvjp(_delay_fwd, _delay_bwd)


def _conv_silu(x, w0, w1, w2, w3):
    y = w3 * x + w2 * _delay(x, 1) + w1 * _delay(x, 2) + w0 * _delay(x, 3)
    return _silu(y)


def _dn_step(s, q, k, v, braw, araw, alog, dtb):
    nh, c, d = q.shape
    row = lax.broadcasted_iota(jnp.int32, (c, c), 0)
    col = lax.broadcasted_iota(jnp.int32, (c, c), 1)
    causal, strict = (row >= col)[None], (row > col)[None]
    lower = (row >= col).astype(F32)
    strict_f = (row > col).astype(F32)
    eye = (row == col).astype(F32)[None]
    bmm = functools.partial(jnp.einsum, precision=HI, preferred_element_type=F32)

    qn = q * lax.rsqrt(jnp.sum(q * q, axis=-1, keepdims=True) + NORM_EPS) * (d ** -0.5)
    kn = k * lax.rsqrt(jnp.sum(k * k, axis=-1, keepdims=True) + NORM_EPS)
    beta = _sigmoid(braw)
    g = -jnp.exp(alog) * _softplus(araw + dtb)
    lg = lower[None] * g
    gcum = jnp.sum(lg, axis=-1, keepdims=True)
    diff = jnp.dot(lg.reshape(nh * c, c), strict_f, precision=HI,
                   preferred_element_type=F32).reshape(nh, c, c)
    decay = jnp.where(causal, jnp.exp(diff), 0.0)
    bcol = jnp.sum(eye * beta, axis=-1, keepdims=True)
    kb = kn * bcol
    a = jnp.where(strict, bmm("htd,hsd->hts", kb, kn) * decay, 0.0)
    inv = eye - a
    p = bmm("hts,hsr->htr", a, a)
    n_fac = int(math.log2(c)) - 1
    for it in range(n_fac):
        inv = inv + bmm("hts,hsr->htr", inv, p)
        if it < n_fac - 1:
            p = bmm("hts,hsr->htr", p, p)
    eg = jnp.exp(gcum)
    u = bmm("hts,hsd->htd", inv, v * bcol)
    w = bmm("hts,hsd->htd", inv, kb * eg)
    qk = jnp.where(causal, bmm("htd,hsd->hts", qn, kn) * decay, 0.0)
    v_new = u - bmm("hck,hkv->hcv", w, s)
    o = bmm("hck,hkv->hcv", qn * eg, s) + bmm("hts,hsv->htv", qk, v_new)
    glast = jnp.sum(g, axis=-1, keepdims=True)
    kdec = kn * jnp.exp(glast - gcum)
    s_new = s * jnp.exp(glast) + bmm("hck,hcv->hkv", kdec, v_new)
    return s_new, o


def _ogate(o, z, w):
    return [_rms(oh, w) * _silu(zh) for oh, zh in zip(o, z)]


def _xattn(q, k, v):
    outs = []
    for qh, kh, vh in zip(q, k, v):
        s = lax.dot_general(qh.astype(BF16), kh.astype(BF16), (((1,), (1,)), ((), ())),
                            preferred_element_type=F32) * (qh.shape[-1] ** -0.5)
        s = s - jnp.max(s, axis=-1, keepdims=True)
        e = jnp.exp(s)
        p = e / jnp.sum(e, axis=-1, keepdims=True)
        outs.append(jnp.dot(p.astype(BF16), vh.astype(BF16), preferred_element_type=F32))
    return outs


def _rows(t):
    return min(ROW_TILE, t)


def k_rms_fwd(name, x, g):
    t, d = x.shape
    tm = _rows(t)
    return stepk(name, lambda xv, gv: (_rms(xv, gv),), (t // tm,),
                 [(x, (tm, d), lambda i: (i, 0)), (g, (1, d), lambda i: (0, 0))],
                 [((t, d), BF16, (tm, d), lambda i: (i, 0))])[0]


def k_rms_bwd(name, x, g, dhs, dx_res):
    t, d = x.shape
    tm = _rows(t)
    n = len(dhs)

    def fn(xv, gv, *rest):
        dh = sum(r.astype(F32) for r in rest[:n])
        dxr, dg_c = rest[n], rest[n + 1]
        _, vjp = jax.vjp(_rms, xv, gv)
        dx, dg = vjp(dh)
        return dx + dxr, dg_c + dg

    row = lambda arr: (arr, (tm, d), lambda i: (i, 0))
    return stepk(name, fn, (t // tm,), [row(x), (g, (1, d), lambda i: (0, 0))] + [row(h) for h in dhs] + [row(dx_res)],
                 [((t, d), F32, (tm, d), lambda i: (i, 0))], carries=[((1, d), F32)])


def k_post_fwd(name, x, f, g, scale):
    t, d = x.shape
    tm = _rows(t)
    row = lambda arr: (arr, (tm, d), lambda i: (i, 0))
    return stepk(name, lambda xv, fv, gv: (xv + scale * _rms(fv, gv),), (t // tm,),
                 [row(x), row(f), (g, (1, d), lambda i: (0, 0))], [((t, d), F32, (tm, d), lambda i: (i, 0))])[0]


def k_post_bwd(name, f, g, dxo, scale):
    t, d = f.shape
    tm = _rows(t)

    def fn(fv, gv, dv, dg_c):
        _, vjp = jax.vjp(lambda a, b: scale * _rms(a, b), fv, gv)
        df, dg = vjp(dv)
        return df, dg_c + dg

    row = lambda arr: (arr, (tm, d), lambda i: (i, 0))
    return stepk(name, fn, (t // tm,), [row(f), (g, (1, d), lambda i: (0, 0)), row(dxo)],
                 [((t, d), BF16, (tm, d), lambda i: (i, 0))], carries=[((1, d), F32)])


def k_swiglu_fwd(name, hu):
    nb, t, c = hu.shape
    half = nb // 2
    tm = _rows(t)
    fn = lambda gv, uv: (_silu(gv.astype(F32)) * uv.astype(F32),)
    return stepk(name, fn, (half, t // tm),
                 [(hu, (None, tm, c), lambda b, i: (b, i, 0)), (hu, (None, tm, c), lambda b, i: (b + half, i, 0))],
                 [((half, t, c), BF16, (None, tm, c), lambda b, i: (b, i, 0))])[0]


def k_swiglu_bwd(name, hu, da):
    nb, t, c = hu.shape
    half = nb // 2
    tm = _rows(t)

    def body(g_ref, u_ref, d_ref, o_ref):
        gv, uv, dv = g_ref[...].astype(F32), u_ref[...].astype(F32), d_ref[...].astype(F32)
        sg = _sigmoid(gv)

        @pl.when(pl.program_id(0) == 0)
        def _():
            o_ref[...] = (dv * uv * sg * (1.0 + gv * (1.0 - sg))).astype(o_ref.dtype)

        @pl.when(pl.program_id(0) == 1)
        def _():
            o_ref[...] = (dv * gv * sg).astype(o_ref.dtype)

    blk = lambda off: pl.BlockSpec((None, tm, c), lambda p, b, i: (b + off, i, 0))
    return pl.pallas_call(
        body, name=name, grid=(2, half, t // tm), in_specs=[blk(0), blk(half), blk(0)],
        out_specs=pl.BlockSpec((None, tm, c), lambda p, b, i: (p * half + b, i, 0)),
        out_shape=jax.ShapeDtypeStruct((nb, t, c), BF16), compiler_params=_cparams(3))(hu, hu, da)


def k_loss(name, y, tgt):
    t, d = y.shape
    tm = _rows(t)

    def fn(yv, tv, acc):
        e = yv - tv
        part = jnp.sum(jnp.sum(e * e, axis=-1, keepdims=True), axis=0, keepdims=True)
        return e * (1.0 / d), acc + (0.5 / d) * part

    row = lambda arr: (arr, (tm, d), lambda i: (i, 0))
    return stepk(name, fn, (t // tm,), [row(y), row(tgt)], [((t, d), F32, (tm, d), lambda i: (i, 0))],
                 carries=[((1, 1), F32)])


GM_W = GM_HEADS * 128
QKV_COL0 = 2 * GM_W
QKV_W = 3 * DN_HEADS * DN_DIM
Z_COL0 = QKV_COL0 + QKV_W
MAIN_W = Z_COL0 + DN_HEADS * DN_DIM


def _sgu_ins(proj, p):
    c = GM_CHUNK
    return [(proj, (c, GM_W), lambda i: (i, 0)), (proj, (c, GM_W), lambda i: (i, 1)),
            (p["sm_w"], (GM_HEADS, c, c), lambda i: (0, 0, 0)), (p["sm_b"], (GM_HEADS, c, 1), lambda i: (0, 0, 0)),
            (p["sm_ln_g"], (1, GM_W), lambda i: (0, 0)), (p["sm_ln_b"], (1, GM_W), lambda i: (0, 0))]


def _sgu_lists(uv, vv, sw, sb, lg, lb):
    nh = GM_HEADS
    return (_split(uv, nh), _split(vv, nh), [sw[h] for h in range(nh)], [sb[h] for h in range(nh)],
            _split(lg, nh), _split(lb, nh))


def k_sgu_fwd(name, proj, p):
    t = proj.shape[0]

    def fn(*vals):
        return (jnp.concatenate(_sgu(*_sgu_lists(*vals)), axis=-1),)

    return stepk(name, fn, (t // GM_CHUNK,), _sgu_ins(proj, p),
                 [((t, GM_W), BF16, (GM_CHUNK, GM_W), lambda i: (i, 0))])[0]


def k_sgu_bwd(name, proj, p, dy):
    t = proj.shape[0]
    c = GM_CHUNK

    def fn(uv, vv, sw, sb, lg, lb, dv, dsw, dsb, dlg, dlb):
        _, vjp = jax.vjp(_sgu, *_sgu_lists(uv, vv, sw, sb, lg, lb))
        gu, gv, gsw, gsb, glg, glb = vjp(_split(dv.astype(F32), GM_HEADS))
        cat = lambda l: jnp.concatenate(l, axis=-1)
        return (cat(gu), cat(gv), dsw + jnp.stack(gsw), dsb + jnp.stack(gsb), dlg + cat(glg), dlb + cat(glb))

    return stepk(name, fn, (t // c,), _sgu_ins(proj, p) + [(dy, (None, c, GM_W), lambda i: (0, i, 0))],
                 [((t, GM_W), BF16, (c, GM_W), lambda i: (i, 0)), ((t, GM_W), BF16, (c, GM_W), lambda i: (i, 0))],
                 carries=[((GM_HEADS, c, c), F32), ((GM_HEADS, c, 1), F32), ((1, GM_W), F32), ((1, GM_W), F32)])


def _conv_ins(proj, conv_w):
    t = proj.shape[0]
    return [(proj, (t, 128), lambda j: (0, QKV_COL0 // 128 + j)), (conv_w, (CONV_W, 128), lambda j: (0, j))]


def k_conv_fwd(name, proj, conv_w):
    t = proj.shape[0]
    n = QKV_W // 128
    fn = lambda xv, wv: (_conv_silu(xv, *[wv[i:i + 1] for i in range(CONV_W)]),)
    return stepk(name, fn, (n,), _conv_ins(proj, conv_w), [((n, t, 128), F32, (None, t, 128), lambda j: (j, 0, 0))])[0]


def k_conv_bwd(name, proj, conv_w, dqkv):
    t = proj.shape[0]
    n = QKV_W // 128

    def fn(xv, wv, dv):
        _, vjp = jax.vjp(_conv_silu, xv, *[wv[i:i + 1] for i in range(CONV_W)])
        gx, *gw = vjp(dv)
        return gx, jnp.concatenate(gw, axis=0)

    return stepk(name, fn, (n,), _conv_ins(proj, conv_w) + [(dqkv, (None, t, 128), lambda j: (j, 0, 0))],
                 [((t, QKV_W), BF16, (t, 128), lambda j: (0, j)), ((CONV_W, QKV_W), F32, (CONV_W, 128), lambda j: (0, j))])


def _dn_ins(qkv, braw, araw, p, order):
    h, c, d = DN_HEADS, DN_CHUNK, DN_DIM
    qkv_in = lambda part: (qkv, (h, c, d), lambda n: (part, order(n), 0))
    gate_in = lambda arr: (arr, (None, h, 1, c), lambda n: (order(n), 0, 0, 0))
    par_in = lambda arr: (arr, (h, 1, 1), lambda n: (0, 0, 0))
    return [qkv_in(0), qkv_in(1), qkv_in(2), gate_in(braw), gate_in(araw), par_in(p["a_log"]), par_in(p["dt_bias"])]


def k_dn_fwd(name, qkv, braw, araw, p):
    t = qkv.shape[1]
    h, c, d = DN_HEADS, DN_CHUNK, DN_DIM
    nc = t // c

    def fn(q, k, v, b, a, al, dt, s):
        s_new, o = _dn_step(s, q, k, v, b, a, al, dt)
        return o, s, s_new

    o, s_all, _ = stepk(name, fn, (nc,), _dn_ins(qkv, braw, araw, p, lambda n: n),
                        [((h, t, d), F32, (h, c, d), lambda n: (0, n, 0)),
                         ((nc, h, d, d), F32, (None, h, d, d), lambda n: (n, 0, 0, 0))],
                        carries=[((h, d, d), F32)])
    return o, s_all


def k_dn_bwd(name, qkv, braw, araw, p, s_all, do):
    t = qkv.shape[1]
    h, c, d = DN_HEADS, DN_CHUNK, DN_DIM
    nc = t // c
    rev = lambda n: nc - 1 - n

    def fn(q, k, v, b, a, al, dt, s, dov, ds_c, dal_c, ddt_c):
        _, vjp = jax.vjp(_dn_step, s, q, k, v, b, a, al, dt)
        ds, dq, dk, dv, db, da, dal, ddt = vjp((ds_c, dov))
        return dq, dk, dv, db, da, ds, dal_c + dal, ddt_c + ddt

    ins = _dn_ins(qkv, braw, araw, p, rev) + [(s_all, (None, h, d, d), lambda n: (rev(n), 0, 0, 0)),
                                               (do, (h, c, d), lambda n: (0, rev(n), 0))]
    hd = ((h, t, d), F32, (h, c, d), lambda n: (0, rev(n), 0))
    gate = ((nc, h, 1, c), F32, (None, h, 1, c), lambda n: (rev(n), 0, 0, 0))
    dq, dk, dv, db, da, _, dal, ddt = stepk(name, fn, (nc,), ins, [hd, hd, hd, gate, gate],
                                            carries=[((h, d, d), F32), ((h, 1, 1), F32), ((h, 1, 1), F32)])
    return dq, dk, dv, db, da, dal, ddt


def _ogate_ins(o, proj, p):
    t = o.shape[1]
    tm = _rows(t)
    return tm, [(o, (DN_HEADS, tm, DN_DIM), lambda i: (0, i, 0)), (proj, (tm, GM_W), lambda i: (i, Z_COL0 // GM_W)),
                (p["dn_norm_w"], (1, DN_DIM), lambda i: (0, 0))]


def k_ogate_fwd(name, o, proj, p):
    t = o.shape[1]
    tm, ins = _ogate_ins(o, proj, p)

    def fn(ov, zv, wv):
        return (jnp.concatenate(_ogate([ov[h] for h in range(DN_HEADS)], _split(zv, DN_HEADS), wv), axis=-1),)

    return stepk(name, fn, (t // tm,), ins, [((t, GM_W), BF16, (tm, GM_W), lambda i: (i, 0))])[0]


def k_ogate_bwd(name, o, proj, p, dy):
    t = o.shape[1]
    tm, ins = _ogate_ins(o, proj, p)

    def fn(ov, zv, wv, dv, dw_c):
        _, vjp = jax.vjp(_ogate, [ov[h] for h in range(DN_HEADS)], _split(zv, DN_HEADS), wv)
        go, gz, gw = vjp(_split(dv.astype(F32), DN_HEADS))
        return jnp.stack(go), jnp.concatenate(gz, axis=-1), dw_c + gw

    return stepk(name, fn, (t // tm,), ins + [(dy, (None, tm, GM_W), lambda i: (1, i, 0))],
                 [((DN_HEADS, t, DN_DIM), F32, (DN_HEADS, tm, DN_DIM), lambda i: (0, i, 0)),
                  ((t, GM_W), BF16, (tm, GM_W), lambda i: (i, 0))], carries=[((1, DN_DIM), F32)])


def _xattn_lists(qv, kvv):
    nh = XA_HEADS
    return (_split(qv.astype(F32), nh), [kvv[h].astype(F32) for h in range(nh)],
            [kvv[nh + h].astype(F32) for h in range(nh)])


def k_xattn_fwd(name, q, kv):
    t, d = q.shape
    tm = _rows(t)
    fn = lambda qv, kvv: (jnp.concatenate(_xattn(*_xattn_lists(qv, kvv)), axis=-1),)
    return stepk(name, fn, (t // tm,), [(q, (tm, d), lambda i: (i, 0)), (kv, kv.shape, lambda i: (0, 0, 0))],
                 [((t, d), BF16, (tm, d), lambda i: (i, 0))])[0]


def k_xattn_bwd(name, q, kv, do):
    t, d = q.shape
    tm = _rows(t)

    def fn(qv, kvv, dv, dkv_c):
        _, vjp = jax.vjp(_xattn, *_xattn_lists(qv, kvv))
        gq, gk, gv = vjp(_split(dv.astype(F32), XA_HEADS))
        return jnp.concatenate(gq, axis=-1), dkv_c + jnp.stack(gk + gv)

    return stepk(name, fn, (t // tm,),
                 [(q, (tm, d), lambda i: (i, 0)), (kv, kv.shape, lambda i: (0, 0, 0)), (do, (None, tm, d), lambda i: (0, i, 0))],
                 [((t, d), BF16, (tm, d), lambda i: (i, 0))], carries=[(kv.shape, F32)])


def ffn_fwd(tag, x, p, pre, post, gu, dn):
    h = k_rms_fwd(f"{tag}_pre", x, p[pre])
    hu = mm(f"{tag}_gu", h[None], p[gu], "nn", BF16, nbo=N_DEV)
    a = k_swiglu_fwd(f"{tag}_act", hu)
    f = mm(f"{tag}_down", a, p[dn], "nn", F32)[0]
    y = k_post_fwd(f"{tag}_post", x, f, p[post], 0.5)
    return y, (x, h, hu, a, f)


def ffn_bwd(tag, dy, saved, p, pre, post, gu, dn, grads):
    x, h, hu, a, f = saved
    df, grads[post] = k_post_bwd(f"{tag}_post_b", f, p[post], dy, 0.5)
    da = mm(f"{tag}_down_dx", df[None], p[dn], "nt", BF16, nbo=N_DEV // 2)
    grads[dn] = mm(f"{tag}_down_dw", a, df[None], "tn", BF16)
    dhu = k_swiglu_bwd(f"{tag}_act_b", hu, da)
    dh = mm(f"{tag}_gu_dx", dhu, p[gu], "nt", BF16)[0]
    grads[gu] = mm(f"{tag}_gu_dw", h[None], dhu, "tn", BF16, nbo=N_DEV)
    dx, grads[pre] = k_rms_bwd(f"{tag}_pre_b", x, p[pre], [dh], dy)
    return dx


def _to_chunks(a):
    t, h = a.shape
    return a.reshape(t // DN_CHUNK, DN_CHUNK, h).transpose(0, 2, 1).reshape(t // DN_CHUNK, h, 1, DN_CHUNK)


def _from_chunks(a):
    nc, h, _, c = a.shape
    return a.reshape(nc, h, c).transpose(0, 2, 1).reshape(nc * c, h)


def mix_fwd(tag, x, p):
    h = k_rms_fwd(f"{tag}_pre", x, p["mix_norm_pre"])
    proj = mm(f"{tag}_in", h[None], p["w_in_main"], "nn", F32)[0]
    small = mm(f"{tag}_in_s", h[None], p["w_in_small"], "nn", F32)[0]
    braw, araw = _to_chunks(small[:, :DN_HEADS]), _to_chunks(small[:, DN_HEADS:2 * DN_HEADS])
    ya = k_sgu_fwd(f"{tag}_sgu", proj, p)
    qkv = k_conv_fwd(f"{tag}_conv", proj, p["conv_w"])
    o, s_all = k_dn_fwd(f"{tag}_dn", qkv, braw, araw, p)
    yb = k_ogate_fwd(f"{tag}_og", o, proj, p)
    y = jnp.stack([ya, yb])
    m = mm(f"{tag}_out", y, p["w_out"], "nn", F32)[0]
    out = k_post_fwd(f"{tag}_post", x, m, p["mix_norm_post"], 1.0)
    return out, (x, h, proj, braw, araw, qkv, o, s_all, y, m)


def mix_bwd(tag, dy, saved, p, grads):
    x, h, proj, braw, araw, qkv, o, s_all, y, m = saved
    t = x.shape[0]
    dm, grads["mix_norm_post"] = k_post_bwd(f"{tag}_post_b", m, p["mix_norm_post"], dy, 1.0)
    dyy = mm(f"{tag}_out_dx", dm[None], p["w_out"], "nt", BF16, nbo=2)
    grads["w_out"] = mm(f"{tag}_out_dw", y, dm[None], "tn", BF16)
    do, dz, grads["dn_norm_w"] = k_ogate_bwd(f"{tag}_og_b", o, proj, p, dyy)
    dq, dk, dv, db, da, grads["a_log"], grads["dt_bias"] = k_dn_bwd(f"{tag}_dn_b", qkv, braw, araw, p, s_all, do)
    dconv, grads["conv_w"] = k_conv_bwd(f"{tag}_conv_b", proj, p["conv_w"], jnp.concatenate([dq, dk, dv], axis=0))
    du, dva, grads["sm_w"], grads["sm_b"], grads["sm_ln_g"], grads["sm_ln_b"] = k_sgu_bwd(f"{tag}_sgu_b", proj, p, dyy)
    dproj = jnp.concatenate([du, dva, dconv, dz], axis=-1)
    dsmall = jnp.concatenate([_from_chunks(db), _from_chunks(da), jnp.zeros((t, 128 - 2 * DN_HEADS), F32)], axis=-1)
    dh = mm(f"{tag}_in_dx", dproj[None], p["w_in_main"], "nt", BF16)[0]
    dh_s = mm(f"{tag}_in_s_dx", dsmall[None], p["w_in_small"], "nt", BF16)[0]
    grads["w_in_main"] = mm(f"{tag}_in_dw", h[None], dproj[None], "tn", BF16)
    grads["w_in_small"] = mm(f"{tag}_in_s_dw", h[None], dsmall[None], "tn", F32)
    dx, grads["mix_norm_pre"] = k_rms_bwd(f"{tag}_pre_b", x, p["mix_norm_pre"], [dh, dh_s], dy)
    return dx


def xa_fwd(tag, x, mem, p):
    hx = k_rms_fwd(f"{tag}_pre", x, p["xa_norm_pre"])
    mh = k_rms_fwd(f"{tag}_mem", mem, p["mem_norm"])
    q = mm(f"{tag}_q", hx[None], p["w_xq"], "nn", BF16)[0]
    kv = mm(f"{tag}_kv", mh[None], p["w_xkv"], "nn", BF16, nbo=N_DEV)
    o = k_xattn_fwd(f"{tag}_att", q, kv)
    c = mm(f"{tag}_o", o[None], p["w_xo"], "nn", F32)[0]
    out = k_post_fwd(f"{tag}_post", x, c, p["xa_norm_post"], 1.0)
    return out, (x, mem, hx, mh, q, kv, o, c)


def xa_bwd(tag, dy, saved, p, grads):
    x, mem, hx, mh, q, kv, o, c = saved
    dc, grads["xa_norm_post"] = k_post_bwd(f"{tag}_post_b", c, p["xa_norm_post"], dy, 1.0)
    do = mm(f"{tag}_o_dx", dc[None], p["w_xo"], "nt", BF16)
    grads["w_xo"] = mm(f"{tag}_o_dw", o[None], dc[None], "tn", BF16)
    dq, dkv = k_xattn_bwd(f"{tag}_att_b", q, kv, do)
    dhx = mm(f"{tag}_q_dx", dq[None], p["w_xq"], "nt", BF16)[0]
    grads["w_xq"] = mm(f"{tag}_q_dw", hx[None], dq[None], "tn", BF16)
    dkv16 = dkv.astype(BF16)
    dmh = mm(f"{tag}_kv_dx", dkv16, p["w_xkv"], "nt", BF16)[0]
    grads["w_xkv"] = mm(f"{tag}_kv_dw", mh[None], dkv16, "tn", BF16, nbo=N_DEV)
    _, grads["mem_norm"] = k_rms_bwd(f"{tag}_mem_b", mem, p["mem_norm"], [dmh], jnp.zeros_like(mem))
    dx, grads["xa_norm_pre"] = k_rms_bwd(f"{tag}_pre_b", x, p["xa_norm_pre"], [dhx], dy)
    return dx


def layer_fwd(l, x, mem, p):
    x, s1 = ffn_fwd(f"l{l}f1", x, p, "ffn1_norm_pre", "ffn1_norm_post", "ffn1_w_gate_up", "ffn1_w_down")
    x, s2 = mix_fwd(f"l{l}mx", x, p)
    x, s3 = xa_fwd(f"l{l}xa", x, mem, p)
    x, s4 = ffn_fwd(f"l{l}f2", x, p, "ffn2_norm_pre", "ffn2_norm_post", "ffn2_w_gate_up", "ffn2_w_down")
    return x, (s1, s2, s3, s4)


def layer_bwd(l, dy, saved, p):
    s1, s2, s3, s4 = saved
    grads = {}
    dy = ffn_bwd(f"l{l}f2", dy, s4, p, "ffn2_norm_pre", "ffn2_norm_post", "ffn2_w_gate_up", "ffn2_w_down", grads)
    dy = xa_bwd(f"l{l}xa", dy, s3, p, grads)
    dy = mix_bwd(f"l{l}mx", dy, s2, p, grads)
    dy = ffn_bwd(f"l{l}f1", dy, s1, p, "ffn1_norm_pre", "ffn1_norm_post", "ffn1_w_gate_up", "ffn1_w_down", grads)
    return dy, grads


def _place():
    return lax.axis_index("x"), lax.axis_index("y"), lax.axis_index("c")


def _other_chips(x, y):
    return [(1 - x, y), (x, 1 - y), (1 - x, 1 - y)]


_ANY = pl.BlockSpec(memory_space=pl.ANY)


def all_gather(name, shards):
    n = len(shards)

    def body(*refs):
        ins, outs = refs[:n], refs[n:2 * n]
        send_sems, recv_sems, local_sems = refs[2 * n:]
        x, y, c = _place()
        me, sibling = (x, y, c), (x, y, 1 - c)
        chips = _other_chips(x, y)
        idx = lambda px, py, pc: 4 * px + 2 * py + pc

        def copy(a, k, block, to, src=None):
            dst = outs[a].at[idx(*block)]
            return pltpu.make_async_remote_copy(src_ref=dst if src is None else src, dst_ref=dst,
                                                send_sem=send_sems.at[7 * a + k], recv_sem=recv_sems.at[7 * a + k],
                                                device_id=to, device_id_type=MESH)

        mine = [pltpu.make_async_copy(ins[a], outs[a].at[idx(*me)], local_sems.at[a]) for a in range(n)]
        for cp in mine:
            cp.start()
        first = []
        for a in range(n):
            first.append(copy(a, 0, me, sibling, src=ins[a]))
            first += [copy(a, 1 + j, me, (*chip, c), src=ins[a]) for j, chip in enumerate(chips)]
        for cp in first:
            cp.start()
        passed = []
        for a in range(n):
            for j, chip in enumerate(chips):
                copy(a, 1 + j, (*chip, c), me).wait_recv()
                passed.append(copy(a, 4 + j, (*chip, c), sibling))
                passed[-1].start()
        for a in range(n):
            copy(a, 0, sibling, me).wait_recv()
            for j, chip in enumerate(chips):
                copy(a, 4 + j, (*chip, 1 - c), me).wait_recv()
        for cp in first + passed:
            cp.wait_send()
        for cp in mine:
            cp.wait()

    return pl.pallas_call(
        body, name=name, in_specs=[_ANY] * n, out_specs=[_ANY] * n,
        out_shape=[jax.ShapeDtypeStruct((N_DEV,) + s.shape, s.dtype) for s in shards],
        scratch_shapes=[pltpu.SemaphoreType.DMA((7 * n,)), pltpu.SemaphoreType.DMA((7 * n,)),
                        pltpu.SemaphoreType.DMA((n,))])(*shards)


def xchg_sibling(name, grads):
    n = len(grads)

    def body(*refs):
        ins, outs = refs[:n], refs[n:2 * n]
        send_sems, recv_sems = refs[2 * n:]
        x, y, c = _place()
        copies = [pltpu.make_async_remote_copy(src_ref=ins[a].at[j, 1 - c], dst_ref=outs[a].at[j],
                                               send_sem=send_sems.at[4 * a + j], recv_sem=recv_sems.at[4 * a + j],
                                               device_id=(x, y, 1 - c), device_id_type=MESH)
                  for a in range(n) for j in range(4)]
        for cp in copies:
            cp.start()
        for cp in copies:
            cp.wait_recv()
        for cp in copies:
            cp.wait_send()

    return pl.pallas_call(
        body, name=name, in_specs=[_ANY] * n, out_specs=[_ANY] * n,
        out_shape=[jax.ShapeDtypeStruct((4,) + g.shape[2:], g.dtype) for g in grads],
        scratch_shapes=[pltpu.SemaphoreType.DMA((4 * n,)), pltpu.SemaphoreType.DMA((4 * n,))])(*grads)


def xchg_chips(name, parts):
    n = len(parts)

    def body(*refs):
        ins, outs = refs[:n], refs[n:2 * n]
        send_sems, recv_sems = refs[2 * n:]
        x, y, c = _place()
        copies = [pltpu.make_async_remote_copy(src_ref=ins[a].at[2 * px + py], dst_ref=outs[a].at[j],
                                               send_sem=send_sems.at[3 * a + j], recv_sem=recv_sems.at[3 * a + j],
                                               device_id=(px, py, c), device_id_type=MESH)
                  for a in range(n) for j, (px, py) in enumerate(_other_chips(x, y))]
        for cp in copies:
            cp.start()
        for cp in copies:
            cp.wait_recv()
        for cp in copies:
            cp.wait_send()

    return pl.pallas_call(
        body, name=name, in_specs=[_ANY] * n, out_specs=[_ANY] * n,
        out_shape=[jax.ShapeDtypeStruct((3,) + p.shape[1:], p.dtype) for p in parts],
        scratch_shapes=[pltpu.SemaphoreType.DMA((3 * n,)), pltpu.SemaphoreType.DMA((3 * n,))])(*parts)


ELEMWISE_BLOCK = 256 * 1024


def _row_tile(r, cc):
    t = r
    while t * cc > ELEMWISE_BLOCK and t % 32 == 0:
        t //= 2
    if t * cc > ELEMWISE_BLOCK:
        for cand in range(t, 15, -16):
            if r % cand == 0 and cand * cc <= ELEMWISE_BLOCK:
                return cand
    return t


def k_pair_add(name, g4, r1, core):
    _, _, r, cc = g4.shape
    tr = _row_tile(r, cc)
    fn = lambda av, bv: (av.astype(F32) + bv.astype(F32),)
    return stepk(name, fn, (4, r // tr),
                 [(g4, (None, None, tr, cc), lambda j, i, pre: (j, pre[0], i, 0)),
                  (r1, (None, tr, cc), lambda j, i, pre: (j, i, 0))],
                 [((4, r, cc), BF16, (None, tr, cc), lambda j, i, pre: (j, i, 0))], prefetch=core)[0]


def _adamw(g, w, m, v):
    m2 = ADAM_B1 * m + (1.0 - ADAM_B1) * g
    v2 = ADAM_B2 * v + (1.0 - ADAM_B2) * jnp.square(g)
    m_hat = m2 / (1.0 - ADAM_B1 ** ADAM_STEP)
    v_hat = v2 / (1.0 - ADAM_B2 ** ADAM_STEP)
    delta = -ADAM_LR * (m_hat / (jnp.sqrt(v_hat) + ADAM_EPS) + ADAM_WD * w)
    return g, delta, m2, v2


def k_adamw_shard(name, part, others, w, m, v, layer, chip, fill):
    _, r, cc = part.shape
    tr = _row_tile(r, cc)

    def fn(pv, o0, o1, o2, wv, mv, vv):
        g = ((pv.astype(F32) + o0.astype(F32)) + o1.astype(F32)) + o2.astype(F32)
        return _adamw(g, wv, mv, vv)

    other = lambda k: (others, (None, tr, cc), lambda i, pre: (k, i, 0))
    state = lambda arr: (arr, (None, tr, cc), lambda i, pre: (layer, i, 0))
    out = ((2, r, cc), F32, (None, tr, cc), lambda i, pre: (layer, i, 0))
    return stepk(name, fn, (r // tr,),
                 [(part, (None, tr, cc), lambda i, pre: (pre[0], i, 0)), other(0), other(1), other(2),
                  state(w), state(m), state(v)],
                 [out] * 4, prefetch=chip, fill=[(f, k) for k, f in enumerate(fill)])


def k_sum8(name, parts):
    _, rows, lanes = parts.shape

    def fn(pv):
        acc = pv[0]
        for d in range(1, N_DEV):
            acc = acc + pv[d]
        return (acc,)

    return stepk(name, fn, (1,), [(parts, parts.shape, lambda i: (0, 0, 0))],
                 [((rows, lanes), F32, (rows, lanes), lambda i: (0, 0))])[0]


def k_adamw_flat(name, g, w, m, v):
    whole = lambda arr: (arr, arr.shape, lambda i: (0, 0))
    out = (g.shape, F32, g.shape, lambda i: (0, 0))
    return stepk(name, _adamw, (1,), [whole(g), whole(w), whole(m), whole(v)], [out] * 4)


WEIGHTS = ("ffn1_norm_pre", "ffn1_w_gate_up", "ffn1_w_down", "ffn1_norm_post", "mix_norm_pre", "w_in", "conv_w", "a_log",
           "dt_bias", "sm_w", "sm_b", "sm_ln_g", "sm_ln_b", "dn_norm_w", "w_out", "mix_norm_post", "xa_norm_pre", "mem_norm",
           "w_xq", "w_xkv", "w_xo", "xa_norm_post", "ffn2_norm_pre", "ffn2_w_gate_up", "ffn2_w_down", "ffn2_norm_post")
BIG = ("ffn1_w_gate_up", "ffn1_w_down", "w_in", "w_out", "w_xq", "w_xkv", "w_xo", "ffn2_w_gate_up", "ffn2_w_down")
ROW_SHARDED = ("ffn1_w_down", "w_out", "w_xq", "w_xo", "ffn2_w_down")
SMALL = tuple(n for n in WEIGHTS if n not in BIG and n != "conv_w")
N_LAYERS = 2
IN_COLS = MAIN_W + 2 * DN_HEADS


def _layer_params(l, full, conv_full, a):
    p = {}
    for n in BIG:
        w = full[n]
        if n in ROW_SHARDED:
            p[n] = w.reshape(1, w.shape[0] * w.shape[1], w.shape[2])
        elif n == "w_in":
            w = w.transpose(1, 0, 2).reshape(w.shape[1], IN_COLS)
            p["w_in_main"] = w[None, :, :MAIN_W]
            p["w_in_small"] = jnp.pad(w[:, MAIN_W:], ((0, 0), (0, 128 - 2 * DN_HEADS)))[None]
        else:
            p[n] = w
    p["conv_w"] = conv_full[l]
    for n in SMALL:
        w = a[n][l]
        if n in ("a_log", "dt_bias"):
            p[n] = w.reshape(DN_HEADS, 1, 1)
        elif n == "sm_b":
            p[n] = w[..., None]
        elif n == "sm_w":
            p[n] = w
        else:
            p[n] = w[None]
    return p


def _scatter_layout(n, g):
    if n in ROW_SHARDED:
        return g.reshape(N_DEV, g.shape[1] // N_DEV, g.shape[2])
    return g


def _pack(flat_parts):
    flat = jnp.concatenate(flat_parts)
    rows = -(-flat.shape[0] // 1024) * 8
    return jnp.pad(flat, (0, rows * 128 - flat.shape[0])).reshape(rows, 128)


def kernel(x, mem, ffn1_norm_pre, ffn1_w_gate_up, ffn1_w_down, ffn1_norm_post, mix_norm_pre, w_in, conv_w, a_log, dt_bias, sm_w, sm_b, sm_ln_g, sm_ln_b, dn_norm_w, w_out, mix_norm_post, xa_norm_pre, mem_norm, w_xq, w_xkv, w_xo, xa_norm_post, ffn2_norm_pre, ffn2_w_gate_up, ffn2_w_down, ffn2_norm_post, loss_target, m_ffn1_norm_pre, m_ffn1_w_gate_up, m_ffn1_w_down, m_ffn1_norm_post, m_mix_norm_pre, m_w_in, m_conv_w, m_a_log, m_dt_bias, m_sm_w, m_sm_b, m_sm_ln_g, m_sm_ln_b, m_dn_norm_w, m_w_out, m_mix_norm_post, m_xa_norm_pre, m_mem_norm, m_w_xq, m_w_xkv, m_w_xo, m_xa_norm_post, m_ffn2_norm_pre, m_ffn2_w_gate_up, m_ffn2_w_down, m_ffn2_norm_post, v_ffn1_norm_pre, v_ffn1_w_gate_up, v_ffn1_w_down, v_ffn1_norm_post, v_mix_norm_pre, v_w_in, v_conv_w, v_a_log, v_dt_bias, v_sm_w, v_sm_b, v_sm_ln_g, v_sm_ln_b, v_dn_norm_w, v_w_out, v_mix_norm_post, v_xa_norm_pre, v_mem_norm, v_w_xq, v_w_xkv, v_w_xo, v_xa_norm_post, v_ffn2_norm_pre, v_ffn2_w_gate_up, v_ffn2_w_down, v_ffn2_norm_post):
    a = dict(locals())
    px, py, pc = _place()
    core = jnp.reshape(pc, (1,)).astype(jnp.int32)
    chip = jnp.reshape(2 * px + py, (1,)).astype(jnp.int32)
    me = 4 * px + 2 * py + pc
    xs, mems, tgt = x[0], mem[0], loss_target[0]

    conv_full = all_gather("ag_conv", [conv_w])[0]
    conv_full = conv_full.transpose(1, 2, 0, 3).reshape(N_LAYERS, CONV_W, QKV_W)
    params = []
    for l in range(N_LAYERS):
        full = all_gather(f"ag_l{l}", [a[n][l].astype(BF16) for n in BIG])
        params.append(_layer_params(l, dict(zip(BIG, full)), conv_full, a))

    saved = []
    h = xs
    for l in range(N_LAYERS):
        h, s = layer_fwd(l, h, mems, params[l])
        saved.append(s)
    dy, loss = k_loss("loss", h, tgt)
    loss = lax.psum(loss[0, 0], ("x", "y", "c"))

    grads = [None] * N_LAYERS
    big_out = {}
    for l in reversed(range(N_LAYERS)):
        dy, g = layer_bwd(l, dy, saved[l], params[l])
        grads[l] = g
        w_in_g = jnp.concatenate([g["w_in_main"][0], g["w_in_small"][0][:, :2 * DN_HEADS].astype(BF16)], axis=-1)
        g["w_in"] = w_in_g.reshape(w_in_g.shape[0], N_DEV, IN_COLS // N_DEV).transpose(1, 0, 2)
        g4 = [_scatter_layout(n, g[n]) for n in BIG]
        g4 = [t.reshape(4, 2, *t.shape[1:]) for t in g4]
        r1 = xchg_sibling(f"rs1_l{l}", g4)
        parts = [k_pair_add(f"rs_add_{n}", gg, rr, core) for n, gg, rr in zip(BIG, g4, r1)]
        r2 = xchg_chips(f"rs2_l{l}", parts)
        for n, part, others in zip(BIG, parts, r2):
            big_out[n] = k_adamw_shard(f"adamw_{n}_l{l}", part, others, a[n], a["m_" + n], a["v_" + n], l, chip,
                                       big_out.get(n, ()))

    flat = [jnp.concatenate([grads[l][n].reshape(-1) for l in range(N_LAYERS)]) for n in SMALL]
    flat += [grads[l]["conv_w"].reshape(-1) for l in range(N_LAYERS)]
    gsum = k_sum8("small_sum", all_gather("ag_small", [_pack(flat)])[0]).reshape(-1)
    sizes = [a[n].size for n in SMALL]
    n_rep = sum(sizes)
    conv_g = gsum[n_rep:n_rep + N_LAYERS * CONV_W * QKV_W].reshape(N_LAYERS, CONV_W, QKV_W)
    conv_g = lax.dynamic_slice_in_dim(conv_g, me * (QKV_W // N_DEV), QKV_W // N_DEV, axis=2)
    pack_state = lambda pre: _pack([a[pre + n].reshape(-1) for n in SMALL] + [a[pre + "conv_w"].reshape(-1)])
    small_out = k_adamw_flat("small_adamw", _pack([gsum[:n_rep], conv_g.reshape(-1)]), pack_state(""), pack_state("m_"),
                             pack_state("v_"))
    small_out = [o.reshape(-1) for o in small_out]
    outs = {}
    off = 0
    for n, sz in zip(SMALL + ("conv_w",), sizes + [conv_w.size]):
        outs[n] = [o[off:off + sz].reshape(a[n].shape) for o in small_out]
        off += sz
    for n in BIG:
        outs[n] = list(big_out[n])

    return (loss, dy[None], *[outs[n][0] for n in WEIGHTS], *[outs[n][1] for n in WEIGHTS],
            *[outs[n][2] for n in WEIGHTS], *[outs[n][3] for n in WEIGHTS])
```

```python
import functools
import math

import jax
import jax.numpy as jnp
from jax import lax
from jax.experimental import pallas as pl
from jax.experimental.pallas import tpu as pltpu

F32, BF16 = jnp.float32, jnp.bfloat16
HI = lax.Precision.HIGHEST
MESH = pl.DeviceIdType.MESH

N_DEV = 8
NORM_EPS = 1e-6
GM_HEADS, GM_CHUNK = 8, 128
DN_HEADS, DN_CHUNK, DN_DIM = 8, 64, 128
XA_HEADS, XA_DIM = 4, 512
CONV_W = 4
ADAM_LR, ADAM_B1, ADAM_B2, ADAM_EPS, ADAM_WD, ADAM_STEP = 0.001, 0.9, 0.999, 1e-08, 0.01, 10

V7X_VMEM_LIMIT = 56 * 1024 * 1024
WHOLE_TILE_MAX = 1536
ROW_TILE = 256


def _cparams(n_grid):
    return pltpu.CompilerParams(dimension_semantics=("arbitrary",) * n_grid, vmem_limit_bytes=V7X_VMEM_LIMIT)


def stepk(name, fn, grid, ins, outs, carries=(), prefetch=None, fill=(), after=()):
    n_in, n_out, n_c, n_fill = len(ins), len(outs), len(carries), len(fill)
    n_pre = 0 if prefetch is None else 1

    def body(*refs):
        refs = refs[n_pre:]
        in_refs, refs = refs[:n_in], refs[n_in + n_fill + len(after):]
        out_refs, c_refs = refs[:n_out], refs[n_out:]
        if n_c:
            first = functools.reduce(jnp.logical_and, [pl.program_id(a) == 0 for a in range(len(grid))])

            @pl.when(first)
            def _():
                for r in c_refs:
                    r[...] = jnp.zeros(r.shape, r.dtype)
        res = fn(*[r[...] for r in in_refs], *[r[...] for r in c_refs])
        for r, v in zip(tuple(out_refs) + tuple(c_refs), res):
            r[...] = v.astype(r.dtype)

    in_specs = [pl.BlockSpec(bs, im) for _, bs, im in ins] + [pl.BlockSpec(memory_space=pl.ANY)] * (n_fill + len(after))
    aliases = {n_pre + n_in + i: k for i, (_, k) in enumerate(fill)}
    out_specs = [pl.BlockSpec(bs, im) for _, _, bs, im in outs]
    out_shape = [jax.ShapeDtypeStruct(s, d) for s, d, _, _ in outs]
    for s, d in carries:
        zeros = (0,) * len(s)
        out_specs.append(pl.BlockSpec(s, lambda *a, _z=zeros: _z))
        out_shape.append(jax.ShapeDtypeStruct(s, d))
    args = [a for a, _, _ in ins] + [a for a, _ in fill] + list(after)
    if prefetch is None:
        call = pl.pallas_call(body, name=name, grid=grid, in_specs=in_specs, out_specs=out_specs, out_shape=out_shape,
                              input_output_aliases=aliases, compiler_params=_cparams(len(grid)))
        return call(*args)
    spec = pltpu.PrefetchScalarGridSpec(num_scalar_prefetch=1, grid=grid, in_specs=in_specs, out_specs=out_specs)
    call = pl.pallas_call(body, name=name, grid_spec=spec, out_shape=out_shape, input_output_aliases=aliases,
                          compiler_params=_cparams(len(grid)))
    return call(prefetch, *args)


def _tile(n, pref, align=128):
    if n <= WHOLE_TILE_MAX:
        return n
    t = (pref // align) * align
    while t > align and n % t:
        t -= align
    assert n % t == 0, (n, pref)
    return t


def mm(name, a, b, mode, out_dtype, nbo=1):
    nba, ra, ca = a.shape
    nbb, rb, cb = b.shape
    if mode == "nn":
        m, k, n = ra, nba * ca, nbb * cb
        assert rb == k
    elif mode == "nt":
        m, k, n = ra, nba * ca, rb
        assert nbb * cb == k
    else:
        k, m, n = ra, nba * ca, nbb * cb
        assert rb == k
    co = n // nbo
    if mode == "nn":
        tm, tk, tn = _tile(m, 1024, 8), _tile(ca, 512), _tile(math.gcd(cb, co), 1024)
        a_spec = pl.BlockSpec((None, tm, tk), lambda i, j, kk: (kk // (ca // tk), i, kk % (ca // tk)))
        b_spec = pl.BlockSpec((None, tk, tn), lambda i, j, kk: (j // (cb // tn), kk, j % (cb // tn)))
        dims = (((1,), (0,)), ((), ()))
    elif mode == "nt":
        tm, tk, tn = _tile(m, 1024, 8), _tile(math.gcd(ca, cb), 512), _tile(co, 1024)
        a_spec = pl.BlockSpec((None, tm, tk), lambda i, j, kk: (kk // (ca // tk), i, kk % (ca // tk)))
        b_spec = pl.BlockSpec((None, tn, tk), lambda i, j, kk: (kk // (cb // tk), j, kk % (cb // tk)))
        dims = (((1,), (1,)), ((), ()))
    else:
        tm, tk, tn = _tile(ca, 1024), _tile(k, 512), _tile(math.gcd(cb, co), 1024)
        a_spec = pl.BlockSpec((None, tk, tm), lambda i, j, kk: (i // (ca // tm), kk, i % (ca // tm)))
        b_spec = pl.BlockSpec((None, tk, tn), lambda i, j, kk: (j // (cb // tn), kk, j % (cb // tn)))
        dims = (((0,), (0,)), ((), ()))
    o_spec = pl.BlockSpec((None, tm, tn), lambda i, j, kk: (j // (co // tn), i, j % (co // tn)))
    nk = k // tk

    def body(a_ref, b_ref, o_ref, acc_ref):
        kk = pl.program_id(2)

        @pl.when(kk == 0)
        def _():
            acc_ref[...] = jnp.zeros(acc_ref.shape, F32)

        acc_ref[...] += lax.dot_general(a_ref[...].astype(BF16), b_ref[...].astype(BF16), dims,
                                        preferred_element_type=F32)

        @pl.when(kk == nk - 1)
        def _():
            o_ref[...] = acc_ref[...].astype(o_ref.dtype)

    return pl.pallas_call(
        body, name=name, grid=(m // tm, n // tn, nk), in_specs=[a_spec, b_spec], out_specs=o_spec,
        out_shape=jax.ShapeDtypeStruct((nbo, m, co), out_dtype), scratch_shapes=[pltpu.VMEM((tm, tn), F32)],
        compiler_params=_cparams(3))(a, b)


def _rms(x, g):
    return x * lax.rsqrt(jnp.mean(x * x, axis=-1, keepdims=True) + NORM_EPS) * g


def _sigmoid(x):
    return 1.0 / (1.0 + jnp.exp(-x))


def _silu(x):
    return x * _sigmoid(x)


def _gelu(x):
    return 0.5 * x * (1.0 + lax.erf(x * 0.7071067811865476))


def _softplus(x):
    return jnp.maximum(x, 0.0) + jnp.log(1.0 + jnp.exp(-jnp.abs(x)))


def _split(x, n):
    w = x.shape[-1] // n
    return [x[..., h * w:(h + 1) * w] for h in range(n)]


def _sgu(ua, va, smw, smb, lng, lnb):
    c = ua[0].shape[0]
    width = len(va) * va[0].shape[-1]
    vf = [_gelu(v) for v in va]
    mu = sum(jnp.sum(v, axis=-1, keepdims=True) for v in vf) / width
    var = sum(jnp.sum(jnp.square(v - mu), axis=-1, keepdims=True) for v in vf) / width
    r = lax.rsqrt(var + NORM_EPS)
    causal = lax.broadcasted_iota(jnp.int32, (c, c), 0) >= lax.broadcasted_iota(jnp.int32, (c, c), 1)
    outs = []
    for h in range(len(ua)):
        vn = (vf[h] - mu) * r * lng[h] + lnb[h]
        w = jnp.where(causal, smw[h], 0.0)
        mixed = jnp.dot(w.astype(BF16), vn.astype(BF16), preferred_element_type=F32) + smb[h]
        outs.append(_gelu(ua[h]) * mixed)
    return outs


def _shift_rows(x, k, up):
    rows = x.shape[0]
    row = lax.broadcasted_iota(jnp.int32, x.shape, 0)
    if up:
        return jnp.where(row < rows - k, pltpu.roll(x, rows - k, 0), 0.0)
    return jnp.where(row >= k, pltpu.roll(x, k, 0), 0.0)


@functools.partial(jax.custom_vjp, nondiff_argnums=(1,))
def _delay(x, k):
    return _shift_rows(x, k, False)


def _delay_fwd(x, k):
    return _shift_rows(x, k, False), None


def _delay_bwd(k, _, g):
    return (_shift_rows(g, k, True),)


_delay.defvjp(_delay_fwd, _delay_bwd)


def _conv_silu(x, w0, w1, w2, w3):
    y = w3 * x + w2 * _delay(x, 1) + w1 * _delay(x, 2) + w0 * _delay(x, 3)
    return _silu(y)


def _dn_step(s, q, k, v, braw, araw, alog, dtb):
    nh, c, d = q.shape
    row = lax.broadcasted_iota(jnp.int32, (c, c), 0)
    col = lax.broadcasted_iota(jnp.int32, (c, c), 1)
    causal, strict = (row >= col)[None], (row > col)[None]
    lower = (row >= col).astype(F32)
    strict_f = (row > col).astype(F32)
    eye = (row == col).astype(F32)[None]
    bmm = functools.partial(jnp.einsum, precision=HI, preferred_element_type=F32)

    qn = q * lax.rsqrt(jnp.sum(q * q, axis=-1, keepdims=True) + NORM_EPS) * (d ** -0.5)
    kn = k * lax.rsqrt(jnp.sum(k * k, axis=-1, keepdims=True) + NORM_EPS)
    beta = _sigmoid(braw)
    g = -jnp.exp(alog) * _softplus(araw + dtb)
    lg = lower[None] * g
    gcum = jnp.sum(lg, axis=-1, keepdims=True)
    diff = jnp.dot(lg.reshape(nh * c, c), strict_f, precision=HI,
                   preferred_element_type=F32).reshape(nh, c, c)
    decay = jnp.where(causal, jnp.exp(diff), 0.0)
    bcol = jnp.sum(eye * beta, axis=-1, keepdims=True)
    kb = kn * bcol
    a = jnp.where(strict, bmm("htd,hsd->hts", kb, kn) * decay, 0.0)
    inv = eye - a
    p = bmm("hts,hsr->htr", a, a)
    n_fac = int(math.log2(c)) - 1
    for it in range(n_fac):
        inv = inv + bmm("hts,hsr->htr", inv, p)
        if it < n_fac - 1:
            p = bmm("hts,hsr->htr", p, p)
    eg = jnp.exp(gcum)
    u = bmm("hts,hsd->htd", inv, v * bcol)
    w = bmm("hts,hsd->htd", inv, kb * eg)
    qk = jnp.where(causal, bmm("htd,hsd->hts", qn, kn) * decay, 0.0)
    v_new = u - bmm("hck,hkv->hcv", w, s)
    o = bmm("hck,hkv->hcv", qn * eg, s) + bmm("hts,hsv->htv", qk, v_new)
    glast = jnp.sum(g, axis=-1, keepdims=True)
    kdec = kn * jnp.exp(glast - gcum)
    s_new = s * jnp.exp(glast) + bmm("hck,hcv->hkv", kdec, v_new)
    return s_new, o


def _ogate(o, z, w):
    return [_rms(oh, w) * _silu(zh) for oh, zh in zip(o, z)]


def _xattn(q, k, v):
    outs = []
    for qh, kh, vh in zip(q, k, v):
        s = lax.dot_general(qh.astype(BF16), kh.astype(BF16), (((1,), (1,)), ((), ())),
                            preferred_element_type=F32) * (qh.shape[-1] ** -0.5)
        s = s - jnp.max(s, axis=-1, keepdims=True)
        e = jnp.exp(s)
        p = e / jnp.sum(e, axis=-1, keepdims=True)
        outs.append(jnp.dot(p.astype(BF16), vh.astype(BF16), preferred_element_type=F32))
    return outs


def _rows(t):
    return min(ROW_TILE, t)


def k_rms_fwd(name, x, g, after=()):
    t, d = x.shape
    tm = _rows(t)
    return stepk(name, lambda xv, gv: (_rms(xv, gv),), (t // tm,),
                 [(x, (tm, d), lambda i: (i, 0)), (g, (1, d), lambda i: (0, 0))],
                 [((t, d), BF16, (tm, d), lambda i: (i, 0))], after=after)[0]


def k_rms_bwd(name, x, g, dhs, dx_res):
    t, d = x.shape
    tm = _rows(t)
    n = len(dhs)

    def fn(xv, gv, *rest):
        dh = sum(r.astype(F32) for r in rest[:n])
        dxr, dg_c = rest[n], rest[n + 1]
        _, vjp = jax.vjp(_rms, xv, gv)
        dx, dg = vjp(dh)
        return dx + dxr, dg_c + dg

    row = lambda arr: (arr, (tm, d), lambda i: (i, 0))
    return stepk(name, fn, (t // tm,), [row(x), (g, (1, d), lambda i: (0, 0))] + [row(h) for h in dhs] + [row(dx_res)],
                 [((t, d), F32, (tm, d), lambda i: (i, 0))], carries=[((1, d), F32)])


def k_post_fwd(name, x, f, g, scale):
    t, d = x.shape
    tm = _rows(t)
    row = lambda arr: (arr, (tm, d), lambda i: (i, 0))
    return stepk(name, lambda xv, fv, gv: (xv + scale * _rms(fv, gv),), (t // tm,),
                 [row(x), row(f), (g, (1, d), lambda i: (0, 0))], [((t, d), F32, (tm, d), lambda i: (i, 0))])[0]


def k_post_bwd(name, f, g, dxo, scale, after=()):
    t, d = f.shape
    tm = _rows(t)

    def fn(fv, gv, dv, dg_c):
        _, vjp = jax.vjp(lambda a, b: scale * _rms(a, b), fv, gv)
        df, dg = vjp(dv)
        return df, dg_c + dg

    row = lambda arr: (arr, (tm, d), lambda i: (i, 0))
    return stepk(name, fn, (t // tm,), [row(f), (g, (1, d), lambda i: (0, 0)), row(dxo)],
                 [((t, d), BF16, (tm, d), lambda i: (i, 0))], carries=[((1, d), F32)], after=after)


def k_swiglu_fwd(name, hu):
    nb, t, c = hu.shape
    half = nb // 2
    tm = _rows(t)
    fn = lambda gv, uv: (_silu(gv.astype(F32)) * uv.astype(F32),)
    return stepk(name, fn, (half, t // tm),
                 [(hu, (None, tm, c), lambda b, i: (b, i, 0)), (hu, (None, tm, c), lambda b, i: (b + half, i, 0))],
                 [((half, t, c), BF16, (None, tm, c), lambda b, i: (b, i, 0))])[0]


def k_swiglu_bwd(name, hu, da):
    nb, t, c = hu.shape
    half = nb // 2
    tm = _rows(t)

    def body(g_ref, u_ref, d_ref, o_ref):
        gv, uv, dv = g_ref[...].astype(F32), u_ref[...].astype(F32), d_ref[...].astype(F32)
        sg = _sigmoid(gv)

        @pl.when(pl.program_id(0) == 0)
        def _():
            o_ref[...] = (dv * uv * sg * (1.0 + gv * (1.0 - sg))).astype(o_ref.dtype)

        @pl.when(pl.program_id(0) == 1)
        def _():
            o_ref[...] = (dv * gv * sg).astype(o_ref.dtype)

    blk = lambda off: pl.BlockSpec((None, tm, c), lambda p, b, i: (b + off, i, 0))
    return pl.pallas_call(
        body, name=name, grid=(2, half, t // tm), in_specs=[blk(0), blk(half), blk(0)],
        out_specs=pl.BlockSpec((None, tm, c), lambda p, b, i: (p * half + b, i, 0)),
        out_shape=jax.ShapeDtypeStruct((nb, t, c), BF16), compiler_params=_cparams(3))(hu, hu, da)


def k_loss(name, y, tgt):
    t, d = y.shape
    tm = _rows(t)

    def fn(yv, tv, acc):
        e = yv - tv
        part = jnp.sum(jnp.sum(e * e, axis=-1, keepdims=True), axis=0, keepdims=True)
        return e * (1.0 / d), acc + (0.5 / d) * part

    row = lambda arr: (arr, (tm, d), lambda i: (i, 0))
    return stepk(name, fn, (t // tm,), [row(y), row(tgt)], [((t, d), F32, (tm, d), lambda i: (i, 0))],
                 carries=[((1, 1), F32)])


GM_W = GM_HEADS * 128
QKV_COL0 = 2 * GM_W
QKV_W = 3 * DN_HEADS * DN_DIM
Z_COL0 = QKV_COL0 + QKV_W
MAIN_W = Z_COL0 + DN_HEADS * DN_DIM


def _sgu_ins(proj, p):
    c = GM_CHUNK
    return [(proj, (c, GM_W), lambda i: (i, 0)), (proj, (c, GM_W), lambda i: (i, 1)),
            (p["sm_w"], (GM_HEADS, c, c), lambda i: (0, 0, 0)), (p["sm_b"], (GM_HEADS, c, 1), lambda i: (0, 0, 0)),
            (p["sm_ln_g"], (1, GM_W), lambda i: (0, 0)), (p["sm_ln_b"], (1, GM_W), lambda i: (0, 0))]


def _sgu_lists(uv, vv, sw, sb, lg, lb):
    nh = GM_HEADS
    return (_split(uv, nh), _split(vv, nh), [sw[h] for h in range(nh)], [sb[h] for h in range(nh)],
            _split(lg, nh), _split(lb, nh))


def k_sgu_fwd(name, proj, p):
    t = proj.shape[0]

    def fn(*vals):
        return (jnp.concatenate(_sgu(*_sgu_lists(*vals)), axis=-1),)

    return stepk(name, fn, (t // GM_CHUNK,), _sgu_ins(proj, p),
                 [((t, GM_W), BF16, (GM_CHUNK, GM_W), lambda i: (i, 0))])[0]


def k_sgu_bwd(name, proj, p, dy):
    t = proj.shape[0]
    c = GM_CHUNK

    def fn(uv, vv, sw, sb, lg, lb, dv, dsw, dsb, dlg, dlb):
        _, vjp = jax.vjp(_sgu, *_sgu_lists(uv, vv, sw, sb, lg, lb))
        gu, gv, gsw, gsb, glg, glb = vjp(_split(dv.astype(F32), GM_HEADS))
        cat = lambda l: jnp.concatenate(l, axis=-1)
        return (cat(gu), cat(gv), dsw + jnp.stack(gsw), dsb + jnp.stack(gsb), dlg + cat(glg), dlb + cat(glb))

    return stepk(name, fn, (t // c,), _sgu_ins(proj, p) + [(dy, (None, c, GM_W), lambda i: (0, i, 0))],
                 [((t, GM_W), BF16, (c, GM_W), lambda i: (i, 0)), ((t, GM_W), BF16, (c, GM_W), lambda i: (i, 0))],
                 carries=[((GM_HEADS, c, c), F32), ((GM_HEADS, c, 1), F32), ((1, GM_W), F32), ((1, GM_W), F32)])


def _conv_ins(proj, conv_w):
    t = proj.shape[0]
    return [(proj, (t, 128), lambda j: (0, QKV_COL0 // 128 + j)), (conv_w, (CONV_W, 128), lambda j: (0, j))]


def k_conv_fwd(name, proj, conv_w):
    t = proj.shape[0]
    n = QKV_W // 128
    fn = lambda xv, wv: (_conv_silu(xv, *[wv[i:i + 1] for i in range(CONV_W)]),)
    return stepk(name, fn, (n,), _conv_ins(proj, conv_w), [((n, t, 128), F32, (None, t, 128), lambda j: (j, 0, 0))])[0]


def k_conv_bwd(name, proj, conv_w, dqkv):
    t = proj.shape[0]
    n = QKV_W // 128

    def fn(xv, wv, dv):
        _, vjp = jax.vjp(_conv_silu, xv, *[wv[i:i + 1] for i in range(CONV_W)])
        gx, *gw = vjp(dv)
        return gx, jnp.concatenate(gw, axis=0)

    return stepk(name, fn, (n,), _conv_ins(proj, conv_w) + [(dqkv, (None, t, 128), lambda j: (j, 0, 0))],
                 [((t, QKV_W), BF16, (t, 128), lambda j: (0, j)), ((CONV_W, QKV_W), F32, (CONV_W, 128), lambda j: (0, j))])


def _dn_ins(qkv, braw, araw, p, order):
    h, c, d = DN_HEADS, DN_CHUNK, DN_DIM
    qkv_in = lambda part: (qkv, (h, c, d), lambda n: (part, order(n), 0))
    gate_in = lambda arr: (arr, (None, h, 1, c), lambda n: (order(n), 0, 0, 0))
    par_in = lambda arr: (arr, (h, 1, 1), lambda n: (0, 0, 0))
    return [qkv_in(0), qkv_in(1), qkv_in(2), gate_in(braw), gate_in(araw), par_in(p["a_log"]), par_in(p["dt_bias"])]


def k_dn_fwd(name, qkv, braw, araw, p):
    t = qkv.shape[1]
    h, c, d = DN_HEADS, DN_CHUNK, DN_DIM
    nc = t // c

    def fn(q, k, v, b, a, al, dt, s):
        s_new, o = _dn_step(s, q, k, v, b, a, al, dt)
        return o, s, s_new

    o, s_all, _ = stepk(name, fn, (nc,), _dn_ins(qkv, braw, araw, p, lambda n: n),
                        [((h, t, d), F32, (h, c, d), lambda n: (0, n, 0)),
                         ((nc, h, d, d), F32, (None, h, d, d), lambda n: (n, 0, 0, 0))],
                        carries=[((h, d, d), F32)])
    return o, s_all


def k_dn_bwd(name, qkv, braw, araw, p, s_all, do):
    t = qkv.shape[1]
    h, c, d = DN_HEADS, DN_CHUNK, DN_DIM
    nc = t // c
    rev = lambda n: nc - 1 - n

    def fn(q, k, v, b, a, al, dt, s, dov, ds_c, dal_c, ddt_c):
        _, vjp = jax.vjp(_dn_step, s, q, k, v, b, a, al, dt)
        ds, dq, dk, dv, db, da, dal, ddt = vjp((ds_c, dov))
        return dq, dk, dv, db, da, ds, dal_c + dal, ddt_c + ddt

    ins = _dn_ins(qkv, braw, araw, p, rev) + [(s_all, (None, h, d, d), lambda n: (rev(n), 0, 0, 0)),
                                               (do, (h, c, d), lambda n: (0, rev(n), 0))]
    hd = ((h, t, d), F32, (h, c, d), lambda n: (0, rev(n), 0))
    gate = ((nc, h, 1, c), F32, (None, h, 1, c), lambda n: (rev(n), 0, 0, 0))
    dq, dk, dv, db, da, _, dal, ddt = stepk(name, fn, (nc,), ins, [hd, hd, hd, gate, gate],
                                            carries=[((h, d, d), F32), ((h, 1, 1), F32), ((h, 1, 1), F32)])
    return dq, dk, dv, db, da, dal, ddt


def _ogate_ins(o, proj, p):
    t = o.shape[1]
    tm = _rows(t)
    return tm, [(o, (DN_HEADS, tm, DN_DIM), lambda i: (0, i, 0)), (proj, (tm, GM_W), lambda i: (i, Z_COL0 // GM_W)),
                (p["dn_norm_w"], (1, DN_DIM), lambda i: (0, 0))]


def k_ogate_fwd(name, o, proj, p):
    t = o.shape[1]
    tm, ins = _ogate_ins(o, proj, p)

    def fn(ov, zv, wv):
        return (jnp.concatenate(_ogate([ov[h] for h in range(DN_HEADS)], _split(zv, DN_HEADS), wv), axis=-1),)

    return stepk(name, fn, (t // tm,), ins, [((t, GM_W), BF16, (tm, GM_W), lambda i: (i, 0))])[0]


def k_ogate_bwd(name, o, proj, p, dy):
    t = o.shape[1]
    tm, ins = _ogate_ins(o, proj, p)

    def fn(ov, zv, wv, dv, dw_c):
        _, vjp = jax.vjp(_ogate, [ov[h] for h in range(DN_HEADS)], _split(zv, DN_HEADS), wv)
        go, gz, gw = vjp(_split(dv.astype(F32), DN_HEADS))
        return jnp.stack(go), jnp.concatenate(gz, axis=-1), dw_c + gw

    return stepk(name, fn, (t // tm,), ins + [(dy, (None, tm, GM_W), lambda i: (1, i, 0))],
                 [((DN_HEADS, t, DN_DIM), F32, (DN_HEADS, tm, DN_DIM), lambda i: (0, i, 0)),
                  ((t, GM_W), BF16, (tm, GM_W), lambda i: (i, 0))], carries=[((1, DN_DIM), F32)])


def _xattn_lists(qv, kvv):
    nh = XA_HEADS
    return (_split(qv.astype(F32), nh), [kvv[h].astype(F32) for h in range(nh)],
            [kvv[nh + h].astype(F32) for h in range(nh)])


def k_xattn_fwd(name, q, kv):
    t, d = q.shape
    tm = _rows(t)
    fn = lambda qv, kvv: (jnp.concatenate(_xattn(*_xattn_lists(qv, kvv)), axis=-1),)
    return stepk(name, fn, (t // tm,), [(q, (tm, d), lambda i: (i, 0)), (kv, kv.shape, lambda i: (0, 0, 0))],
                 [((t, d), BF16, (tm, d), lambda i: (i, 0))])[0]


def k_xattn_bwd(name, q, kv, do):
    t, d = q.shape
    tm = _rows(t)

    def fn(qv, kvv, dv, dkv_c):
        _, vjp = jax.vjp(_xattn, *_xattn_lists(qv, kvv))
        gq, gk, gv = vjp(_split(dv.astype(F32), XA_HEADS))
        return jnp.concatenate(gq, axis=-1), dkv_c + jnp.stack(gk + gv)

    return stepk(name, fn, (t // tm,),
                 [(q, (tm, d), lambda i: (i, 0)), (kv, kv.shape, lambda i: (0, 0, 0)), (do, (None, tm, d), lambda i: (0, i, 0))],
                 [((t, d), BF16, (tm, d), lambda i: (i, 0))], carries=[(kv.shape, F32)])


def ffn_fwd(tag, x, p, pre, post, gu, dn, after=()):
    h = k_rms_fwd(f"{tag}_pre", x, p[pre], after)
    hu = mm(f"{tag}_gu", h[None], p[gu], "nn", BF16, nbo=N_DEV)
    a = k_swiglu_fwd(f"{tag}_act", hu)
    f = mm(f"{tag}_down", a, p[dn], "nn", F32)[0]
    y = k_post_fwd(f"{tag}_post", x, f, p[post], 0.5)
    return y, (x, h, hu, a, f)


def ffn_bwd(tag, dy, saved, p, pre, post, gu, dn, grads, after=()):
    x, h, hu, a, f = saved
    df, grads[post] = k_post_bwd(f"{tag}_post_b", f, p[post], dy, 0.5, after)
    da = mm(f"{tag}_down_dx", df[None], p[dn], "nt", BF16, nbo=N_DEV // 2)
    grads[dn] = mm(f"{tag}_down_dw", a, df[None], "tn", BF16)
    dhu = k_swiglu_bwd(f"{tag}_act_b", hu, da)
    dh = mm(f"{tag}_gu_dx", dhu, p[gu], "nt", BF16)[0]
    grads[gu] = mm(f"{tag}_gu_dw", h[None], dhu, "tn", BF16, nbo=N_DEV)
    dx, grads[pre] = k_rms_bwd(f"{tag}_pre_b", x, p[pre], [dh], dy)
    return dx


def _to_chunks(a):
    t, h = a.shape
    return a.reshape(t // DN_CHUNK, DN_CHUNK, h).transpose(0, 2, 1).reshape(t // DN_CHUNK, h, 1, DN_CHUNK)


def _from_chunks(a):
    nc, h, _, c = a.shape
    return a.reshape(nc, h, c).transpose(0, 2, 1).reshape(nc * c, h)


def mix_fwd(tag, x, p, after=()):
    h = k_rms_fwd(f"{tag}_pre", x, p["mix_norm_pre"], after)
    proj = mm(f"{tag}_in", h[None], p["w_in_main"], "nn", F32)[0]
    small = mm(f"{tag}_in_s", h[None], p["w_in_small"], "nn", F32)[0]
    braw, araw = _to_chunks(small[:, :DN_HEADS]), _to_chunks(small[:, DN_HEADS:2 * DN_HEADS])
    ya = k_sgu_fwd(f"{tag}_sgu", proj, p)
    qkv = k_conv_fwd(f"{tag}_conv", proj, p["conv_w"])
    o, s_all = k_dn_fwd(f"{tag}_dn", qkv, braw, araw, p)
    yb = k_ogate_fwd(f"{tag}_og", o, proj, p)
    y = jnp.stack([ya, yb])
    m = mm(f"{tag}_out", y, p["w_out"], "nn", F32)[0]
    out = k_post_fwd(f"{tag}_post", x, m, p["mix_norm_post"], 1.0)
    return out, (x, h, proj, braw, araw, qkv, o, s_all, y, m)


def mix_bwd(tag, dy, saved, p, grads, after=()):
    x, h, proj, braw, araw, qkv, o, s_all, y, m = saved
    t = x.shape[0]
    dm, grads["mix_norm_post"] = k_post_bwd(f"{tag}_post_b", m, p["mix_norm_post"], dy, 1.0, after)
    dyy = mm(f"{tag}_out_dx", dm[None], p["w_out"], "nt", BF16, nbo=2)
    grads["w_out"] = mm(f"{tag}_out_dw", y, dm[None], "tn", BF16)
    do, dz, grads["dn_norm_w"] = k_ogate_bwd(f"{tag}_og_b", o, proj, p, dyy)
    dq, dk, dv, db, da, grads["a_log"], grads["dt_bias"] = k_dn_bwd(f"{tag}_dn_b", qkv, braw, araw, p, s_all, do)
    dconv, grads["conv_w"] = k_conv_bwd(f"{tag}_conv_b", proj, p["conv_w"], jnp.concatenate([dq, dk, dv], axis=0))
    du, dva, grads["sm_w"], grads["sm_b"], grads["sm_ln_g"], grads["sm_ln_b"] = k_sgu_bwd(f"{tag}_sgu_b", proj, p, dyy)
    dproj = jnp.concatenate([du, dva, dconv, dz], axis=-1)
    dsmall = jnp.concatenate([_from_chunks(db), _from_chunks(da), jnp.zeros((t, 128 - 2 * DN_HEADS), F32)], axis=-1)
    dh = mm(f"{tag}_in_dx", dproj[None], p["w_in_main"], "nt", BF16)[0]
    dh_s = mm(f"{tag}_in_s_dx", dsmall[None], p["w_in_small"], "nt", BF16)[0]
    grads["w_in_main"] = mm(f"{tag}_in_dw", h[None], dproj[None], "tn", BF16)
    grads["w_in_small"] = mm(f"{tag}_in_s_dw", h[None], dsmall[None], "tn", F32)
    dx, grads["mix_norm_pre"] = k_rms_bwd(f"{tag}_pre_b", x, p["mix_norm_pre"], [dh, dh_s], dy)
    return dx


def xa_fwd(tag, x, mem, p, after=()):
    hx = k_rms_fwd(f"{tag}_pre", x, p["xa_norm_pre"], after)
    mh = k_rms_fwd(f"{tag}_mem", mem, p["mem_norm"], after)
    q = mm(f"{tag}_q", hx[None], p["w_xq"], "nn", BF16)[0]
    kv = mm(f"{tag}_kv", mh[None], p["w_xkv"], "nn", BF16, nbo=N_DEV)
    o = k_xattn_fwd(f"{tag}_att", q, kv)
    c = mm(f"{tag}_o", o[None], p["w_xo"], "nn", F32)[0]
    out = k_post_fwd(f"{tag}_post", x, c, p["xa_norm_post"], 1.0)
    return out, (x, mem, hx, mh, q, kv, o, c)


def xa_bwd(tag, dy, saved, p, grads, after=()):
    x, mem, hx, mh, q, kv, o, c = saved
    dc, grads["xa_norm_post"] = k_post_bwd(f"{tag}_post_b", c, p["xa_norm_post"], dy, 1.0, after)
    do = mm(f"{tag}_o_dx", dc[None], p["w_xo"], "nt", BF16)
    grads["w_xo"] = mm(f"{tag}_o_dw", o[None], dc[None], "tn", BF16)
    dq, dkv = k_xattn_bwd(f"{tag}_att_b", q, kv, do)
    dhx = mm(f"{tag}_q_dx", dq[None], p["w_xq"], "nt", BF16)[0]
    grads["w_xq"] = mm(f"{tag}_q_dw", hx[None], dq[None], "tn", BF16)
    dkv16 = dkv.astype(BF16)
    dmh = mm(f"{tag}_kv_dx", dkv16, p["w_xkv"], "nt", BF16)[0]
    grads["w_xkv"] = mm(f"{tag}_kv_dw", mh[None], dkv16, "tn", BF16, nbo=N_DEV)
    _, grads["mem_norm"] = k_rms_bwd(f"{tag}_mem_b", mem, p["mem_norm"], [dmh], jnp.zeros_like(mem))
    dx, grads["xa_norm_pre"] = k_rms_bwd(f"{tag}_pre_b", x, p["xa_norm_pre"], [dhx], dy)
    return dx


def _place():
    return lax.axis_index("x"), lax.axis_index("y"), lax.axis_index("c")


def _other_chips(x, y):
    return [(1 - x, y), (x, 1 - y), (1 - x, 1 - y)]


_ANY = pl.BlockSpec(memory_space=pl.ANY)


def all_gather(name, shards):
    n = len(shards)

    def body(*refs):
        ins, outs = refs[:n], refs[n:2 * n]
        send_sems, recv_sems, local_sems = refs[2 * n:]
        x, y, c = _place()
        me, sibling = (x, y, c), (x, y, 1 - c)
        chips = _other_chips(x, y)
        idx = lambda px, py, pc: 4 * px + 2 * py + pc

        def copy(a, k, block, to, src=None):
            dst = outs[a].at[idx(*block)]
            return pltpu.make_async_remote_copy(src_ref=dst if src is None else src, dst_ref=dst,
                                                send_sem=send_sems.at[7 * a + k], recv_sem=recv_sems.at[7 * a + k],
                                                device_id=to, device_id_type=MESH)

        mine = [pltpu.make_async_copy(ins[a], outs[a].at[idx(*me)], local_sems.at[a]) for a in range(n)]
        for cp in mine:
            cp.start()
        first = []
        for a in range(n):
            first.append(copy(a, 0, me, sibling, src=ins[a]))
            first += [copy(a, 1 + j, me, (*chip, c), src=ins[a]) for j, chip in enumerate(chips)]
        for cp in first:
            cp.start()
        passed = []
        for a in range(n):
            for j, chip in enumerate(chips):
                copy(a, 1 + j, (*chip, c), me).wait_recv()
                passed.append(copy(a, 4 + j, (*chip, c), sibling))
                passed[-1].start()
        for a in range(n):
            copy(a, 0, sibling, me).wait_recv()
            for j, chip in enumerate(chips):
                copy(a, 4 + j, (*chip, 1 - c), me).wait_recv()
        for cp in first + passed:
            cp.wait_send()
        for cp in mine:
            cp.wait()

    return pl.pallas_call(
        body, name=name, in_specs=[_ANY] * n, out_specs=[_ANY] * n,
        out_shape=[jax.ShapeDtypeStruct((N_DEV,) + s.shape, s.dtype) for s in shards],
        scratch_shapes=[pltpu.SemaphoreType.DMA((7 * n,)), pltpu.SemaphoreType.DMA((7 * n,)),
                        pltpu.SemaphoreType.DMA((n,))])(*shards)


def place_own(name, shards):
    n = len(shards)

    def body(*refs):
        ins, outs, sems = refs[:n], refs[n:2 * n], refs[2 * n]
        x, y, c = _place()
        copies = [pltpu.make_async_copy(ins[a], outs[a].at[4 * x + 2 * y + c], sems.at[a]) for a in range(n)]
        for cp in copies:
            cp.start()
        for cp in copies:
            cp.wait()

    return pl.pallas_call(
        body, name=name, in_specs=[_ANY] * n, out_specs=[_ANY] * n,
        out_shape=[jax.ShapeDtypeStruct((N_DEV,) + s.shape, s.dtype) for s in shards],
        scratch_shapes=[pltpu.SemaphoreType.DMA((n,))])(*shards)


_HBM = pl.BlockSpec(memory_space=pltpu.HBM)
_SEM = pl.BlockSpec(memory_space=pltpu.SEMAPHORE)
_DATAFLOW = pltpu.SideEffectType.DATAFLOW_SIDE_EFFECTING


def xfer_start(name, plan, n_sems, srcs, lands, after=()):
    ns, nl, na = len(srcs), len(lands), len(after)
    bufs = list(srcs) + list(lands)

    def body(*refs):
        send_sems, recv_sems, token = refs[ns + nl + na], refs[ns + nl + na + 1], refs[-1]
        for cp in plan(refs[:ns], refs[ns:ns + nl], send_sems, recv_sems):
            cp.start()
        token[...] = jnp.zeros(token.shape, token.dtype)

    res = pl.pallas_call(
        body, name=name, in_specs=[_HBM] * (ns + nl) + [_ANY] * na,
        out_specs=(_SEM, _SEM, *[_HBM] * (ns + nl), pl.BlockSpec(memory_space=pltpu.VMEM)),
        out_shape=(pltpu.SemaphoreType.DMA((n_sems,)), pltpu.SemaphoreType.DMA((n_sems,)),
                   *[pltpu.HBM(b.shape, b.dtype) for b in bufs], jax.ShapeDtypeStruct((8, 128), F32)),
        input_output_aliases={i: 2 + i for i in range(ns + nl)},
        compiler_params=pltpu.CompilerParams(has_side_effects=_DATAFLOW),
    )(*[pltpu.with_memory_space_constraint(b, pltpu.HBM) for b in bufs], *after)
    return dict(send=res[0], recv=res[1], srcs=list(res[2:2 + ns]), lands=list(res[2 + ns:2 + ns + nl]), token=res[-1])


def xfer_wait(name, plan, started, after=()):
    ns, nl = len(started["srcs"]), len(started["lands"])
    bufs = started["srcs"] + started["lands"]

    def body(*refs):
        for cp in plan(refs[:ns], refs[ns:ns + nl], refs[ns + nl], refs[ns + nl + 1]):
            cp.wait_send()
            cp.wait_recv()

    res = pl.pallas_call(
        body, name=name, in_specs=[_HBM] * (ns + nl) + [_SEM, _SEM] + [_ANY] * len(after),
        out_specs=tuple([_HBM] * (ns + nl)), out_shape=tuple(pltpu.HBM(b.shape, b.dtype) for b in bufs),
        input_output_aliases={i: i for i in range(ns + nl)},
        compiler_params=pltpu.CompilerParams(has_side_effects=_DATAFLOW),
    )(*bufs, started["send"], started["recv"], *after)
    return list(res[:ns]), list(res[ns:])


def _remote(src, dst, send_sems, recv_sems, k, to):
    return pltpu.make_async_remote_copy(src_ref=src, dst_ref=dst, send_sem=send_sems.at[k], recv_sem=recv_sems.at[k],
                                        device_id=to, device_id_type=MESH)


def plan_gather_out(srcs, lands, send_sems, recv_sems):
    x, y, c = _place()
    targets = [(x, y, 1 - c)] + [(px, py, c) for px, py in _other_chips(x, y)]
    return [_remote(srcs[a], lands[a].at[4 * x + 2 * y + c], send_sems, recv_sems, 4 * a + k, to)
            for a in range(len(srcs)) for k, to in enumerate(targets)]


def plan_gather_pass(srcs, lands, send_sems, recv_sems):
    x, y, c = _place()
    return [_remote(lands[a].at[4 * px + 2 * py + c], lands[a].at[4 * px + 2 * py + c], send_sems, recv_sems,
                    3 * a + j, (x, y, 1 - c))
            for a in range(len(lands)) for j, (px, py) in enumerate(_other_chips(x, y))]


def plan_scatter_sibling(srcs, lands, send_sems, recv_sems):
    x, y, c = _place()
    return [_remote(srcs[a].at[j, 1 - c], lands[a].at[j], send_sems, recv_sems, 4 * a + j, (x, y, 1 - c))
            for a in range(len(srcs)) for j in range(4)]


def plan_scatter_chips(srcs, lands, send_sems, recv_sems):
    x, y, c = _place()
    return [_remote(srcs[a].at[2 * px + py], lands[a].at[j], send_sems, recv_sems, 3 * a + j, (px, py, c))
            for a in range(len(srcs)) for j, (px, py) in enumerate(_other_chips(x, y))]


ELEMWISE_BLOCK = 256 * 1024


def _row_tile(r, cc):
    t = r
    while t * cc > ELEMWISE_BLOCK and t % 32 == 0:
        t //= 2
    if t * cc > ELEMWISE_BLOCK:
        for cand in range(t, 15, -16):
            if r % cand == 0 and cand * cc <= ELEMWISE_BLOCK:
                return cand
    return t


def k_pair_add(name, g4, r1, core):
    _, _, r, cc = g4.shape
    tr = _row_tile(r, cc)
    fn = lambda av, bv: (av.astype(F32) + bv.astype(F32),)
    return stepk(name, fn, (4, r // tr),
                 [(g4, (None, None, tr, cc), lambda j, i, pre: (j, pre[0], i, 0)),
                  (r1, (None, tr, cc), lambda j, i, pre: (j, i, 0))],
                 [((4, r, cc), BF16, (None, tr, cc), lambda j, i, pre: (j, i, 0))], prefetch=core)[0]


def _adamw(g, w, m, v):
    m2 = ADAM_B1 * m + (1.0 - ADAM_B1) * g
    v2 = ADAM_B2 * v + (1.0 - ADAM_B2) * jnp.square(g)
    m_hat = m2 / (1.0 - ADAM_B1 ** ADAM_STEP)
    v_hat = v2 / (1.0 - ADAM_B2 ** ADAM_STEP)
    delta = -ADAM_LR * (m_hat / (jnp.sqrt(v_hat) + ADAM_EPS) + ADAM_WD * w)
    return g, delta, m2, v2


def k_adamw_shard(name, part, others, w, m, v, layer, chip, fill):
    _, r, cc = part.shape
    tr = _row_tile(r, cc)

    def fn(pv, o0, o1, o2, wv, mv, vv):
        g = ((pv.astype(F32) + o0.astype(F32)) + o1.astype(F32)) + o2.astype(F32)
        return _adamw(g, wv, mv, vv)

    other = lambda k: (others, (None, tr, cc), lambda i, pre: (k, i, 0))
    state = lambda arr: (arr, (None, tr, cc), lambda i, pre: (layer, i, 0))
    out = ((2, r, cc), F32, (None, tr, cc), lambda i, pre: (layer, i, 0))
    return stepk(name, fn, (r // tr,),
                 [(part, (None, tr, cc), lambda i, pre: (pre[0], i, 0)), other(0), other(1), other(2),
                  state(w), state(m), state(v)],
                 [out] * 4, prefetch=chip, fill=[(f, k) for k, f in enumerate(fill)])


def k_sum8(name, parts):
    _, rows, lanes = parts.shape

    def fn(pv):
        acc = pv[0]
        for d in range(1, N_DEV):
            acc = acc + pv[d]
        return (acc,)

    return stepk(name, fn, (1,), [(parts, parts.shape, lambda i: (0, 0, 0))],
                 [((rows, lanes), F32, (rows, lanes), lambda i: (0, 0))])[0]


def k_adamw_flat(name, g, w, m, v):
    whole = lambda arr: (arr, arr.shape, lambda i: (0, 0))
    out = (g.shape, F32, g.shape, lambda i: (0, 0))
    return stepk(name, _adamw, (1,), [whole(g), whole(w), whole(m), whole(v)], [out] * 4)


WEIGHTS = ("ffn1_norm_pre", "ffn1_w_gate_up", "ffn1_w_down", "ffn1_norm_post", "mix_norm_pre", "w_in", "conv_w", "a_log",
           "dt_bias", "sm_w", "sm_b", "sm_ln_g", "sm_ln_b", "dn_norm_w", "w_out", "mix_norm_post", "xa_norm_pre", "mem_norm",
           "w_xq", "w_xkv", "w_xo", "xa_norm_post", "ffn2_norm_pre", "ffn2_w_gate_up", "ffn2_w_down", "ffn2_norm_post")
BIG = ("ffn1_w_gate_up", "ffn1_w_down", "w_in", "w_out", "w_xq", "w_xkv", "w_xo", "ffn2_w_gate_up", "ffn2_w_down")
ROW_SHARDED = ("ffn1_w_down", "w_out", "w_xq", "w_xo", "ffn2_w_down")
SMALL = tuple(n for n in WEIGHTS if n not in BIG and n != "conv_w")
N_LAYERS = 2
IN_COLS = MAIN_W + 2 * DN_HEADS


BLOCKS = ("f1", "mx", "xa", "f2")
AG_BEFORE = {"f1": ("ffn1_w_gate_up", "ffn1_w_down"), "mx": ("w_in", "w_out"),
             "xa": ("w_xq", "w_xkv", "w_xo", "ffn2_w_gate_up", "ffn2_w_down")}
RS_AFTER = {"f2": ("ffn2_w_gate_up", "ffn2_w_down"), "xa": ("w_xq", "w_xkv", "w_xo"), "mx": ("w_in", "w_out"),
            "f1": ("ffn1_w_gate_up", "ffn1_w_down")}
RS_IN_FLIGHT = 2


def _fwd_block(l, sb, h, mem, p, after):
    tag = f"l{l}{sb}"
    if sb == "f1":
        return ffn_fwd(tag, h, p, "ffn1_norm_pre", "ffn1_norm_post", "ffn1_w_gate_up", "ffn1_w_down", after)
    if sb == "mx":
        return mix_fwd(tag, h, p, after)
    if sb == "xa":
        return xa_fwd(tag, h, mem, p, after)
    return ffn_fwd(tag, h, p, "ffn2_norm_pre", "ffn2_norm_post", "ffn2_w_gate_up", "ffn2_w_down", after)


def _bwd_block(l, sb, dy, saved, p, grads, after):
    tag = f"l{l}{sb}"
    if sb == "f1":
        return ffn_bwd(tag, dy, saved, p, "ffn1_norm_pre", "ffn1_norm_post", "ffn1_w_gate_up", "ffn1_w_down", grads, after)
    if sb == "mx":
        return mix_bwd(tag, dy, saved, p, grads, after)
    if sb == "xa":
        return xa_bwd(tag, dy, saved, p, grads, after)
    return ffn_bwd(tag, dy, saved, p, "ffn2_norm_pre", "ffn2_norm_post", "ffn2_w_gate_up", "ffn2_w_down", grads, after)


def _big_params(full):
    p = {}
    for n, w in full.items():
        if n in ROW_SHARDED:
            p[n] = w.reshape(1, w.shape[0] * w.shape[1], w.shape[2])
        elif n == "w_in":
            w = w.transpose(1, 0, 2).reshape(w.shape[1], IN_COLS)
            p["w_in_main"] = w[None, :, :MAIN_W]
            p["w_in_small"] = jnp.pad(w[:, MAIN_W:], ((0, 0), (0, 128 - 2 * DN_HEADS)))[None]
        else:
            p[n] = w
    return p


def _small_params(l, conv_full, a):
    p = {"conv_w": conv_full[l]}
    for n in SMALL:
        w = a[n][l]
        if n in ("a_log", "dt_bias"):
            p[n] = w.reshape(DN_HEADS, 1, 1)
        elif n == "sm_b":
            p[n] = w[..., None]
        elif n == "sm_w":
            p[n] = w
        else:
            p[n] = w[None]
    return p


def _scatter_layout(n, g):
    if n in ROW_SHARDED:
        return g.reshape(N_DEV, g.shape[1] // N_DEV, g.shape[2])
    return g


def _pack(flat_parts):
    flat = jnp.concatenate(flat_parts)
    rows = -(-flat.shape[0] // 1024) * 8
    return jnp.pad(flat, (0, rows * 128 - flat.shape[0])).reshape(rows, 128)


def kernel(x, mem, ffn1_norm_pre, ffn1_w_gate_up, ffn1_w_down, ffn1_norm_post, mix_norm_pre, w_in, conv_w, a_log, dt_bias, sm_w, sm_b, sm_ln_g, sm_ln_b, dn_norm_w, w_out, mix_norm_post, xa_norm_pre, mem_norm, w_xq, w_xkv, w_xo, xa_norm_post, ffn2_norm_pre, ffn2_w_gate_up, ffn2_w_down, ffn2_norm_post, loss_target, m_ffn1_norm_pre, m_ffn1_w_gate_up, m_ffn1_w_down, m_ffn1_norm_post, m_mix_norm_pre, m_w_in, m_conv_w, m_a_log, m_dt_bias, m_sm_w, m_sm_b, m_sm_ln_g, m_sm_ln_b, m_dn_norm_w, m_w_out, m_mix_norm_post, m_xa_norm_pre, m_mem_norm, m_w_xq, m_w_xkv, m_w_xo, m_xa_norm_post, m_ffn2_norm_pre, m_ffn2_w_gate_up, m_ffn2_w_down, m_ffn2_norm_post, v_ffn1_norm_pre, v_ffn1_w_gate_up, v_ffn1_w_down, v_ffn1_norm_post, v_mix_norm_pre, v_w_in, v_conv_w, v_a_log, v_dt_bias, v_sm_w, v_sm_b, v_sm_ln_g, v_sm_ln_b, v_dn_norm_w, v_w_out, v_mix_norm_post, v_xa_norm_pre, v_mem_norm, v_w_xq, v_w_xkv, v_w_xo, v_xa_norm_post, v_ffn2_norm_pre, v_ffn2_w_gate_up, v_ffn2_w_down, v_ffn2_norm_post):
    a = dict(locals())
    px, py, pc = _place()
    core = jnp.reshape(pc, (1,)).astype(jnp.int32)
    chip = jnp.reshape(2 * px + py, (1,)).astype(jnp.int32)
    me = 4 * px + 2 * py + pc
    xs, mems, tgt = x[0], mem[0], loss_target[0]
    chain = []

    conv_full = all_gather("ag_conv", [conv_w])[0]
    conv_full = conv_full.transpose(1, 2, 0, 3).reshape(N_LAYERS, CONV_W, QKV_W)
    params = [_small_params(l, conv_full, a) for l in range(N_LAYERS)]

    keys = [(l, n) for l in range(N_LAYERS) for n in BIG]
    shards = {k: a[k[1]][k[0]].astype(BF16) for k in keys}
    lands = dict(zip(keys, place_own("place", [shards[k] for k in keys])))
    going = {}
    for l in range(N_LAYERS):
        for sb, names in AG_BEFORE.items():
            going[l, sb] = xfer_start(f"ag_out_l{l}{sb}", plan_gather_out, 4 * len(names), [shards[l, n] for n in names],
                                      [lands[l, n] for n in names], after=chain)
            chain = [going[l, sb]["token"]]

    saved = [{} for _ in range(N_LAYERS)]
    h = xs
    for l in range(N_LAYERS):
        for sb in BLOCKS:
            if sb in AG_BEFORE:
                names = AG_BEFORE[sb]
                _, got = xfer_wait(f"ag_outw_l{l}{sb}", plan_gather_out, going[l, sb], after=chain)
                passing = xfer_start(f"ag_pass_l{l}{sb}", plan_gather_pass, 3 * len(names), [], got)
                _, full = xfer_wait(f"ag_passw_l{l}{sb}", plan_gather_pass, passing)
                params[l].update(_big_params(dict(zip(names, full))))
                chain = [full[0]]
            h, saved[l][sb] = _fwd_block(l, sb, h, mems, params[l], chain)
            chain = [h]
    dy, loss = k_loss("loss", h, tgt)
    loss = lax.psum(loss[0, 0], ("x", "y", "c"))

    grads = [{} for _ in range(N_LAYERS)]
    big_out = {}
    pending = []

    def advance(chain, drain):
        for grp in list(pending):
            l, sb, names = grp["l"], grp["sb"], grp["names"]
            if grp["stage"] == "sibling":
                g4, r1 = xfer_wait(f"rs_sibw_l{l}{sb}", plan_scatter_sibling, grp["going"], after=chain)
                parts = [k_pair_add(f"rs_add_l{l}{n}", gg, rr, core) for n, gg, rr in zip(names, g4, r1)]
                zones = [lax.empty((3,) + p.shape[1:], BF16) for p in parts]
                grp["going"] = xfer_start(f"rs_chip_l{l}{sb}", plan_scatter_chips, 3 * len(names), parts, zones)
                grp["stage"], grp["age"] = "chips", 0
                chain = [grp["going"]["token"]]
            elif drain or grp["age"] >= RS_IN_FLIGHT:
                parts, r2 = xfer_wait(f"rs_chipw_l{l}{sb}", plan_scatter_chips, grp["going"], after=chain)
                for n, part, others in zip(names, parts, r2):
                    big_out[n] = k_adamw_shard(f"adamw_{n}_l{l}", part, others, a[n], a["m_" + n], a["v_" + n], l, chip,
                                               big_out.get(n, ()))
                    chain = [big_out[n][0]]
                pending.remove(grp)
            else:
                grp["age"] += 1
        return chain

    for l in reversed(range(N_LAYERS)):
        for sb in reversed(BLOCKS):
            g = grads[l]
            dy = _bwd_block(l, sb, dy, saved[l][sb], params[l], g, chain)
            chain = advance([dy], drain=False)
            names = RS_AFTER[sb]
            if sb == "mx":
                w_in_g = jnp.concatenate([g["w_in_main"][0], g["w_in_small"][0][:, :2 * DN_HEADS].astype(BF16)], axis=-1)
                g["w_in"] = w_in_g.reshape(w_in_g.shape[0], N_DEV, IN_COLS // N_DEV).transpose(1, 0, 2)
            g4 = [_scatter_layout(n, g[n]) for n in names]
            g4 = [t.reshape(4, 2, *t.shape[1:]) for t in g4]
            zones = [lax.empty((4,) + t.shape[2:], BF16) for t in g4]
            going_sib = xfer_start(f"rs_sib_l{l}{sb}", plan_scatter_sibling, 4 * len(names), g4, zones, after=chain)
            chain = [going_sib["token"]]
            pending.append(dict(l=l, sb=sb, names=names, stage="sibling", going=going_sib, age=0))
    while pending:
        chain = advance(chain, drain=True)

    flat = [jnp.concatenate([grads[l][n].reshape(-1) for l in range(N_LAYERS)]) for n in SMALL]
    flat += [grads[l]["conv_w"].reshape(-1) for l in range(N_LAYERS)]
    gsum = k_sum8("small_sum", all_gather("ag_small", [_pack(flat)])[0]).reshape(-1)
    sizes = [a[n].size for n in SMALL]
    n_rep = sum(sizes)
    conv_g = gsum[n_rep:n_rep + N_LAYERS * CONV_W * QKV_W].reshape(N_LAYERS, CONV_W, QKV_W)
    conv_g = lax.dynamic_slice_in_dim(conv_g, me * (QKV_W // N_DEV), QKV_W // N_DEV, axis=2)
    pack_state = lambda pre: _pack([a[pre + n].reshape(-1) for n in SMALL] + [a[pre + "conv_w"].reshape(-1)])
    small_out = k_adamw_flat("small_adamw", _pack([gsum[:n_rep], conv_g.reshape(-1)]), pack_state(""), pack_state("m_"),
                             pack_state("v_"))
    small_out = [o.reshape(-1) for o in small_out]
    outs = {}
    off = 0
    for n, sz in zip(SMALL + ("conv_w",), sizes + [conv_w.size]):
        outs[n] = [o[off:off + sz].reshape(a[n].shape) for o in small_out]
        off += sz
    for n in BIG:
        outs[n] = list(big_out[n])

    return (loss, dy[None], *[outs[n][0] for n in WEIGHTS], *[outs[n][1] for n in WEIGHTS],
            *[outs[n][2] for n in WEIGHTS], *[outs[n][3] for n in WEIGHTS])
```

```python
import functools
import math

import jax
import jax.numpy as jnp
from jax import lax
from jax.experimental import pallas as pl
from jax.experimental.pallas import tpu as pltpu

F32, BF16 = jnp.float32, jnp.bfloat16
HI = lax.Precision.HIGHEST
MESH = pl.DeviceIdType.MESH

N_DEV = 8
NORM_EPS = 1e-6
GM_HEADS, GM_CHUNK = 8, 128
DN_HEADS, DN_CHUNK, DN_DIM = 8, 64, 128
XA_HEADS, XA_DIM = 4, 512
CONV_W = 4
ADAM_LR, ADAM_B1, ADAM_B2, ADAM_EPS, ADAM_WD, ADAM_STEP = 0.001, 0.9, 0.999, 1e-08, 0.01, 10

V7X_VMEM_LIMIT = 56 * 1024 * 1024
WHOLE_TILE_MAX = 1536
ROW_TILE = 256


def _cparams(n_grid):
    return pltpu.CompilerParams(dimension_semantics=("arbitrary",) * n_grid, vmem_limit_bytes=V7X_VMEM_LIMIT)


def stepk(name, fn, grid, ins, outs, carries=(), prefetch=None, fill=(), after=()):
    n_in, n_out, n_c, n_fill = len(ins), len(outs), len(carries), len(fill)
    n_pre = 0 if prefetch is None else 1

    def body(*refs):
        refs = refs[n_pre:]
        in_refs, refs = refs[:n_in], refs[n_in + n_fill + len(after):]
        out_refs, c_refs = refs[:n_out], refs[n_out:]
        if n_c:
            first = functools.reduce(jnp.logical_and, [pl.program_id(a) == 0 for a in range(len(grid))])

            @pl.when(first)
            def _():
                for r in c_refs:
                    r[...] = jnp.zeros(r.shape, r.dtype)
        res = fn(*[r[...] for r in in_refs], *[r[...] for r in c_refs])
        for r, v in zip(tuple(out_refs) + tuple(c_refs), res):
            r[...] = v.astype(r.dtype)

    in_specs = [pl.BlockSpec(bs, im) for _, bs, im in ins] + [pl.BlockSpec(memory_space=pl.ANY)] * (n_fill + len(after))
    aliases = {n_pre + n_in + i: k for i, (_, k) in enumerate(fill)}
    out_specs = [pl.BlockSpec(bs, im) for _, _, bs, im in outs]
    out_shape = [jax.ShapeDtypeStruct(s, d) for s, d, _, _ in outs]
    for s, d in carries:
        zeros = (0,) * len(s)
        out_specs.append(pl.BlockSpec(s, lambda *a, _z=zeros: _z))
        out_shape.append(jax.ShapeDtypeStruct(s, d))
    args = [a for a, _, _ in ins] + [a for a, _ in fill] + list(after)
    if prefetch is None:
        call = pl.pallas_call(body, name=name, grid=grid, in_specs=in_specs, out_specs=out_specs, out_shape=out_shape,
                              input_output_aliases=aliases, compiler_params=_cparams(len(grid)))
        return call(*args)
    spec = pltpu.PrefetchScalarGridSpec(num_scalar_prefetch=1, grid=grid, in_specs=in_specs, out_specs=out_specs)
    call = pl.pallas_call(body, name=name, grid_spec=spec, out_shape=out_shape, input_output_aliases=aliases,
                          compiler_params=_cparams(len(grid)))
    return call(prefetch, *args)


def _tile(n, pref, align=128):
    if n <= WHOLE_TILE_MAX:
        return n
    t = (pref // align) * align
    while t > align and n % t:
        t -= align
    assert n % t == 0, (n, pref)
    return t


def mm(name, a, b, mode, out_dtype, nbo=1, after=()):
    nba, ra, ca = a.shape
    nbb, rb, cb = b.shape
    if mode == "nn":
        m, k, n = ra, nba * ca, nbb * cb
        assert rb == k
    elif mode == "nt":
        m, k, n = ra, nba * ca, rb
        assert nbb * cb == k
    else:
        k, m, n = ra, nba * ca, nbb * cb
        assert rb == k
    co = n // nbo
    if mode == "nn":
        tm, tk, tn = _tile(m, 1024, 8), _tile(ca, 512), _tile(math.gcd(cb, co), 1024)
        a_spec = pl.BlockSpec((None, tm, tk), lambda i, j, kk: (kk // (ca // tk), i, kk % (ca // tk)))
        b_spec = pl.BlockSpec((None, tk, tn), lambda i, j, kk: (j // (cb // tn), kk, j % (cb // tn)))
        dims = (((1,), (0,)), ((), ()))
    elif mode == "nt":
        tm, tk, tn = _tile(m, 1024, 8), _tile(math.gcd(ca, cb), 512), _tile(co, 1024)
        a_spec = pl.BlockSpec((None, tm, tk), lambda i, j, kk: (kk // (ca // tk), i, kk % (ca // tk)))
        b_spec = pl.BlockSpec((None, tn, tk), lambda i, j, kk: (kk // (cb // tk), j, kk % (cb // tk)))
        dims = (((1,), (1,)), ((), ()))
    else:
        tm, tk, tn = _tile(ca, 1024), _tile(k, 512), _tile(math.gcd(cb, co), 1024)
        a_spec = pl.BlockSpec((None, tk, tm), lambda i, j, kk: (i // (ca // tm), kk, i % (ca // tm)))
        b_spec = pl.BlockSpec((None, tk, tn), lambda i, j, kk: (j // (cb // tn), kk, j % (cb // tn)))
        dims = (((0,), (0,)), ((), ()))
    o_spec = pl.BlockSpec((None, tm, tn), lambda i, j, kk: (j // (co // tn), i, j % (co // tn)))
    nk = k // tk

    def body(a_ref, b_ref, *rest):
        o_ref, acc_ref = rest[-2:]
        kk = pl.program_id(2)

        @pl.when(kk == 0)
        def _():
            acc_ref[...] = jnp.zeros(acc_ref.shape, F32)

        acc_ref[...] += lax.dot_general(a_ref[...].astype(BF16), b_ref[...].astype(BF16), dims,
                                        preferred_element_type=F32)

        @pl.when(kk == nk - 1)
        def _():
            o_ref[...] = acc_ref[...].astype(o_ref.dtype)

    return pl.pallas_call(
        body, name=name, grid=(m // tm, n // tn, nk),
        in_specs=[a_spec, b_spec] + [pl.BlockSpec(memory_space=pl.ANY)] * len(after), out_specs=o_spec,
        out_shape=jax.ShapeDtypeStruct((nbo, m, co), out_dtype), scratch_shapes=[pltpu.VMEM((tm, tn), F32)],
        compiler_params=_cparams(3))(a, b, *after)


def _rms(x, g):
    return x * lax.rsqrt(jnp.mean(x * x, axis=-1, keepdims=True) + NORM_EPS) * g


def _sigmoid(x):
    return 1.0 / (1.0 + jnp.exp(-x))


def _silu(x):
    return x * _sigmoid(x)


def _gelu(x):
    return 0.5 * x * (1.0 + lax.erf(x * 0.7071067811865476))


def _softplus(x):
    return jnp.maximum(x, 0.0) + jnp.log(1.0 + jnp.exp(-jnp.abs(x)))


def _split(x, n):
    w = x.shape[-1] // n
    return [x[..., h * w:(h + 1) * w] for h in range(n)]


def _sgu(ua, va, smw, smb, lng, lnb):
    c = ua[0].shape[0]
    width = len(va) * va[0].shape[-1]
    vf = [_gelu(v) for v in va]
    mu = sum(jnp.sum(v, axis=-1, keepdims=True) for v in vf) / width
    var = sum(jnp.sum(jnp.square(v - mu), axis=-1, keepdims=True) for v in vf) / width
    r = lax.rsqrt(var + NORM_EPS)
    causal = lax.broadcasted_iota(jnp.int32, (c, c), 0) >= lax.broadcasted_iota(jnp.int32, (c, c), 1)
    outs = []
    for h in range(len(ua)):
        vn = (vf[h] - mu) * r * lng[h] + lnb[h]
        w = jnp.where(causal, smw[h], 0.0)
        mixed = jnp.dot(w.astype(BF16), vn.astype(BF16), preferred_element_type=F32) + smb[h]
        outs.append(_gelu(ua[h]) * mixed)
    return outs


def _shift_rows(x, k, up):
    rows = x.shape[0]
    row = lax.broadcasted_iota(jnp.int32, x.shape, 0)
    if up:
        return jnp.where(row < rows - k, pltpu.roll(x, rows - k, 0), 0.0)
    return jnp.where(row >= k, pltpu.roll(x, k, 0), 0.0)


@functools.partial(jax.custom_vjp, nondiff_argnums=(1,))
def _delay(x, k):
    return _shift_rows(x, k, False)


def _delay_fwd(x, k):
    return _shift_rows(x, k, False), None


def _delay_bwd(k, _, g):
    return (_shift_rows(g, k, True),)


_delay.defvjp(_delay_fwd, _delay_bwd)


def _conv_silu(x, w0, w1, w2, w3):
    y = w3 * x + w2 * _delay(x, 1) + w1 * _delay(x, 2) + w0 * _delay(x, 3)
    return _silu(y)


def _dn_step(s, q, k, v, braw, araw, alog, dtb):
    nh, c, d = q.shape
    row = lax.broadcasted_iota(jnp.int32, (c, c), 0)
    col = lax.broadcasted_iota(jnp.int32, (c, c), 1)
    causal, strict = (row >= col)[None], (row > col)[None]
    lower = (row >= col).astype(F32)
    strict_f = (row > col).astype(F32)
    eye = (row == col).astype(F32)[None]
    bmm = functools.partial(jnp.einsum, precision=HI, preferred_element_type=F32)

    qn = q * lax.rsqrt(jnp.sum(q * q, axis=-1, keepdims=True) + NORM_EPS) * (d ** -0.5)
    kn = k * lax.rsqrt(jnp.sum(k * k, axis=-1, keepdims=True) + NORM_EPS)
    beta = _sigmoid(braw)
    g = -jnp.exp(alog) * _softplus(araw + dtb)
    lg = lower[None] * g
    gcum = jnp.sum(lg, axis=-1, keepdims=True)
    diff = jnp.dot(lg.reshape(nh * c, c), strict_f, precision=HI,
                   preferred_element_type=F32).reshape(nh, c, c)
    decay = jnp.where(causal, jnp.exp(diff), 0.0)
    bcol = jnp.sum(eye * beta, axis=-1, keepdims=True)
    kb = kn * bcol
    a = jnp.where(strict, bmm("htd,hsd->hts", kb, kn) * decay, 0.0)
    inv = eye - a
    p = bmm("hts,hsr->htr", a, a)
    n_fac = int(math.log2(c)) - 1
    for it in range(n_fac):
        inv = inv + bmm("hts,hsr->htr", inv, p)
        if it < n_fac - 1:
            p = bmm("hts,hsr->htr", p, p)
    eg = jnp.exp(gcum)
    u = bmm("hts,hsd->htd", inv, v * bcol)
    w = bmm("hts,hsd->htd", inv, kb * eg)
    qk = jnp.where(causal, bmm("htd,hsd->hts", qn, kn) * decay, 0.0)
    v_new = u - bmm("hck,hkv->hcv", w, s)
    o = bmm("hck,hkv->hcv", qn * eg, s) + bmm("hts,hsv->htv", qk, v_new)
    glast = jnp.sum(g, axis=-1, keepdims=True)
    kdec = kn * jnp.exp(glast - gcum)
    s_new = s * jnp.exp(glast) + bmm("hck,hcv->hkv", kdec, v_new)
    return s_new, o


def _ogate(o, z, w):
    return [_rms(oh, w) * _silu(zh) for oh, zh in zip(o, z)]


def _xattn(q, k, v):
    outs = []
    for qh, kh, vh in zip(q, k, v):
        s = lax.dot_general(qh.astype(BF16), kh.astype(BF16), (((1,), (1,)), ((), ())),
                            preferred_element_type=F32) * (qh.shape[-1] ** -0.5)
        s = s - jnp.max(s, axis=-1, keepdims=True)
        e = jnp.exp(s)
        p = e / jnp.sum(e, axis=-1, keepdims=True)
        outs.append(jnp.dot(p.astype(BF16), vh.astype(BF16), preferred_element_type=F32))
    return outs


def _rows(t):
    return min(ROW_TILE, t)


def k_rms_fwd(name, x, g, after=()):
    t, d = x.shape
    tm = _rows(t)
    return stepk(name, lambda xv, gv: (_rms(xv, gv),), (t // tm,),
                 [(x, (tm, d), lambda i: (i, 0)), (g, (1, d), lambda i: (0, 0))],
                 [((t, d), BF16, (tm, d), lambda i: (i, 0))], after=after)[0]


def k_rms_bwd(name, x, g, dhs, dx_res):
    t, d = x.shape
    tm = _rows(t)
    n = len(dhs)

    def fn(xv, gv, *rest):
        dh = sum(r.astype(F32) for r in rest[:n])
        dxr, dg_c = rest[n], rest[n + 1]
        _, vjp = jax.vjp(_rms, xv, gv)
        dx, dg = vjp(dh)
        return dx + dxr, dg_c + dg

    row = lambda arr: (arr, (tm, d), lambda i: (i, 0))
    return stepk(name, fn, (t // tm,), [row(x), (g, (1, d), lambda i: (0, 0))] + [row(h) for h in dhs] + [row(dx_res)],
                 [((t, d), F32, (tm, d), lambda i: (i, 0))], carries=[((1, d), F32)])


def k_post_fwd(name, x, f, g, scale):
    t, d = x.shape
    tm = _rows(t)
    row = lambda arr: (arr, (tm, d), lambda i: (i, 0))
    return stepk(name, lambda xv, fv, gv: (xv + scale * _rms(fv, gv),), (t // tm,),
                 [row(x), row(f), (g, (1, d), lambda i: (0, 0))], [((t, d), F32, (tm, d), lambda i: (i, 0))])[0]


def k_post_bwd(name, f, g, dxo, scale, after=()):
    t, d = f.shape
    tm = _rows(t)

    def fn(fv, gv, dv, dg_c):
        _, vjp = jax.vjp(lambda a, b: scale * _rms(a, b), fv, gv)
        df, dg = vjp(dv)
        return df, dg_c + dg

    row = lambda arr: (arr, (tm, d), lambda i: (i, 0))
    return stepk(name, fn, (t // tm,), [row(f), (g, (1, d), lambda i: (0, 0)), row(dxo)],
                 [((t, d), BF16, (tm, d), lambda i: (i, 0))], carries=[((1, d), F32)], after=after)


def k_swiglu_fwd(name, hu):
    nb, t, c = hu.shape
    half = nb // 2
    tm = _rows(t)
    fn = lambda gv, uv: (_silu(gv.astype(F32)) * uv.astype(F32),)
    return stepk(name, fn, (half, t // tm),
                 [(hu, (None, tm, c), lambda b, i: (b, i, 0)), (hu, (None, tm, c), lambda b, i: (b + half, i, 0))],
                 [((half, t, c), BF16, (None, tm, c), lambda b, i: (b, i, 0))])[0]


def k_swiglu_bwd(name, hu, da):
    nb, t, c = hu.shape
    half = nb // 2
    tm = _rows(t)

    def body(g_ref, u_ref, d_ref, o_ref):
        gv, uv, dv = g_ref[...].astype(F32), u_ref[...].astype(F32), d_ref[...].astype(F32)
        sg = _sigmoid(gv)

        @pl.when(pl.program_id(0) == 0)
        def _():
            o_ref[...] = (dv * uv * sg * (1.0 + gv * (1.0 - sg))).astype(o_ref.dtype)

        @pl.when(pl.program_id(0) == 1)
        def _():
            o_ref[...] = (dv * gv * sg).astype(o_ref.dtype)

    blk = lambda off: pl.BlockSpec((None, tm, c), lambda p, b, i: (b + off, i, 0))
    return pl.pallas_call(
        body, name=name, grid=(2, half, t // tm), in_specs=[blk(0), blk(half), blk(0)],
        out_specs=pl.BlockSpec((None, tm, c), lambda p, b, i: (p * half + b, i, 0)),
        out_shape=jax.ShapeDtypeStruct((nb, t, c), BF16), compiler_params=_cparams(3))(hu, hu, da)


def k_loss(name, y, tgt):
    t, d = y.shape
    tm = _rows(t)

    def fn(yv, tv, acc):
        e = yv - tv
        part = jnp.sum(jnp.sum(e * e, axis=-1, keepdims=True), axis=0, keepdims=True)
        return e * (1.0 / d), acc + (0.5 / d) * part

    row = lambda arr: (arr, (tm, d), lambda i: (i, 0))
    return stepk(name, fn, (t // tm,), [row(y), row(tgt)], [((t, d), F32, (tm, d), lambda i: (i, 0))],
                 carries=[((1, 1), F32)])


GM_W = GM_HEADS * 128
QKV_COL0 = 2 * GM_W
QKV_W = 3 * DN_HEADS * DN_DIM
Z_COL0 = QKV_COL0 + QKV_W
MAIN_W = Z_COL0 + DN_HEADS * DN_DIM


def _sgu_ins(proj, p):
    c = GM_CHUNK
    return [(proj, (c, GM_W), lambda i: (i, 0)), (proj, (c, GM_W), lambda i: (i, 1)),
            (p["sm_w"], (GM_HEADS, c, c), lambda i: (0, 0, 0)), (p["sm_b"], (GM_HEADS, c, 1), lambda i: (0, 0, 0)),
            (p["sm_ln_g"], (1, GM_W), lambda i: (0, 0)), (p["sm_ln_b"], (1, GM_W), lambda i: (0, 0))]


def _sgu_lists(uv, vv, sw, sb, lg, lb):
    nh = GM_HEADS
    return (_split(uv, nh), _split(vv, nh), [sw[h] for h in range(nh)], [sb[h] for h in range(nh)],
            _split(lg, nh), _split(lb, nh))


def k_sgu_fwd(name, proj, p):
    t = proj.shape[0]

    def fn(*vals):
        return (jnp.concatenate(_sgu(*_sgu_lists(*vals)), axis=-1),)

    return stepk(name, fn, (t // GM_CHUNK,), _sgu_ins(proj, p),
                 [((t, GM_W), BF16, (GM_CHUNK, GM_W), lambda i: (i, 0))])[0]


def k_sgu_bwd(name, proj, p, dy):
    t = proj.shape[0]
    c = GM_CHUNK

    def fn(uv, vv, sw, sb, lg, lb, dv, dsw, dsb, dlg, dlb):
        _, vjp = jax.vjp(_sgu, *_sgu_lists(uv, vv, sw, sb, lg, lb))
        gu, gv, gsw, gsb, glg, glb = vjp(_split(dv.astype(F32), GM_HEADS))
        cat = lambda l: jnp.concatenate(l, axis=-1)
        return (cat(gu), cat(gv), dsw + jnp.stack(gsw), dsb + jnp.stack(gsb), dlg + cat(glg), dlb + cat(glb))

    return stepk(name, fn, (t // c,), _sgu_ins(proj, p) + [(dy, (None, c, GM_W), lambda i: (0, i, 0))],
                 [((t, GM_W), BF16, (c, GM_W), lambda i: (i, 0)), ((t, GM_W), BF16, (c, GM_W), lambda i: (i, 0))],
                 carries=[((GM_HEADS, c, c), F32), ((GM_HEADS, c, 1), F32), ((1, GM_W), F32), ((1, GM_W), F32)])


def _conv_ins(proj, conv_w):
    t = proj.shape[0]
    return [(proj, (t, 128), lambda j: (0, QKV_COL0 // 128 + j)), (conv_w, (CONV_W, 128), lambda j: (0, j))]


def k_conv_fwd(name, proj, conv_w):
    t = proj.shape[0]
    n = QKV_W // 128
    fn = lambda xv, wv: (_conv_silu(xv, *[wv[i:i + 1] for i in range(CONV_W)]),)
    return stepk(name, fn, (n,), _conv_ins(proj, conv_w), [((n, t, 128), F32, (None, t, 128), lambda j: (j, 0, 0))])[0]


def k_conv_bwd(name, proj, conv_w, dqkv):
    t = proj.shape[0]
    n = QKV_W // 128

    def fn(xv, wv, dv):
        _, vjp = jax.vjp(_conv_silu, xv, *[wv[i:i + 1] for i in range(CONV_W)])
        gx, *gw = vjp(dv)
        return gx, jnp.concatenate(gw, axis=0)

    return stepk(name, fn, (n,), _conv_ins(proj, conv_w) + [(dqkv, (None, t, 128), lambda j: (j, 0, 0))],
                 [((t, QKV_W), BF16, (t, 128), lambda j: (0, j)), ((CONV_W, QKV_W), F32, (CONV_W, 128), lambda j: (0, j))])


def _dn_ins(qkv, braw, araw, p, order):
    h, c, d = DN_HEADS, DN_CHUNK, DN_DIM
    qkv_in = lambda part: (qkv, (h, c, d), lambda n: (part, order(n), 0))
    gate_in = lambda arr: (arr, (None, h, 1, c), lambda n: (order(n), 0, 0, 0))
    par_in = lambda arr: (arr, (h, 1, 1), lambda n: (0, 0, 0))
    return [qkv_in(0), qkv_in(1), qkv_in(2), gate_in(braw), gate_in(araw), par_in(p["a_log"]), par_in(p["dt_bias"])]


def k_dn_fwd(name, qkv, braw, araw, p):
    t = qkv.shape[1]
    h, c, d = DN_HEADS, DN_CHUNK, DN_DIM
    nc = t // c

    def fn(q, k, v, b, a, al, dt, s):
        s_new, o = _dn_step(s, q, k, v, b, a, al, dt)
        return o, s, s_new

    o, s_all, _ = stepk(name, fn, (nc,), _dn_ins(qkv, braw, araw, p, lambda n: n),
                        [((h, t, d), F32, (h, c, d), lambda n: (0, n, 0)),
                         ((nc, h, d, d), F32, (None, h, d, d), lambda n: (n, 0, 0, 0))],
                        carries=[((h, d, d), F32)])
    return o, s_all


def k_dn_bwd(name, qkv, braw, araw, p, s_all, do):
    t = qkv.shape[1]
    h, c, d = DN_HEADS, DN_CHUNK, DN_DIM
    nc = t // c
    rev = lambda n: nc - 1 - n

    def fn(q, k, v, b, a, al, dt, s, dov, ds_c, dal_c, ddt_c):
        _, vjp = jax.vjp(_dn_step, s, q, k, v, b, a, al, dt)
        ds, dq, dk, dv, db, da, dal, ddt = vjp((ds_c, dov))
        return dq, dk, dv, db, da, ds, dal_c + dal, ddt_c + ddt

    ins = _dn_ins(qkv, braw, araw, p, rev) + [(s_all, (None, h, d, d), lambda n: (rev(n), 0, 0, 0)),
                                               (do, (h, c, d), lambda n: (0, rev(n), 0))]
    hd = ((h, t, d), F32, (h, c, d), lambda n: (0, rev(n), 0))
    gate = ((nc, h, 1, c), F32, (None, h, 1, c), lambda n: (rev(n), 0, 0, 0))
    dq, dk, dv, db, da, _, dal, ddt = stepk(name, fn, (nc,), ins, [hd, hd, hd, gate, gate],
                                            carries=[((h, d, d), F32), ((h, 1, 1), F32), ((h, 1, 1), F32)])
    return dq, dk, dv, db, da, dal, ddt


def _ogate_ins(o, proj, p):
    t = o.shape[1]
    tm = _rows(t)
    return tm, [(o, (DN_HEADS, tm, DN_DIM), lambda i: (0, i, 0)), (proj, (tm, GM_W), lambda i: (i, Z_COL0 // GM_W)),
                (p["dn_norm_w"], (1, DN_DIM), lambda i: (0, 0))]


def k_ogate_fwd(name, o, proj, p):
    t = o.shape[1]
    tm, ins = _ogate_ins(o, proj, p)

    def fn(ov, zv, wv):
        return (jnp.concatenate(_ogate([ov[h] for h in range(DN_HEADS)], _split(zv, DN_HEADS), wv), axis=-1),)

    return stepk(name, fn, (t // tm,), ins, [((t, GM_W), BF16, (tm, GM_W), lambda i: (i, 0))])[0]


def k_ogate_bwd(name, o, proj, p, dy):
    t = o.shape[1]
    tm, ins = _ogate_ins(o, proj, p)

    def fn(ov, zv, wv, dv, dw_c):
        _, vjp = jax.vjp(_ogate, [ov[h] for h in range(DN_HEADS)], _split(zv, DN_HEADS), wv)
        go, gz, gw = vjp(_split(dv.astype(F32), DN_HEADS))
        return jnp.stack(go), jnp.concatenate(gz, axis=-1), dw_c + gw

    return stepk(name, fn, (t // tm,), ins + [(dy, (None, tm, GM_W), lambda i: (1, i, 0))],
                 [((DN_HEADS, t, DN_DIM), F32, (DN_HEADS, tm, DN_DIM), lambda i: (0, i, 0)),
                  ((t, GM_W), BF16, (tm, GM_W), lambda i: (i, 0))], carries=[((1, DN_DIM), F32)])


def _xattn_lists(qv, kvv):
    nh = XA_HEADS
    return (_split(qv.astype(F32), nh), [kvv[h].astype(F32) for h in range(nh)],
            [kvv[nh + h].astype(F32) for h in range(nh)])


def k_xattn_fwd(name, q, kv):
    t, d = q.shape
    tm = _rows(t)
    fn = lambda qv, kvv: (jnp.concatenate(_xattn(*_xattn_lists(qv, kvv)), axis=-1),)
    return stepk(name, fn, (t // tm,), [(q, (tm, d), lambda i: (i, 0)), (kv, kv.shape, lambda i: (0, 0, 0))],
                 [((t, d), BF16, (tm, d), lambda i: (i, 0))])[0]


def k_xattn_bwd(name, q, kv, do):
    t, d = q.shape
    tm = _rows(t)

    def fn(qv, kvv, dv, dkv_c):
        _, vjp = jax.vjp(_xattn, *_xattn_lists(qv, kvv))
        gq, gk, gv = vjp(_split(dv.astype(F32), XA_HEADS))
        return jnp.concatenate(gq, axis=-1), dkv_c + jnp.stack(gk + gv)

    return stepk(name, fn, (t // tm,),
                 [(q, (tm, d), lambda i: (i, 0)), (kv, kv.shape, lambda i: (0, 0, 0)), (do, (None, tm, d), lambda i: (0, i, 0))],
                 [((t, d), BF16, (tm, d), lambda i: (i, 0))], carries=[(kv.shape, F32)])


def ffn_fwd(tag, x, p, pre, post, gu, dn, after=()):
    h = k_rms_fwd(f"{tag}_pre", x, p[pre], after)
    hu = mm(f"{tag}_gu", h[None], p[gu], "nn", BF16, nbo=N_DEV)
    a = k_swiglu_fwd(f"{tag}_act", hu)
    f = mm(f"{tag}_down", a, p[dn], "nn", F32)[0]
    y = k_post_fwd(f"{tag}_post", x, f, p[post], 0.5)
    return y, (x, h, hu, a, f)


class NoExchange:
    def take(self):
        return ()

    def emit(self, grads, names):
        pass


def ffn_bwd(tag, dy, saved, p, pre, post, gu, dn, grads, sch):
    x, h, hu, a, f = saved
    df, grads[post] = k_post_bwd(f"{tag}_post_b", f, p[post], dy, 0.5, sch.take())
    grads[dn] = mm(f"{tag}_down_dw", a, df[None], "tn", BF16)
    sch.emit(grads, (dn,))
    da = mm(f"{tag}_down_dx", df[None], p[dn], "nt", BF16, nbo=N_DEV // 2, after=sch.take())
    dhu = k_swiglu_bwd(f"{tag}_act_b", hu, da)
    grads[gu] = mm(f"{tag}_gu_dw", h[None], dhu, "tn", BF16, nbo=N_DEV)
    sch.emit(grads, (gu,))
    dh = mm(f"{tag}_gu_dx", dhu, p[gu], "nt", BF16, after=sch.take())[0]
    dx, grads[pre] = k_rms_bwd(f"{tag}_pre_b", x, p[pre], [dh], dy)
    return dx


def _to_chunks(a):
    t, h = a.shape
    return a.reshape(t // DN_CHUNK, DN_CHUNK, h).transpose(0, 2, 1).reshape(t // DN_CHUNK, h, 1, DN_CHUNK)


def _from_chunks(a):
    nc, h, _, c = a.shape
    return a.reshape(nc, h, c).transpose(0, 2, 1).reshape(nc * c, h)


def mix_fwd(tag, x, p, after=()):
    h = k_rms_fwd(f"{tag}_pre", x, p["mix_norm_pre"], after)
    proj = mm(f"{tag}_in", h[None], p["w_in_main"], "nn", F32)[0]
    small = mm(f"{tag}_in_s", h[None], p["w_in_small"], "nn", F32)[0]
    braw, araw = _to_chunks(small[:, :DN_HEADS]), _to_chunks(small[:, DN_HEADS:2 * DN_HEADS])
    ya = k_sgu_fwd(f"{tag}_sgu", proj, p)
    qkv = k_conv_fwd(f"{tag}_conv", proj, p["conv_w"])
    o, s_all = k_dn_fwd(f"{tag}_dn", qkv, braw, araw, p)
    yb = k_ogate_fwd(f"{tag}_og", o, proj, p)
    y = jnp.stack([ya, yb])
    m = mm(f"{tag}_out", y, p["w_out"], "nn", F32)[0]
    out = k_post_fwd(f"{tag}_post", x, m, p["mix_norm_post"], 1.0)
    return out, (x, h, proj, braw, araw, qkv, o, s_all, y, m)


def mix_bwd(tag, dy, saved, p, grads, sch):
    x, h, proj, braw, araw, qkv, o, s_all, y, m = saved
    t = x.shape[0]
    dm, grads["mix_norm_post"] = k_post_bwd(f"{tag}_post_b", m, p["mix_norm_post"], dy, 1.0, sch.take())
    grads["w_out"] = mm(f"{tag}_out_dw", y, dm[None], "tn", BF16)
    sch.emit(grads, ("w_out",))
    dyy = mm(f"{tag}_out_dx", dm[None], p["w_out"], "nt", BF16, nbo=2, after=sch.take())
    do, dz, grads["dn_norm_w"] = k_ogate_bwd(f"{tag}_og_b", o, proj, p, dyy)
    dq, dk, dv, db, da, grads["a_log"], grads["dt_bias"] = k_dn_bwd(f"{tag}_dn_b", qkv, braw, araw, p, s_all, do)
    dconv, grads["conv_w"] = k_conv_bwd(f"{tag}_conv_b", proj, p["conv_w"], jnp.concatenate([dq, dk, dv], axis=0))
    du, dva, grads["sm_w"], grads["sm_b"], grads["sm_ln_g"], grads["sm_ln_b"] = k_sgu_bwd(f"{tag}_sgu_b", proj, p, dyy)
    dproj = jnp.concatenate([du, dva, dconv, dz], axis=-1)
    dsmall = jnp.concatenate([_from_chunks(db), _from_chunks(da), jnp.zeros((t, 128 - 2 * DN_HEADS), F32)], axis=-1)
    g_main = mm(f"{tag}_in_dw", h[None], dproj[None], "tn", BF16)[0]
    g_small = mm(f"{tag}_in_s_dw", h[None], dsmall[None], "tn", BF16)[0][:, :2 * DN_HEADS]
    g_in = jnp.concatenate([g_main, g_small], axis=-1)
    grads["w_in"] = g_in.reshape(g_in.shape[0], N_DEV, g_in.shape[1] // N_DEV).transpose(1, 0, 2)
    sch.emit(grads, ("w_in",))
    dh = mm(f"{tag}_in_dx", dproj[None], p["w_in_main"], "nt", BF16, after=sch.take())[0]
    dh_s = mm(f"{tag}_in_s_dx", dsmall[None], p["w_in_small"], "nt", BF16)[0]
    dx, grads["mix_norm_pre"] = k_rms_bwd(f"{tag}_pre_b", x, p["mix_norm_pre"], [dh, dh_s], dy)
    return dx


def xa_fwd(tag, x, mem, p, after=()):
    hx = k_rms_fwd(f"{tag}_pre", x, p["xa_norm_pre"], after)
    mh = k_rms_fwd(f"{tag}_mem", mem, p["mem_norm"], after)
    q = mm(f"{tag}_q", hx[None], p["w_xq"], "nn", BF16)[0]
    kv = mm(f"{tag}_kv", mh[None], p["w_xkv"], "nn", BF16, nbo=N_DEV)
    o = k_xattn_fwd(f"{tag}_att", q, kv)
    c = mm(f"{tag}_o", o[None], p["w_xo"], "nn", F32)[0]
    out = k_post_fwd(f"{tag}_post", x, c, p["xa_norm_post"], 1.0)
    return out, (x, mem, hx, mh, q, kv, o, c)


def xa_bwd(tag, dy, saved, p, grads, sch):
    x, mem, hx, mh, q, kv, o, c = saved
    dc, grads["xa_norm_post"] = k_post_bwd(f"{tag}_post_b", c, p["xa_norm_post"], dy, 1.0, sch.take())
    grads["w_xo"] = mm(f"{tag}_o_dw", o[None], dc[None], "tn", BF16)
    sch.emit(grads, ("w_xo",))
    do = mm(f"{tag}_o_dx", dc[None], p["w_xo"], "nt", BF16, after=sch.take())
    dq, dkv = k_xattn_bwd(f"{tag}_att_b", q, kv, do)
    dkv16 = dkv.astype(BF16)
    grads["w_xq"] = mm(f"{tag}_q_dw", hx[None], dq[None], "tn", BF16)
    grads["w_xkv"] = mm(f"{tag}_kv_dw", mh[None], dkv16, "tn", BF16, nbo=N_DEV)
    sch.emit(grads, ("w_xq", "w_xkv"))
    dhx = mm(f"{tag}_q_dx", dq[None], p["w_xq"], "nt", BF16, after=sch.take())[0]
    dmh = mm(f"{tag}_kv_dx", dkv16, p["w_xkv"], "nt", BF16)[0]
    _, grads["mem_norm"] = k_rms_bwd(f"{tag}_mem_b", mem, p["mem_norm"], [dmh], jnp.zeros_like(mem))
    dx, grads["xa_norm_pre"] = k_rms_bwd(f"{tag}_pre_b", x, p["xa_norm_pre"], [dhx], dy)
    return dx


def _place():
    return lax.axis_index("x"), lax.axis_index("y"), lax.axis_index("c")


def _other_chips(x, y):
    return [(1 - x, y), (x, 1 - y), (1 - x, 1 - y)]


_ANY = pl.BlockSpec(memory_space=pl.ANY)


def all_gather(name, shards):
    n = len(shards)

    def body(*refs):
        ins, outs = refs[:n], refs[n:2 * n]
        send_sems, recv_sems, local_sems = refs[2 * n:]
        x, y, c = _place()
        me, sibling = (x, y, c), (x, y, 1 - c)
        chips = _other_chips(x, y)
        idx = lambda px, py, pc: 4 * px + 2 * py + pc

        def copy(a, k, block, to, src=None):
            dst = outs[a].at[idx(*block)]
            return pltpu.make_async_remote_copy(src_ref=dst if src is None else src, dst_ref=dst,
                                                send_sem=send_sems.at[7 * a + k], recv_sem=recv_sems.at[7 * a + k],
                                                device_id=to, device_id_type=MESH)

        mine = [pltpu.make_async_copy(ins[a], outs[a].at[idx(*me)], local_sems.at[a]) for a in range(n)]
        for cp in mine:
            cp.start()
        first = []
        for a in range(n):
            first.append(copy(a, 0, me, sibling, src=ins[a]))
            first += [copy(a, 1 + j, me, (*chip, c), src=ins[a]) for j, chip in enumerate(chips)]
        for cp in first:
            cp.start()
        passed = []
        for a in range(n):
            for j, chip in enumerate(chips):
                copy(a, 1 + j, (*chip, c), me).wait_recv()
                passed.append(copy(a, 4 + j, (*chip, c), sibling))
                passed[-1].start()
        for a in range(n):
            copy(a, 0, sibling, me).wait_recv()
            for j, chip in enumerate(chips):
                copy(a, 4 + j, (*chip, 1 - c), me).wait_recv()
        for cp in first + passed:
            cp.wait_send()
        for cp in mine:
            cp.wait()

    return pl.pallas_call(
        body, name=name, in_specs=[_ANY] * n, out_specs=[_ANY] * n,
        out_shape=[jax.ShapeDtypeStruct((N_DEV,) + s.shape, s.dtype) for s in shards],
        scratch_shapes=[pltpu.SemaphoreType.DMA((7 * n,)), pltpu.SemaphoreType.DMA((7 * n,)),
                        pltpu.SemaphoreType.DMA((n,))])(*shards)


def k_cast_place(name, w, layer, me, after):
    _, r, cc = w.shape
    tr = _row_tile(r, cc)
    return stepk(name, lambda v: (v,), (r // tr,), [(w, (None, tr, cc), lambda i, pre: (layer, i, 0))],
                 [((N_DEV, r, cc), BF16, (None, tr, cc), lambda i, pre: (pre[0], i, 0))], prefetch=me, after=after)[0]


_HBM = pl.BlockSpec(memory_space=pltpu.HBM)
_SEM = pl.BlockSpec(memory_space=pltpu.SEMAPHORE)
_DATAFLOW = pltpu.SideEffectType.DATAFLOW_SIDE_EFFECTING


def xfer_start(name, plan, n_sems, srcs, lands, after=()):
    ns, nl, na = len(srcs), len(lands), len(after)
    bufs = list(srcs) + list(lands)

    def body(*refs):
        send_sems, recv_sems, token = refs[ns + nl + na], refs[ns + nl + na + 1], refs[-1]
        for cp in plan(refs[:ns], refs[ns:ns + nl], send_sems, recv_sems):
            cp.start()
        token[...] = jnp.zeros(token.shape, token.dtype)

    res = pl.pallas_call(
        body, name=name, in_specs=[_HBM] * (ns + nl) + [_ANY] * na,
        out_specs=(_SEM, _SEM, *[_HBM] * (ns + nl), pl.BlockSpec(memory_space=pltpu.VMEM)),
        out_shape=(pltpu.SemaphoreType.DMA((n_sems,)), pltpu.SemaphoreType.DMA((n_sems,)),
                   *[pltpu.HBM(b.shape, b.dtype) for b in bufs], jax.ShapeDtypeStruct((8, 128), F32)),
        input_output_aliases={i: 2 + i for i in range(ns + nl)},
        compiler_params=pltpu.CompilerParams(has_side_effects=_DATAFLOW),
    )(*[pltpu.with_memory_space_constraint(b, pltpu.HBM) for b in bufs], *after)
    return dict(send=res[0], recv=res[1], srcs=list(res[2:2 + ns]), lands=list(res[2 + ns:2 + ns + nl]), token=res[-1])


def xfer_wait(name, plan, started, after=()):
    ns, nl = len(started["srcs"]), len(started["lands"])
    bufs = started["srcs"] + started["lands"]

    def body(*refs):
        for cp in plan(refs[:ns], refs[ns:ns + nl], refs[ns + nl], refs[ns + nl + 1]):
            cp.wait_send()
            cp.wait_recv()

    res = pl.pallas_call(
        body, name=name, in_specs=[_HBM] * (ns + nl) + [_SEM, _SEM] + [_ANY] * len(after),
        out_specs=tuple([_HBM] * (ns + nl)), out_shape=tuple(pltpu.HBM(b.shape, b.dtype) for b in bufs),
        input_output_aliases={i: i for i in range(ns + nl)},
        compiler_params=pltpu.CompilerParams(has_side_effects=_DATAFLOW),
    )(*bufs, started["send"], started["recv"], *after)
    return list(res[:ns]), list(res[ns:])


def _remote(src, dst, send_sems, recv_sems, k, to):
    return pltpu.make_async_remote_copy(src_ref=src, dst_ref=dst, send_sem=send_sems.at[k], recv_sem=recv_sems.at[k],
                                        device_id=to, device_id_type=MESH)


def plan_gather_out(srcs, lands, send_sems, recv_sems):
    x, y, c = _place()
    me = 4 * x + 2 * y + c
    targets = [(x, y, 1 - c)] + [(px, py, c) for px, py in _other_chips(x, y)]
    return [_remote(lands[a].at[me], lands[a].at[me], send_sems, recv_sems, 4 * a + k, to)
            for a in range(len(lands)) for k, to in enumerate(targets)]


def plan_gather_pass(srcs, lands, send_sems, recv_sems):
    x, y, c = _place()
    return [_remote(lands[a].at[4 * px + 2 * py + c], lands[a].at[4 * px + 2 * py + c], send_sems, recv_sems,
                    3 * a + j, (x, y, 1 - c))
            for a in range(len(lands)) for j, (px, py) in enumerate(_other_chips(x, y))]


def plan_scatter_sibling(srcs, lands, send_sems, recv_sems):
    x, y, c = _place()
    return [_remote(srcs[a].at[j, 1 - c], lands[a].at[j], send_sems, recv_sems, 4 * a + j, (x, y, 1 - c))
            for a in range(len(srcs)) for j in range(4)]


def plan_scatter_chips(srcs, lands, send_sems, recv_sems):
    x, y, c = _place()
    return [_remote(srcs[a].at[2 * px + py], lands[a].at[j], send_sems, recv_sems, 3 * a + j, (px, py, c))
            for a in range(len(srcs)) for j, (px, py) in enumerate(_other_chips(x, y))]


ELEMWISE_BLOCK = 256 * 1024


def _row_tile(r, cc):
    t = r
    while t * cc > ELEMWISE_BLOCK and t % 32 == 0:
        t //= 2
    if t * cc > ELEMWISE_BLOCK:
        for cand in range(t, 15, -16):
            if r % cand == 0 and cand * cc <= ELEMWISE_BLOCK:
                return cand
    return t


def k_pair_add(name, g4, r1, core):
    _, _, r, cc = g4.shape
    tr = _row_tile(r, cc)
    fn = lambda av, bv: (av.astype(F32) + bv.astype(F32),)
    return stepk(name, fn, (4, r // tr),
                 [(g4, (None, None, tr, cc), lambda j, i, pre: (j, pre[0], i, 0)),
                  (r1, (None, tr, cc), lambda j, i, pre: (j, i, 0))],
                 [((4, r, cc), BF16, (None, tr, cc), lambda j, i, pre: (j, i, 0))], prefetch=core)[0]


def _adamw(g, w, m, v):
    m2 = ADAM_B1 * m + (1.0 - ADAM_B1) * g
    v2 = ADAM_B2 * v + (1.0 - ADAM_B2) * jnp.square(g)
    m_hat = m2 / (1.0 - ADAM_B1 ** ADAM_STEP)
    v_hat = v2 / (1.0 - ADAM_B2 ** ADAM_STEP)
    delta = -ADAM_LR * (m_hat / (jnp.sqrt(v_hat) + ADAM_EPS) + ADAM_WD * w)
    return g, delta, m2, v2


def k_adamw_shard(name, part, others, w, m, v, layer, chip, fill):
    _, r, cc = part.shape
    tr = _row_tile(r, cc)

    def fn(pv, o0, o1, o2, wv, mv, vv):
        g = ((pv.astype(F32) + o0.astype(F32)) + o1.astype(F32)) + o2.astype(F32)
        return _adamw(g, wv, mv, vv)

    other = lambda k: (others, (None, tr, cc), lambda i, pre: (k, i, 0))
    state = lambda arr: (arr, (None, tr, cc), lambda i, pre: (layer, i, 0))
    out = ((2, r, cc), F32, (None, tr, cc), lambda i, pre: (layer, i, 0))
    return stepk(name, fn, (r // tr,),
                 [(part, (None, tr, cc), lambda i, pre: (pre[0], i, 0)), other(0), other(1), other(2),
                  state(w), state(m), state(v)],
                 [out] * 4, prefetch=chip, fill=[(f, k) for k, f in enumerate(fill)])


def k_sum8(name, parts):
    _, rows, lanes = parts.shape

    def fn(pv):
        acc = pv[0]
        for d in range(1, N_DEV):
            acc = acc + pv[d]
        return (acc,)

    return stepk(name, fn, (1,), [(parts, parts.shape, lambda i: (0, 0, 0))],
                 [((rows, lanes), F32, (rows, lanes), lambda i: (0, 0))])[0]


def k_adamw_flat(name, g, w, m, v):
    whole = lambda arr: (arr, arr.shape, lambda i: (0, 0))
    out = (g.shape, F32, g.shape, lambda i: (0, 0))
    return stepk(name, _adamw, (1,), [whole(g), whole(w), whole(m), whole(v)], [out] * 4)


WEIGHTS = ("ffn1_norm_pre", "ffn1_w_gate_up", "ffn1_w_down", "ffn1_norm_post", "mix_norm_pre", "w_in", "conv_w", "a_log",
           "dt_bias", "sm_w", "sm_b", "sm_ln_g", "sm_ln_b", "dn_norm_w", "w_out", "mix_norm_post", "xa_norm_pre", "mem_norm",
           "w_xq", "w_xkv", "w_xo", "xa_norm_post", "ffn2_norm_pre", "ffn2_w_gate_up", "ffn2_w_down", "ffn2_norm_post")
BIG = ("ffn1_w_gate_up", "ffn1_w_down", "w_in", "w_out", "w_xq", "w_xkv", "w_xo", "ffn2_w_gate_up", "ffn2_w_down")
ROW_SHARDED = ("ffn1_w_down", "w_out", "w_xq", "w_xo", "ffn2_w_down")
SMALL = tuple(n for n in WEIGHTS if n not in BIG and n != "conv_w")
N_LAYERS = 2
IN_COLS = MAIN_W + 2 * DN_HEADS


BLOCKS = ("f1", "mx", "xa", "f2")
AG_BEFORE = {"f1": ("ffn1_w_gate_up", "ffn1_w_down"), "mx": ("w_in", "w_out"),
             "xa": ("w_xq", "w_xkv", "w_xo", "ffn2_w_gate_up", "ffn2_w_down")}
RS_IN_FLIGHT = 1


def _fwd_block(l, sb, h, mem, p, after):
    tag = f"l{l}{sb}"
    if sb == "f1":
        return ffn_fwd(tag, h, p, "ffn1_norm_pre", "ffn1_norm_post", "ffn1_w_gate_up", "ffn1_w_down", after)
    if sb == "mx":
        return mix_fwd(tag, h, p, after)
    if sb == "xa":
        return xa_fwd(tag, h, mem, p, after)
    return ffn_fwd(tag, h, p, "ffn2_norm_pre", "ffn2_norm_post", "ffn2_w_gate_up", "ffn2_w_down", after)


def _bwd_block(l, sb, dy, saved, p, grads, sch):
    tag = f"l{l}{sb}"
    if sb == "f1":
        return ffn_bwd(tag, dy, saved, p, "ffn1_norm_pre", "ffn1_norm_post", "ffn1_w_gate_up", "ffn1_w_down", grads, sch)
    if sb == "mx":
        return mix_bwd(tag, dy, saved, p, grads, sch)
    if sb == "xa":
        return xa_bwd(tag, dy, saved, p, grads, sch)
    return ffn_bwd(tag, dy, saved, p, "ffn2_norm_pre", "ffn2_norm_post", "ffn2_w_gate_up", "ffn2_w_down", grads, sch)


class GradExchange:
    def __init__(self, a, core, chip):
        self.a, self.core, self.chip = a, core, chip
        self.layer = None
        self.pending, self.tie, self.out = [], [], {}

    def take(self):
        tie, self.tie = self.tie, []
        return tie

    def _advance(self, chain, drain):
        a = self.a
        between_chips = [grp for grp in self.pending if grp["stage"] == "chips"]
        for grp in self.pending:
            if grp["stage"] == "sibling":
                l, names = grp["l"], grp["names"]
                g4, r1 = xfer_wait(f"rs_sibw_l{l}{names[0]}", plan_scatter_sibling, grp["going"], after=chain)
                parts = [k_pair_add(f"rs_add_l{l}{n}", gg, rr, self.core) for n, gg, rr in zip(names, g4, r1)]
                zones = [lax.empty((3,) + p.shape[1:], BF16) for p in parts]
                grp["going"] = xfer_start(f"rs_chip_l{l}{names[0]}", plan_scatter_chips, 3 * len(names), parts, zones)
                grp["stage"], grp["age"] = "chips", 0
                chain = [grp["going"]["token"]]
        for grp in between_chips:
            l, names = grp["l"], grp["names"]
            if drain or grp["age"] >= RS_IN_FLIGHT:
                parts, r2 = xfer_wait(f"rs_chipw_l{l}{names[0]}", plan_scatter_chips, grp["going"], after=chain)
                for n, part, others in zip(names, parts, r2):
                    self.out[n] = k_adamw_shard(f"adamw_{n}_l{l}", part, others, a[n], a["m_" + n], a["v_" + n], l,
                                                self.chip, self.out.get(n, ()))
                self.pending.remove(grp)
            else:
                grp["age"] += 1
        return chain

    def emit(self, grads, names):
        l = self.layer
        chain = self._advance([grads[names[-1]]], drain=False)
        g4 = [_scatter_layout(n, grads[n]) for n in names]
        g4 = [t.reshape(4, 2, *t.shape[1:]) for t in g4]
        zones = [lax.empty((4,) + t.shape[2:], BF16) for t in g4]
        going = xfer_start(f"rs_sib_l{l}{names[0]}", plan_scatter_sibling, 4 * len(names), g4, zones, after=chain)
        self.pending.append(dict(l=l, names=names, stage="sibling", going=going, age=0))
        self.tie = [going["token"]]

    def finish(self, last):
        chain = [last]
        while self.pending:
            chain = self._advance(chain, drain=True)
        return self.out


def _big_params(full):
    p = {}
    for n, w in full.items():
        if n in ROW_SHARDED:
            p[n] = w.reshape(1, w.shape[0] * w.shape[1], w.shape[2])
        elif n == "w_in":
            w = w.transpose(1, 0, 2).reshape(w.shape[1], IN_COLS)
            p["w_in_main"] = w[None, :, :MAIN_W]
            p["w_in_small"] = jnp.pad(w[:, MAIN_W:], ((0, 0), (0, 128 - 2 * DN_HEADS)))[None]
        else:
            p[n] = w
    return p


def _small_params(l, conv_full, a):
    p = {"conv_w": conv_full[l]}
    for n in SMALL:
        w = a[n][l]
        if n in ("a_log", "dt_bias"):
            p[n] = w.reshape(DN_HEADS, 1, 1)
        elif n == "sm_b":
            p[n] = w[..., None]
        elif n == "sm_w":
            p[n] = w
        else:
            p[n] = w[None]
    return p


def _scatter_layout(n, g):
    if n in ROW_SHARDED:
        return g.reshape(N_DEV, g.shape[1] // N_DEV, g.shape[2])
    return g


def _pack(flat_parts):
    flat = jnp.concatenate(flat_parts)
    rows = -(-flat.shape[0] // 1024) * 8
    return jnp.pad(flat, (0, rows * 128 - flat.shape[0])).reshape(rows, 128)


def kernel(x, mem, ffn1_norm_pre, ffn1_w_gate_up, ffn1_w_down, ffn1_norm_post, mix_norm_pre, w_in, conv_w, a_log, dt_bias, sm_w, sm_b, sm_ln_g, sm_ln_b, dn_norm_w, w_out, mix_norm_post, xa_norm_pre, mem_norm, w_xq, w_xkv, w_xo, xa_norm_post, ffn2_norm_pre, ffn2_w_gate_up, ffn2_w_down, ffn2_norm_post, loss_target, m_ffn1_norm_pre, m_ffn1_w_gate_up, m_ffn1_w_down, m_ffn1_norm_post, m_mix_norm_pre, m_w_in, m_conv_w, m_a_log, m_dt_bias, m_sm_w, m_sm_b, m_sm_ln_g, m_sm_ln_b, m_dn_norm_w, m_w_out, m_mix_norm_post, m_xa_norm_pre, m_mem_norm, m_w_xq, m_w_xkv, m_w_xo, m_xa_norm_post, m_ffn2_norm_pre, m_ffn2_w_gate_up, m_ffn2_w_down, m_ffn2_norm_post, v_ffn1_norm_pre, v_ffn1_w_gate_up, v_ffn1_w_down, v_ffn1_norm_post, v_mix_norm_pre, v_w_in, v_conv_w, v_a_log, v_dt_bias, v_sm_w, v_sm_b, v_sm_ln_g, v_sm_ln_b, v_dn_norm_w, v_w_out, v_mix_norm_post, v_xa_norm_pre, v_mem_norm, v_w_xq, v_w_xkv, v_w_xo, v_xa_norm_post, v_ffn2_norm_pre, v_ffn2_w_gate_up, v_ffn2_w_down, v_ffn2_norm_post):
    a = dict(locals())
    px, py, pc = _place()
    core = jnp.reshape(pc, (1,)).astype(jnp.int32)
    chip = jnp.reshape(2 * px + py, (1,)).astype(jnp.int32)
    me = 4 * px + 2 * py + pc
    xs, mems, tgt = x[0], mem[0], loss_target[0]
    conv_all = all_gather("ag_conv", [conv_w])[0]
    conv_full = conv_all.transpose(1, 2, 0, 3).reshape(N_LAYERS, CONV_W, QKV_W)
    params = [_small_params(l, conv_full, a) for l in range(N_LAYERS)]
    chain = [conv_all]

    me1 = jnp.reshape(me, (1,)).astype(jnp.int32)
    going = {}
    for l in range(N_LAYERS):
        for sb, names in AG_BEFORE.items():
            lands = [k_cast_place(f"place_l{l}{n}", a[n], l, me1, chain) for n in names]
            going[l, sb] = xfer_start(f"ag_out_l{l}{sb}", plan_gather_out, 4 * len(names), [], lands)
            chain = [going[l, sb]["token"]]

    saved = [{} for _ in range(N_LAYERS)]
    h = xs
    for l in range(N_LAYERS):
        for sb in BLOCKS:
            if sb in AG_BEFORE:
                names = AG_BEFORE[sb]
                _, got = xfer_wait(f"ag_outw_l{l}{sb}", plan_gather_out, going[l, sb], after=chain)
                passing = xfer_start(f"ag_pass_l{l}{sb}", plan_gather_pass, 3 * len(names), [], got)
                _, full = xfer_wait(f"ag_passw_l{l}{sb}", plan_gather_pass, passing)
                params[l].update(_big_params(dict(zip(names, full))))
                chain = [full[0]]
            h, saved[l][sb] = _fwd_block(l, sb, h, mems, params[l], chain)
            chain = [h]
    dy, loss = k_loss("loss", h, tgt)
    loss = lax.psum(loss[0, 0], ("x", "y", "c"))

    grads = [{} for _ in range(N_LAYERS)]
    exchange = GradExchange(a, core, chip)
    for l in reversed(range(N_LAYERS)):
        exchange.layer = l
        for sb in reversed(BLOCKS):
            dy = _bwd_block(l, sb, dy, saved[l][sb], params[l], grads[l], exchange)
    big_out = exchange.finish(dy)

    flat = [jnp.concatenate([grads[l][n].reshape(-1) for l in range(N_LAYERS)]) for n in SMALL]
    flat += [grads[l]["conv_w"].reshape(-1) for l in range(N_LAYERS)]
    gsum = k_sum8("small_sum", all_gather("ag_small", [_pack(flat)])[0]).reshape(-1)
    sizes = [a[n].size for n in SMALL]
    n_rep = sum(sizes)
    conv_g = gsum[n_rep:n_rep + N_LAYERS * CONV_W * QKV_W].reshape(N_LAYERS, CONV_W, QKV_W)
    conv_g = lax.dynamic_slice_in_dim(conv_g, me * (QKV_W // N_DEV), QKV_W // N_DEV, axis=2)
    pack_state = lambda pre: _pack([a[pre + n].reshape(-1) for n in SMALL] + [a[pre + "conv_w"].reshape(-1)])
    small_out = k_adamw_flat("small_adamw", _pack([gsum[:n_rep], conv_g.reshape(-1)]), pack_state(""), pack_state("m_"),
                             pack_state("v_"))
    small_out = [o.reshape(-1) for o in small_out]
    outs = {}
    off = 0
    for n, sz in zip(SMALL + ("conv_w",), sizes + [conv_w.size]):
        outs[n] = [o[off:off + sz].reshape(a[n].shape) for o in small_out]
        off += sz
    for n in BIG:
        outs[n] = list(big_out[n])

    return (loss, dy[None], *[outs[n][0] for n in WEIGHTS], *[outs[n][1] for n in WEIGHTS],
            *[outs[n][2] for n in WEIGHTS], *[outs[n][3] for n in WEIGHTS])
```

```python
import functools
import math

import jax
import jax.numpy as jnp
from jax import lax
from jax.experimental import pallas as pl
from jax.experimental.pallas import tpu as pltpu

F32, BF16 = jnp.float32, jnp.bfloat16
HI = lax.Precision.HIGHEST
MESH = pl.DeviceIdType.MESH

N_DEV = 8
NORM_EPS = 1e-6
GM_HEADS, GM_CHUNK = 8, 128
DN_HEADS, DN_CHUNK, DN_DIM = 8, 64, 128
XA_HEADS, XA_DIM = 4, 512
CONV_W = 4
ADAM_LR, ADAM_B1, ADAM_B2, ADAM_EPS, ADAM_WD, ADAM_STEP = 0.001, 0.9, 0.999, 1e-08, 0.01, 10

V7X_VMEM_LIMIT = 56 * 1024 * 1024
WHOLE_TILE_MAX = 1536
ROW_TILE = 256


def _cparams(n_grid):
    return pltpu.CompilerParams(dimension_semantics=("arbitrary",) * n_grid, vmem_limit_bytes=V7X_VMEM_LIMIT)


def stepk(name, fn, grid, ins, outs, carries=(), prefetch=None, fill=(), after=()):
    n_in, n_out, n_c, n_fill = len(ins), len(outs), len(carries), len(fill)
    n_pre = 0 if prefetch is None else 1

    def body(*refs):
        refs = refs[n_pre:]
        in_refs, refs = refs[:n_in], refs[n_in + n_fill + len(after):]
        out_refs, c_refs = refs[:n_out], refs[n_out:]
        if n_c:
            first = functools.reduce(jnp.logical_and, [pl.program_id(a) == 0 for a in range(len(grid))])

            @pl.when(first)
            def _():
                for r in c_refs:
                    r[...] = jnp.zeros(r.shape, r.dtype)
        res = fn(*[r[...] for r in in_refs], *[r[...] for r in c_refs])
        for r, v in zip(tuple(out_refs) + tuple(c_refs), res):
            r[...] = v.astype(r.dtype)

    in_specs = [pl.BlockSpec(bs, im) for _, bs, im in ins] + [pl.BlockSpec(memory_space=pl.ANY)] * (n_fill + len(after))
    aliases = {n_pre + n_in + i: k for i, (_, k) in enumerate(fill)}
    out_specs = [pl.BlockSpec(bs, im) for _, _, bs, im in outs]
    out_shape = [jax.ShapeDtypeStruct(s, d) for s, d, _, _ in outs]
    for s, d in carries:
        zeros = (0,) * len(s)
        out_specs.append(pl.BlockSpec(s, lambda *a, _z=zeros: _z))
        out_shape.append(jax.ShapeDtypeStruct(s, d))
    args = [a for a, _, _ in ins] + [a for a, _ in fill] + list(after)
    if prefetch is None:
        call = pl.pallas_call(body, name=name, grid=grid, in_specs=in_specs, out_specs=out_specs, out_shape=out_shape,
                              input_output_aliases=aliases, compiler_params=_cparams(len(grid)))
        return call(*args)
    spec = pltpu.PrefetchScalarGridSpec(num_scalar_prefetch=1, grid=grid, in_specs=in_specs, out_specs=out_specs)
    call = pl.pallas_call(body, name=name, grid_spec=spec, out_shape=out_shape, input_output_aliases=aliases,
                          compiler_params=_cparams(len(grid)))
    return call(prefetch, *args)


def _tile(n, pref, align=128):
    if n <= WHOLE_TILE_MAX:
        return n
    t = (pref // align) * align
    while t > align and n % t:
        t -= align
    assert n % t == 0, (n, pref)
    return t


def mm(name, a, b, mode, out_dtype, nbo=1, after=()):
    nba, ra, ca = a.shape
    nbb, rb, cb = b.shape
    if mode == "nn":
        m, k, n = ra, nba * ca, nbb * cb
        assert rb == k
    elif mode == "nt":
        m, k, n = ra, nba * ca, rb
        assert nbb * cb == k
    else:
        k, m, n = ra, nba * ca, nbb * cb
        assert rb == k
    co = n // nbo
    if mode == "nn":
        tm, tk, tn = _tile(m, 1024, 8), _tile(ca, 512), _tile(math.gcd(cb, co), 1024)
        a_spec = pl.BlockSpec((None, tm, tk), lambda i, j, kk: (kk // (ca // tk), i, kk % (ca // tk)))
        b_spec = pl.BlockSpec((None, tk, tn), lambda i, j, kk: (j // (cb // tn), kk, j % (cb // tn)))
        dims = (((1,), (0,)), ((), ()))
    elif mode == "nt":
        tm, tk, tn = _tile(m, 1024, 8), _tile(math.gcd(ca, cb), 512), _tile(co, 1024)
        a_spec = pl.BlockSpec((None, tm, tk), lambda i, j, kk: (kk // (ca // tk), i, kk % (ca // tk)))
        b_spec = pl.BlockSpec((None, tn, tk), lambda i, j, kk: (kk // (cb // tk), j, kk % (cb // tk)))
        dims = (((1,), (1,)), ((), ()))
    else:
        tm, tk, tn = _tile(ca, 1024), _tile(k, 512), _tile(math.gcd(cb, co), 1024)
        a_spec = pl.BlockSpec((None, tk, tm), lambda i, j, kk: (i // (ca // tm), kk, i % (ca // tm)))
        b_spec = pl.BlockSpec((None, tk, tn), lambda i, j, kk: (j // (cb // tn), kk, j % (cb // tn)))
        dims = (((0,), (0,)), ((), ()))
    o_spec = pl.BlockSpec((None, tm, tn), lambda i, j, kk: (j // (co // tn), i, j % (co // tn)))
    nk = k // tk

    def body(a_ref, b_ref, *rest):
        o_ref, acc_ref = rest[-2:]
        kk = pl.program_id(2)

        @pl.when(kk == 0)
        def _():
            acc_ref[...] = jnp.zeros(acc_ref.shape, F32)

        acc_ref[...] += lax.dot_general(a_ref[...].astype(BF16), b_ref[...].astype(BF16), dims,
                                        preferred_element_type=F32)

        @pl.when(kk == nk - 1)
        def _():
            o_ref[...] = acc_ref[...].astype(o_ref.dtype)

    return pl.pallas_call(
        body, name=name, grid=(m // tm, n // tn, nk),
        in_specs=[a_spec, b_spec] + [pl.BlockSpec(memory_space=pl.ANY)] * len(after), out_specs=o_spec,
        out_shape=jax.ShapeDtypeStruct((nbo, m, co), out_dtype), scratch_shapes=[pltpu.VMEM((tm, tn), F32)],
        compiler_params=_cparams(3))(a, b, *after)


def _rms(x, g):
    return x * lax.rsqrt(jnp.mean(x * x, axis=-1, keepdims=True) + NORM_EPS) * g


def _sigmoid(x):
    return 1.0 / (1.0 + jnp.exp(-x))


def _silu(x):
    return x * _sigmoid(x)


def _gelu(x):
    return 0.5 * x * (1.0 + lax.erf(x * 0.7071067811865476))


def _softplus(x):
    return jnp.maximum(x, 0.0) + jnp.log(1.0 + jnp.exp(-jnp.abs(x)))


def _split(x, n):
    w = x.shape[-1] // n
    return [x[..., h * w:(h + 1) * w] for h in range(n)]


def _sgu(ua, va, smw, smb, lng, lnb):
    c = ua[0].shape[0]
    width = len(va) * va[0].shape[-1]
    vf = [_gelu(v) for v in va]
    mu = sum(jnp.sum(v, axis=-1, keepdims=True) for v in vf) / width
    var = sum(jnp.sum(jnp.square(v - mu), axis=-1, keepdims=True) for v in vf) / width
    r = lax.rsqrt(var + NORM_EPS)
    causal = lax.broadcasted_iota(jnp.int32, (c, c), 0) >= lax.broadcasted_iota(jnp.int32, (c, c), 1)
    outs = []
    for h in range(len(ua)):
        vn = (vf[h] - mu) * r * lng[h] + lnb[h]
        w = jnp.where(causal, smw[h], 0.0)
        mixed = jnp.dot(w.astype(BF16), vn.astype(BF16), preferred_element_type=F32) + smb[h]
        outs.append(_gelu(ua[h]) * mixed)
    return outs


def _shift_rows(x, k, up):
    rows = x.shape[0]
    row = lax.broadcasted_iota(jnp.int32, x.shape, 0)
    if up:
        return jnp.where(row < rows - k, pltpu.roll(x, rows - k, 0), 0.0)
    return jnp.where(row >= k, pltpu.roll(x, k, 0), 0.0)


@functools.partial(jax.custom_vjp, nondiff_argnums=(1,))
def _delay(x, k):
    return _shift_rows(x, k, False)


def _delay_fwd(x, k):
    return _shift_rows(x, k, False), None


def _delay_bwd(k, _, g):
    return (_shift_rows(g, k, True),)


_delay.defvjp(_delay_fwd, _delay_bwd)


def _conv_silu(x, w0, w1, w2, w3):
    y = w3 * x + w2 * _delay(x, 1) + w1 * _delay(x, 2) + w0 * _delay(x, 3)
    return _silu(y)


_BDOT_DIMS = {"nn": (((2,), (1,)), ((0,), (0,))), "nt": (((2,), (2,)), ((0,), (0,))), "tn": (((1,), (1,)), ((0,), (0,)))}


def _bdot_passes(a, b, kind, passes):
    one = lambda x, y: lax.dot_general(x, y, _BDOT_DIMS[kind], preferred_element_type=F32)
    ah, bh = a.astype(BF16), b.astype(BF16)
    if passes == 1:
        return one(ah, bh)
    al, bl = (a - ah.astype(F32)).astype(BF16), (b - bh.astype(F32)).astype(BF16)
    return one(ah, bh) + (one(ah, bl) + one(al, bh))


@functools.partial(jax.custom_vjp, nondiff_argnums=(2, 3))
def bdot(a, b, kind, passes):
    return _bdot_passes(a, b, kind, passes)


def _bdot_fwd(a, b, kind, passes):
    return _bdot_passes(a, b, kind, passes), (a, b)


def _bdot_bwd(kind, passes, res, g):
    a, b = res
    if kind == "nn":
        return bdot(g, b, "nt", passes), bdot(a, g, "tn", passes)
    if kind == "nt":
        return bdot(g, b, "nn", passes), bdot(g, a, "tn", passes)
    return bdot(b, g, "nt", passes), bdot(a, g, "nn", passes)


bdot.defvjp(_bdot_fwd, _bdot_bwd)


def _dn_step(s, q, k, v, braw, araw, alog, dtb):
    nh, c, d = q.shape
    row = lax.broadcasted_iota(jnp.int32, (c, c), 0)
    col = lax.broadcasted_iota(jnp.int32, (c, c), 1)
    causal, strict = (row >= col)[None], (row > col)[None]
    lower = (row >= col).astype(F32)
    strict_f = jnp.broadcast_to((row > col).astype(F32)[None], (nh, c, c))
    eye = (row == col).astype(F32)[None]

    qn = q * lax.rsqrt(jnp.sum(q * q, axis=-1, keepdims=True) + NORM_EPS) * (d ** -0.5)
    kn = k * lax.rsqrt(jnp.sum(k * k, axis=-1, keepdims=True) + NORM_EPS)
    beta = _sigmoid(braw)
    g = -jnp.exp(alog) * _softplus(araw + dtb)
    lg = lower[None] * g
    gcum = jnp.sum(lg, axis=-1, keepdims=True)
    diff = bdot(lg, strict_f, "nn", 3)
    decay = jnp.where(causal, jnp.exp(diff), 0.0)
    bcol = jnp.sum(eye * beta, axis=-1, keepdims=True)
    kb = kn * bcol
    a = jnp.where(strict, bdot(kb, kn, "nt", 1) * decay, 0.0)
    inv = eye - a
    p = bdot(a, a, "nn", 3)
    n_fac = int(math.log2(c)) - 1
    for it in range(n_fac):
        inv = inv + bdot(inv, p, "nn", 3)
        if it < n_fac - 1:
            p = bdot(p, p, "nn", 3)
    eg = jnp.exp(gcum)
    u = bdot(inv, v * bcol, "nn", 1)
    w = bdot(inv, kb * eg, "nn", 1)
    qk = jnp.where(causal, bdot(qn, kn, "nt", 1) * decay, 0.0)
    v_new = u - bdot(w, s, "nn", 1)
    o = bdot(qn * eg, s, "nn", 1) + bdot(qk, v_new, "nn", 1)
    glast = jnp.sum(g, axis=-1, keepdims=True)
    kdec = kn * jnp.exp(glast - gcum)
    s_new = s * jnp.exp(glast) + bdot(kdec, v_new, "tn", 1)
    return s_new, o


def _ogate(o, z, w):
    return [_rms(oh, w) * _silu(zh) for oh, zh in zip(o, z)]


def _xattn(q, k, v):
    outs = []
    for qh, kh, vh in zip(q, k, v):
        s = lax.dot_general(qh.astype(BF16), kh.astype(BF16), (((1,), (1,)), ((), ())),
                            preferred_element_type=F32) * (qh.shape[-1] ** -0.5)
        s = s - jnp.max(s, axis=-1, keepdims=True)
        e = jnp.exp(s)
        p = e / jnp.sum(e, axis=-1, keepdims=True)
        outs.append(jnp.dot(p.astype(BF16), vh.astype(BF16), preferred_element_type=F32))
    return outs


def _rows(t):
    return min(ROW_TILE, t)


def k_rms_fwd(name, x, g, after=()):
    t, d = x.shape
    tm = _rows(t)
    return stepk(name, lambda xv, gv: (_rms(xv, gv),), (t // tm,),
                 [(x, (tm, d), lambda i: (i, 0)), (g, (1, d), lambda i: (0, 0))],
                 [((t, d), BF16, (tm, d), lambda i: (i, 0))], after=after)[0]


def k_rms_bwd(name, x, g, dhs, dx_res):
    t, d = x.shape
    tm = _rows(t)
    n = len(dhs)

    def fn(xv, gv, *rest):
        dh = sum(r.astype(F32) for r in rest[:n])
        dxr, dg_c = rest[n], rest[n + 1]
        _, vjp = jax.vjp(_rms, xv, gv)
        dx, dg = vjp(dh)
        return dx + dxr, dg_c + dg

    row = lambda arr: (arr, (tm, d), lambda i: (i, 0))
    return stepk(name, fn, (t // tm,), [row(x), (g, (1, d), lambda i: (0, 0))] + [row(h) for h in dhs] + [row(dx_res)],
                 [((t, d), F32, (tm, d), lambda i: (i, 0))], carries=[((1, d), F32)])


def k_post_fwd(name, x, f, g, scale):
    t, d = x.shape
    tm = _rows(t)
    row = lambda arr: (arr, (tm, d), lambda i: (i, 0))
    return stepk(name, lambda xv, fv, gv: (xv + scale * _rms(fv, gv),), (t // tm,),
                 [row(x), row(f), (g, (1, d), lambda i: (0, 0))], [((t, d), F32, (tm, d), lambda i: (i, 0))])[0]


def k_post_bwd(name, f, g, dxo, scale, after=()):
    t, d = f.shape
    tm = _rows(t)

    def fn(fv, gv, dv, dg_c):
        _, vjp = jax.vjp(lambda a, b: scale * _rms(a, b), fv, gv)
        df, dg = vjp(dv)
        return df, dg_c + dg

    row = lambda arr: (arr, (tm, d), lambda i: (i, 0))
    return stepk(name, fn, (t // tm,), [row(f), (g, (1, d), lambda i: (0, 0)), row(dxo)],
                 [((t, d), BF16, (tm, d), lambda i: (i, 0))], carries=[((1, d), F32)], after=after)


def k_swiglu_fwd(name, hu):
    nb, t, c = hu.shape
    half = nb // 2
    tm = _rows(t)
    fn = lambda gv, uv: (_silu(gv.astype(F32)) * uv.astype(F32),)
    return stepk(name, fn, (half, t // tm),
                 [(hu, (None, tm, c), lambda b, i: (b, i, 0)), (hu, (None, tm, c), lambda b, i: (b + half, i, 0))],
                 [((half, t, c), BF16, (None, tm, c), lambda b, i: (b, i, 0))])[0]


def k_swiglu_bwd(name, hu, da):
    nb, t, c = hu.shape
    half = nb // 2
    tm = _rows(t)

    def fn(gv, uv, dv):
        gv, uv, dv = gv.astype(F32), uv.astype(F32), dv.astype(F32)
        sg = _sigmoid(gv)
        return (jnp.stack([dv * uv * sg * (1.0 + gv * (1.0 - sg)), dv * gv * sg]),)

    blk = lambda arr, off: (arr, (None, tm, c), lambda b, i: (b + off, i, 0))
    out = stepk(name, fn, (half, t // tm), [blk(hu, 0), blk(hu, half), blk(da, 0)],
                [((2, half, t, c), BF16, (2, None, tm, c), lambda b, i: (0, b, i, 0))])[0]
    return out.reshape(nb, t, c)


def k_loss(name, y, tgt):
    t, d = y.shape
    tm = _rows(t)

    def fn(yv, tv, acc):
        e = yv - tv
        part = jnp.sum(jnp.sum(e * e, axis=-1, keepdims=True), axis=0, keepdims=True)
        return e * (1.0 / d), acc + (0.5 / d) * part

    row = lambda arr: (arr, (tm, d), lambda i: (i, 0))
    return stepk(name, fn, (t // tm,), [row(y), row(tgt)], [((t, d), F32, (tm, d), lambda i: (i, 0))],
                 carries=[((1, 1), F32)])


GM_W = GM_HEADS * 128
QKV_COL0 = 2 * GM_W
QKV_W = 3 * DN_HEADS * DN_DIM
Z_COL0 = QKV_COL0 + QKV_W
MAIN_W = Z_COL0 + DN_HEADS * DN_DIM


def _sgu_ins(proj, p):
    c = GM_CHUNK
    return [(proj, (c, GM_W), lambda i: (i, 0)), (proj, (c, GM_W), lambda i: (i, 1)),
            (p["sm_w"], (GM_HEADS, c, c), lambda i: (0, 0, 0)), (p["sm_b"], (GM_HEADS, c, 1), lambda i: (0, 0, 0)),
            (p["sm_ln_g"], (1, GM_W), lambda i: (0, 0)), (p["sm_ln_b"], (1, GM_W), lambda i: (0, 0))]


def _sgu_lists(uv, vv, sw, sb, lg, lb):
    nh = GM_HEADS
    return (_split(uv, nh), _split(vv, nh), [sw[h] for h in range(nh)], [sb[h] for h in range(nh)],
            _split(lg, nh), _split(lb, nh))


def k_sgu_fwd(name, proj, p):
    t = proj.shape[0]

    def fn(*vals):
        return (jnp.concatenate(_sgu(*_sgu_lists(*vals)), axis=-1),)

    return stepk(name, fn, (t // GM_CHUNK,), _sgu_ins(proj, p),
                 [((t, GM_W), BF16, (GM_CHUNK, GM_W), lambda i: (i, 0))])[0]


def k_sgu_bwd(name, proj, p, dy):
    t = proj.shape[0]
    c = GM_CHUNK

    def fn(uv, vv, sw, sb, lg, lb, dv, dsw, dsb, dlg, dlb):
        _, vjp = jax.vjp(_sgu, *_sgu_lists(uv, vv, sw, sb, lg, lb))
        gu, gv, gsw, gsb, glg, glb = vjp(_split(dv.astype(F32), GM_HEADS))
        cat = lambda l: jnp.concatenate(l, axis=-1)
        return (cat(gu), cat(gv), dsw + jnp.stack(gsw), dsb + jnp.stack(gsb), dlg + cat(glg), dlb + cat(glb))

    return stepk(name, fn, (t // c,), _sgu_ins(proj, p) + [(dy, (None, c, GM_W), lambda i: (0, i, 0))],
                 [((t, GM_W), BF16, (c, GM_W), lambda i: (i, 0)), ((t, GM_W), BF16, (c, GM_W), lambda i: (i, 0))],
                 carries=[((GM_HEADS, c, c), F32), ((GM_HEADS, c, 1), F32), ((1, GM_W), F32), ((1, GM_W), F32)])


def _conv_ins(proj, conv_w):
    t = proj.shape[0]
    return [(proj, (t, 128), lambda j: (0, QKV_COL0 // 128 + j)), (conv_w, (CONV_W, 128), lambda j: (0, j))]


def k_conv_fwd(name, proj, conv_w):
    t = proj.shape[0]
    n = QKV_W // 128
    fn = lambda xv, wv: (_conv_silu(xv, *[wv[i:i + 1] for i in range(CONV_W)]),)
    return stepk(name, fn, (n,), _conv_ins(proj, conv_w), [((n, t, 128), F32, (None, t, 128), lambda j: (j, 0, 0))])[0]


def k_conv_bwd(name, proj, conv_w, dqkv):
    t = proj.shape[0]
    n = QKV_W // 128

    def fn(xv, wv, dv):
        _, vjp = jax.vjp(_conv_silu, xv, *[wv[i:i + 1] for i in range(CONV_W)])
        gx, *gw = vjp(dv)
        return gx, jnp.concatenate(gw, axis=0)

    return stepk(name, fn, (n,), _conv_ins(proj, conv_w) + [(dqkv, (None, t, 128), lambda j: (j, 0, 0))],
                 [((t, QKV_W), BF16, (t, 128), lambda j: (0, j)), ((CONV_W, QKV_W), F32, (CONV_W, 128), lambda j: (0, j))])


def _dn_ins(qkv, braw, araw, p, order):
    h, c, d = DN_HEADS, DN_CHUNK, DN_DIM
    qkv_in = lambda part: (qkv, (h, c, d), lambda n: (part, order(n), 0))
    gate_in = lambda arr: (arr, (None, h, 1, c), lambda n: (order(n), 0, 0, 0))
    par_in = lambda arr: (arr, (h, 1, 1), lambda n: (0, 0, 0))
    return [qkv_in(0), qkv_in(1), qkv_in(2), gate_in(braw), gate_in(araw), par_in(p["a_log"]), par_in(p["dt_bias"])]


def k_dn_fwd(name, qkv, braw, araw, p):
    t = qkv.shape[1]
    h, c, d = DN_HEADS, DN_CHUNK, DN_DIM
    nc = t // c

    def fn(q, k, v, b, a, al, dt, s):
        s_new, o = _dn_step(s, q, k, v, b, a, al, dt)
        return o, s, s_new

    o, s_all, _ = stepk(name, fn, (nc,), _dn_ins(qkv, braw, araw, p, lambda n: n),
                        [((h, t, d), F32, (h, c, d), lambda n: (0, n, 0)),
                         ((nc, h, d, d), F32, (None, h, d, d), lambda n: (n, 0, 0, 0))],
                        carries=[((h, d, d), F32)])
    return o, s_all


def k_dn_bwd(name, qkv, braw, araw, p, s_all, do):
    t = qkv.shape[1]
    h, c, d = DN_HEADS, DN_CHUNK, DN_DIM
    nc = t // c
    rev = lambda n: nc - 1 - n

    def fn(q, k, v, b, a, al, dt, s, dov, ds_c, dal_c, ddt_c):
        _, vjp = jax.vjp(_dn_step, s, q, k, v, b, a, al, dt)
        ds, dq, dk, dv, db, da, dal, ddt = vjp((ds_c, dov))
        return dq, dk, dv, db, da, ds, dal_c + dal, ddt_c + ddt

    ins = _dn_ins(qkv, braw, araw, p, rev) + [(s_all, (None, h, d, d), lambda n: (rev(n), 0, 0, 0)),
                                               (do, (h, c, d), lambda n: (0, rev(n), 0))]
    hd = ((h, t, d), F32, (h, c, d), lambda n: (0, rev(n), 0))
    gate = ((nc, h, 1, c), F32, (None, h, 1, c), lambda n: (rev(n), 0, 0, 0))
    dq, dk, dv, db, da, _, dal, ddt = stepk(name, fn, (nc,), ins, [hd, hd, hd, gate, gate],
                                            carries=[((h, d, d), F32), ((h, 1, 1), F32), ((h, 1, 1), F32)])
    return dq, dk, dv, db, da, dal, ddt


def _ogate_ins(o, proj, p):
    t = o.shape[1]
    tm = _rows(t)
    return tm, [(o, (DN_HEADS, tm, DN_DIM), lambda i: (0, i, 0)), (proj, (tm, GM_W), lambda i: (i, Z_COL0 // GM_W)),
                (p["dn_norm_w"], (1, DN_DIM), lambda i: (0, 0))]


def k_ogate_fwd(name, o, proj, p):
    t = o.shape[1]
    tm, ins = _ogate_ins(o, proj, p)

    def fn(ov, zv, wv):
        return (jnp.concatenate(_ogate([ov[h] for h in range(DN_HEADS)], _split(zv, DN_HEADS), wv), axis=-1),)

    return stepk(name, fn, (t // tm,), ins, [((t, GM_W), BF16, (tm, GM_W), lambda i: (i, 0))])[0]


def k_ogate_bwd(name, o, proj, p, dy):
    t = o.shape[1]
    tm, ins = _ogate_ins(o, proj, p)

    def fn(ov, zv, wv, dv, dw_c):
        _, vjp = jax.vjp(_ogate, [ov[h] for h in range(DN_HEADS)], _split(zv, DN_HEADS), wv)
        go, gz, gw = vjp(_split(dv.astype(F32), DN_HEADS))
        return jnp.stack(go), jnp.concatenate(gz, axis=-1), dw_c + gw

    return stepk(name, fn, (t // tm,), ins + [(dy, (None, tm, GM_W), lambda i: (1, i, 0))],
                 [((DN_HEADS, t, DN_DIM), F32, (DN_HEADS, tm, DN_DIM), lambda i: (0, i, 0)),
                  ((t, GM_W), BF16, (tm, GM_W), lambda i: (i, 0))], carries=[((1, DN_DIM), F32)])


def _xattn_lists(qv, kvv):
    nh = XA_HEADS
    return (_split(qv.astype(F32), nh), [kvv[h].astype(F32) for h in range(nh)],
            [kvv[nh + h].astype(F32) for h in range(nh)])


def k_xattn_fwd(name, q, kv):
    t, d = q.shape
    tm = _rows(t)
    fn = lambda qv, kvv: (jnp.concatenate(_xattn(*_xattn_lists(qv, kvv)), axis=-1),)
    return stepk(name, fn, (t // tm,), [(q, (tm, d), lambda i: (i, 0)), (kv, kv.shape, lambda i: (0, 0, 0))],
                 [((t, d), BF16, (tm, d), lambda i: (i, 0))])[0]


def k_xattn_bwd(name, q, kv, do):
    t, d = q.shape
    tm = _rows(t)

    def fn(qv, kvv, dv, dkv_c):
        _, vjp = jax.vjp(_xattn, *_xattn_lists(qv, kvv))
        gq, gk, gv = vjp(_split(dv.astype(F32), XA_HEADS))
        return jnp.concatenate(gq, axis=-1), dkv_c + jnp.stack(gk + gv)

    return stepk(name, fn, (t // tm,),
                 [(q, (tm, d), lambda i: (i, 0)), (kv, kv.shape, lambda i: (0, 0, 0)), (do, (None, tm, d), lambda i: (0, i, 0))],
                 [((t, d), BF16, (tm, d), lambda i: (i, 0))], carries=[(kv.shape, F32)])


def ffn_fwd(tag, x, p, pre, post, gu, dn, after=()):
    h = k_rms_fwd(f"{tag}_pre", x, p[pre], after)
    hu = mm(f"{tag}_gu", h[None], p[gu], "nn", BF16, nbo=N_DEV)
    a = k_swiglu_fwd(f"{tag}_act", hu)
    f = mm(f"{tag}_down", a, p[dn], "nn", F32)[0]
    y = k_post_fwd(f"{tag}_post", x, f, p[post], 0.5)
    return y, (x, h, hu, a, f)


class NoExchange:
    def take(self):
        return ()

    def emit(self, grads, names):
        pass


def ffn_bwd(tag, dy, saved, p, pre, post, gu, dn, grads, sch):
    x, h, hu, a, f = saved
    df, grads[post] = k_post_bwd(f"{tag}_post_b", f, p[post], dy, 0.5, sch.take())
    grads[dn] = mm(f"{tag}_down_dw", a, df[None], "tn", BF16)
    sch.emit(grads, (dn,))
    da = mm(f"{tag}_down_dx", df[None], p[dn], "nt", BF16, nbo=N_DEV // 2, after=sch.take())
    dhu = k_swiglu_bwd(f"{tag}_act_b", hu, da)
    grads[gu] = mm(f"{tag}_gu_dw", h[None], dhu, "tn", BF16, nbo=N_DEV)
    sch.emit(grads, (gu,))
    dh = mm(f"{tag}_gu_dx", dhu, p[gu], "nt", BF16, after=sch.take())[0]
    dx, grads[pre] = k_rms_bwd(f"{tag}_pre_b", x, p[pre], [dh], dy)
    return dx


def _to_chunks(a):
    t, h = a.shape
    return a.reshape(t // DN_CHUNK, DN_CHUNK, h).transpose(0, 2, 1).reshape(t // DN_CHUNK, h, 1, DN_CHUNK)


def _from_chunks(a):
    nc, h, _, c = a.shape
    return a.reshape(nc, h, c).transpose(0, 2, 1).reshape(nc * c, h)


def mix_fwd(tag, x, p, after=()):
    h = k_rms_fwd(f"{tag}_pre", x, p["mix_norm_pre"], after)
    proj = mm(f"{tag}_in", h[None], p["w_in_main"], "nn", F32)[0]
    small = mm(f"{tag}_in_s", h[None], p["w_in_small"], "nn", F32)[0]
    braw, araw = _to_chunks(small[:, :DN_HEADS]), _to_chunks(small[:, DN_HEADS:2 * DN_HEADS])
    ya = k_sgu_fwd(f"{tag}_sgu", proj, p)
    qkv = k_conv_fwd(f"{tag}_conv", proj, p["conv_w"])
    o, s_all = k_dn_fwd(f"{tag}_dn", qkv, braw, araw, p)
    yb = k_ogate_fwd(f"{tag}_og", o, proj, p)
    y = jnp.stack([ya, yb])
    m = mm(f"{tag}_out", y, p["w_out"], "nn", F32)[0]
    out = k_post_fwd(f"{tag}_post", x, m, p["mix_norm_post"], 1.0)
    return out, (x, h, proj, braw, araw, qkv, o, s_all, y, m)


def mix_bwd(tag, dy, saved, p, grads, sch):
    x, h, proj, braw, araw, qkv, o, s_all, y, m = saved
    t = x.shape[0]
    dm, grads["mix_norm_post"] = k_post_bwd(f"{tag}_post_b", m, p["mix_norm_post"], dy, 1.0, sch.take())
    grads["w_out"] = mm(f"{tag}_out_dw", y, dm[None], "tn", BF16)
    sch.emit(grads, ("w_out",))
    dyy = mm(f"{tag}_out_dx", dm[None], p["w_out"], "nt", BF16, nbo=2, after=sch.take())
    do, dz, grads["dn_norm_w"] = k_ogate_bwd(f"{tag}_og_b", o, proj, p, dyy)
    dq, dk, dv, db, da, grads["a_log"], grads["dt_bias"] = k_dn_bwd(f"{tag}_dn_b", qkv, braw, araw, p, s_all, do)
    dconv, grads["conv_w"] = k_conv_bwd(f"{tag}_conv_b", proj, p["conv_w"], jnp.concatenate([dq, dk, dv], axis=0))
    du, dva, grads["sm_w"], grads["sm_b"], grads["sm_ln_g"], grads["sm_ln_b"] = k_sgu_bwd(f"{tag}_sgu_b", proj, p, dyy)
    dproj = jnp.concatenate([du, dva, dconv, dz], axis=-1)
    dsmall = jnp.concatenate([_from_chunks(db), _from_chunks(da), jnp.zeros((t, 128 - 2 * DN_HEADS), F32)], axis=-1)
    g_main = mm(f"{tag}_in_dw", h[None], dproj[None], "tn", BF16)[0]
    g_small = mm(f"{tag}_in_s_dw", h[None], dsmall[None], "tn", BF16)[0][:, :2 * DN_HEADS]
    g_in = jnp.concatenate([g_main, g_small], axis=-1)
    grads["w_in"] = g_in.reshape(g_in.shape[0], N_DEV, g_in.shape[1] // N_DEV).transpose(1, 0, 2)
    sch.emit(grads, ("w_in",))
    dh = mm(f"{tag}_in_dx", dproj[None], p["w_in_main"], "nt", BF16, after=sch.take())[0]
    dh_s = mm(f"{tag}_in_s_dx", dsmall[None], p["w_in_small"], "nt", BF16)[0]
    dx, grads["mix_norm_pre"] = k_rms_bwd(f"{tag}_pre_b", x, p["mix_norm_pre"], [dh, dh_s], dy)
    return dx


def xa_fwd(tag, x, mem, p, after=()):
    hx = k_rms_fwd(f"{tag}_pre", x, p["xa_norm_pre"], after)
    mh = k_rms_fwd(f"{tag}_mem", mem, p["mem_norm"], after)
    q = mm(f"{tag}_q", hx[None], p["w_xq"], "nn", BF16)[0]
    kv = mm(f"{tag}_kv", mh[None], p["w_xkv"], "nn", BF16, nbo=N_DEV)
    o = k_xattn_fwd(f"{tag}_att", q, kv)
    c = mm(f"{tag}_o", o[None], p["w_xo"], "nn", F32)[0]
    out = k_post_fwd(f"{tag}_post", x, c, p["xa_norm_post"], 1.0)
    return out, (x, mem, hx, mh, q, kv, o, c)


def xa_bwd(tag, dy, saved, p, grads, sch):
    x, mem, hx, mh, q, kv, o, c = saved
    dc, grads["xa_norm_post"] = k_post_bwd(f"{tag}_post_b", c, p["xa_norm_post"], dy, 1.0, sch.take())
    grads["w_xo"] = mm(f"{tag}_o_dw", o[None], dc[None], "tn", BF16)
    sch.emit(grads, ("w_xo",))
    do = mm(f"{tag}_o_dx", dc[None], p["w_xo"], "nt", BF16, after=sch.take())
    dq, dkv = k_xattn_bwd(f"{tag}_att_b", q, kv, do)
    dkv16 = dkv.astype(BF16)
    grads["w_xq"] = mm(f"{tag}_q_dw", hx[None], dq[None], "tn", BF16)
    grads["w_xkv"] = mm(f"{tag}_kv_dw", mh[None], dkv16, "tn", BF16, nbo=N_DEV)
    sch.emit(grads, ("w_xq", "w_xkv"))
    dhx = mm(f"{tag}_q_dx", dq[None], p["w_xq"], "nt", BF16, after=sch.take())[0]
    dmh = mm(f"{tag}_kv_dx", dkv16, p["w_xkv"], "nt", BF16)[0]
    _, grads["mem_norm"] = k_rms_bwd(f"{tag}_mem_b", mem, p["mem_norm"], [dmh], jnp.zeros_like(mem))
    dx, grads["xa_norm_pre"] = k_rms_bwd(f"{tag}_pre_b", x, p["xa_norm_pre"], [dhx], dy)
    return dx


def _place():
    return lax.axis_index("x"), lax.axis_index("y"), lax.axis_index("c")


def _other_chips(x, y):
    return [(1 - x, y), (x, 1 - y), (1 - x, 1 - y)]


_ANY = pl.BlockSpec(memory_space=pl.ANY)


def all_gather(name, shards):
    n = len(shards)

    def body(*refs):
        ins, outs = refs[:n], refs[n:2 * n]
        send_sems, recv_sems, local_sems = refs[2 * n:]
        x, y, c = _place()
        me, sibling = (x, y, c), (x, y, 1 - c)
        chips = _other_chips(x, y)
        idx = lambda px, py, pc: 4 * px + 2 * py + pc

        def copy(a, k, block, to, src=None):
            dst = outs[a].at[idx(*block)]
            return pltpu.make_async_remote_copy(src_ref=dst if src is None else src, dst_ref=dst,
                                                send_sem=send_sems.at[7 * a + k], recv_sem=recv_sems.at[7 * a + k],
                                                device_id=to, device_id_type=MESH)

        mine = [pltpu.make_async_copy(ins[a], outs[a].at[idx(*me)], local_sems.at[a]) for a in range(n)]
        for cp in mine:
            cp.start()
        first = []
        for a in range(n):
            first.append(copy(a, 0, me, sibling, src=ins[a]))
            first += [copy(a, 1 + j, me, (*chip, c), src=ins[a]) for j, chip in enumerate(chips)]
        for cp in first:
            cp.start()
        passed = []
        for a in range(n):
            for j, chip in enumerate(chips):
                copy(a, 1 + j, (*chip, c), me).wait_recv()
                passed.append(copy(a, 4 + j, (*chip, c), sibling))
                passed[-1].start()
        for a in range(n):
            copy(a, 0, sibling, me).wait_recv()
            for j, chip in enumerate(chips):
                copy(a, 4 + j, (*chip, 1 - c), me).wait_recv()
        for cp in first + passed:
            cp.wait_send()
        for cp in mine:
            cp.wait()

    return pl.pallas_call(
        body, name=name, in_specs=[_ANY] * n, out_specs=[_ANY] * n,
        out_shape=[jax.ShapeDtypeStruct((N_DEV,) + s.shape, s.dtype) for s in shards],
        scratch_shapes=[pltpu.SemaphoreType.DMA((7 * n,)), pltpu.SemaphoreType.DMA((7 * n,)),
                        pltpu.SemaphoreType.DMA((n,))])(*shards)


def k_cast_place(name, w, layer, me, after):
    _, r, cc = w.shape
    tr = _row_tile(r, cc)
    return stepk(name, lambda v: (v,), (r // tr,), [(w, (None, tr, cc), lambda i, pre: (layer, i, 0))],
                 [((N_DEV, r, cc), BF16, (None, tr, cc), lambda i, pre: (pre[0], i, 0))], prefetch=me, after=after)[0]


_HBM = pl.BlockSpec(memory_space=pltpu.HBM)
_SEM = pl.BlockSpec(memory_space=pltpu.SEMAPHORE)
_DATAFLOW = pltpu.SideEffectType.DATAFLOW_SIDE_EFFECTING


def xfer_start(name, plan, n_sems, srcs, lands, after=()):
    ns, nl, na = len(srcs), len(lands), len(after)
    bufs = list(srcs) + list(lands)

    def body(*refs):
        send_sems, recv_sems, token = refs[ns + nl + na], refs[ns + nl + na + 1], refs[-1]
        for cp in plan(refs[:ns], refs[ns:ns + nl], send_sems, recv_sems):
            cp.start()
        token[...] = jnp.zeros(token.shape, token.dtype)

    res = pl.pallas_call(
        body, name=name, in_specs=[_HBM] * (ns + nl) + [_ANY] * na,
        out_specs=(_SEM, _SEM, *[_HBM] * (ns + nl), pl.BlockSpec(memory_space=pltpu.VMEM)),
        out_shape=(pltpu.SemaphoreType.DMA((n_sems,)), pltpu.SemaphoreType.DMA((n_sems,)),
                   *[pltpu.HBM(b.shape, b.dtype) for b in bufs], jax.ShapeDtypeStruct((8, 128), F32)),
        input_output_aliases={i: 2 + i for i in range(ns + nl)},
        compiler_params=pltpu.CompilerParams(has_side_effects=_DATAFLOW),
    )(*[pltpu.with_memory_space_constraint(b, pltpu.HBM) for b in bufs], *after)
    return dict(send=res[0], recv=res[1], srcs=list(res[2:2 + ns]), lands=list(res[2 + ns:2 + ns + nl]), token=res[-1])


def xfer_wait(name, plan, started, after=()):
    ns, nl = len(started["srcs"]), len(started["lands"])
    bufs = started["srcs"] + started["lands"]

    def body(*refs):
        for cp in plan(refs[:ns], refs[ns:ns + nl], refs[ns + nl], refs[ns + nl + 1]):
            cp.wait_send()
            cp.wait_recv()

    res = pl.pallas_call(
        body, name=name, in_specs=[_HBM] * (ns + nl) + [_SEM, _SEM] + [_ANY] * len(after),
        out_specs=tuple([_HBM] * (ns + nl)), out_shape=tuple(pltpu.HBM(b.shape, b.dtype) for b in bufs),
        input_output_aliases={i: i for i in range(ns + nl)},
        compiler_params=pltpu.CompilerParams(has_side_effects=_DATAFLOW),
    )(*bufs, started["send"], started["recv"], *after)
    return list(res[:ns]), list(res[ns:])


def _remote(src, dst, send_sems, recv_sems, k, to):
    return pltpu.make_async_remote_copy(src_ref=src, dst_ref=dst, send_sem=send_sems.at[k], recv_sem=recv_sems.at[k],
                                        device_id=to, device_id_type=MESH)


def plan_gather_out(srcs, lands, send_sems, recv_sems):
    x, y, c = _place()
    me = 4 * x + 2 * y + c
    targets = [(x, y, 1 - c)] + [(px, py, c) for px, py in _other_chips(x, y)]
    return [_remote(lands[a].at[me], lands[a].at[me], send_sems, recv_sems, 4 * a + k, to)
            for a in range(len(lands)) for k, to in enumerate(targets)]


def plan_gather_pass(srcs, lands, send_sems, recv_sems):
    x, y, c = _place()
    return [_remote(lands[a].at[4 * px + 2 * py + c], lands[a].at[4 * px + 2 * py + c], send_sems, recv_sems,
                    3 * a + j, (x, y, 1 - c))
            for a in range(len(lands)) for j, (px, py) in enumerate(_other_chips(x, y))]


def plan_scatter_sibling(srcs, lands, send_sems, recv_sems):
    x, y, c = _place()
    return [_remote(srcs[a].at[j, 1 - c], lands[a].at[j], send_sems, recv_sems, 4 * a + j, (x, y, 1 - c))
            for a in range(len(srcs)) for j in range(4)]


def plan_scatter_chips(srcs, lands, send_sems, recv_sems):
    x, y, c = _place()
    return [_remote(srcs[a].at[2 * px + py], lands[a].at[j], send_sems, recv_sems, 3 * a + j, (px, py, c))
            for a in range(len(srcs)) for j, (px, py) in enumerate(_other_chips(x, y))]


ELEMWISE_BLOCK = 512 * 1024


def _row_tile(r, cc, limit=ELEMWISE_BLOCK):
    t = r
    while t * cc > limit and t % 32 == 0:
        t //= 2
    if t * cc > limit:
        for cand in range(t, 15, -16):
            if r % cand == 0 and cand * cc <= limit:
                return cand
    return t


def k_pair_add(name, g4, r1, core):
    _, _, r, cc = g4.shape
    tr = _row_tile(r, cc, 2 * ELEMWISE_BLOCK)
    fn = lambda av, bv: (av.astype(F32) + bv.astype(F32),)
    return stepk(name, fn, (4, r // tr),
                 [(g4, (None, None, tr, cc), lambda j, i, pre: (j, pre[0], i, 0)),
                  (r1, (None, tr, cc), lambda j, i, pre: (j, i, 0))],
                 [((4, r, cc), BF16, (None, tr, cc), lambda j, i, pre: (j, i, 0))], prefetch=core)[0]


def _adamw(g, w, m, v):
    m2 = ADAM_B1 * m + (1.0 - ADAM_B1) * g
    v2 = ADAM_B2 * v + (1.0 - ADAM_B2) * jnp.square(g)
    m_hat = m2 / (1.0 - ADAM_B1 ** ADAM_STEP)
    v_hat = v2 / (1.0 - ADAM_B2 ** ADAM_STEP)
    delta = -ADAM_LR * (m_hat / (jnp.sqrt(v_hat) + ADAM_EPS) + ADAM_WD * w)
    return g, delta, m2, v2


def k_adamw_shard(name, part, others, w, m, v, layer, chip, fill):
    _, r, cc = part.shape
    tr = _row_tile(r, cc)

    def fn(pv, o0, o1, o2, wv, mv, vv):
        g = ((pv.astype(F32) + o0.astype(F32)) + o1.astype(F32)) + o2.astype(F32)
        return _adamw(g, wv, mv, vv)

    other = lambda k: (others, (None, tr, cc), lambda i, pre: (k, i, 0))
    state = lambda arr: (arr, (None, tr, cc), lambda i, pre: (layer, i, 0))
    out = ((2, r, cc), F32, (None, tr, cc), lambda i, pre: (layer, i, 0))
    return stepk(name, fn, (r // tr,),
                 [(part, (None, tr, cc), lambda i, pre: (pre[0], i, 0)), other(0), other(1), other(2),
                  state(w), state(m), state(v)],
                 [out] * 4, prefetch=chip, fill=[(f, k) for k, f in enumerate(fill)])


def k_sum8(name, parts):
    _, rows, lanes = parts.shape

    def fn(pv):
        acc = pv[0]
        for d in range(1, N_DEV):
            acc = acc + pv[d]
        return (acc,)

    return stepk(name, fn, (1,), [(parts, parts.shape, lambda i: (0, 0, 0))],
                 [((rows, lanes), F32, (rows, lanes), lambda i: (0, 0))])[0]


def k_adamw_flat(name, g, w, m, v):
    whole = lambda arr: (arr, arr.shape, lambda i: (0, 0))
    out = (g.shape, F32, g.shape, lambda i: (0, 0))
    return stepk(name, _adamw, (1,), [whole(g), whole(w), whole(m), whole(v)], [out] * 4)


WEIGHTS = ("ffn1_norm_pre", "ffn1_w_gate_up", "ffn1_w_down", "ffn1_norm_post", "mix_norm_pre", "w_in", "conv_w", "a_log",
           "dt_bias", "sm_w", "sm_b", "sm_ln_g", "sm_ln_b", "dn_norm_w", "w_out", "mix_norm_post", "xa_norm_pre", "mem_norm",
           "w_xq", "w_xkv", "w_xo", "xa_norm_post", "ffn2_norm_pre", "ffn2_w_gate_up", "ffn2_w_down", "ffn2_norm_post")
BIG = ("ffn1_w_gate_up", "ffn1_w_down", "w_in", "w_out", "w_xq", "w_xkv", "w_xo", "ffn2_w_gate_up", "ffn2_w_down")
ROW_SHARDED = ("ffn1_w_down", "w_out", "w_xq", "w_xo", "ffn2_w_down")
SMALL = tuple(n for n in WEIGHTS if n not in BIG and n != "conv_w")
N_LAYERS = 2
IN_COLS = MAIN_W + 2 * DN_HEADS


BLOCKS = ("f1", "mx", "xa", "f2")
AG_BEFORE = {"f1": ("ffn1_w_gate_up", "ffn1_w_down"), "mx": ("w_in", "w_out"),
             "xa": ("w_xq", "w_xkv", "w_xo", "ffn2_w_gate_up", "ffn2_w_down")}
RS_IN_FLIGHT = 1


def _fwd_block(l, sb, h, mem, p, after):
    tag = f"l{l}{sb}"
    if sb == "f1":
        return ffn_fwd(tag, h, p, "ffn1_norm_pre", "ffn1_norm_post", "ffn1_w_gate_up", "ffn1_w_down", after)
    if sb == "mx":
        return mix_fwd(tag, h, p, after)
    if sb == "xa":
        return xa_fwd(tag, h, mem, p, after)
    return ffn_fwd(tag, h, p, "ffn2_norm_pre", "ffn2_norm_post", "ffn2_w_gate_up", "ffn2_w_down", after)


def _bwd_block(l, sb, dy, saved, p, grads, sch):
    tag = f"l{l}{sb}"
    if sb == "f1":
        return ffn_bwd(tag, dy, saved, p, "ffn1_norm_pre", "ffn1_norm_post", "ffn1_w_gate_up", "ffn1_w_down", grads, sch)
    if sb == "mx":
        return mix_bwd(tag, dy, saved, p, grads, sch)
    if sb == "xa":
        return xa_bwd(tag, dy, saved, p, grads, sch)
    return ffn_bwd(tag, dy, saved, p, "ffn2_norm_pre", "ffn2_norm_post", "ffn2_w_gate_up", "ffn2_w_down", grads, sch)


class GradExchange:
    def __init__(self, a, core, chip):
        self.a, self.core, self.chip = a, core, chip
        self.layer = None
        self.pending, self.tie, self.out = [], [], {}

    def take(self):
        tie, self.tie = self.tie, []
        return tie

    def _advance(self, chain, drain):
        a = self.a
        between_chips = [grp for grp in self.pending if grp["stage"] == "chips"]
        for grp in self.pending:
            if grp["stage"] == "sibling":
                l, names = grp["l"], grp["names"]
                g4, r1 = xfer_wait(f"rs_sibw_l{l}{names[0]}", plan_scatter_sibling, grp["going"], after=chain)
                parts = [k_pair_add(f"rs_add_l{l}{n}", gg, rr, self.core) for n, gg, rr in zip(names, g4, r1)]
                zones = [lax.empty((3,) + p.shape[1:], BF16) for p in parts]
                grp["going"] = xfer_start(f"rs_chip_l{l}{names[0]}", plan_scatter_chips, 3 * len(names), parts, zones)
                grp["stage"], grp["age"] = "chips", 0
                chain = [grp["going"]["token"]]
        for grp in between_chips:
            l, names = grp["l"], grp["names"]
            if drain or grp["age"] >= RS_IN_FLIGHT:
                parts, r2 = xfer_wait(f"rs_chipw_l{l}{names[0]}", plan_scatter_chips, grp["going"], after=chain)
                for n, part, others in zip(names, parts, r2):
                    self.out[n] = k_adamw_shard(f"adamw_{n}_l{l}", part, others, a[n], a["m_" + n], a["v_" + n], l,
                                                self.chip, self.out.get(n, ()))
                self.pending.remove(grp)
            else:
                grp["age"] += 1
        return chain

    def emit(self, grads, names):
        l = self.layer
        chain = self._advance([grads[names[-1]]], drain=False)
        g4 = [_scatter_layout(n, grads[n]) for n in names]
        g4 = [t.reshape(4, 2, *t.shape[1:]) for t in g4]
        zones = [lax.empty((4,) + t.shape[2:], BF16) for t in g4]
        going = xfer_start(f"rs_sib_l{l}{names[0]}", plan_scatter_sibling, 4 * len(names), g4, zones, after=chain)
        self.pending.append(dict(l=l, names=names, stage="sibling", going=going, age=0))
        self.tie = [going["token"]]

    def finish(self, last):
        chain = [last]
        while self.pending:
            chain = self._advance(chain, drain=True)
        return self.out


def _big_params(full):
    p = {}
    for n, w in full.items():
        if n in ROW_SHARDED:
            p[n] = w.reshape(1, w.shape[0] * w.shape[1], w.shape[2])
        elif n == "w_in":
            w = w.transpose(1, 0, 2).reshape(w.shape[1], IN_COLS)
            p["w_in_main"] = w[None, :, :MAIN_W]
            p["w_in_small"] = jnp.pad(w[:, MAIN_W:], ((0, 0), (0, 128 - 2 * DN_HEADS)))[None]
        else:
            p[n] = w
    return p


def _small_params(l, conv_full, a):
    p = {"conv_w": conv_full[l]}
    for n in SMALL:
        w = a[n][l]
        if n in ("a_log", "dt_bias"):
            p[n] = w.reshape(DN_HEADS, 1, 1)
        elif n == "sm_b":
            p[n] = w[..., None]
        elif n == "sm_w":
            p[n] = w
        else:
            p[n] = w[None]
    return p


def _scatter_layout(n, g):
    if n in ROW_SHARDED:
        return g.reshape(N_DEV, g.shape[1] // N_DEV, g.shape[2])
    return g


def _seg_rows(size):
    return -(-size // 1024) * 8


def _pack(flat_parts):
    return jnp.concatenate([jnp.pad(f, (0, _seg_rows(f.shape[0]) * 128 - f.shape[0])).reshape(-1, 128) for f in flat_parts])


def kernel(x, mem, ffn1_norm_pre, ffn1_w_gate_up, ffn1_w_down, ffn1_norm_post, mix_norm_pre, w_in, conv_w, a_log, dt_bias, sm_w, sm_b, sm_ln_g, sm_ln_b, dn_norm_w, w_out, mix_norm_post, xa_norm_pre, mem_norm, w_xq, w_xkv, w_xo, xa_norm_post, ffn2_norm_pre, ffn2_w_gate_up, ffn2_w_down, ffn2_norm_post, loss_target, m_ffn1_norm_pre, m_ffn1_w_gate_up, m_ffn1_w_down, m_ffn1_norm_post, m_mix_norm_pre, m_w_in, m_conv_w, m_a_log, m_dt_bias, m_sm_w, m_sm_b, m_sm_ln_g, m_sm_ln_b, m_dn_norm_w, m_w_out, m_mix_norm_post, m_xa_norm_pre, m_mem_norm, m_w_xq, m_w_xkv, m_w_xo, m_xa_norm_post, m_ffn2_norm_pre, m_ffn2_w_gate_up, m_ffn2_w_down, m_ffn2_norm_post, v_ffn1_norm_pre, v_ffn1_w_gate_up, v_ffn1_w_down, v_ffn1_norm_post, v_mix_norm_pre, v_w_in, v_conv_w, v_a_log, v_dt_bias, v_sm_w, v_sm_b, v_sm_ln_g, v_sm_ln_b, v_dn_norm_w, v_w_out, v_mix_norm_post, v_xa_norm_pre, v_mem_norm, v_w_xq, v_w_xkv, v_w_xo, v_xa_norm_post, v_ffn2_norm_pre, v_ffn2_w_gate_up, v_ffn2_w_down, v_ffn2_norm_post):
    a = dict(locals())
    px, py, pc = _place()
    core = jnp.reshape(pc, (1,)).astype(jnp.int32)
    chip = jnp.reshape(2 * px + py, (1,)).astype(jnp.int32)
    me = 4 * px + 2 * py + pc
    xs, mems, tgt = x[0], mem[0], loss_target[0]
    conv_all = all_gather("ag_conv", [conv_w])[0]
    conv_full = conv_all.transpose(1, 2, 0, 3).reshape(N_LAYERS, CONV_W, QKV_W)
    params = [_small_params(l, conv_full, a) for l in range(N_LAYERS)]
    chain = [conv_all]

    me1 = jnp.reshape(me, (1,)).astype(jnp.int32)
    going = {}
    for l in range(N_LAYERS):
        for sb, names in AG_BEFORE.items():
            lands = [k_cast_place(f"place_l{l}{n}", a[n], l, me1, chain) for n in names]
            going[l, sb] = xfer_start(f"ag_out_l{l}{sb}", plan_gather_out, 4 * len(names), [], lands)
            chain = [going[l, sb]["token"]]

    saved = [{} for _ in range(N_LAYERS)]
    h = xs
    for l in range(N_LAYERS):
        for sb in BLOCKS:
            if sb in AG_BEFORE:
                names = AG_BEFORE[sb]
                _, got = xfer_wait(f"ag_outw_l{l}{sb}", plan_gather_out, going[l, sb], after=chain)
                passing = xfer_start(f"ag_pass_l{l}{sb}", plan_gather_pass, 3 * len(names), [], got)
                _, full = xfer_wait(f"ag_passw_l{l}{sb}", plan_gather_pass, passing)
                params[l].update(_big_params(dict(zip(names, full))))
                chain = [full[0]]
            h, saved[l][sb] = _fwd_block(l, sb, h, mems, params[l], chain)
            chain = [h]
    dy, loss = k_loss("loss", h, tgt)
    loss = lax.psum(loss[0, 0], ("x", "y", "c"))

    grads = [{} for _ in range(N_LAYERS)]
    exchange = GradExchange(a, core, chip)
    for l in reversed(range(N_LAYERS)):
        exchange.layer = l
        for sb in reversed(BLOCKS):
            dy = _bwd_block(l, sb, dy, saved[l][sb], params[l], grads[l], exchange)
    big_out = exchange.finish(dy)

    flat = [jnp.concatenate([grads[l][n].reshape(-1) for l in range(N_LAYERS)]) for n in SMALL]
    flat.append(jnp.concatenate([grads[l]["conv_w"].reshape(-1) for l in range(N_LAYERS)]))
    gsum = k_sum8("small_sum", all_gather("ag_small", [_pack(flat)])[0])
    rep_rows = sum(_seg_rows(a[n].size) for n in SMALL)
    conv_g = gsum[rep_rows:].reshape(-1)[:N_LAYERS * CONV_W * QKV_W].reshape(N_LAYERS, CONV_W, QKV_W)
    conv_g = lax.dynamic_slice_in_dim(conv_g, me * (QKV_W // N_DEV), QKV_W // N_DEV, axis=2)
    pack_state = lambda pre: _pack([a[pre + n].reshape(-1) for n in SMALL] + [a[pre + "conv_w"].reshape(-1)])
    small_g = jnp.concatenate([gsum[:rep_rows], _pack([conv_g.reshape(-1)])])
    small_out = k_adamw_flat("small_adamw", small_g, pack_state(""), pack_state("m_"), pack_state("v_"))
    outs = {}
    row = 0
    for n in SMALL + ("conv_w",):
        rows = _seg_rows(a[n].size)
        outs[n] = [o[row:row + rows].reshape(-1)[:a[n].size].reshape(a[n].shape) for o in small_out]
        row += rows
    for n in BIG:
        outs[n] = list(big_out[n])

    return (loss, dy[None], *[outs[n][0] for n in WEIGHTS], *[outs[n][1] for n in WEIGHTS],
            *[outs[n][2] for n in WEIGHTS], *[outs[n][3] for n in WEIGHTS])
```

```python
import functools
import math

import jax
import jax.numpy as jnp
from jax import lax
from jax.experimental import pallas as pl
from jax.experimental.pallas import tpu as pltpu

F32, BF16 = jnp.float32, jnp.bfloat16
HI = lax.Precision.HIGHEST
MESH = pl.DeviceIdType.MESH

N_DEV = 8
NORM_EPS = 1e-6
GM_HEADS, GM_CHUNK = 8, 128
DN_HEADS, DN_CHUNK, DN_DIM = 8, 64, 128
XA_HEADS, XA_DIM = 4, 512
CONV_W = 4
ADAM_LR, ADAM_B1, ADAM_B2, ADAM_EPS, ADAM_WD, ADAM_STEP = 0.001, 0.9, 0.999, 1e-08, 0.01, 10

V7X_VMEM_LIMIT = 56 * 1024 * 1024
WHOLE_TILE_MAX = 1536
WHOLE_K_MAX = 2048
ROW_TILE = 256


def _cparams(n_grid):
    return pltpu.CompilerParams(dimension_semantics=("arbitrary",) * n_grid, vmem_limit_bytes=V7X_VMEM_LIMIT)


def stepk(name, fn, grid, ins, outs, carries=(), prefetch=None, fill=(), after=()):
    n_in, n_out, n_c, n_fill = len(ins), len(outs), len(carries), len(fill)
    n_pre = 0 if prefetch is None else 1

    def body(*refs):
        refs = refs[n_pre:]
        in_refs, refs = refs[:n_in], refs[n_in + n_fill + len(after):]
        out_refs, c_refs = refs[:n_out], refs[n_out:]
        if n_c:
            first = functools.reduce(jnp.logical_and, [pl.program_id(a) == 0 for a in range(len(grid))])

            @pl.when(first)
            def _():
                for r in c_refs:
                    r[...] = jnp.zeros(r.shape, r.dtype)
        res = fn(*[r[...] for r in in_refs], *[r[...] for r in c_refs])
        for r, v in zip(tuple(out_refs) + tuple(c_refs), res):
            r[...] = v.astype(r.dtype)

    in_specs = [pl.BlockSpec(bs, im) for _, bs, im in ins] + [pl.BlockSpec(memory_space=pl.ANY)] * (n_fill + len(after))
    aliases = {n_pre + n_in + i: k for i, (_, k) in enumerate(fill)}
    out_specs = [pl.BlockSpec(bs, im) for _, _, bs, im in outs]
    out_shape = [jax.ShapeDtypeStruct(s, d) for s, d, _, _ in outs]
    for s, d in carries:
        zeros = (0,) * len(s)
        out_specs.append(pl.BlockSpec(s, lambda *a, _z=zeros: _z))
        out_shape.append(jax.ShapeDtypeStruct(s, d))
    args = [a for a, _, _ in ins] + [a for a, _ in fill] + list(after)
    if prefetch is None:
        call = pl.pallas_call(body, name=name, grid=grid, in_specs=in_specs, out_specs=out_specs, out_shape=out_shape,
                              input_output_aliases=aliases, compiler_params=_cparams(len(grid)))
        return call(*args)
    spec = pltpu.PrefetchScalarGridSpec(num_scalar_prefetch=1, grid=grid, in_specs=in_specs, out_specs=out_specs)
    call = pl.pallas_call(body, name=name, grid_spec=spec, out_shape=out_shape, input_output_aliases=aliases,
                          compiler_params=_cparams(len(grid)))
    return call(prefetch, *args)


def _tile(n, pref, align=128):
    if n <= WHOLE_TILE_MAX:
        return n
    t = (pref // align) * align
    while t > align and n % t:
        t -= align
    assert n % t == 0, (n, pref)
    return t


def mm(name, a, b, mode, out_dtype, nbo=1, after=()):
    nba, ra, ca = a.shape
    nbb, rb, cb = b.shape
    if mode == "nn":
        m, k, n = ra, nba * ca, nbb * cb
        assert rb == k
    elif mode == "nt":
        m, k, n = ra, nba * ca, rb
        assert nbb * cb == k
    else:
        k, m, n = ra, nba * ca, nbb * cb
        assert rb == k
    co = n // nbo
    whole_k = k <= WHOLE_K_MAX and (mode == "tn" or (nba == 1 and (mode == "nn" or nbb == 1)))
    k_pref = k if whole_k else 512
    if mode == "nn":
        tm, tk, tn = _tile(m, 1024, 8), _tile(ca, k_pref), _tile(math.gcd(cb, co), 1024)
        a_spec = pl.BlockSpec((None, tm, tk), lambda i, j, kk: (kk // (ca // tk), i, kk % (ca // tk)))
        b_spec = pl.BlockSpec((None, tk, tn), lambda i, j, kk: (j // (cb // tn), kk, j % (cb // tn)))
        dims = (((1,), (0,)), ((), ()))
    elif mode == "nt":
        tm, tk, tn = _tile(m, 1024, 8), _tile(math.gcd(ca, cb), k_pref), _tile(co, 1024)
        a_spec = pl.BlockSpec((None, tm, tk), lambda i, j, kk: (kk // (ca // tk), i, kk % (ca // tk)))
        b_spec = pl.BlockSpec((None, tn, tk), lambda i, j, kk: (kk // (cb // tk), j, kk % (cb // tk)))
        dims = (((1,), (1,)), ((), ()))
    else:
        tm, tk, tn = _tile(ca, 1024), _tile(k, k_pref), _tile(math.gcd(cb, co), 1024)
        a_spec = pl.BlockSpec((None, tk, tm), lambda i, j, kk: (i // (ca // tm), kk, i % (ca // tm)))
        b_spec = pl.BlockSpec((None, tk, tn), lambda i, j, kk: (j // (cb // tn), kk, j % (cb // tn)))
        dims = (((0,), (0,)), ((), ()))
    o_spec = pl.BlockSpec((None, tm, tn), lambda i, j, kk: (j // (co // tn), i, j % (co // tn)))
    nk = k // tk

    def tile_product(a_ref, b_ref):
        return lax.dot_general(a_ref[...].astype(BF16), b_ref[...].astype(BF16), dims, preferred_element_type=F32)

    def body_one_step(a_ref, b_ref, *rest):
        o_ref = rest[-1]
        o_ref[...] = tile_product(a_ref, b_ref).astype(o_ref.dtype)

    def body(a_ref, b_ref, *rest):
        o_ref, acc_ref = rest[-2:]
        kk = pl.program_id(2)

        @pl.when(kk == 0)
        def _():
            acc_ref[...] = jnp.zeros(acc_ref.shape, F32)

        acc_ref[...] += tile_product(a_ref, b_ref)

        @pl.when(kk == nk - 1)
        def _():
            o_ref[...] = acc_ref[...].astype(o_ref.dtype)

    return pl.pallas_call(
        body_one_step if nk == 1 else body, name=name, grid=(m // tm, n // tn, nk),
        in_specs=[a_spec, b_spec] + [pl.BlockSpec(memory_space=pl.ANY)] * len(after), out_specs=o_spec,
        out_shape=jax.ShapeDtypeStruct((nbo, m, co), out_dtype),
        scratch_shapes=[] if nk == 1 else [pltpu.VMEM((tm, tn), F32)], compiler_params=_cparams(3))(a, b, *after)


def _rms(x, g):
    return x * lax.rsqrt(jnp.mean(x * x, axis=-1, keepdims=True) + NORM_EPS) * g


def _sigmoid(x):
    return 1.0 / (1.0 + jnp.exp(-x))


def _silu(x):
    return x * _sigmoid(x)


def _gelu(x):
    return 0.5 * x * (1.0 + lax.erf(x * 0.7071067811865476))


def _softplus(x):
    return jnp.maximum(x, 0.0) + jnp.log(1.0 + jnp.exp(-jnp.abs(x)))


def _split(x, n):
    w = x.shape[-1] // n
    return [x[..., h * w:(h + 1) * w] for h in range(n)]


def _sgu(ua, va, smw, smb, lng, lnb):
    c = ua[0].shape[0]
    width = len(va) * va[0].shape[-1]
    vf = [_gelu(v) for v in va]
    mu = sum(jnp.sum(v, axis=-1, keepdims=True) for v in vf) / width
    var = sum(jnp.sum(jnp.square(v - mu), axis=-1, keepdims=True) for v in vf) / width
    r = lax.rsqrt(var + NORM_EPS)
    causal = lax.broadcasted_iota(jnp.int32, (c, c), 0) >= lax.broadcasted_iota(jnp.int32, (c, c), 1)
    outs = []
    for h in range(len(ua)):
        vn = (vf[h] - mu) * r * lng[h] + lnb[h]
        w = jnp.where(causal, smw[h], 0.0)
        mixed = jnp.dot(w.astype(BF16), vn.astype(BF16), preferred_element_type=F32) + smb[h]
        outs.append(_gelu(ua[h]) * mixed)
    return outs


def _shift_rows(x, k, up):
    rows = x.shape[0]
    row = lax.broadcasted_iota(jnp.int32, x.shape, 0)
    if up:
        return jnp.where(row < rows - k, pltpu.roll(x, rows - k, 0), 0.0)
    return jnp.where(row >= k, pltpu.roll(x, k, 0), 0.0)


@functools.partial(jax.custom_vjp, nondiff_argnums=(1,))
def _delay(x, k):
    return _shift_rows(x, k, False)


def _delay_fwd(x, k):
    return _shift_rows(x, k, False), None


def _delay_bwd(k, _, g):
    return (_shift_rows(g, k, True),)


_delay.defvjp(_delay_fwd, _delay_bwd)


def _conv_silu(x, w0, w1, w2, w3):
    y = w3 * x + w2 * _delay(x, 1) + w1 * _delay(x, 2) + w0 * _delay(x, 3)
    return _silu(y)


_BDOT_DIMS = {"nn": (((2,), (1,)), ((0,), (0,))), "nt": (((2,), (2,)), ((0,), (0,))), "tn": (((1,), (1,)), ((0,), (0,)))}


def _bdot_passes(a, b, kind, passes):
    one = lambda x, y: lax.dot_general(x, y, _BDOT_DIMS[kind], preferred_element_type=F32)
    ah, bh = a.astype(BF16), b.astype(BF16)
    if passes == 1:
        return one(ah, bh)
    al, bl = (a - ah.astype(F32)).astype(BF16), (b - bh.astype(F32)).astype(BF16)
    return one(ah, bh) + (one(ah, bl) + one(al, bh))


@functools.partial(jax.custom_vjp, nondiff_argnums=(2, 3))
def bdot(a, b, kind, passes):
    return _bdot_passes(a, b, kind, passes)


def _bdot_fwd(a, b, kind, passes):
    return _bdot_passes(a, b, kind, passes), (a, b)


def _bdot_bwd(kind, passes, res, g):
    a, b = res
    if kind == "nn":
        return bdot(g, b, "nt", passes), bdot(a, g, "tn", passes)
    if kind == "nt":
        return bdot(g, b, "nn", passes), bdot(g, a, "tn", passes)
    return bdot(b, g, "nt", passes), bdot(a, g, "nn", passes)


bdot.defvjp(_bdot_fwd, _bdot_bwd)


def _dn_step(s, q, k, v, braw, araw, alog, dtb):
    nh, c, d = q.shape
    row = lax.broadcasted_iota(jnp.int32, (c, c), 0)
    col = lax.broadcasted_iota(jnp.int32, (c, c), 1)
    causal, strict = (row >= col)[None], (row > col)[None]
    lower = (row >= col).astype(F32)
    strict_f = jnp.broadcast_to((row > col).astype(F32)[None], (nh, c, c))
    eye = (row == col).astype(F32)[None]

    qn = q * lax.rsqrt(jnp.sum(q * q, axis=-1, keepdims=True) + NORM_EPS) * (d ** -0.5)
    kn = k * lax.rsqrt(jnp.sum(k * k, axis=-1, keepdims=True) + NORM_EPS)
    beta = _sigmoid(braw)
    g = -jnp.exp(alog) * _softplus(araw + dtb)
    lg = lower[None] * g
    gcum = jnp.sum(lg, axis=-1, keepdims=True)
    diff = bdot(lg, strict_f, "nn", 3)
    decay = jnp.where(causal, jnp.exp(diff), 0.0)
    bcol = jnp.sum(eye * beta, axis=-1, keepdims=True)
    kb = kn * bcol
    a = jnp.where(strict, bdot(kb, kn, "nt", 1) * decay, 0.0)
    inv = eye - a
    p = bdot(a, a, "nn", 3)
    n_fac = int(math.log2(c)) - 1
    for it in range(n_fac):
        inv = inv + bdot(inv, p, "nn", 3)
        if it < n_fac - 1:
            p = bdot(p, p, "nn", 3)
    eg = jnp.exp(gcum)
    u = bdot(inv, v * bcol, "nn", 1)
    w = bdot(inv, kb * eg, "nn", 1)
    qk = jnp.where(causal, bdot(qn, kn, "nt", 1) * decay, 0.0)
    v_new = u - bdot(w, s, "nn", 1)
    o = bdot(qn * eg, s, "nn", 1) + bdot(qk, v_new, "nn", 1)
    glast = jnp.sum(g, axis=-1, keepdims=True)
    kdec = kn * jnp.exp(glast - gcum)
    s_new = s * jnp.exp(glast) + bdot(kdec, v_new, "tn", 1)
    return s_new, o


def _ogate(o, z, w):
    return [_rms(oh, w) * _silu(zh) for oh, zh in zip(o, z)]


def _xattn(q, k, v):
    outs = []
    for qh, kh, vh in zip(q, k, v):
        s = lax.dot_general(qh.astype(BF16), kh.astype(BF16), (((1,), (1,)), ((), ())),
                            preferred_element_type=F32) * (qh.shape[-1] ** -0.5)
        s = s - jnp.max(s, axis=-1, keepdims=True)
        e = jnp.exp(s)
        p = e / jnp.sum(e, axis=-1, keepdims=True)
        outs.append(jnp.dot(p.astype(BF16), vh.astype(BF16), preferred_element_type=F32))
    return outs


def _rows(t):
    return min(ROW_TILE, t)


def k_rms_fwd(name, x, g, after=()):
    t, d = x.shape
    tm = _rows(t)
    return stepk(name, lambda xv, gv: (_rms(xv, gv),), (t // tm,),
                 [(x, (tm, d), lambda i: (i, 0)), (g, (1, d), lambda i: (0, 0))],
                 [((t, d), BF16, (tm, d), lambda i: (i, 0))], after=after)[0]


def k_rms_bwd(name, x, g, dhs, dx_res):
    t, d = x.shape
    tm = _rows(t)
    n = len(dhs)

    def fn(xv, gv, *rest):
        dh = sum(r.astype(F32) for r in rest[:n])
        dxr, dg_c = rest[n], rest[n + 1]
        _, vjp = jax.vjp(_rms, xv, gv)
        dx, dg = vjp(dh)
        return dx + dxr, dg_c + dg

    row = lambda arr: (arr, (tm, d), lambda i: (i, 0))
    return stepk(name, fn, (t // tm,), [row(x), (g, (1, d), lambda i: (0, 0))] + [row(h) for h in dhs] + [row(dx_res)],
                 [((t, d), F32, (tm, d), lambda i: (i, 0))], carries=[((1, d), F32)])


def k_post_fwd(name, x, f, g, scale):
    t, d = x.shape
    tm = _rows(t)
    row = lambda arr: (arr, (tm, d), lambda i: (i, 0))
    return stepk(name, lambda xv, fv, gv: (xv + scale * _rms(fv, gv),), (t // tm,),
                 [row(x), row(f), (g, (1, d), lambda i: (0, 0))], [((t, d), F32, (tm, d), lambda i: (i, 0))])[0]


def k_post_bwd(name, f, g, dxo, scale, after=()):
    t, d = f.shape
    tm = _rows(t)

    def fn(fv, gv, dv, dg_c):
        _, vjp = jax.vjp(lambda a, b: scale * _rms(a, b), fv, gv)
        df, dg = vjp(dv)
        return df, dg_c + dg

    row = lambda arr: (arr, (tm, d), lambda i: (i, 0))
    return stepk(name, fn, (t // tm,), [row(f), (g, (1, d), lambda i: (0, 0)), row(dxo)],
                 [((t, d), BF16, (tm, d), lambda i: (i, 0))], carries=[((1, d), F32)], after=after)


def k_swiglu_fwd(name, hu):
    nb, t, c = hu.shape
    half = nb // 2
    tm = _rows(t)
    fn = lambda gv, uv: (_silu(gv.astype(F32)) * uv.astype(F32),)
    return stepk(name, fn, (half, t // tm),
                 [(hu, (None, tm, c), lambda b, i: (b, i, 0)), (hu, (None, tm, c), lambda b, i: (b + half, i, 0))],
                 [((half, t, c), BF16, (None, tm, c), lambda b, i: (b, i, 0))])[0]


def k_swiglu_bwd(name, hu, da):
    nb, t, c = hu.shape
    half = nb // 2
    tm = _rows(t)

    def fn(gv, uv, dv):
        gv, uv, dv = gv.astype(F32), uv.astype(F32), dv.astype(F32)
        sg = _sigmoid(gv)
        return (jnp.stack([dv * uv * sg * (1.0 + gv * (1.0 - sg)), dv * gv * sg]),)

    blk = lambda arr, off: (arr, (None, tm, c), lambda b, i: (b + off, i, 0))
    out = stepk(name, fn, (half, t // tm), [blk(hu, 0), blk(hu, half), blk(da, 0)],
                [((2, half, t, c), BF16, (2, None, tm, c), lambda b, i: (0, b, i, 0))])[0]
    return out.reshape(nb, t, c)


def k_loss(name, y, tgt):
    t, d = y.shape
    tm = _rows(t)

    def fn(yv, tv, acc):
        e = yv - tv
        part = jnp.sum(jnp.sum(e * e, axis=-1, keepdims=True), axis=0, keepdims=True)
        return e * (1.0 / d), acc + (0.5 / d) * part

    row = lambda arr: (arr, (tm, d), lambda i: (i, 0))
    return stepk(name, fn, (t // tm,), [row(y), row(tgt)], [((t, d), F32, (tm, d), lambda i: (i, 0))],
                 carries=[((1, 1), F32)])


GM_W = GM_HEADS * 128
QKV_COL0 = 2 * GM_W
QKV_W = 3 * DN_HEADS * DN_DIM
Z_COL0 = QKV_COL0 + QKV_W
MAIN_W = Z_COL0 + DN_HEADS * DN_DIM


def _sgu_ins(proj, p):
    c = GM_CHUNK
    return [(proj, (c, GM_W), lambda i: (i, 0)), (proj, (c, GM_W), lambda i: (i, 1)),
            (p["sm_w"], (GM_HEADS, c, c), lambda i: (0, 0, 0)), (p["sm_b"], (GM_HEADS, c, 1), lambda i: (0, 0, 0)),
            (p["sm_ln_g"], (1, GM_W), lambda i: (0, 0)), (p["sm_ln_b"], (1, GM_W), lambda i: (0, 0))]


def _sgu_lists(uv, vv, sw, sb, lg, lb):
    nh = GM_HEADS
    return (_split(uv, nh), _split(vv, nh), [sw[h] for h in range(nh)], [sb[h] for h in range(nh)],
            _split(lg, nh), _split(lb, nh))


def k_sgu_fwd(name, proj, p):
    t = proj.shape[0]

    def fn(*vals):
        return (jnp.concatenate(_sgu(*_sgu_lists(*vals)), axis=-1),)

    return stepk(name, fn, (t // GM_CHUNK,), _sgu_ins(proj, p),
                 [((t, GM_W), BF16, (GM_CHUNK, GM_W), lambda i: (i, 0))])[0]


def k_sgu_bwd(name, proj, p, dy):
    t = proj.shape[0]
    c = GM_CHUNK

    def fn(uv, vv, sw, sb, lg, lb, dv, dsw, dsb, dlg, dlb):
        _, vjp = jax.vjp(_sgu, *_sgu_lists(uv, vv, sw, sb, lg, lb))
        gu, gv, gsw, gsb, glg, glb = vjp(_split(dv.astype(F32), GM_HEADS))
        cat = lambda l: jnp.concatenate(l, axis=-1)
        return (cat(gu), cat(gv), dsw + jnp.stack(gsw), dsb + jnp.stack(gsb), dlg + cat(glg), dlb + cat(glb))

    return stepk(name, fn, (t // c,), _sgu_ins(proj, p) + [(dy, (None, c, GM_W), lambda i: (0, i, 0))],
                 [((t, GM_W), BF16, (c, GM_W), lambda i: (i, 0)), ((t, GM_W), BF16, (c, GM_W), lambda i: (i, 0))],
                 carries=[((GM_HEADS, c, c), F32), ((GM_HEADS, c, 1), F32), ((1, GM_W), F32), ((1, GM_W), F32)])


def _conv_ins(proj, conv_w):
    t = proj.shape[0]
    return [(proj, (t, 128), lambda j: (0, QKV_COL0 // 128 + j)), (conv_w, (CONV_W, 128), lambda j: (0, j))]


def k_conv_fwd(name, proj, conv_w):
    t = proj.shape[0]
    n = QKV_W // 128
    fn = lambda xv, wv: (_conv_silu(xv, *[wv[i:i + 1] for i in range(CONV_W)]),)
    return stepk(name, fn, (n,), _conv_ins(proj, conv_w), [((n, t, 128), F32, (None, t, 128), lambda j: (j, 0, 0))])[0]


def k_conv_bwd(name, proj, conv_w, dqkv):
    t = proj.shape[0]
    n = QKV_W // 128

    def fn(xv, wv, dv):
        _, vjp = jax.vjp(_conv_silu, xv, *[wv[i:i + 1] for i in range(CONV_W)])
        gx, *gw = vjp(dv)
        return gx, jnp.concatenate(gw, axis=0)

    return stepk(name, fn, (n,), _conv_ins(proj, conv_w) + [(dqkv, (None, t, 128), lambda j: (j, 0, 0))],
                 [((t, QKV_W), BF16, (t, 128), lambda j: (0, j)), ((CONV_W, QKV_W), F32, (CONV_W, 128), lambda j: (0, j))])


def _dn_ins(qkv, braw, araw, p, order):
    h, c, d = DN_HEADS, DN_CHUNK, DN_DIM
    qkv_in = lambda part: (qkv, (h, c, d), lambda n: (part, order(n), 0))
    gate_in = lambda arr: (arr, (None, h, 1, c), lambda n: (order(n), 0, 0, 0))
    par_in = lambda arr: (arr, (h, 1, 1), lambda n: (0, 0, 0))
    return [qkv_in(0), qkv_in(1), qkv_in(2), gate_in(braw), gate_in(araw), par_in(p["a_log"]), par_in(p["dt_bias"])]


def k_dn_fwd(name, qkv, braw, araw, p):
    t = qkv.shape[1]
    h, c, d = DN_HEADS, DN_CHUNK, DN_DIM
    nc = t // c

    def fn(q, k, v, b, a, al, dt, s):
        s_new, o = _dn_step(s, q, k, v, b, a, al, dt)
        return o, s, s_new

    o, s_all, _ = stepk(name, fn, (nc,), _dn_ins(qkv, braw, araw, p, lambda n: n),
                        [((h, t, d), F32, (h, c, d), lambda n: (0, n, 0)),
                         ((nc, h, d, d), F32, (None, h, d, d), lambda n: (n, 0, 0, 0))],
                        carries=[((h, d, d), F32)])
    return o, s_all


def k_dn_bwd(name, qkv, braw, araw, p, s_all, do):
    t = qkv.shape[1]
    h, c, d = DN_HEADS, DN_CHUNK, DN_DIM
    nc = t // c
    rev = lambda n: nc - 1 - n

    def fn(q, k, v, b, a, al, dt, s, dov, ds_c, dal_c, ddt_c):
        _, vjp = jax.vjp(_dn_step, s, q, k, v, b, a, al, dt)
        ds, dq, dk, dv, db, da, dal, ddt = vjp((ds_c, dov))
        return dq, dk, dv, db, da, ds, dal_c + dal, ddt_c + ddt

    ins = _dn_ins(qkv, braw, araw, p, rev) + [(s_all, (None, h, d, d), lambda n: (rev(n), 0, 0, 0)),
                                               (do, (h, c, d), lambda n: (0, rev(n), 0))]
    hd = ((h, t, d), F32, (h, c, d), lambda n: (0, rev(n), 0))
    gate = ((nc, h, 1, c), F32, (None, h, 1, c), lambda n: (rev(n), 0, 0, 0))
    dq, dk, dv, db, da, _, dal, ddt = stepk(name, fn, (nc,), ins, [hd, hd, hd, gate, gate],
                                            carries=[((h, d, d), F32), ((h, 1, 1), F32), ((h, 1, 1), F32)])
    return dq, dk, dv, db, da, dal, ddt


def _ogate_ins(o, proj, p):
    t = o.shape[1]
    tm = _rows(t)
    return tm, [(o, (DN_HEADS, tm, DN_DIM), lambda i: (0, i, 0)), (proj, (tm, GM_W), lambda i: (i, Z_COL0 // GM_W)),
                (p["dn_norm_w"], (1, DN_DIM), lambda i: (0, 0))]


def k_ogate_fwd(name, o, proj, p):
    t = o.shape[1]
    tm, ins = _ogate_ins(o, proj, p)

    def fn(ov, zv, wv):
        return (jnp.concatenate(_ogate([ov[h] for h in range(DN_HEADS)], _split(zv, DN_HEADS), wv), axis=-1),)

    return stepk(name, fn, (t // tm,), ins, [((t, GM_W), BF16, (tm, GM_W), lambda i: (i, 0))])[0]


def k_ogate_bwd(name, o, proj, p, dy):
    t = o.shape[1]
    tm, ins = _ogate_ins(o, proj, p)

    def fn(ov, zv, wv, dv, dw_c):
        _, vjp = jax.vjp(_ogate, [ov[h] for h in range(DN_HEADS)], _split(zv, DN_HEADS), wv)
        go, gz, gw = vjp(_split(dv.astype(F32), DN_HEADS))
        return jnp.stack(go), jnp.concatenate(gz, axis=-1), dw_c + gw

    return stepk(name, fn, (t // tm,), ins + [(dy, (None, tm, GM_W), lambda i: (1, i, 0))],
                 [((DN_HEADS, t, DN_DIM), F32, (DN_HEADS, tm, DN_DIM), lambda i: (0, i, 0)),
                  ((t, GM_W), BF16, (tm, GM_W), lambda i: (i, 0))], carries=[((1, DN_DIM), F32)])


def _xattn_lists(qv, kvv):
    nh = XA_HEADS
    return (_split(qv.astype(F32), nh), [kvv[h].astype(F32) for h in range(nh)],
            [kvv[nh + h].astype(F32) for h in range(nh)])


def k_xattn_fwd(name, q, kv):
    t, d = q.shape
    tm = _rows(t)
    fn = lambda qv, kvv: (jnp.concatenate(_xattn(*_xattn_lists(qv, kvv)), axis=-1),)
    return stepk(name, fn, (t // tm,), [(q, (tm, d), lambda i: (i, 0)), (kv, kv.shape, lambda i: (0, 0, 0))],
                 [((t, d), BF16, (tm, d), lambda i: (i, 0))])[0]


def k_xattn_bwd(name, q, kv, do):
    t, d = q.shape
    tm = _rows(t)

    def fn(qv, kvv, dv, dkv_c):
        _, vjp = jax.vjp(_xattn, *_xattn_lists(qv, kvv))
        gq, gk, gv = vjp(_split(dv.astype(F32), XA_HEADS))
        return jnp.concatenate(gq, axis=-1), dkv_c + jnp.stack(gk + gv)

    return stepk(name, fn, (t // tm,),
                 [(q, (tm, d), lambda i: (i, 0)), (kv, kv.shape, lambda i: (0, 0, 0)), (do, (None, tm, d), lambda i: (0, i, 0))],
                 [((t, d), BF16, (tm, d), lambda i: (i, 0))], carries=[(kv.shape, F32)])


def ffn_fwd(tag, x, p, pre, post, gu, dn, after=()):
    h = k_rms_fwd(f"{tag}_pre", x, p[pre], after)
    hu = mm(f"{tag}_gu", h[None], p[gu], "nn", BF16, nbo=N_DEV)
    a = k_swiglu_fwd(f"{tag}_act", hu)
    f = mm(f"{tag}_down", a, p[dn], "nn", F32)[0]
    y = k_post_fwd(f"{tag}_post", x, f, p[post], 0.5)
    return y, (x, h, hu, a, f)


class NoExchange:
    def take(self):
        return ()

    def emit(self, grads, names):
        pass


def ffn_bwd(tag, dy, saved, p, pre, post, gu, dn, grads, sch):
    x, h, hu, a, f = saved
    df, grads[post] = k_post_bwd(f"{tag}_post_b", f, p[post], dy, 0.5, sch.take())
    grads[dn] = mm(f"{tag}_down_dw", a, df[None], "tn", BF16)
    sch.emit(grads, (dn,))
    da = mm(f"{tag}_down_dx", df[None], p[dn], "nt", BF16, nbo=N_DEV // 2, after=sch.take())
    dhu = k_swiglu_bwd(f"{tag}_act_b", hu, da)
    grads[gu] = mm(f"{tag}_gu_dw", h[None], dhu, "tn", BF16, nbo=N_DEV)
    sch.emit(grads, (gu,))
    dh = mm(f"{tag}_gu_dx", dhu, p[gu], "nt", BF16, after=sch.take())[0]
    dx, grads[pre] = k_rms_bwd(f"{tag}_pre_b", x, p[pre], [dh], dy)
    return dx


def _to_chunks(a):
    t, h = a.shape
    return a.reshape(t // DN_CHUNK, DN_CHUNK, h).transpose(0, 2, 1).reshape(t // DN_CHUNK, h, 1, DN_CHUNK)


def _from_chunks(a):
    nc, h, _, c = a.shape
    return a.reshape(nc, h, c).transpose(0, 2, 1).reshape(nc * c, h)


def mix_fwd(tag, x, p, after=()):
    h = k_rms_fwd(f"{tag}_pre", x, p["mix_norm_pre"], after)
    proj = mm(f"{tag}_in", h[None], p["w_in_main"], "nn", F32)[0]
    small = mm(f"{tag}_in_s", h[None], p["w_in_small"], "nn", F32)[0]
    braw, araw = _to_chunks(small[:, :DN_HEADS]), _to_chunks(small[:, DN_HEADS:2 * DN_HEADS])
    ya = k_sgu_fwd(f"{tag}_sgu", proj, p)
    qkv = k_conv_fwd(f"{tag}_conv", proj, p["conv_w"])
    o, s_all = k_dn_fwd(f"{tag}_dn", qkv, braw, araw, p)
    yb = k_ogate_fwd(f"{tag}_og", o, proj, p)
    y = jnp.stack([ya, yb])
    m = mm(f"{tag}_out", y, p["w_out"], "nn", F32)[0]
    out = k_post_fwd(f"{tag}_post", x, m, p["mix_norm_post"], 1.0)
    return out, (x, h, proj, braw, araw, qkv, o, s_all, y, m)


def mix_bwd(tag, dy, saved, p, grads, sch):
    x, h, proj, braw, araw, qkv, o, s_all, y, m = saved
    t = x.shape[0]
    dm, grads["mix_norm_post"] = k_post_bwd(f"{tag}_post_b", m, p["mix_norm_post"], dy, 1.0, sch.take())
    grads["w_out"] = mm(f"{tag}_out_dw", y, dm[None], "tn", BF16)
    sch.emit(grads, ("w_out",))
    dyy = mm(f"{tag}_out_dx", dm[None], p["w_out"], "nt", BF16, nbo=2, after=sch.take())
    do, dz, grads["dn_norm_w"] = k_ogate_bwd(f"{tag}_og_b", o, proj, p, dyy)
    dq, dk, dv, db, da, grads["a_log"], grads["dt_bias"] = k_dn_bwd(f"{tag}_dn_b", qkv, braw, araw, p, s_all, do)
    dconv, grads["conv_w"] = k_conv_bwd(f"{tag}_conv_b", proj, p["conv_w"], jnp.concatenate([dq, dk, dv], axis=0))
    du, dva, grads["sm_w"], grads["sm_b"], grads["sm_ln_g"], grads["sm_ln_b"] = k_sgu_bwd(f"{tag}_sgu_b", proj, p, dyy)
    dproj = jnp.concatenate([du, dva, dconv, dz], axis=-1)
    dsmall = jnp.concatenate([_from_chunks(db), _from_chunks(da), jnp.zeros((t, 128 - 2 * DN_HEADS), F32)], axis=-1)
    g_main = mm(f"{tag}_in_dw", h[None], dproj[None], "tn", BF16)[0]
    g_small = mm(f"{tag}_in_s_dw", h[None], dsmall[None], "tn", BF16)[0][:, :2 * DN_HEADS]
    g_in = jnp.concatenate([g_main, g_small], axis=-1)
    grads["w_in"] = g_in.reshape(g_in.shape[0], N_DEV, g_in.shape[1] // N_DEV).transpose(1, 0, 2)
    sch.emit(grads, ("w_in",))
    dh = mm(f"{tag}_in_dx", dproj[None], p["w_in_main"], "nt", BF16, after=sch.take())[0]
    dh_s = mm(f"{tag}_in_s_dx", dsmall[None], p["w_in_small"], "nt", BF16)[0]
    dx, grads["mix_norm_pre"] = k_rms_bwd(f"{tag}_pre_b", x, p["mix_norm_pre"], [dh, dh_s], dy)
    return dx


def xa_fwd(tag, x, mem, p, after=()):
    hx = k_rms_fwd(f"{tag}_pre", x, p["xa_norm_pre"], after)
    mh = k_rms_fwd(f"{tag}_mem", mem, p["mem_norm"], after)
    q = mm(f"{tag}_q", hx[None], p["w_xq"], "nn", BF16)[0]
    kv = mm(f"{tag}_kv", mh[None], p["w_xkv"], "nn", BF16, nbo=N_DEV)
    o = k_xattn_fwd(f"{tag}_att", q, kv)
    c = mm(f"{tag}_o", o[None], p["w_xo"], "nn", F32)[0]
    out = k_post_fwd(f"{tag}_post", x, c, p["xa_norm_post"], 1.0)
    return out, (x, mem, hx, mh, q, kv, o, c)


def xa_bwd(tag, dy, saved, p, grads, sch):
    x, mem, hx, mh, q, kv, o, c = saved
    dc, grads["xa_norm_post"] = k_post_bwd(f"{tag}_post_b", c, p["xa_norm_post"], dy, 1.0, sch.take())
    grads["w_xo"] = mm(f"{tag}_o_dw", o[None], dc[None], "tn", BF16)
    sch.emit(grads, ("w_xo",))
    do = mm(f"{tag}_o_dx", dc[None], p["w_xo"], "nt", BF16, after=sch.take())
    dq, dkv = k_xattn_bwd(f"{tag}_att_b", q, kv, do)
    dkv16 = dkv.astype(BF16)
    grads["w_xq"] = mm(f"{tag}_q_dw", hx[None], dq[None], "tn", BF16)
    grads["w_xkv"] = mm(f"{tag}_kv_dw", mh[None], dkv16, "tn", BF16, nbo=N_DEV)
    sch.emit(grads, ("w_xq", "w_xkv"))
    dhx = mm(f"{tag}_q_dx", dq[None], p["w_xq"], "nt", BF16, after=sch.take())[0]
    dmh = mm(f"{tag}_kv_dx", dkv16, p["w_xkv"], "nt", BF16)[0]
    _, grads["mem_norm"] = k_rms_bwd(f"{tag}_mem_b", mem, p["mem_norm"], [dmh], jnp.zeros_like(mem))
    dx, grads["xa_norm_pre"] = k_rms_bwd(f"{tag}_pre_b", x, p["xa_norm_pre"], [dhx], dy)
    return dx


def _place():
    return lax.axis_index("x"), lax.axis_index("y"), lax.axis_index("c")


def _other_chips(x, y):
    return [(1 - x, y), (x, 1 - y), (1 - x, 1 - y)]


_ANY = pl.BlockSpec(memory_space=pl.ANY)


def all_gather(name, shards):
    n = len(shards)

    def body(*refs):
        ins, outs = refs[:n], refs[n:2 * n]
        send_sems, recv_sems, local_sems = refs[2 * n:]
        x, y, c = _place()
        me, sibling = (x, y, c), (x, y, 1 - c)
        chips = _other_chips(x, y)
        idx = lambda px, py, pc: 4 * px + 2 * py + pc

        def copy(a, k, block, to, src=None):
            dst = outs[a].at[idx(*block)]
            return pltpu.make_async_remote_copy(src_ref=dst if src is None else src, dst_ref=dst,
                                                send_sem=send_sems.at[7 * a + k], recv_sem=recv_sems.at[7 * a + k],
                                                device_id=to, device_id_type=MESH)

        mine = [pltpu.make_async_copy(ins[a], outs[a].at[idx(*me)], local_sems.at[a]) for a in range(n)]
        for cp in mine:
            cp.start()
        first = []
        for a in range(n):
            first.append(copy(a, 0, me, sibling, src=ins[a]))
            first += [copy(a, 1 + j, me, (*chip, c), src=ins[a]) for j, chip in enumerate(chips)]
        for cp in first:
            cp.start()
        passed = []
        for a in range(n):
            for j, chip in enumerate(chips):
                copy(a, 1 + j, (*chip, c), me).wait_recv()
                passed.append(copy(a, 4 + j, (*chip, c), sibling))
                passed[-1].start()
        for a in range(n):
            copy(a, 0, sibling, me).wait_recv()
            for j, chip in enumerate(chips):
                copy(a, 4 + j, (*chip, 1 - c), me).wait_recv()
        for cp in first + passed:
            cp.wait_send()
        for cp in mine:
            cp.wait()

    return pl.pallas_call(
        body, name=name, in_specs=[_ANY] * n, out_specs=[_ANY] * n,
        out_shape=[jax.ShapeDtypeStruct((N_DEV,) + s.shape, s.dtype) for s in shards],
        scratch_shapes=[pltpu.SemaphoreType.DMA((7 * n,)), pltpu.SemaphoreType.DMA((7 * n,)),
                        pltpu.SemaphoreType.DMA((n,))])(*shards)


def k_cast_place(name, w, layer, me, after):
    _, r, cc = w.shape
    tr = _row_tile(r, cc)
    return stepk(name, lambda v: (v,), (r // tr,), [(w, (None, tr, cc), lambda i, pre: (layer, i, 0))],
                 [((N_DEV, r, cc), BF16, (None, tr, cc), lambda i, pre: (pre[0], i, 0))], prefetch=me, after=after)[0]


_HBM = pl.BlockSpec(memory_space=pltpu.HBM)
_SEM = pl.BlockSpec(memory_space=pltpu.SEMAPHORE)
_DATAFLOW = pltpu.SideEffectType.DATAFLOW_SIDE_EFFECTING


def xfer_start(name, plan, n_sems, srcs, lands, after=()):
    ns, nl, na = len(srcs), len(lands), len(after)
    bufs = list(srcs) + list(lands)

    def body(*refs):
        send_sems, recv_sems, token = refs[ns + nl + na], refs[ns + nl + na + 1], refs[-1]
        for cp in plan(refs[:ns], refs[ns:ns + nl], send_sems, recv_sems):
            cp.start()
        token[...] = jnp.zeros(token.shape, token.dtype)

    res = pl.pallas_call(
        body, name=name, in_specs=[_HBM] * (ns + nl) + [_ANY] * na,
        out_specs=(_SEM, _SEM, *[_HBM] * (ns + nl), pl.BlockSpec(memory_space=pltpu.VMEM)),
        out_shape=(pltpu.SemaphoreType.DMA((n_sems,)), pltpu.SemaphoreType.DMA((n_sems,)),
                   *[pltpu.HBM(b.shape, b.dtype) for b in bufs], jax.ShapeDtypeStruct((8, 128), F32)),
        input_output_aliases={i: 2 + i for i in range(ns + nl)},
        compiler_params=pltpu.CompilerParams(has_side_effects=_DATAFLOW),
    )(*[pltpu.with_memory_space_constraint(b, pltpu.HBM) for b in bufs], *after)
    return dict(send=res[0], recv=res[1], srcs=list(res[2:2 + ns]), lands=list(res[2 + ns:2 + ns + nl]), token=res[-1])


def xfer_wait(name, plan, started, after=()):
    ns, nl = len(started["srcs"]), len(started["lands"])
    bufs = started["srcs"] + started["lands"]

    def body(*refs):
        for cp in plan(refs[:ns], refs[ns:ns + nl], refs[ns + nl], refs[ns + nl + 1]):
            cp.wait_send()
            cp.wait_recv()

    res = pl.pallas_call(
        body, name=name, in_specs=[_HBM] * (ns + nl) + [_SEM, _SEM] + [_ANY] * len(after),
        out_specs=tuple([_HBM] * (ns + nl)), out_shape=tuple(pltpu.HBM(b.shape, b.dtype) for b in bufs),
        input_output_aliases={i: i for i in range(ns + nl)},
        compiler_params=pltpu.CompilerParams(has_side_effects=_DATAFLOW),
    )(*bufs, started["send"], started["recv"], *after)
    return list(res[:ns]), list(res[ns:])


def _remote(src, dst, send_sems, recv_sems, k, to):
    return pltpu.make_async_remote_copy(src_ref=src, dst_ref=dst, send_sem=send_sems.at[k], recv_sem=recv_sems.at[k],
                                        device_id=to, device_id_type=MESH)


def plan_gather_out(srcs, lands, send_sems, recv_sems):
    x, y, c = _place()
    me = 4 * x + 2 * y + c
    targets = [(x, y, 1 - c)] + [(px, py, c) for px, py in _other_chips(x, y)]
    return [_remote(lands[a].at[me], lands[a].at[me], send_sems, recv_sems, 4 * a + k, to)
            for a in range(len(lands)) for k, to in enumerate(targets)]


def plan_gather_pass(srcs, lands, send_sems, recv_sems):
    x, y, c = _place()
    return [_remote(lands[a].at[4 * px + 2 * py + c], lands[a].at[4 * px + 2 * py + c], send_sems, recv_sems,
                    3 * a + j, (x, y, 1 - c))
            for a in range(len(lands)) for j, (px, py) in enumerate(_other_chips(x, y))]


def plan_scatter_sibling(srcs, lands, send_sems, recv_sems):
    x, y, c = _place()
    return [_remote(srcs[a].at[j, 1 - c], lands[a].at[j], send_sems, recv_sems, 4 * a + j, (x, y, 1 - c))
            for a in range(len(srcs)) for j in range(4)]


def plan_scatter_chips(srcs, lands, send_sems, recv_sems):
    x, y, c = _place()
    return [_remote(srcs[a].at[2 * px + py], lands[a].at[j], send_sems, recv_sems, 3 * a + j, (px, py, c))
            for a in range(len(srcs)) for j, (px, py) in enumerate(_other_chips(x, y))]


ELEMWISE_BLOCK = 512 * 1024


def _row_tile(r, cc, limit=ELEMWISE_BLOCK):
    t = r
    while t * cc > limit and t % 32 == 0:
        t //= 2
    if t * cc > limit:
        for cand in range(t, 15, -16):
            if r % cand == 0 and cand * cc <= limit:
                return cand
    return t


def k_pair_add(name, g4, r1, core):
    _, _, r, cc = g4.shape
    tr = _row_tile(r, cc, 2 * ELEMWISE_BLOCK)
    fn = lambda av, bv: (av.astype(F32) + bv.astype(F32),)
    return stepk(name, fn, (4, r // tr),
                 [(g4, (None, None, tr, cc), lambda j, i, pre: (j, pre[0], i, 0)),
                  (r1, (None, tr, cc), lambda j, i, pre: (j, i, 0))],
                 [((4, r, cc), BF16, (None, tr, cc), lambda j, i, pre: (j, i, 0))], prefetch=core)[0]


def _adamw(g, w, m, v):
    m2 = ADAM_B1 * m + (1.0 - ADAM_B1) * g
    v2 = ADAM_B2 * v + (1.0 - ADAM_B2) * jnp.square(g)
    m_hat = m2 / (1.0 - ADAM_B1 ** ADAM_STEP)
    v_hat = v2 / (1.0 - ADAM_B2 ** ADAM_STEP)
    delta = -ADAM_LR * (m_hat / (jnp.sqrt(v_hat) + ADAM_EPS) + ADAM_WD * w)
    return g, delta, m2, v2


def k_adamw_shard(name, part, others, w, m, v, layer, chip, fill):
    _, r, cc = part.shape
    tr = _row_tile(r, cc)

    def fn(pv, o0, o1, o2, wv, mv, vv):
        g = ((pv.astype(F32) + o0.astype(F32)) + o1.astype(F32)) + o2.astype(F32)
        return _adamw(g, wv, mv, vv)

    other = lambda k: (others, (None, tr, cc), lambda i, pre: (k, i, 0))
    state = lambda arr: (arr, (None, tr, cc), lambda i, pre: (layer, i, 0))
    out = ((2, r, cc), F32, (None, tr, cc), lambda i, pre: (layer, i, 0))
    return stepk(name, fn, (r // tr,),
                 [(part, (None, tr, cc), lambda i, pre: (pre[0], i, 0)), other(0), other(1), other(2),
                  state(w), state(m), state(v)],
                 [out] * 4, prefetch=chip, fill=[(f, k) for k, f in enumerate(fill)])


def k_sum8(name, parts):
    _, rows, lanes = parts.shape

    def fn(pv):
        acc = pv[0]
        for d in range(1, N_DEV):
            acc = acc + pv[d]
        return (acc,)

    return stepk(name, fn, (1,), [(parts, parts.shape, lambda i: (0, 0, 0))],
                 [((rows, lanes), F32, (rows, lanes), lambda i: (0, 0))])[0]


def k_adamw_flat(name, g, w, m, v):
    whole = lambda arr: (arr, arr.shape, lambda i: (0, 0))
    out = (g.shape, F32, g.shape, lambda i: (0, 0))
    return stepk(name, _adamw, (1,), [whole(g), whole(w), whole(m), whole(v)], [out] * 4)


WEIGHTS = ("ffn1_norm_pre", "ffn1_w_gate_up", "ffn1_w_down", "ffn1_norm_post", "mix_norm_pre", "w_in", "conv_w", "a_log",
           "dt_bias", "sm_w", "sm_b", "sm_ln_g", "sm_ln_b", "dn_norm_w", "w_out", "mix_norm_post", "xa_norm_pre", "mem_norm",
           "w_xq", "w_xkv", "w_xo", "xa_norm_post", "ffn2_norm_pre", "ffn2_w_gate_up", "ffn2_w_down", "ffn2_norm_post")
BIG = ("ffn1_w_gate_up", "ffn1_w_down", "w_in", "w_out", "w_xq", "w_xkv", "w_xo", "ffn2_w_gate_up", "ffn2_w_down")
ROW_SHARDED = ("ffn1_w_down", "w_out", "w_xq", "w_xo", "ffn2_w_down")
SMALL = tuple(n for n in WEIGHTS if n not in BIG and n != "conv_w")
N_LAYERS = 2
IN_COLS = MAIN_W + 2 * DN_HEADS


BLOCKS = ("f1", "mx", "xa", "f2")
AG_BEFORE = {"f1": ("ffn1_w_gate_up", "ffn1_w_down"), "mx": ("w_in", "w_out"),
             "xa": ("w_xq", "w_xkv", "w_xo", "ffn2_w_gate_up", "ffn2_w_down")}
RS_IN_FLIGHT = 1


def _fwd_block(l, sb, h, mem, p, after):
    tag = f"l{l}{sb}"
    if sb == "f1":
        return ffn_fwd(tag, h, p, "ffn1_norm_pre", "ffn1_norm_post", "ffn1_w_gate_up", "ffn1_w_down", after)
    if sb == "mx":
        return mix_fwd(tag, h, p, after)
    if sb == "xa":
        return xa_fwd(tag, h, mem, p, after)
    return ffn_fwd(tag, h, p, "ffn2_norm_pre", "ffn2_norm_post", "ffn2_w_gate_up", "ffn2_w_down", after)


def _bwd_block(l, sb, dy, saved, p, grads, sch):
    tag = f"l{l}{sb}"
    if sb == "f1":
        return ffn_bwd(tag, dy, saved, p, "ffn1_norm_pre", "ffn1_norm_post", "ffn1_w_gate_up", "ffn1_w_down", grads, sch)
    if sb == "mx":
        return mix_bwd(tag, dy, saved, p, grads, sch)
    if sb == "xa":
        return xa_bwd(tag, dy, saved, p, grads, sch)
    return ffn_bwd(tag, dy, saved, p, "ffn2_norm_pre", "ffn2_norm_post", "ffn2_w_gate_up", "ffn2_w_down", grads, sch)


class GradExchange:
    def __init__(self, a, core, chip):
        self.a, self.core, self.chip = a, core, chip
        self.layer = None
        self.pending, self.tie, self.out = [], [], {}

    def take(self):
        tie, self.tie = self.tie, []
        return tie

    def _advance(self, chain, drain):
        a = self.a
        between_chips = [grp for grp in self.pending if grp["stage"] == "chips"]
        for grp in self.pending:
            if grp["stage"] == "sibling":
                l, names = grp["l"], grp["names"]
                g4, r1 = xfer_wait(f"rs_sibw_l{l}{names[0]}", plan_scatter_sibling, grp["going"], after=chain)
                parts = [k_pair_add(f"rs_add_l{l}{n}", gg, rr, self.core) for n, gg, rr in zip(names, g4, r1)]
                zones = [lax.empty((3,) + p.shape[1:], BF16) for p in parts]
                grp["going"] = xfer_start(f"rs_chip_l{l}{names[0]}", plan_scatter_chips, 3 * len(names), parts, zones)
                grp["stage"], grp["age"] = "chips", 0
                chain = [grp["going"]["token"]]
        for grp in between_chips:
            l, names = grp["l"], grp["names"]
            if drain or grp["age"] >= RS_IN_FLIGHT:
                parts, r2 = xfer_wait(f"rs_chipw_l{l}{names[0]}", plan_scatter_chips, grp["going"], after=chain)
                for n, part, others in zip(names, parts, r2):
                    self.out[n] = k_adamw_shard(f"adamw_{n}_l{l}", part, others, a[n], a["m_" + n], a["v_" + n], l,
                                                self.chip, self.out.get(n, ()))
                self.pending.remove(grp)
            else:
                grp["age"] += 1
        return chain

    def emit(self, grads, names):
        l = self.layer
        chain = self._advance([grads[names[-1]]], drain=False)
        g4 = [_scatter_layout(n, grads[n]) for n in names]
        g4 = [t.reshape(4, 2, *t.shape[1:]) for t in g4]
        zones = [lax.empty((4,) + t.shape[2:], BF16) for t in g4]
        going = xfer_start(f"rs_sib_l{l}{names[0]}", plan_scatter_sibling, 4 * len(names), g4, zones, after=chain)
        self.pending.append(dict(l=l, names=names, stage="sibling", going=going, age=0))
        self.tie = [going["token"]]

    def finish(self, last):
        chain = [last]
        while self.pending:
            chain = self._advance(chain, drain=True)
        return self.out


def _big_params(full):
    p = {}
    for n, w in full.items():
        if n in ROW_SHARDED:
            p[n] = w.reshape(1, w.shape[0] * w.shape[1], w.shape[2])
        elif n == "w_in":
            w = w.transpose(1, 0, 2).reshape(w.shape[1], IN_COLS)
            p["w_in_main"] = w[None, :, :MAIN_W]
            p["w_in_small"] = jnp.pad(w[:, MAIN_W:], ((0, 0), (0, 128 - 2 * DN_HEADS)))[None]
        else:
            p[n] = w
    return p


def _small_params(l, conv_full, a):
    p = {"conv_w": conv_full[l]}
    for n in SMALL:
        w = a[n][l]
        if n in ("a_log", "dt_bias"):
            p[n] = w.reshape(DN_HEADS, 1, 1)
        elif n == "sm_b":
            p[n] = w[..., None]
        elif n == "sm_w":
            p[n] = w
        else:
            p[n] = w[None]
    return p


def _scatter_layout(n, g):
    if n in ROW_SHARDED:
        return g.reshape(N_DEV, g.shape[1] // N_DEV, g.shape[2])
    return g


def _seg_rows(size):
    return -(-size // 1024) * 8


def _pack(flat_parts):
    return jnp.concatenate([jnp.pad(f, (0, _seg_rows(f.shape[0]) * 128 - f.shape[0])).reshape(-1, 128) for f in flat_parts])


def kernel(x, mem, ffn1_norm_pre, ffn1_w_gate_up, ffn1_w_down, ffn1_norm_post, mix_norm_pre, w_in, conv_w, a_log, dt_bias, sm_w, sm_b, sm_ln_g, sm_ln_b, dn_norm_w, w_out, mix_norm_post, xa_norm_pre, mem_norm, w_xq, w_xkv, w_xo, xa_norm_post, ffn2_norm_pre, ffn2_w_gate_up, ffn2_w_down, ffn2_norm_post, loss_target, m_ffn1_norm_pre, m_ffn1_w_gate_up, m_ffn1_w_down, m_ffn1_norm_post, m_mix_norm_pre, m_w_in, m_conv_w, m_a_log, m_dt_bias, m_sm_w, m_sm_b, m_sm_ln_g, m_sm_ln_b, m_dn_norm_w, m_w_out, m_mix_norm_post, m_xa_norm_pre, m_mem_norm, m_w_xq, m_w_xkv, m_w_xo, m_xa_norm_post, m_ffn2_norm_pre, m_ffn2_w_gate_up, m_ffn2_w_down, m_ffn2_norm_post, v_ffn1_norm_pre, v_ffn1_w_gate_up, v_ffn1_w_down, v_ffn1_norm_post, v_mix_norm_pre, v_w_in, v_conv_w, v_a_log, v_dt_bias, v_sm_w, v_sm_b, v_sm_ln_g, v_sm_ln_b, v_dn_norm_w, v_w_out, v_mix_norm_post, v_xa_norm_pre, v_mem_norm, v_w_xq, v_w_xkv, v_w_xo, v_xa_norm_post, v_ffn2_norm_pre, v_ffn2_w_gate_up, v_ffn2_w_down, v_ffn2_norm_post):
    a = dict(locals())
    px, py, pc = _place()
    core = jnp.reshape(pc, (1,)).astype(jnp.int32)
    chip = jnp.reshape(2 * px + py, (1,)).astype(jnp.int32)
    me = 4 * px + 2 * py + pc
    xs, mems, tgt = x[0], mem[0], loss_target[0]
    conv_all = all_gather("ag_conv", [conv_w])[0]
    conv_full = conv_all.transpose(1, 2, 0, 3).reshape(N_LAYERS, CONV_W, QKV_W)
    params = [_small_params(l, conv_full, a) for l in range(N_LAYERS)]
    chain = [conv_all]

    me1 = jnp.reshape(me, (1,)).astype(jnp.int32)
    going = {}
    for l in range(N_LAYERS):
        for sb, names in AG_BEFORE.items():
            lands = [k_cast_place(f"place_l{l}{n}", a[n], l, me1, chain) for n in names]
            going[l, sb] = xfer_start(f"ag_out_l{l}{sb}", plan_gather_out, 4 * len(names), [], lands)
            chain = [going[l, sb]["token"]]

    saved = [{} for _ in range(N_LAYERS)]
    h = xs
    for l in range(N_LAYERS):
        for sb in BLOCKS:
            if sb in AG_BEFORE:
                names = AG_BEFORE[sb]
                _, got = xfer_wait(f"ag_outw_l{l}{sb}", plan_gather_out, going[l, sb], after=chain)
                passing = xfer_start(f"ag_pass_l{l}{sb}", plan_gather_pass, 3 * len(names), [], got)
                _, full = xfer_wait(f"ag_passw_l{l}{sb}", plan_gather_pass, passing)
                params[l].update(_big_params(dict(zip(names, full))))
                chain = [full[0]]
            h, saved[l][sb] = _fwd_block(l, sb, h, mems, params[l], chain)
            chain = [h]
    dy, loss = k_loss("loss", h, tgt)
    loss = lax.psum(loss[0, 0], ("x", "y", "c"))

    grads = [{} for _ in range(N_LAYERS)]
    exchange = GradExchange(a, core, chip)
    for l in reversed(range(N_LAYERS)):
        exchange.layer = l
        for sb in reversed(BLOCKS):
            dy = _bwd_block(l, sb, dy, saved[l][sb], params[l], grads[l], exchange)
    big_out = exchange.finish(dy)

    flat = [jnp.concatenate([grads[l][n].reshape(-1) for l in range(N_LAYERS)]) for n in SMALL]
    flat.append(jnp.concatenate([grads[l]["conv_w"].reshape(-1) for l in range(N_LAYERS)]))
    gsum = k_sum8("small_sum", all_gather("ag_small", [_pack(flat)])[0])
    rep_rows = sum(_seg_rows(a[n].size) for n in SMALL)
    conv_g = gsum[rep_rows:].reshape(-1)[:N_LAYERS * CONV_W * QKV_W].reshape(N_LAYERS, CONV_W, QKV_W)
    conv_g = lax.dynamic_slice_in_dim(conv_g, me * (QKV_W // N_DEV), QKV_W // N_DEV, axis=2)
    pack_state = lambda pre: _pack([a[pre + n].reshape(-1) for n in SMALL] + [a[pre + "conv_w"].reshape(-1)])
    small_g = jnp.concatenate([gsum[:rep_rows], _pack([conv_g.reshape(-1)])])
    small_out = k_adamw_flat("small_adamw", small_g, pack_state(""), pack_state("m_"), pack_state("v_"))
    outs = {}
    row = 0
    for n in SMALL + ("conv_w",):
        rows = _seg_rows(a[n].size)
        outs[n] = [o[row:row + rows].reshape(-1)[:a[n].size].reshape(a[n].shape) for o in small_out]
        row += rows
    for n in BIG:
        outs[n] = list(big_out[n])

    return (loss, dy[None], *[outs[n][0] for n in WEIGHTS], *[outs[n][1] for n in WEIGHTS],
            *[outs[n][2] for n in WEIGHTS], *[outs[n][3] for n in WEIGHTS])
```

```python
import functools
import math

import jax
import jax.numpy as jnp
from jax import lax
from jax.experimental import pallas as pl
from jax.experimental.pallas import tpu as pltpu

F32, BF16 = jnp.float32, jnp.bfloat16
HI = lax.Precision.HIGHEST
MESH = pl.DeviceIdType.MESH

N_DEV = 8
NORM_EPS = 1e-6
GM_HEADS, GM_CHUNK = 8, 128
DN_HEADS, DN_CHUNK, DN_DIM = 8, 64, 128
XA_HEADS, XA_DIM = 4, 512
CONV_W = 4
ADAM_LR, ADAM_B1, ADAM_B2, ADAM_EPS, ADAM_WD, ADAM_STEP = 0.001, 0.9, 0.999, 1e-08, 0.01, 10

V7X_VMEM_LIMIT = 56 * 1024 * 1024
WHOLE_TILE_MAX = 1536
WHOLE_K_MAX = 2048
ROW_TILE = 256


def _cparams(n_grid):
    return pltpu.CompilerParams(dimension_semantics=("arbitrary",) * n_grid, vmem_limit_bytes=V7X_VMEM_LIMIT)


def stepk(name, fn, grid, ins, outs, carries=(), prefetch=None, fill=(), after=()):
    n_in, n_out, n_c, n_fill = len(ins), len(outs), len(carries), len(fill)
    n_pre = 0 if prefetch is None else 1

    def body(*refs):
        refs = refs[n_pre:]
        in_refs, refs = refs[:n_in], refs[n_in + n_fill + len(after):]
        out_refs, c_refs = refs[:n_out], refs[n_out:]
        if n_c:
            first = functools.reduce(jnp.logical_and, [pl.program_id(a) == 0 for a in range(len(grid))])

            @pl.when(first)
            def _():
                for r in c_refs:
                    r[...] = jnp.zeros(r.shape, r.dtype)
        res = fn(*[r[...] for r in in_refs], *[r[...] for r in c_refs])
        for r, v in zip(tuple(out_refs) + tuple(c_refs), res):
            r[...] = v.astype(r.dtype)

    in_specs = [pl.BlockSpec(bs, im) for _, bs, im in ins] + [pl.BlockSpec(memory_space=pl.ANY)] * (n_fill + len(after))
    aliases = {n_pre + n_in + i: k for i, (_, k) in enumerate(fill)}
    out_specs = [pl.BlockSpec(bs, im) for _, _, bs, im in outs]
    out_shape = [jax.ShapeDtypeStruct(s, d) for s, d, _, _ in outs]
    for s, d in carries:
        zeros = (0,) * len(s)
        out_specs.append(pl.BlockSpec(s, lambda *a, _z=zeros: _z))
        out_shape.append(jax.ShapeDtypeStruct(s, d))
    args = [a for a, _, _ in ins] + [a for a, _ in fill] + list(after)
    if prefetch is None:
        call = pl.pallas_call(body, name=name, grid=grid, in_specs=in_specs, out_specs=out_specs, out_shape=out_shape,
                              input_output_aliases=aliases, compiler_params=_cparams(len(grid)))
        return call(*args)
    spec = pltpu.PrefetchScalarGridSpec(num_scalar_prefetch=1, grid=grid, in_specs=in_specs, out_specs=out_specs)
    call = pl.pallas_call(body, name=name, grid_spec=spec, out_shape=out_shape, input_output_aliases=aliases,
                          compiler_params=_cparams(len(grid)))
    return call(prefetch, *args)


def _tile(n, pref, align=128):
    if n <= WHOLE_TILE_MAX:
        return n
    t = (pref // align) * align
    while t > align and n % t:
        t -= align
    assert n % t == 0, (n, pref)
    return t


def mm(name, a, b, mode, out_dtype, nbo=1, after=(), b_rows=None):
    nba, ra, ca = a.shape
    nbb, rb, cb = b.shape
    b_row0 = 0
    if b_rows is not None:
        assert nbb == 1 and mode != "tn"
        b_row0, rb = b_rows
    if mode == "nn":
        m, k, n = ra, nba * ca, nbb * cb
        assert rb == k
    elif mode == "nt":
        m, k, n = ra, nba * ca, rb
        assert nbb * cb == k
    else:
        k, m, n = ra, nba * ca, nbb * cb
        assert rb == k
    co = n // nbo
    whole_k = k <= WHOLE_K_MAX and (mode == "tn" or (nba == 1 and (mode == "nn" or nbb == 1)))
    k_pref = k if whole_k else 512
    if mode == "nn":
        tm, tk, tn = _tile(m, 1024, 8), _tile(ca, k_pref), _tile(math.gcd(cb, co), 1024)
        a_spec = pl.BlockSpec((None, tm, tk), lambda i, j, kk: (kk // (ca // tk), i, kk % (ca // tk)))
        assert b_row0 % tk == 0
        b_spec = pl.BlockSpec((None, tk, tn), lambda i, j, kk: (j // (cb // tn), kk + b_row0 // tk, j % (cb // tn)))
        dims = (((1,), (0,)), ((), ()))
    elif mode == "nt":
        tm, tk, tn = _tile(m, 1024, 8), _tile(math.gcd(ca, cb), k_pref), _tile(co, 1024)
        assert b_row0 % tn == 0
        a_spec = pl.BlockSpec((None, tm, tk), lambda i, j, kk: (kk // (ca // tk), i, kk % (ca // tk)))
        b_spec = pl.BlockSpec((None, tn, tk), lambda i, j, kk: (kk // (cb // tk), j + b_row0 // tn, kk % (cb // tk)))
        dims = (((1,), (1,)), ((), ()))
    else:
        tm, tk, tn = _tile(ca, 1024), _tile(k, k_pref), _tile(math.gcd(cb, co), 1024)
        a_spec = pl.BlockSpec((None, tk, tm), lambda i, j, kk: (i // (ca // tm), kk, i % (ca // tm)))
        b_spec = pl.BlockSpec((None, tk, tn), lambda i, j, kk: (j // (cb // tn), kk, j % (cb // tn)))
        dims = (((0,), (0,)), ((), ()))
    o_spec = pl.BlockSpec((None, tm, tn), lambda i, j, kk: (j // (co // tn), i, j % (co // tn)))
    nk = k // tk

    def tile_product(a_ref, b_ref):
        return lax.dot_general(a_ref[...].astype(BF16), b_ref[...].astype(BF16), dims, preferred_element_type=F32)

    def body_one_step(a_ref, b_ref, *rest):
        o_ref = rest[-1]
        o_ref[...] = tile_product(a_ref, b_ref).astype(o_ref.dtype)

    def body(a_ref, b_ref, *rest):
        o_ref, acc_ref = rest[-2:]
        kk = pl.program_id(2)

        @pl.when(kk == 0)
        def _():
            acc_ref[...] = jnp.zeros(acc_ref.shape, F32)

        acc_ref[...] += tile_product(a_ref, b_ref)

        @pl.when(kk == nk - 1)
        def _():
            o_ref[...] = acc_ref[...].astype(o_ref.dtype)

    return pl.pallas_call(
        body_one_step if nk == 1 else body, name=name, grid=(m // tm, n // tn, nk),
        in_specs=[a_spec, b_spec] + [pl.BlockSpec(memory_space=pl.ANY)] * len(after), out_specs=o_spec,
        out_shape=jax.ShapeDtypeStruct((nbo, m, co), out_dtype),
        scratch_shapes=[] if nk == 1 else [pltpu.VMEM((tm, tn), F32)], compiler_params=_cparams(3))(a, b, *after)


def _rms(x, g):
    return x * lax.rsqrt(jnp.mean(x * x, axis=-1, keepdims=True) + NORM_EPS) * g


def _sigmoid(x):
    return 1.0 / (1.0 + jnp.exp(-x))


def _silu(x):
    return x * _sigmoid(x)


def _gelu(x):
    return 0.5 * x * (1.0 + lax.erf(x * 0.7071067811865476))


def _softplus(x):
    return jnp.maximum(x, 0.0) + jnp.log(1.0 + jnp.exp(-jnp.abs(x)))


def _split(x, n):
    w = x.shape[-1] // n
    return [x[..., h * w:(h + 1) * w] for h in range(n)]


def _sgu(ua, va, smw, smb, lng, lnb):
    c = ua[0].shape[0]
    width = len(va) * va[0].shape[-1]
    vf = [_gelu(v) for v in va]
    mu = sum(jnp.sum(v, axis=-1, keepdims=True) for v in vf) / width
    var = sum(jnp.sum(jnp.square(v - mu), axis=-1, keepdims=True) for v in vf) / width
    r = lax.rsqrt(var + NORM_EPS)
    causal = lax.broadcasted_iota(jnp.int32, (c, c), 0) >= lax.broadcasted_iota(jnp.int32, (c, c), 1)
    outs = []
    for h in range(len(ua)):
        vn = (vf[h] - mu) * r * lng[h] + lnb[h]
        w = jnp.where(causal, smw[h], 0.0)
        mixed = jnp.dot(w.astype(BF16), vn.astype(BF16), preferred_element_type=F32) + smb[h]
        outs.append(_gelu(ua[h]) * mixed)
    return outs


def _shift_rows(x, k, up):
    rows = x.shape[0]
    row = lax.broadcasted_iota(jnp.int32, x.shape, 0)
    if up:
        return jnp.where(row < rows - k, pltpu.roll(x, rows - k, 0), 0.0)
    return jnp.where(row >= k, pltpu.roll(x, k, 0), 0.0)


@functools.partial(jax.custom_vjp, nondiff_argnums=(1,))
def _delay(x, k):
    return _shift_rows(x, k, False)


def _delay_fwd(x, k):
    return _shift_rows(x, k, False), None


def _delay_bwd(k, _, g):
    return (_shift_rows(g, k, True),)


_delay.defvjp(_delay_fwd, _delay_bwd)


def _conv_silu(x, w0, w1, w2, w3):
    y = w3 * x + w2 * _delay(x, 1) + w1 * _delay(x, 2) + w0 * _delay(x, 3)
    return _silu(y)


_BDOT_DIMS = {"nn": (((2,), (1,)), ((0,), (0,))), "nt": (((2,), (2,)), ((0,), (0,))), "tn": (((1,), (1,)), ((0,), (0,)))}


def _bdot_passes(a, b, kind, passes):
    one = lambda x, y: lax.dot_general(x, y, _BDOT_DIMS[kind], preferred_element_type=F32)
    ah, bh = a.astype(BF16), b.astype(BF16)
    if passes == 1:
        return one(ah, bh)
    al, bl = (a - ah.astype(F32)).astype(BF16), (b - bh.astype(F32)).astype(BF16)
    return one(ah, bh) + (one(ah, bl) + one(al, bh))


@functools.partial(jax.custom_vjp, nondiff_argnums=(2, 3))
def bdot(a, b, kind, passes):
    return _bdot_passes(a, b, kind, passes)


def _bdot_fwd(a, b, kind, passes):
    return _bdot_passes(a, b, kind, passes), (a, b)


def _bdot_bwd(kind, passes, res, g):
    a, b = res
    if kind == "nn":
        return bdot(g, b, "nt", passes), bdot(a, g, "tn", passes)
    if kind == "nt":
        return bdot(g, b, "nn", passes), bdot(g, a, "tn", passes)
    return bdot(b, g, "nt", passes), bdot(a, g, "nn", passes)


bdot.defvjp(_bdot_fwd, _bdot_bwd)


def _dn_step(s, q, k, v, braw, araw, alog, dtb):
    nh, c, d = q.shape
    row = lax.broadcasted_iota(jnp.int32, (c, c), 0)
    col = lax.broadcasted_iota(jnp.int32, (c, c), 1)
    causal, strict = (row >= col)[None], (row > col)[None]
    lower = (row >= col).astype(F32)
    strict_f = jnp.broadcast_to((row > col).astype(F32)[None], (nh, c, c))
    eye = (row == col).astype(F32)[None]

    qn = q * lax.rsqrt(jnp.sum(q * q, axis=-1, keepdims=True) + NORM_EPS) * (d ** -0.5)
    kn = k * lax.rsqrt(jnp.sum(k * k, axis=-1, keepdims=True) + NORM_EPS)
    beta = _sigmoid(braw)
    g = -jnp.exp(alog) * _softplus(araw + dtb)
    lg = lower[None] * g
    gcum = jnp.sum(lg, axis=-1, keepdims=True)
    diff = bdot(lg, strict_f, "nn", 3)
    decay = jnp.where(causal, jnp.exp(diff), 0.0)
    bcol = jnp.sum(eye * beta, axis=-1, keepdims=True)
    kb = kn * bcol
    a = jnp.where(strict, bdot(kb, kn, "nt", 1) * decay, 0.0)
    inv = eye - a
    p = bdot(a, a, "nn", 3)
    n_fac = int(math.log2(c)) - 1
    for it in range(n_fac):
        inv = inv + bdot(inv, p, "nn", 3)
        if it < n_fac - 1:
            p = bdot(p, p, "nn", 3)
    eg = jnp.exp(gcum)
    u = bdot(inv, v * bcol, "nn", 1)
    w = bdot(inv, kb * eg, "nn", 1)
    qk = jnp.where(causal, bdot(qn, kn, "nt", 1) * decay, 0.0)
    v_new = u - bdot(w, s, "nn", 1)
    o = bdot(qn * eg, s, "nn", 1) + bdot(qk, v_new, "nn", 1)
    glast = jnp.sum(g, axis=-1, keepdims=True)
    kdec = kn * jnp.exp(glast - gcum)
    s_new = s * jnp.exp(glast) + bdot(kdec, v_new, "tn", 1)
    return s_new, o


def _ogate(o, z, w):
    return [_rms(oh, w) * _silu(zh) for oh, zh in zip(o, z)]


def _xattn(q, k, v):
    outs = []
    for qh, kh, vh in zip(q, k, v):
        s = lax.dot_general(qh.astype(BF16), kh.astype(BF16), (((1,), (1,)), ((), ())),
                            preferred_element_type=F32) * (qh.shape[-1] ** -0.5)
        s = s - jnp.max(s, axis=-1, keepdims=True)
        e = jnp.exp(s)
        p = e / jnp.sum(e, axis=-1, keepdims=True)
        outs.append(jnp.dot(p.astype(BF16), vh.astype(BF16), preferred_element_type=F32))
    return outs


def _rows(t):
    return min(ROW_TILE, t)


def k_rms_fwd(name, x, g, after=()):
    t, d = x.shape
    tm = _rows(t)
    return stepk(name, lambda xv, gv: (_rms(xv, gv),), (t // tm,),
                 [(x, (tm, d), lambda i: (i, 0)), (g, (1, d), lambda i: (0, 0))],
                 [((t, d), BF16, (tm, d), lambda i: (i, 0))], after=after)[0]


def k_rms_bwd(name, x, g, dhs, dx_res):
    t, d = x.shape
    tm = _rows(t)
    n = len(dhs)

    def fn(xv, gv, *rest):
        dh = sum(r.astype(F32) for r in rest[:n])
        dxr, dg_c = rest[n], rest[n + 1]
        _, vjp = jax.vjp(_rms, xv, gv)
        dx, dg = vjp(dh)
        return dx + dxr, dg_c + dg

    row = lambda arr: (arr, (tm, d), lambda i: (i, 0))
    return stepk(name, fn, (t // tm,), [row(x), (g, (1, d), lambda i: (0, 0))] + [row(h) for h in dhs] + [row(dx_res)],
                 [((t, d), F32, (tm, d), lambda i: (i, 0))], carries=[((1, d), F32)])


def k_post_fwd(name, x, f, g, scale, after=()):
    t, d = x.shape
    tm = _rows(t)
    row = lambda arr: (arr, (tm, d), lambda i: (i, 0))
    return stepk(name, lambda xv, fv, gv: (xv + scale * _rms(fv, gv),), (t // tm,),
                 [row(x), row(f), (g, (1, d), lambda i: (0, 0))], [((t, d), F32, (tm, d), lambda i: (i, 0))],
                 after=after)[0]


def k_post_bwd(name, f, g, dxo, scale, after=()):
    t, d = f.shape
    tm = _rows(t)

    def fn(fv, gv, dv, dg_c):
        _, vjp = jax.vjp(lambda a, b: scale * _rms(a, b), fv, gv)
        df, dg = vjp(dv)
        return df, dg_c + dg

    row = lambda arr: (arr, (tm, d), lambda i: (i, 0))
    return stepk(name, fn, (t // tm,), [row(f), (g, (1, d), lambda i: (0, 0)), row(dxo)],
                 [((t, d), BF16, (tm, d), lambda i: (i, 0))], carries=[((1, d), F32)], after=after)


def k_swiglu_fwd(name, hu, after=()):
    nb, t, c = hu.shape
    half = nb // 2
    tm = _rows(t)
    fn = lambda gv, uv: (_silu(gv.astype(F32)) * uv.astype(F32),)
    return stepk(name, fn, (half, t // tm),
                 [(hu, (None, tm, c), lambda b, i: (b, i, 0)), (hu, (None, tm, c), lambda b, i: (b + half, i, 0))],
                 [((half, t, c), BF16, (None, tm, c), lambda b, i: (b, i, 0))], after=after)[0]


def k_swiglu_bwd(name, hu, da):
    nb, t, c = hu.shape
    half = nb // 2
    tm = _rows(t)

    def fn(gv, uv, dv):
        gv, uv, dv = gv.astype(F32), uv.astype(F32), dv.astype(F32)
        sg = _sigmoid(gv)
        return (jnp.stack([dv * uv * sg * (1.0 + gv * (1.0 - sg)), dv * gv * sg]),)

    blk = lambda arr, off: (arr, (None, tm, c), lambda b, i: (b + off, i, 0))
    out = stepk(name, fn, (half, t // tm), [blk(hu, 0), blk(hu, half), blk(da, 0)],
                [((2, half, t, c), BF16, (2, None, tm, c), lambda b, i: (0, b, i, 0))])[0]
    return out.reshape(nb, t, c)


def k_loss(name, y, tgt):
    t, d = y.shape
    tm = _rows(t)

    def fn(yv, tv, acc):
        e = yv - tv
        part = jnp.sum(jnp.sum(e * e, axis=-1, keepdims=True), axis=0, keepdims=True)
        return e * (1.0 / d), acc + (0.5 / d) * part

    row = lambda arr: (arr, (tm, d), lambda i: (i, 0))
    return stepk(name, fn, (t // tm,), [row(y), row(tgt)], [((t, d), F32, (tm, d), lambda i: (i, 0))],
                 carries=[((1, 1), F32)])


GM_W = GM_HEADS * 128
QKV_COL0 = 2 * GM_W
QKV_W = 3 * DN_HEADS * DN_DIM
Z_COL0 = QKV_COL0 + QKV_W
MAIN_W = Z_COL0 + DN_HEADS * DN_DIM


def _sgu_ins(proj, p):
    c = GM_CHUNK
    return [(proj, (c, GM_W), lambda i: (i, 0)), (proj, (c, GM_W), lambda i: (i, 1)),
            (p["sm_w"], (GM_HEADS, c, c), lambda i: (0, 0, 0)), (p["sm_b"], (GM_HEADS, c, 1), lambda i: (0, 0, 0)),
            (p["sm_ln_g"], (1, GM_W), lambda i: (0, 0)), (p["sm_ln_b"], (1, GM_W), lambda i: (0, 0))]


def _sgu_lists(uv, vv, sw, sb, lg, lb):
    nh = GM_HEADS
    return (_split(uv, nh), _split(vv, nh), [sw[h] for h in range(nh)], [sb[h] for h in range(nh)],
            _split(lg, nh), _split(lb, nh))


def k_sgu_fwd(name, proj, p):
    t = proj.shape[0]

    def fn(*vals):
        return (jnp.concatenate(_sgu(*_sgu_lists(*vals)), axis=-1),)

    return stepk(name, fn, (t // GM_CHUNK,), _sgu_ins(proj, p),
                 [((t, GM_W), BF16, (GM_CHUNK, GM_W), lambda i: (i, 0))])[0]


def k_sgu_bwd(name, proj, p, dy):
    t = proj.shape[0]
    c = GM_CHUNK

    def fn(uv, vv, sw, sb, lg, lb, dv, dsw, dsb, dlg, dlb):
        _, vjp = jax.vjp(_sgu, *_sgu_lists(uv, vv, sw, sb, lg, lb))
        gu, gv, gsw, gsb, glg, glb = vjp(_split(dv.astype(F32), GM_HEADS))
        cat = lambda l: jnp.concatenate(l, axis=-1)
        return (cat(gu), cat(gv), dsw + jnp.stack(gsw), dsb + jnp.stack(gsb), dlg + cat(glg), dlb + cat(glb))

    return stepk(name, fn, (t // c,), _sgu_ins(proj, p) + [(dy, (None, c, GM_W), lambda i: (0, i, 0))],
                 [((t, GM_W), BF16, (c, GM_W), lambda i: (i, 0)), ((t, GM_W), BF16, (c, GM_W), lambda i: (i, 0))],
                 carries=[((GM_HEADS, c, c), F32), ((GM_HEADS, c, 1), F32), ((1, GM_W), F32), ((1, GM_W), F32)])


def _conv_ins(proj, conv_w):
    t = proj.shape[0]
    return [(proj, (t, 128), lambda j: (0, QKV_COL0 // 128 + j)), (conv_w, (CONV_W, 128), lambda j: (0, j))]


def k_conv_fwd(name, proj, conv_w):
    t = proj.shape[0]
    n = QKV_W // 128
    fn = lambda xv, wv: (_conv_silu(xv, *[wv[i:i + 1] for i in range(CONV_W)]),)
    return stepk(name, fn, (n,), _conv_ins(proj, conv_w), [((n, t, 128), F32, (None, t, 128), lambda j: (j, 0, 0))])[0]


def k_conv_bwd(name, proj, conv_w, dqkv):
    t = proj.shape[0]
    n = QKV_W // 128

    def fn(xv, wv, dv):
        _, vjp = jax.vjp(_conv_silu, xv, *[wv[i:i + 1] for i in range(CONV_W)])
        gx, *gw = vjp(dv)
        return gx, jnp.concatenate(gw, axis=0)

    return stepk(name, fn, (n,), _conv_ins(proj, conv_w) + [(dqkv, (None, t, 128), lambda j: (j, 0, 0))],
                 [((t, QKV_W), BF16, (t, 128), lambda j: (0, j)), ((CONV_W, QKV_W), F32, (CONV_W, 128), lambda j: (0, j))])


def _dn_ins(qkv, braw, araw, p, order):
    h, c, d = DN_HEADS, DN_CHUNK, DN_DIM
    qkv_in = lambda part: (qkv, (h, c, d), lambda n: (part, order(n), 0))
    gate_in = lambda arr: (arr, (None, h, 1, c), lambda n: (order(n), 0, 0, 0))
    par_in = lambda arr: (arr, (h, 1, 1), lambda n: (0, 0, 0))
    return [qkv_in(0), qkv_in(1), qkv_in(2), gate_in(braw), gate_in(araw), par_in(p["a_log"]), par_in(p["dt_bias"])]


def k_dn_fwd(name, qkv, braw, araw, p):
    t = qkv.shape[1]
    h, c, d = DN_HEADS, DN_CHUNK, DN_DIM
    nc = t // c

    def fn(q, k, v, b, a, al, dt, s):
        s_new, o = _dn_step(s, q, k, v, b, a, al, dt)
        return o, s, s_new

    o, s_all, _ = stepk(name, fn, (nc,), _dn_ins(qkv, braw, araw, p, lambda n: n),
                        [((h, t, d), F32, (h, c, d), lambda n: (0, n, 0)),
                         ((nc, h, d, d), F32, (None, h, d, d), lambda n: (n, 0, 0, 0))],
                        carries=[((h, d, d), F32)])
    return o, s_all


def k_dn_bwd(name, qkv, braw, araw, p, s_all, do):
    t = qkv.shape[1]
    h, c, d = DN_HEADS, DN_CHUNK, DN_DIM
    nc = t // c
    rev = lambda n: nc - 1 - n

    def fn(q, k, v, b, a, al, dt, s, dov, ds_c, dal_c, ddt_c):
        _, vjp = jax.vjp(_dn_step, s, q, k, v, b, a, al, dt)
        ds, dq, dk, dv, db, da, dal, ddt = vjp((ds_c, dov))
        return dq, dk, dv, db, da, ds, dal_c + dal, ddt_c + ddt

    ins = _dn_ins(qkv, braw, araw, p, rev) + [(s_all, (None, h, d, d), lambda n: (rev(n), 0, 0, 0)),
                                               (do, (h, c, d), lambda n: (0, rev(n), 0))]
    hd = ((h, t, d), F32, (h, c, d), lambda n: (0, rev(n), 0))
    gate = ((nc, h, 1, c), F32, (None, h, 1, c), lambda n: (rev(n), 0, 0, 0))
    dq, dk, dv, db, da, _, dal, ddt = stepk(name, fn, (nc,), ins, [hd, hd, hd, gate, gate],
                                            carries=[((h, d, d), F32), ((h, 1, 1), F32), ((h, 1, 1), F32)])
    return dq, dk, dv, db, da, dal, ddt


def _ogate_ins(o, proj, p):
    t = o.shape[1]
    tm = _rows(t)
    return tm, [(o, (DN_HEADS, tm, DN_DIM), lambda i: (0, i, 0)), (proj, (tm, GM_W), lambda i: (i, Z_COL0 // GM_W)),
                (p["dn_norm_w"], (1, DN_DIM), lambda i: (0, 0))]


def k_ogate_fwd(name, o, proj, p):
    t = o.shape[1]
    tm, ins = _ogate_ins(o, proj, p)

    def fn(ov, zv, wv):
        return (jnp.concatenate(_ogate([ov[h] for h in range(DN_HEADS)], _split(zv, DN_HEADS), wv), axis=-1),)

    return stepk(name, fn, (t // tm,), ins, [((t, GM_W), BF16, (tm, GM_W), lambda i: (i, 0))])[0]


def k_ogate_bwd(name, o, proj, p, dy):
    t = o.shape[1]
    tm, ins = _ogate_ins(o, proj, p)

    def fn(ov, zv, wv, dv, dw_c):
        _, vjp = jax.vjp(_ogate, [ov[h] for h in range(DN_HEADS)], _split(zv, DN_HEADS), wv)
        go, gz, gw = vjp(_split(dv.astype(F32), DN_HEADS))
        return jnp.stack(go), jnp.concatenate(gz, axis=-1), dw_c + gw

    return stepk(name, fn, (t // tm,), ins + [(dy, (None, tm, GM_W), lambda i: (1, i, 0))],
                 [((DN_HEADS, t, DN_DIM), F32, (DN_HEADS, tm, DN_DIM), lambda i: (0, i, 0)),
                  ((t, GM_W), BF16, (tm, GM_W), lambda i: (i, 0))], carries=[((1, DN_DIM), F32)])


def _xattn_lists(qv, kvv):
    nh = XA_HEADS
    return (_split(qv.astype(F32), nh), [kvv[h].astype(F32) for h in range(nh)],
            [kvv[nh + h].astype(F32) for h in range(nh)])


def k_xattn_fwd(name, q, kv):
    t, d = q.shape
    tm = _rows(t)
    fn = lambda qv, kvv: (jnp.concatenate(_xattn(*_xattn_lists(qv, kvv)), axis=-1),)
    return stepk(name, fn, (t // tm,), [(q, (tm, d), lambda i: (i, 0)), (kv, kv.shape, lambda i: (0, 0, 0))],
                 [((t, d), BF16, (tm, d), lambda i: (i, 0))])[0]


def k_xattn_bwd(name, q, kv, do):
    t, d = q.shape
    tm = _rows(t)

    def fn(qv, kvv, dv, dkv_c):
        _, vjp = jax.vjp(_xattn, *_xattn_lists(qv, kvv))
        gq, gk, gv = vjp(_split(dv.astype(F32), XA_HEADS))
        return jnp.concatenate(gq, axis=-1), dkv_c + jnp.stack(gk + gv)

    return stepk(name, fn, (t // tm,),
                 [(q, (tm, d), lambda i: (i, 0)), (kv, kv.shape, lambda i: (0, 0, 0)), (do, (None, tm, d), lambda i: (0, i, 0))],
                 [((t, d), BF16, (tm, d), lambda i: (i, 0))], carries=[(kv.shape, F32)])


def ffn_fwd(tag, x, p, pre, post, gu, dn, sch):
    h = k_rms_fwd(f"{tag}_pre", x, p[pre], sch.take())
    hu = mm(f"{tag}_gu", h[None], p[gu], "nn", BF16, nbo=N_DEV)
    a = k_swiglu_fwd(f"{tag}_act", hu)
    sch.mid(a, late=False)
    f = mm(f"{tag}_down", a, p[dn], "nn", F32, after=sch.take())[0]
    sch.mid(f, late=True)
    y = k_post_fwd(f"{tag}_post", x, f, p[post], 0.5, sch.take())
    return y, (x, h, hu, a, f)


class NoExchange:
    def take(self):
        return ()

    def emit(self, grads, names):
        pass

    def mid(self, arr, late):
        pass


def ffn_bwd(tag, dy, saved, p, pre, post, gu, dn, grads, sch):
    x, h, hu, a, f = saved
    df, grads[post] = k_post_bwd(f"{tag}_post_b", f, p[post], dy, 0.5, sch.take())
    da = mm(f"{tag}_down_dx", df[None], p[dn], "nt", BF16, nbo=N_DEV // 2)
    dhu = k_swiglu_bwd(f"{tag}_act_b", hu, da)
    grads[gu] = mm(f"{tag}_gu_dw", h[None], dhu, "tn", BF16, nbo=N_DEV)
    sch.emit(grads, (gu,))
    grads[dn] = mm(f"{tag}_down_dw", a, df[None], "tn", BF16, after=sch.take())
    sch.emit(grads, (dn,))
    dh = mm(f"{tag}_gu_dx", dhu, p[gu], "nt", BF16, after=sch.take())[0]
    dx, grads[pre] = k_rms_bwd(f"{tag}_pre_b", x, p[pre], [dh], dy)
    return dx


def _to_chunks(a):
    t, h = a.shape
    return a.reshape(t // DN_CHUNK, DN_CHUNK, h).transpose(0, 2, 1).reshape(t // DN_CHUNK, h, 1, DN_CHUNK)


def _from_chunks(a):
    nc, h, _, c = a.shape
    return a.reshape(nc, h, c).transpose(0, 2, 1).reshape(nc * c, h)


def mix_fwd(tag, x, p, sch):
    h = k_rms_fwd(f"{tag}_pre", x, p["mix_norm_pre"], sch.take())
    proj = mm(f"{tag}_in", h[None], p["w_in_t"], "nt", F32, b_rows=(0, MAIN_W))[0]
    sch.mid(proj, late=False)
    small = mm(f"{tag}_in_s", h[None], p["w_in_t"], "nt", F32, after=sch.take(), b_rows=(MAIN_W, 2 * DN_HEADS))[0]
    braw, araw = _to_chunks(small[:, :DN_HEADS]), _to_chunks(small[:, DN_HEADS:2 * DN_HEADS])
    ya = k_sgu_fwd(f"{tag}_sgu", proj, p)
    qkv = k_conv_fwd(f"{tag}_conv", proj, p["conv_w"])
    o, s_all = k_dn_fwd(f"{tag}_dn", qkv, braw, araw, p)
    yb = k_ogate_fwd(f"{tag}_og", o, proj, p)
    y = jnp.stack([ya, yb])
    m = mm(f"{tag}_out", y, p["w_out"], "nn", F32)[0]
    out = k_post_fwd(f"{tag}_post", x, m, p["mix_norm_post"], 1.0)
    return out, (x, h, proj, braw, araw, qkv, o, s_all, y, m)


def mix_bwd(tag, dy, saved, p, grads, sch):
    x, h, proj, braw, araw, qkv, o, s_all, y, m = saved
    dm, grads["mix_norm_post"] = k_post_bwd(f"{tag}_post_b", m, p["mix_norm_post"], dy, 1.0, sch.take())
    grads["w_out"] = mm(f"{tag}_out_dw", y, dm[None], "tn", BF16)
    sch.emit(grads, ("w_out",))
    dyy = mm(f"{tag}_out_dx", dm[None], p["w_out"], "nt", BF16, nbo=2, after=sch.take())
    do, dz, grads["dn_norm_w"] = k_ogate_bwd(f"{tag}_og_b", o, proj, p, dyy)
    dq, dk, dv, db, da, grads["a_log"], grads["dt_bias"] = k_dn_bwd(f"{tag}_dn_b", qkv, braw, araw, p, s_all, do)
    dconv, grads["conv_w"] = k_conv_bwd(f"{tag}_conv_b", proj, p["conv_w"], jnp.concatenate([dq, dk, dv], axis=0))
    du, dva, grads["sm_w"], grads["sm_b"], grads["sm_ln_g"], grads["sm_ln_b"] = k_sgu_bwd(f"{tag}_sgu_b", proj, p, dyy)
    dproj = jnp.concatenate([du, dva, dconv, dz], axis=-1)
    dsmall = jnp.concatenate([_from_chunks(db), _from_chunks(da)], axis=-1)
    g_main = mm(f"{tag}_in_dw", dproj[None], h[None], "tn", BF16)[0]
    g_small = mm(f"{tag}_in_s_dw", dsmall[None], h[None], "tn", BF16)[0]
    g_in = jnp.concatenate([g_main, g_small], axis=0)
    grads["w_in"] = g_in.reshape(N_DEV, g_in.shape[0] // N_DEV, g_in.shape[1])
    sch.emit(grads, ("w_in",))
    dh = mm(f"{tag}_in_dx", dproj[None], p["w_in_t"], "nn", BF16, after=sch.take(), b_rows=(0, MAIN_W))[0]
    dh_s = mm(f"{tag}_in_s_dx", dsmall[None], p["w_in_t"], "nn", BF16, b_rows=(MAIN_W, 2 * DN_HEADS))[0]
    dx, grads["mix_norm_pre"] = k_rms_bwd(f"{tag}_pre_b", x, p["mix_norm_pre"], [dh, dh_s], dy)
    return dx


def xa_fwd(tag, x, mem, p, sch):
    after = sch.take()
    hx = k_rms_fwd(f"{tag}_pre", x, p["xa_norm_pre"], after)
    mh = k_rms_fwd(f"{tag}_mem", mem, p["mem_norm"], after)
    q = mm(f"{tag}_q", hx[None], p["w_xq"], "nn", BF16)[0]
    sch.mid(q, late=False)
    kv = mm(f"{tag}_kv", mh[None], p["w_xkv"], "nn", BF16, nbo=N_DEV, after=sch.take())
    o = k_xattn_fwd(f"{tag}_att", q, kv)
    sch.mid(o, late=True)
    c = mm(f"{tag}_o", o[None], p["w_xo"], "nn", F32, after=sch.take())[0]
    out = k_post_fwd(f"{tag}_post", x, c, p["xa_norm_post"], 1.0)
    return out, (x, mem, hx, mh, q, kv, o, c)


def xa_bwd(tag, dy, saved, p, grads, sch):
    x, mem, hx, mh, q, kv, o, c = saved
    dc, grads["xa_norm_post"] = k_post_bwd(f"{tag}_post_b", c, p["xa_norm_post"], dy, 1.0, sch.take())
    grads["w_xo"] = mm(f"{tag}_o_dw", o[None], dc[None], "tn", BF16)
    sch.emit(grads, ("w_xo",))
    do = mm(f"{tag}_o_dx", dc[None], p["w_xo"], "nt", BF16, after=sch.take())
    dq, dkv = k_xattn_bwd(f"{tag}_att_b", q, kv, do)
    dkv16 = dkv.astype(BF16)
    grads["w_xq"] = mm(f"{tag}_q_dw", hx[None], dq[None], "tn", BF16)
    grads["w_xkv"] = mm(f"{tag}_kv_dw", mh[None], dkv16, "tn", BF16, nbo=N_DEV)
    sch.emit(grads, ("w_xq", "w_xkv"))
    dhx = mm(f"{tag}_q_dx", dq[None], p["w_xq"], "nt", BF16, after=sch.take())[0]
    dmh = mm(f"{tag}_kv_dx", dkv16, p["w_xkv"], "nt", BF16)[0]
    _, grads["mem_norm"] = k_rms_bwd(f"{tag}_mem_b", mem, p["mem_norm"], [dmh], jnp.zeros_like(mem))
    dx, grads["xa_norm_pre"] = k_rms_bwd(f"{tag}_pre_b", x, p["xa_norm_pre"], [dhx], dy)
    return dx


def _place():
    return lax.axis_index("x"), lax.axis_index("y"), lax.axis_index("c")


def _other_chips(x, y):
    return [(1 - x, y), (x, 1 - y), (1 - x, 1 - y)]


_ANY = pl.BlockSpec(memory_space=pl.ANY)


def all_gather(name, shards, after=()):
    n, na = len(shards), len(after)

    def body(*refs):
        ins, outs = refs[:n], refs[n + na:2 * n + na]
        send_sems, recv_sems, local_sems = refs[2 * n + na:]
        x, y, c = _place()
        me, sibling = (x, y, c), (x, y, 1 - c)
        chips = _other_chips(x, y)
        idx = lambda px, py, pc: 4 * px + 2 * py + pc

        def copy(a, k, block, to, src=None):
            dst = outs[a].at[idx(*block)]
            return pltpu.make_async_remote_copy(src_ref=dst if src is None else src, dst_ref=dst,
                                                send_sem=send_sems.at[7 * a + k], recv_sem=recv_sems.at[7 * a + k],
                                                device_id=to, device_id_type=MESH)

        mine = [pltpu.make_async_copy(ins[a], outs[a].at[idx(*me)], local_sems.at[a]) for a in range(n)]
        for cp in mine:
            cp.start()
        first = []
        for a in range(n):
            first.append(copy(a, 0, me, sibling, src=ins[a]))
            first += [copy(a, 1 + j, me, (*chip, c), src=ins[a]) for j, chip in enumerate(chips)]
        for cp in first:
            cp.start()
        passed = []
        for a in range(n):
            for j, chip in enumerate(chips):
                copy(a, 1 + j, (*chip, c), me).wait_recv()
                passed.append(copy(a, 4 + j, (*chip, c), sibling))
                passed[-1].start()
        for a in range(n):
            copy(a, 0, sibling, me).wait_recv()
            for j, chip in enumerate(chips):
                copy(a, 4 + j, (*chip, 1 - c), me).wait_recv()
        for cp in first + passed:
            cp.wait_send()
        for cp in mine:
            cp.wait()

    return pl.pallas_call(
        body, name=name, in_specs=[_ANY] * (n + na), out_specs=[_ANY] * n,
        out_shape=[jax.ShapeDtypeStruct((N_DEV,) + s.shape, s.dtype) for s in shards],
        scratch_shapes=[pltpu.SemaphoreType.DMA((7 * n,)), pltpu.SemaphoreType.DMA((7 * n,)),
                        pltpu.SemaphoreType.DMA((n,))])(*shards, *after)


def k_cast_place(name, w, layer, me, after):
    _, r, cc = w.shape
    tr, tc = _tile2(r, cc)
    return stepk(name, lambda v: (v,), (r // tr, cc // tc), [(w, (None, tr, tc), lambda i, j, pre: (layer, i, j))],
                 [((N_DEV, r, cc), BF16, (None, tr, tc), lambda i, j, pre: (pre[0], i, j))], prefetch=me,
                 after=after)[0]


_HBM = pl.BlockSpec(memory_space=pltpu.HBM)
_SEM = pl.BlockSpec(memory_space=pltpu.SEMAPHORE)
_DATAFLOW = pltpu.SideEffectType.DATAFLOW_SIDE_EFFECTING


def xfer_start(name, plan, n_sems, srcs, lands, after=()):
    ns, nl, na = len(srcs), len(lands), len(after)
    bufs = list(srcs) + list(lands)

    def body(*refs):
        send_sems, recv_sems, token = refs[ns + nl + na], refs[ns + nl + na + 1], refs[-1]
        for cp in plan(refs[:ns], refs[ns:ns + nl], send_sems, recv_sems):
            cp.start()
        token[...] = jnp.zeros(token.shape, token.dtype)

    res = pl.pallas_call(
        body, name=name, in_specs=[_HBM] * (ns + nl) + [_ANY] * na,
        out_specs=(_SEM, _SEM, *[_HBM] * (ns + nl), pl.BlockSpec(memory_space=pltpu.VMEM)),
        out_shape=(pltpu.SemaphoreType.DMA((n_sems,)), pltpu.SemaphoreType.DMA((n_sems,)),
                   *[pltpu.HBM(b.shape, b.dtype) for b in bufs], jax.ShapeDtypeStruct((8, 128), F32)),
        input_output_aliases={i: 2 + i for i in range(ns + nl)},
        compiler_params=pltpu.CompilerParams(has_side_effects=_DATAFLOW),
    )(*[pltpu.with_memory_space_constraint(b, pltpu.HBM) for b in bufs], *after)
    return dict(send=res[0], recv=res[1], srcs=list(res[2:2 + ns]), lands=list(res[2 + ns:2 + ns + nl]), token=res[-1])


def xfer_wait(name, plan, started, after=()):
    ns, nl = len(started["srcs"]), len(started["lands"])
    bufs = started["srcs"] + started["lands"]

    def body(*refs):
        for cp in plan(refs[:ns], refs[ns:ns + nl], refs[ns + nl], refs[ns + nl + 1]):
            cp.wait_send()
            cp.wait_recv()

    res = pl.pallas_call(
        body, name=name, in_specs=[_HBM] * (ns + nl) + [_SEM, _SEM] + [_ANY] * len(after),
        out_specs=tuple([_HBM] * (ns + nl)), out_shape=tuple(pltpu.HBM(b.shape, b.dtype) for b in bufs),
        input_output_aliases={i: i for i in range(ns + nl)},
        compiler_params=pltpu.CompilerParams(has_side_effects=_DATAFLOW),
    )(*bufs, started["send"], started["recv"], *after)
    return list(res[:ns]), list(res[ns:])


def _remote(src, dst, send_sems, recv_sems, k, to):
    return pltpu.make_async_remote_copy(src_ref=src, dst_ref=dst, send_sem=send_sems.at[k], recv_sem=recv_sems.at[k],
                                        device_id=to, device_id_type=MESH)


def plan_gather_out(srcs, lands, send_sems, recv_sems):
    x, y, c = _place()
    me = 4 * x + 2 * y + c
    targets = [(x, y, 1 - c)] + [(px, py, c) for px, py in _other_chips(x, y)]
    return [_remote(lands[a].at[me], lands[a].at[me], send_sems, recv_sems, 4 * a + k, to)
            for a in range(len(lands)) for k, to in enumerate(targets)]


def plan_gather_pass(srcs, lands, send_sems, recv_sems):
    x, y, c = _place()
    return [_remote(lands[a].at[4 * px + 2 * py + c], lands[a].at[4 * px + 2 * py + c], send_sems, recv_sems,
                    3 * a + j, (x, y, 1 - c))
            for a in range(len(lands)) for j, (px, py) in enumerate(_other_chips(x, y))]


def plan_scatter_sibling(srcs, lands, send_sems, recv_sems):
    x, y, c = _place()
    return [_remote(srcs[a].at[j, 1 - c], lands[a].at[j], send_sems, recv_sems, 4 * a + j, (x, y, 1 - c))
            for a in range(len(srcs)) for j in range(4)]


def plan_scatter_chips(srcs, lands, send_sems, recv_sems):
    x, y, c = _place()
    return [_remote(srcs[a].at[2 * px + py], lands[a].at[j], send_sems, recv_sems, 3 * a + j, (px, py, c))
            for a in range(len(srcs)) for j, (px, py) in enumerate(_other_chips(x, y))]


ELEMWISE_BLOCK = 512 * 1024


def _tile2(r, cc, limit=ELEMWISE_BLOCK):
    if r % 16:
        tc = cc
        while r * tc > limit and tc % 256 == 0:
            tc //= 2
        return r, tc
    t = r
    while t * cc > limit and t % 32 == 0:
        t //= 2
    if t * cc > limit:
        for cand in range(t, 15, -16):
            if r % cand == 0 and cand * cc <= limit:
                return cand, cc
    return t, cc


def k_pair_add(name, g4, r1, core):
    _, _, r, cc = g4.shape
    tr, tc = _tile2(r, cc, 2 * ELEMWISE_BLOCK)
    fn = lambda av, bv: (av.astype(F32) + bv.astype(F32),)
    return stepk(name, fn, (4, r // tr, cc // tc),
                 [(g4, (None, None, tr, tc), lambda b, i, j, pre: (b, pre[0], i, j)),
                  (r1, (None, tr, tc), lambda b, i, j, pre: (b, i, j))],
                 [((4, r, cc), BF16, (None, tr, tc), lambda b, i, j, pre: (b, i, j))], prefetch=core)[0]


def _adamw(g, w, m, v):
    m2 = ADAM_B1 * m + (1.0 - ADAM_B1) * g
    v2 = ADAM_B2 * v + (1.0 - ADAM_B2) * jnp.square(g)
    m_hat = m2 / (1.0 - ADAM_B1 ** ADAM_STEP)
    v_hat = v2 / (1.0 - ADAM_B2 ** ADAM_STEP)
    delta = -ADAM_LR * (m_hat / (jnp.sqrt(v_hat) + ADAM_EPS) + ADAM_WD * w)
    return g, delta, m2, v2


def k_adamw_shard(name, part, others, w, m, v, layer, chip, fill):
    _, r, cc = part.shape
    tr, tc = _tile2(r, cc)

    def fn(pv, o0, o1, o2, wv, mv, vv, done):
        g = ((pv.astype(F32) + o0.astype(F32)) + o1.astype(F32)) + o2.astype(F32)
        return _adamw(g, wv, mv, vv) + (done,)

    other = lambda k: (others, (None, tr, tc), lambda i, j, pre: (k, i, j))
    state = lambda arr: (arr, (None, tr, tc), lambda i, j, pre: (layer, i, j))
    out = ((w.shape[0], r, cc), F32, (None, tr, tc), lambda i, j, pre: (layer, i, j))
    return stepk(name, fn, (r // tr, cc // tc),
                 [(part, (None, tr, tc), lambda i, j, pre: (pre[0], i, j)), other(0), other(1), other(2),
                  state(w), state(m), state(v)],
                 [out] * 4, carries=[((8, 128), F32)], prefetch=chip, fill=[(f, k) for k, f in enumerate(fill)])


def k_sum8(name, parts):
    _, rows, lanes = parts.shape

    def fn(pv):
        acc = pv[0]
        for d in range(1, N_DEV):
            acc = acc + pv[d]
        return (acc,)

    return stepk(name, fn, (1,), [(parts, parts.shape, lambda i: (0, 0, 0))],
                 [((rows, lanes), F32, (rows, lanes), lambda i: (0, 0))])[0]


def k_adamw_flat(name, g, w, m, v):
    whole = lambda arr: (arr, arr.shape, lambda i: (0, 0))
    out = (g.shape, F32, g.shape, lambda i: (0, 0))
    return stepk(name, _adamw, (1,), [whole(g), whole(w), whole(m), whole(v)], [out] * 4)


WEIGHTS = ("ffn1_norm_pre", "ffn1_w_gate_up", "ffn1_w_down", "ffn1_norm_post", "mix_norm_pre", "w_in", "conv_w", "a_log",
           "dt_bias", "sm_w", "sm_b", "sm_ln_g", "sm_ln_b", "dn_norm_w", "w_out", "mix_norm_post", "xa_norm_pre", "mem_norm",
           "w_xq", "w_xkv", "w_xo", "xa_norm_post", "ffn2_norm_pre", "ffn2_w_gate_up", "ffn2_w_down", "ffn2_norm_post")
BIG = ("ffn1_w_gate_up", "ffn1_w_down", "w_in", "w_out", "w_xq", "w_xkv", "w_xo", "ffn2_w_gate_up", "ffn2_w_down")
ROW_SHARDED = ("ffn1_w_down", "w_out", "w_xq", "w_xo", "ffn2_w_down")
SMALL = tuple(n for n in WEIGHTS if n not in BIG and n != "conv_w")
N_LAYERS = 2
IN_COLS = MAIN_W + 2 * DN_HEADS


BLOCKS = ("f1", "mx", "xa", "f2")
AG_BEFORE = {"f1": ("ffn1_w_gate_up", "ffn1_w_down"), "mx": ("w_in", "w_out"), "xa": ("w_xq", "w_xkv", "w_xo"),
             "f2": ("ffn2_w_gate_up", "ffn2_w_down")}
AG_LARGE = ("f1", "f2")
RS_IN_FLIGHT = 1
RS_LAST = ("ffn1_w_gate_up", "ffn1_w_down")


def _fwd_block(l, sb, h, mem, p, sch):
    tag = f"l{l}{sb}"
    if sb == "f1":
        return ffn_fwd(tag, h, p, "ffn1_norm_pre", "ffn1_norm_post", "ffn1_w_gate_up", "ffn1_w_down", sch)
    if sb == "mx":
        return mix_fwd(tag, h, p, sch)
    if sb == "xa":
        return xa_fwd(tag, h, mem, p, sch)
    return ffn_fwd(tag, h, p, "ffn2_norm_pre", "ffn2_norm_post", "ffn2_w_gate_up", "ffn2_w_down", sch)


class WeightGather:
    def __init__(self, a, w_in_t, me1, chain):
        self.order = [(l, sb) for l in range(N_LAYERS) for sb in BLOCKS]
        self.going, self.passing, self.tie = {}, {}, []
        for l, sb in self.order:
            lands = [k_cast_place(f"place_l{l}{n}", *((w_in_t[""][l], 0) if n == "w_in" else (a[n], l)), me1, chain)
                     for n in AG_BEFORE[sb]]
            self.going[l, sb] = xfer_start(f"ag_out_l{l}{sb}", plan_gather_out, 4 * len(lands), [], lands)
            chain = [self.going[l, sb]["token"]]
        self.tie = chain
        self.block = None

    def take(self):
        tie, self.tie = self.tie, []
        return tie

    def _pass_on(self, key, after):
        l, sb = key
        _, got = xfer_wait(f"ag_outw_l{l}{sb}", plan_gather_out, self.going.pop(key), after=after)
        self.passing[key] = xfer_start(f"ag_pass_l{l}{sb}", plan_gather_pass, 3 * len(got), [], got)
        self.tie = [self.passing[key]["token"]]

    def mid(self, arr, late):
        later = self.order[self.order.index(self.block) + 1:]
        if later and later[0] in self.going and (later[0][1] in AG_LARGE) == late:
            self._pass_on(later[0], [arr])

    def weights(self, l, sb, after):
        self.block = (l, sb)
        after = list(after) + self.take()
        if (l, sb) in self.going:
            self._pass_on((l, sb), after)
        _, full = xfer_wait(f"ag_passw_l{l}{sb}", plan_gather_pass, self.passing.pop((l, sb)), after=after)
        self.tie = [full[0]]
        return _big_params(dict(zip(AG_BEFORE[sb], full)))


def _bwd_block(l, sb, dy, saved, p, grads, sch):
    tag = f"l{l}{sb}"
    if sb == "f1":
        return ffn_bwd(tag, dy, saved, p, "ffn1_norm_pre", "ffn1_norm_post", "ffn1_w_gate_up", "ffn1_w_down", grads, sch)
    if sb == "mx":
        return mix_bwd(tag, dy, saved, p, grads, sch)
    if sb == "xa":
        return xa_bwd(tag, dy, saved, p, grads, sch)
    return ffn_bwd(tag, dy, saved, p, "ffn2_norm_pre", "ffn2_norm_post", "ffn2_w_gate_up", "ffn2_w_down", grads, sch)


class GradExchange:
    def __init__(self, a, w_in_t, core, chip):
        self.a, self.w_in_t, self.core, self.chip = a, w_in_t, core, chip
        self.layer = None
        self.pending, self.tie, self.out = [], [], {}
        self.w_in_out = [None] * N_LAYERS

    def take(self):
        tie, self.tie = self.tie, []
        return tie

    def _advance(self, chain, drain, waits=True):
        a = self.a
        between_chips = [grp for grp in self.pending if grp["stage"] == "chips"]
        for grp in self.pending:
            if grp["stage"] == "sibling":
                l, names = grp["l"], grp["names"]
                g4, r1 = xfer_wait(f"rs_sibw_l{l}{names[0]}", plan_scatter_sibling, grp["going"], after=chain)
                parts = [k_pair_add(f"rs_add_l{l}{n}", gg, rr, self.core) for n, gg, rr in zip(names, g4, r1)]
                zones = [lax.empty((3,) + p.shape[1:], BF16) for p in parts]
                grp["going"] = xfer_start(f"rs_chip_l{l}{names[0]}", plan_scatter_chips, 3 * len(names), parts, zones)
                grp["stage"], grp["age"] = "chips", 0
                chain = [grp["going"]["token"]]
        for grp in between_chips if waits else ():
            l, names = grp["l"], grp["names"]
            if drain or grp["age"] >= grp["limit"]:
                parts, r2 = xfer_wait(f"rs_chipw_l{l}{names[0]}", plan_scatter_chips, grp["going"], after=chain)
                for n, part, others in zip(names, parts, r2):
                    if n == "w_in":
                        state = [self.w_in_t[pre][l] for pre in ("", "m_", "v_")]
                        *self.w_in_out[l], done = k_adamw_shard(f"adamw_{n}_l{l}", part, others, *state, 0, self.chip, ())
                    else:
                        *self.out[n], done = k_adamw_shard(f"adamw_{n}_l{l}", part, others, a[n], a["m_" + n],
                                                           a["v_" + n], l, self.chip, self.out.get(n, ()))
                    chain = [done]
                self.pending.remove(grp)
            else:
                grp["age"] += 1
        return chain

    def emit(self, grads, names):
        l = self.layer
        g4 = [_scatter_layout(n, grads[n]) for n in names]
        g4 = [t.reshape(4, 2, *t.shape[1:]) for t in g4]
        zones = [lax.empty((4,) + t.shape[2:], BF16) for t in g4]
        going = xfer_start(f"rs_sib_l{l}{names[0]}", plan_scatter_sibling, 4 * len(names), g4, zones)
        chain = self._advance([going["token"]], drain=False)
        self.pending.append(dict(l=l, names=names, stage="sibling", going=going, age=0, limit=RS_IN_FLIGHT))
        if l == 0 and names[0] in RS_LAST:
            chain = self._advance(chain, drain=False, waits=False)
        self.tie = chain

    def finish(self, last):
        chain = [last]
        while self.pending:
            chain = self._advance(chain, drain=True)
        return self.out


def _big_params(full):
    p = {}
    for n, w in full.items():
        if n in ROW_SHARDED:
            p[n] = w.reshape(1, w.shape[0] * w.shape[1], w.shape[2])
        elif n == "w_in":
            p["w_in_t"] = w.reshape(1, w.shape[0] * w.shape[1], w.shape[2])
        else:
            p[n] = w
    return p


def _small_params(l, conv_full, a):
    p = {"conv_w": conv_full[l]}
    for n in SMALL:
        w = a[n][l]
        if n in ("a_log", "dt_bias"):
            p[n] = w.reshape(DN_HEADS, 1, 1)
        elif n == "sm_b":
            p[n] = w[..., None]
        elif n == "sm_w":
            p[n] = w
        else:
            p[n] = w[None]
    return p


def _scatter_layout(n, g):
    if n in ROW_SHARDED:
        return g.reshape(N_DEV, g.shape[1] // N_DEV, g.shape[2])
    return g


def _seg_rows(size):
    return -(-size // 1024) * 8


def _pack(flat_parts):
    return jnp.concatenate([jnp.pad(f, (0, _seg_rows(f.shape[0]) * 128 - f.shape[0])).reshape(-1, 128) for f in flat_parts])


def kernel(x, mem, ffn1_norm_pre, ffn1_w_gate_up, ffn1_w_down, ffn1_norm_post, mix_norm_pre, w_in, conv_w, a_log, dt_bias, sm_w, sm_b, sm_ln_g, sm_ln_b, dn_norm_w, w_out, mix_norm_post, xa_norm_pre, mem_norm, w_xq, w_xkv, w_xo, xa_norm_post, ffn2_norm_pre, ffn2_w_gate_up, ffn2_w_down, ffn2_norm_post, loss_target, m_ffn1_norm_pre, m_ffn1_w_gate_up, m_ffn1_w_down, m_ffn1_norm_post, m_mix_norm_pre, m_w_in, m_conv_w, m_a_log, m_dt_bias, m_sm_w, m_sm_b, m_sm_ln_g, m_sm_ln_b, m_dn_norm_w, m_w_out, m_mix_norm_post, m_xa_norm_pre, m_mem_norm, m_w_xq, m_w_xkv, m_w_xo, m_xa_norm_post, m_ffn2_norm_pre, m_ffn2_w_gate_up, m_ffn2_w_down, m_ffn2_norm_post, v_ffn1_norm_pre, v_ffn1_w_gate_up, v_ffn1_w_down, v_ffn1_norm_post, v_mix_norm_pre, v_w_in, v_conv_w, v_a_log, v_dt_bias, v_sm_w, v_sm_b, v_sm_ln_g, v_sm_ln_b, v_dn_norm_w, v_w_out, v_mix_norm_post, v_xa_norm_pre, v_mem_norm, v_w_xq, v_w_xkv, v_w_xo, v_xa_norm_post, v_ffn2_norm_pre, v_ffn2_w_gate_up, v_ffn2_w_down, v_ffn2_norm_post):
    a = dict(locals())
    px, py, pc = _place()
    core = jnp.reshape(pc, (1,)).astype(jnp.int32)
    chip = jnp.reshape(2 * px + py, (1,)).astype(jnp.int32)
    me = 4 * px + 2 * py + pc
    xs, mems, tgt = x[0], mem[0], loss_target[0]
    conv_all = all_gather("ag_conv", [conv_w])[0]
    conv_full = conv_all.transpose(1, 2, 0, 3).reshape(N_LAYERS, CONV_W, QKV_W)
    params = [_small_params(l, conv_full, a) for l in range(N_LAYERS)]
    chain = [conv_all]

    w_in_t = {pre: [jnp.transpose(a[pre + "w_in"], (2, 0, 1))[:, l][None] for l in range(N_LAYERS)]
              for pre in ("", "m_", "v_")}
    gather = WeightGather(a, w_in_t, jnp.reshape(me, (1,)).astype(jnp.int32), chain)
    saved = [{} for _ in range(N_LAYERS)]
    h = xs
    for l in range(N_LAYERS):
        for sb in BLOCKS:
            params[l].update(gather.weights(l, sb, [h]))
            h, saved[l][sb] = _fwd_block(l, sb, h, mems, params[l], gather)
    dy, loss = k_loss("loss", h, tgt)
    loss = lax.psum(loss[0, 0], ("x", "y", "c"))

    grads = [{} for _ in range(N_LAYERS)]
    exchange = GradExchange(a, w_in_t, core, chip)
    for l in reversed(range(N_LAYERS)):
        exchange.layer = l
        for sb in reversed(BLOCKS):
            dy = _bwd_block(l, sb, dy, saved[l][sb], params[l], grads[l], exchange)
    big_out = exchange.finish(dy)

    flat = [jnp.concatenate([grads[l][n].reshape(-1) for l in range(N_LAYERS)]) for n in SMALL]
    flat.append(jnp.concatenate([grads[l]["conv_w"].reshape(-1) for l in range(N_LAYERS)]))
    gsum = k_sum8("small_sum", all_gather("ag_small", [_pack(flat)], after=[big_out["ffn1_w_gate_up"][0]])[0])
    rep_rows = sum(_seg_rows(a[n].size) for n in SMALL)
    conv_g = gsum[rep_rows:].reshape(-1)[:N_LAYERS * CONV_W * QKV_W].reshape(N_LAYERS, CONV_W, QKV_W)
    conv_g = lax.dynamic_slice_in_dim(conv_g, me * (QKV_W // N_DEV), QKV_W // N_DEV, axis=2)
    pack_state = lambda pre: _pack([a[pre + n].reshape(-1) for n in SMALL] + [a[pre + "conv_w"].reshape(-1)])
    small_g = jnp.concatenate([gsum[:rep_rows], _pack([conv_g.reshape(-1)])])
    small_out = k_adamw_flat("small_adamw", small_g, pack_state(""), pack_state("m_"), pack_state("v_"))
    outs = {}
    row = 0
    for n in SMALL + ("conv_w",):
        rows = _seg_rows(a[n].size)
        outs[n] = [o[row:row + rows].reshape(-1)[:a[n].size].reshape(a[n].shape) for o in small_out]
        row += rows
    for n in BIG:
        if n == "w_in":
            outs[n] = [jnp.transpose(jnp.concatenate([exchange.w_in_out[l][k] for l in range(N_LAYERS)]), (1, 0, 2))
                       for k in range(4)]
            outs[n] = [jnp.transpose(o, (1, 2, 0)) for o in outs[n]]
        else:
            outs[n] = list(big_out[n])

    return (loss, dy[None], *[outs[n][0] for n in WEIGHTS], *[outs[n][1] for n in WEIGHTS],
            *[outs[n][2] for n in WEIGHTS], *[outs[n][3] for n in WEIGHTS])
```

```python
import functools
import math

import jax
import jax.numpy as jnp
from jax import lax
from jax.experimental import pallas as pl
from jax.experimental.pallas import tpu as pltpu

F32, BF16 = jnp.float32, jnp.bfloat16
HI = lax.Precision.HIGHEST
MESH = pl.DeviceIdType.MESH

N_DEV = 8
NORM_EPS = 1e-6
GM_HEADS, GM_CHUNK = 8, 128
DN_HEADS, DN_CHUNK, DN_DIM = 8, 64, 128
XA_HEADS, XA_DIM = 4, 512
CONV_W = 4
ADAM_LR, ADAM_B1, ADAM_B2, ADAM_EPS, ADAM_WD, ADAM_STEP = 0.001, 0.9, 0.999, 1e-08, 0.01, 10

V7X_VMEM_LIMIT = 56 * 1024 * 1024
WHOLE_TILE_MAX = 1536
WHOLE_K_MAX = 2048
ROW_TILE = 256


def _cparams(n_grid):
    return pltpu.CompilerParams(dimension_semantics=("arbitrary",) * n_grid, vmem_limit_bytes=V7X_VMEM_LIMIT)


def stepk(name, fn, grid, ins, outs, carries=(), prefetch=None, fill=(), after=()):
    n_in, n_out, n_c, n_fill = len(ins), len(outs), len(carries), len(fill)
    n_pre = 0 if prefetch is None else 1

    def body(*refs):
        refs = refs[n_pre:]
        in_refs, refs = refs[:n_in], refs[n_in + n_fill + len(after):]
        out_refs, c_refs = refs[:n_out], refs[n_out:]
        if n_c:
            first = functools.reduce(jnp.logical_and, [pl.program_id(a) == 0 for a in range(len(grid))])

            @pl.when(first)
            def _():
                for r in c_refs:
                    r[...] = jnp.zeros(r.shape, r.dtype)
        res = fn(*[r[...] for r in in_refs], *[r[...] for r in c_refs])
        for r, v in zip(tuple(out_refs) + tuple(c_refs), res):
            r[...] = v.astype(r.dtype)

    in_specs = [pl.BlockSpec(bs, im) for _, bs, im in ins] + [pl.BlockSpec(memory_space=pl.ANY)] * (n_fill + len(after))
    aliases = {n_pre + n_in + i: k for i, (_, k) in enumerate(fill)}
    out_specs = [pl.BlockSpec(bs, im) for _, _, bs, im in outs]
    out_shape = [jax.ShapeDtypeStruct(s, d) for s, d, _, _ in outs]
    for s, d in carries:
        zeros = (0,) * len(s)
        out_specs.append(pl.BlockSpec(s, lambda *a, _z=zeros: _z))
        out_shape.append(jax.ShapeDtypeStruct(s, d))
    args = [a for a, _, _ in ins] + [a for a, _ in fill] + list(after)
    if prefetch is None:
        call = pl.pallas_call(body, name=name, grid=grid, in_specs=in_specs, out_specs=out_specs, out_shape=out_shape,
                              input_output_aliases=aliases, compiler_params=_cparams(len(grid)))
        return call(*args)
    spec = pltpu.PrefetchScalarGridSpec(num_scalar_prefetch=1, grid=grid, in_specs=in_specs, out_specs=out_specs)
    call = pl.pallas_call(body, name=name, grid_spec=spec, out_shape=out_shape, input_output_aliases=aliases,
                          compiler_params=_cparams(len(grid)))
    return call(prefetch, *args)


def _tile(n, pref, align=128):
    if n <= WHOLE_TILE_MAX:
        return n
    t = (pref // align) * align
    while t > align and n % t:
        t -= align
    assert n % t == 0, (n, pref)
    return t


def mm(name, a, b, mode, out_dtype, nbo=1, after=(), b_rows=None):
    nba, ra, ca = a.shape
    nbb, rb, cb = b.shape
    b_row0 = 0
    if b_rows is not None:
        assert nbb == 1 and mode != "tn"
        b_row0, rb = b_rows
    if mode == "nn":
        m, k, n = ra, nba * ca, nbb * cb
        assert rb == k
    elif mode == "nt":
        m, k, n = ra, nba * ca, rb
        assert nbb * cb == k
    else:
        k, m, n = ra, nba * ca, nbb * cb
        assert rb == k
    co = n // nbo
    whole_k = k <= WHOLE_K_MAX and (mode == "tn" or (nba == 1 and (mode == "nn" or nbb == 1)))
    k_pref = k if whole_k else 512
    if mode == "nn":
        tm, tk, tn = _tile(m, 1024, 8), _tile(ca, k_pref), _tile(math.gcd(cb, co), 1024)
        a_spec = pl.BlockSpec((None, tm, tk), lambda i, j, kk: (kk // (ca // tk), i, kk % (ca // tk)))
        assert b_row0 % tk == 0
        b_spec = pl.BlockSpec((None, tk, tn), lambda i, j, kk: (j // (cb // tn), kk + b_row0 // tk, j % (cb // tn)))
        dims = (((1,), (0,)), ((), ()))
    elif mode == "nt":
        tm, tk, tn = _tile(m, 1024, 8), _tile(math.gcd(ca, cb), k_pref), _tile(co, 1024)
        assert b_row0 % tn == 0
        a_spec = pl.BlockSpec((None, tm, tk), lambda i, j, kk: (kk // (ca // tk), i, kk % (ca // tk)))
        b_spec = pl.BlockSpec((None, tn, tk), lambda i, j, kk: (kk // (cb // tk), j + b_row0 // tn, kk % (cb // tk)))
        dims = (((1,), (1,)), ((), ()))
    else:
        tm, tk, tn = _tile(ca, 1024), _tile(k, k_pref), _tile(math.gcd(cb, co), 1024)
        a_spec = pl.BlockSpec((None, tk, tm), lambda i, j, kk: (i // (ca // tm), kk, i % (ca // tm)))
        b_spec = pl.BlockSpec((None, tk, tn), lambda i, j, kk: (j // (cb // tn), kk, j % (cb // tn)))
        dims = (((0,), (0,)), ((), ()))
    o_spec = pl.BlockSpec((None, tm, tn), lambda i, j, kk: (j // (co // tn), i, j % (co // tn)))
    nk = k // tk

    def tile_product(a_ref, b_ref):
        return lax.dot_general(a_ref[...].astype(BF16), b_ref[...].astype(BF16), dims, preferred_element_type=F32)

    def body_one_step(a_ref, b_ref, *rest):
        o_ref = rest[-1]
        o_ref[...] = tile_product(a_ref, b_ref).astype(o_ref.dtype)

    def body(a_ref, b_ref, *rest):
        o_ref, acc_ref = rest[-2:]
        kk = pl.program_id(2)

        @pl.when(kk == 0)
        def _():
            acc_ref[...] = jnp.zeros(acc_ref.shape, F32)

        acc_ref[...] += tile_product(a_ref, b_ref)

        @pl.when(kk == nk - 1)
        def _():
            o_ref[...] = acc_ref[...].astype(o_ref.dtype)

    return pl.pallas_call(
        body_one_step if nk == 1 else body, name=name, grid=(m // tm, n // tn, nk),
        in_specs=[a_spec, b_spec] + [pl.BlockSpec(memory_space=pl.ANY)] * len(after), out_specs=o_spec,
        out_shape=jax.ShapeDtypeStruct((nbo, m, co), out_dtype),
        scratch_shapes=[] if nk == 1 else [pltpu.VMEM((tm, tn), F32)], compiler_params=_cparams(3))(a, b, *after)


def _rms(x, g):
    return x * lax.rsqrt(jnp.mean(x * x, axis=-1, keepdims=True) + NORM_EPS) * g


def _sigmoid(x):
    return 1.0 / (1.0 + jnp.exp(-x))


def _silu(x):
    return x * _sigmoid(x)


def _gelu(x):
    return 0.5 * x * (1.0 + lax.erf(x * 0.7071067811865476))


def _softplus(x):
    return jnp.maximum(x, 0.0) + jnp.log(1.0 + jnp.exp(-jnp.abs(x)))


def _split(x, n):
    w = x.shape[-1] // n
    return [x[..., h * w:(h + 1) * w] for h in range(n)]


def _sgu(ua, va, smw, smb, lng, lnb):
    c = ua[0].shape[0]
    width = len(va) * va[0].shape[-1]
    vf = [_gelu(v) for v in va]
    mu = sum(jnp.sum(v, axis=-1, keepdims=True) for v in vf) / width
    var = sum(jnp.sum(jnp.square(v - mu), axis=-1, keepdims=True) for v in vf) / width
    r = lax.rsqrt(var + NORM_EPS)
    causal = lax.broadcasted_iota(jnp.int32, (c, c), 0) >= lax.broadcasted_iota(jnp.int32, (c, c), 1)
    outs = []
    for h in range(len(ua)):
        vn = (vf[h] - mu) * r * lng[h] + lnb[h]
        w = jnp.where(causal, smw[h], 0.0)
        mixed = jnp.dot(w.astype(BF16), vn.astype(BF16), preferred_element_type=F32) + smb[h]
        outs.append(_gelu(ua[h]) * mixed)
    return outs


def _shift_rows(x, k, up):
    rows = x.shape[0]
    row = lax.broadcasted_iota(jnp.int32, x.shape, 0)
    if up:
        return jnp.where(row < rows - k, pltpu.roll(x, rows - k, 0), 0.0)
    return jnp.where(row >= k, pltpu.roll(x, k, 0), 0.0)


@functools.partial(jax.custom_vjp, nondiff_argnums=(1,))
def _delay(x, k):
    return _shift_rows(x, k, False)


def _delay_fwd(x, k):
    return _shift_rows(x, k, False), None


def _delay_bwd(k, _, g):
    return (_shift_rows(g, k, True),)


_delay.defvjp(_delay_fwd, _delay_bwd)


def _conv_silu(x, w0, w1, w2, w3):
    y = w3 * x + w2 * _delay(x, 1) + w1 * _delay(x, 2) + w0 * _delay(x, 3)
    return _silu(y)


_BDOT_DIMS = {"nn": (((2,), (1,)), ((0,), (0,))), "nt": (((2,), (2,)), ((0,), (0,))), "tn": (((1,), (1,)), ((0,), (0,)))}


def _bdot_passes(a, b, kind, passes):
    one = lambda x, y: lax.dot_general(x, y, _BDOT_DIMS[kind], preferred_element_type=F32)
    ah, bh = a.astype(BF16), b.astype(BF16)
    if passes == 1:
        return one(ah, bh)
    al, bl = (a - ah.astype(F32)).astype(BF16), (b - bh.astype(F32)).astype(BF16)
    return one(ah, bh) + (one(ah, bl) + one(al, bh))


@functools.partial(jax.custom_vjp, nondiff_argnums=(2, 3))
def bdot(a, b, kind, passes):
    return _bdot_passes(a, b, kind, passes)


def _bdot_fwd(a, b, kind, passes):
    return _bdot_passes(a, b, kind, passes), (a, b)


def _bdot_bwd(kind, passes, res, g):
    a, b = res
    if kind == "nn":
        return bdot(g, b, "nt", passes), bdot(a, g, "tn", passes)
    if kind == "nt":
        return bdot(g, b, "nn", passes), bdot(g, a, "tn", passes)
    return bdot(b, g, "nt", passes), bdot(a, g, "nn", passes)


bdot.defvjp(_bdot_fwd, _bdot_bwd)


def _dn_step(s, q, k, v, braw, araw, alog, dtb):
    nh, c, d = q.shape
    row = lax.broadcasted_iota(jnp.int32, (c, c), 0)
    col = lax.broadcasted_iota(jnp.int32, (c, c), 1)
    causal, strict = (row >= col)[None], (row > col)[None]
    lower = (row >= col).astype(F32)
    strict_f = jnp.broadcast_to((row > col).astype(F32)[None], (nh, c, c))
    eye = (row == col).astype(F32)[None]

    qn = q * lax.rsqrt(jnp.sum(q * q, axis=-1, keepdims=True) + NORM_EPS) * (d ** -0.5)
    kn = k * lax.rsqrt(jnp.sum(k * k, axis=-1, keepdims=True) + NORM_EPS)
    beta = _sigmoid(braw)
    g = -jnp.exp(alog) * _softplus(araw + dtb)
    lg = lower[None] * g
    gcum = jnp.sum(lg, axis=-1, keepdims=True)
    diff = bdot(lg, strict_f, "nn", 3)
    decay = jnp.where(causal, jnp.exp(diff), 0.0)
    bcol = jnp.sum(eye * beta, axis=-1, keepdims=True)
    kb = kn * bcol
    a = jnp.where(strict, bdot(kb, kn, "nt", 1) * decay, 0.0)
    inv = eye - a
    p = bdot(a, a, "nn", 3)
    n_fac = int(math.log2(c)) - 1
    for it in range(n_fac):
        inv = inv + bdot(inv, p, "nn", 3)
        if it < n_fac - 1:
            p = bdot(p, p, "nn", 3)
    eg = jnp.exp(gcum)
    u = bdot(inv, v * bcol, "nn", 1)
    w = bdot(inv, kb * eg, "nn", 1)
    qk = jnp.where(causal, bdot(qn, kn, "nt", 1) * decay, 0.0)
    v_new = u - bdot(w, s, "nn", 1)
    o = bdot(qn * eg, s, "nn", 1) + bdot(qk, v_new, "nn", 1)
    glast = jnp.sum(g, axis=-1, keepdims=True)
    kdec = kn * jnp.exp(glast - gcum)
    s_new = s * jnp.exp(glast) + bdot(kdec, v_new, "tn", 1)
    return s_new, o


def _ogate(o, z, w):
    return [_rms(oh, w) * _silu(zh) for oh, zh in zip(o, z)]


def _xattn(q, k, v):
    outs = []
    for qh, kh, vh in zip(q, k, v):
        s = lax.dot_general(qh.astype(BF16), kh.astype(BF16), (((1,), (1,)), ((), ())),
                            preferred_element_type=F32) * (qh.shape[-1] ** -0.5)
        s = s - jnp.max(s, axis=-1, keepdims=True)
        e = jnp.exp(s)
        p = e / jnp.sum(e, axis=-1, keepdims=True)
        outs.append(jnp.dot(p.astype(BF16), vh.astype(BF16), preferred_element_type=F32))
    return outs


def _rows(t):
    return min(ROW_TILE, t)


def k_rms_fwd(name, x, g, after=()):
    t, d = x.shape
    tm = _rows(t)
    return stepk(name, lambda xv, gv: (_rms(xv, gv),), (t // tm,),
                 [(x, (tm, d), lambda i: (i, 0)), (g, (1, d), lambda i: (0, 0))],
                 [((t, d), BF16, (tm, d), lambda i: (i, 0))], after=after)[0]


def k_rms_bwd(name, x, g, dhs, dx_res):
    t, d = x.shape
    tm = _rows(t)
    n = len(dhs)

    def fn(xv, gv, *rest):
        dh = sum(r.astype(F32) for r in rest[:n])
        dxr, dg_c = rest[n], rest[n + 1]
        _, vjp = jax.vjp(_rms, xv, gv)
        dx, dg = vjp(dh)
        return dx + dxr, dg_c + dg

    row = lambda arr: (arr, (tm, d), lambda i: (i, 0))
    return stepk(name, fn, (t // tm,), [row(x), (g, (1, d), lambda i: (0, 0))] + [row(h) for h in dhs] + [row(dx_res)],
                 [((t, d), F32, (tm, d), lambda i: (i, 0))], carries=[((1, d), F32)])


def k_post_fwd(name, x, f, g, scale, after=()):
    t, d = x.shape
    tm = _rows(t)
    row = lambda arr: (arr, (tm, d), lambda i: (i, 0))
    return stepk(name, lambda xv, fv, gv: (xv + scale * _rms(fv, gv),), (t // tm,),
                 [row(x), row(f), (g, (1, d), lambda i: (0, 0))], [((t, d), F32, (tm, d), lambda i: (i, 0))],
                 after=after)[0]


def k_post_bwd(name, f, g, dxo, scale, after=()):
    t, d = f.shape
    tm = _rows(t)

    def fn(fv, gv, dv, dg_c):
        _, vjp = jax.vjp(lambda a, b: scale * _rms(a, b), fv, gv)
        df, dg = vjp(dv)
        return df, dg_c + dg

    row = lambda arr: (arr, (tm, d), lambda i: (i, 0))
    return stepk(name, fn, (t // tm,), [row(f), (g, (1, d), lambda i: (0, 0)), row(dxo)],
                 [((t, d), BF16, (tm, d), lambda i: (i, 0))], carries=[((1, d), F32)], after=after)


def k_swiglu_fwd(name, hu, after=()):
    nb, t, c = hu.shape
    half = nb // 2
    tm = _rows(t)
    fn = lambda gv, uv: (_silu(gv.astype(F32)) * uv.astype(F32),)
    return stepk(name, fn, (half, t // tm),
                 [(hu, (None, tm, c), lambda b, i: (b, i, 0)), (hu, (None, tm, c), lambda b, i: (b + half, i, 0))],
                 [((half, t, c), BF16, (None, tm, c), lambda b, i: (b, i, 0))], after=after)[0]


def k_swiglu_bwd(name, hu, da):
    nb, t, c = hu.shape
    half = nb // 2
    tm = _rows(t)

    def fn(gv, uv, dv):
        gv, uv, dv = gv.astype(F32), uv.astype(F32), dv.astype(F32)
        sg = _sigmoid(gv)
        return (jnp.stack([dv * uv * sg * (1.0 + gv * (1.0 - sg)), dv * gv * sg]),)

    blk = lambda arr, off: (arr, (None, tm, c), lambda b, i: (b + off, i, 0))
    out = stepk(name, fn, (half, t // tm), [blk(hu, 0), blk(hu, half), blk(da, 0)],
                [((2, half, t, c), BF16, (2, None, tm, c), lambda b, i: (0, b, i, 0))])[0]
    return out.reshape(nb, t, c)


def k_loss(name, y, tgt):
    t, d = y.shape
    tm = _rows(t)

    def fn(yv, tv, acc):
        e = yv - tv
        part = jnp.sum(jnp.sum(e * e, axis=-1, keepdims=True), axis=0, keepdims=True)
        return e * (1.0 / d), acc + (0.5 / d) * part

    row = lambda arr: (arr, (tm, d), lambda i: (i, 0))
    return stepk(name, fn, (t // tm,), [row(y), row(tgt)], [((t, d), F32, (tm, d), lambda i: (i, 0))],
                 carries=[((1, 1), F32)])


GM_W = GM_HEADS * 128
QKV_COL0 = 2 * GM_W
QKV_W = 3 * DN_HEADS * DN_DIM
Z_COL0 = QKV_COL0 + QKV_W
MAIN_W = Z_COL0 + DN_HEADS * DN_DIM


def _sgu_ins(proj, p):
    c = GM_CHUNK
    return [(proj, (c, GM_W), lambda i: (i, 0)), (proj, (c, GM_W), lambda i: (i, 1)),
            (p["sm_w"], (GM_HEADS, c, c), lambda i: (0, 0, 0)), (p["sm_b"], (GM_HEADS, c, 1), lambda i: (0, 0, 0)),
            (p["sm_ln_g"], (1, GM_W), lambda i: (0, 0)), (p["sm_ln_b"], (1, GM_W), lambda i: (0, 0))]


def _sgu_lists(uv, vv, sw, sb, lg, lb):
    nh = GM_HEADS
    return (_split(uv, nh), _split(vv, nh), [sw[h] for h in range(nh)], [sb[h] for h in range(nh)],
            _split(lg, nh), _split(lb, nh))


def k_sgu_fwd(name, proj, p):
    t = proj.shape[0]

    def fn(*vals):
        return (jnp.concatenate(_sgu(*_sgu_lists(*vals)), axis=-1),)

    return stepk(name, fn, (t // GM_CHUNK,), _sgu_ins(proj, p),
                 [((2, t, GM_W), BF16, (None, GM_CHUNK, GM_W), lambda i: (0, i, 0))])[0]


def k_sgu_bwd(name, proj, p, dy, dproj):
    t = proj.shape[0]
    c = GM_CHUNK

    def fn(uv, vv, sw, sb, lg, lb, dv, dsw, dsb, dlg, dlb):
        _, vjp = jax.vjp(_sgu, *_sgu_lists(uv, vv, sw, sb, lg, lb))
        gu, gv, gsw, gsb, glg, glb = vjp(_split(dv.astype(F32), GM_HEADS))
        cat = lambda l: jnp.concatenate(l, axis=-1)
        return (cat(gu + gv), dsw + jnp.stack(gsw), dsb + jnp.stack(gsb), dlg + cat(glg), dlb + cat(glb))

    return stepk(name, fn, (t // c,), _sgu_ins(proj, p) + [(dy, (None, c, GM_W), lambda i: (0, i, 0))],
                 [((t, MAIN_W), BF16, (c, 2 * GM_W), lambda i: (i, 0))],
                 carries=[((GM_HEADS, c, c), F32), ((GM_HEADS, c, 1), F32), ((1, GM_W), F32), ((1, GM_W), F32)],
                 fill=[(dproj, 0)])


def _conv_ins(proj, conv_w):
    t = proj.shape[0]
    return [(proj, (t, 128), lambda j: (0, QKV_COL0 // 128 + j)), (conv_w, (CONV_W, 128), lambda j: (0, j))]


def k_conv_fwd(name, proj, conv_w):
    t = proj.shape[0]
    n = QKV_W // 128
    fn = lambda xv, wv: (_conv_silu(xv, *[wv[i:i + 1] for i in range(CONV_W)]),)
    return stepk(name, fn, (n,), _conv_ins(proj, conv_w), [((n, t, 128), F32, (None, t, 128), lambda j: (j, 0, 0))])[0]


def k_conv_bwd(name, proj, conv_w, dq, dk, dv, dproj):
    t = proj.shape[0]
    n = QKV_W // 128
    nh = dq.shape[0]

    def fn(xv, wv, dqv, dkv, dvv):
        part = pl.program_id(0) // nh
        d = jnp.where(part == 0, dqv, jnp.where(part == 1, dkv, dvv))
        _, vjp = jax.vjp(_conv_silu, xv, *[wv[i:i + 1] for i in range(CONV_W)])
        gx, *gw = vjp(d)
        return gx, jnp.concatenate(gw, axis=0)

    head = lambda arr, k: (arr, (None, t, 128), lambda j: (jnp.clip(j - k * nh, 0, nh - 1), 0, 0))
    return stepk(name, fn, (n,), _conv_ins(proj, conv_w) + [head(dq, 0), head(dk, 1), head(dv, 2)],
                 [((t, MAIN_W), BF16, (t, 128), lambda j: (0, QKV_COL0 // 128 + j)),
                  ((CONV_W, QKV_W), F32, (CONV_W, 128), lambda j: (0, j))], fill=[(dproj, 0)])


def _dn_ins(qkv, braw, araw, p, order):
    h, c, d = DN_HEADS, DN_CHUNK, DN_DIM
    qkv_in = lambda part: (qkv, (h, c, d), lambda n: (part, order(n), 0))
    gate_in = lambda arr: (arr, (None, h, 1, c), lambda n: (order(n), 0, 0, 0))
    par_in = lambda arr: (arr, (h, 1, 1), lambda n: (0, 0, 0))
    return [qkv_in(0), qkv_in(1), qkv_in(2), gate_in(braw), gate_in(araw), par_in(p["a_log"]), par_in(p["dt_bias"])]


def k_dn_fwd(name, qkv, braw, araw, p):
    t = qkv.shape[1]
    h, c, d = DN_HEADS, DN_CHUNK, DN_DIM
    nc = t // c

    def fn(q, k, v, b, a, al, dt, s):
        s_new, o = _dn_step(s, q, k, v, b, a, al, dt)
        return o, s, s_new

    o, s_all, _ = stepk(name, fn, (nc,), _dn_ins(qkv, braw, araw, p, lambda n: n),
                        [((h, t, d), F32, (h, c, d), lambda n: (0, n, 0)),
                         ((nc, h, d, d), F32, (None, h, d, d), lambda n: (n, 0, 0, 0))],
                        carries=[((h, d, d), F32)])
    return o, s_all


def k_dn_bwd(name, qkv, braw, araw, p, s_all, do):
    t = qkv.shape[1]
    h, c, d = DN_HEADS, DN_CHUNK, DN_DIM
    nc = t // c
    rev = lambda n: nc - 1 - n

    def fn(q, k, v, b, a, al, dt, s, dov, ds_c, dal_c, ddt_c):
        _, vjp = jax.vjp(_dn_step, s, q, k, v, b, a, al, dt)
        ds, dq, dk, dv, db, da, dal, ddt = vjp((ds_c, dov))
        return dq, dk, dv, db, da, ds, dal_c + dal, ddt_c + ddt

    ins = _dn_ins(qkv, braw, araw, p, rev) + [(s_all, (None, h, d, d), lambda n: (rev(n), 0, 0, 0)),
                                               (do, (h, c, d), lambda n: (0, rev(n), 0))]
    hd = ((h, t, d), F32, (h, c, d), lambda n: (0, rev(n), 0))
    gate = ((nc, h, 1, c), F32, (None, h, 1, c), lambda n: (rev(n), 0, 0, 0))
    dq, dk, dv, db, da, _, dal, ddt = stepk(name, fn, (nc,), ins, [hd, hd, hd, gate, gate],
                                            carries=[((h, d, d), F32), ((h, 1, 1), F32), ((h, 1, 1), F32)])
    return dq, dk, dv, db, da, dal, ddt


def _ogate_ins(o, proj, p):
    t = o.shape[1]
    tm = _rows(t)
    return tm, [(o, (DN_HEADS, tm, DN_DIM), lambda i: (0, i, 0)), (proj, (tm, GM_W), lambda i: (i, Z_COL0 // GM_W)),
                (p["dn_norm_w"], (1, DN_DIM), lambda i: (0, 0))]


def k_ogate_fwd(name, o, proj, p, y):
    t = o.shape[1]
    tm, ins = _ogate_ins(o, proj, p)

    def fn(ov, zv, wv):
        return (jnp.concatenate(_ogate([ov[h] for h in range(DN_HEADS)], _split(zv, DN_HEADS), wv), axis=-1),)

    return stepk(name, fn, (t // tm,), ins, [((2, t, GM_W), BF16, (None, tm, GM_W), lambda i: (1, i, 0))],
                 fill=[(y, 0)])[0]


def k_ogate_bwd(name, o, proj, p, dy):
    t = o.shape[1]
    tm, ins = _ogate_ins(o, proj, p)

    def fn(ov, zv, wv, dv, dw_c):
        _, vjp = jax.vjp(_ogate, [ov[h] for h in range(DN_HEADS)], _split(zv, DN_HEADS), wv)
        go, gz, gw = vjp(_split(dv.astype(F32), DN_HEADS))
        return jnp.stack(go), jnp.concatenate(gz, axis=-1), dw_c + gw

    return stepk(name, fn, (t // tm,), ins + [(dy, (None, tm, GM_W), lambda i: (1, i, 0))],
                 [((DN_HEADS, t, DN_DIM), F32, (DN_HEADS, tm, DN_DIM), lambda i: (0, i, 0)),
                  ((t, MAIN_W), BF16, (tm, GM_W), lambda i: (i, Z_COL0 // GM_W))], carries=[((1, DN_DIM), F32)])


def _xattn_lists(qv, kvv):
    nh = XA_HEADS
    return (_split(qv.astype(F32), nh), [kvv[h].astype(F32) for h in range(nh)],
            [kvv[nh + h].astype(F32) for h in range(nh)])


def k_xattn_fwd(name, q, kv):
    t, d = q.shape
    tm = _rows(t)
    fn = lambda qv, kvv: (jnp.concatenate(_xattn(*_xattn_lists(qv, kvv)), axis=-1),)
    return stepk(name, fn, (t // tm,), [(q, (tm, d), lambda i: (i, 0)), (kv, kv.shape, lambda i: (0, 0, 0))],
                 [((t, d), BF16, (tm, d), lambda i: (i, 0))])[0]


def k_xattn_bwd(name, q, kv, do):
    t, d = q.shape
    tm = _rows(t)

    def fn(qv, kvv, dv, dkv_c):
        _, vjp = jax.vjp(_xattn, *_xattn_lists(qv, kvv))
        gq, gk, gv = vjp(_split(dv.astype(F32), XA_HEADS))
        return jnp.concatenate(gq, axis=-1), dkv_c + jnp.stack(gk + gv)

    return stepk(name, fn, (t // tm,),
                 [(q, (tm, d), lambda i: (i, 0)), (kv, kv.shape, lambda i: (0, 0, 0)), (do, (None, tm, d), lambda i: (0, i, 0))],
                 [((t, d), BF16, (tm, d), lambda i: (i, 0))], carries=[(kv.shape, F32)])


def ffn_fwd(tag, x, p, pre, post, gu, dn, sch):
    h = k_rms_fwd(f"{tag}_pre", x, p[pre], sch.take())
    hu = mm(f"{tag}_gu", h[None], p[gu], "nn", BF16, nbo=N_DEV)
    a = k_swiglu_fwd(f"{tag}_act", hu)
    sch.mid(a, late=False)
    f = mm(f"{tag}_down", a, p[dn], "nn", F32, after=sch.take())[0]
    sch.mid(f, late=True)
    y = k_post_fwd(f"{tag}_post", x, f, p[post], 0.5, sch.take())
    return y, (x, h, hu, a, f)


class NoExchange:
    def take(self):
        return ()

    def emit(self, grads, names):
        pass

    def mid(self, arr, late):
        pass


def ffn_bwd(tag, dy, saved, p, pre, post, gu, dn, grads, sch):
    x, h, hu, a, f = saved
    df, grads[post] = k_post_bwd(f"{tag}_post_b", f, p[post], dy, 0.5, sch.take())
    grads[dn] = mm(f"{tag}_down_dw", a, df[None], "tn", BF16)
    sch.emit(grads, (dn,))
    da = mm(f"{tag}_down_dx", df[None], p[dn], "nt", BF16, nbo=N_DEV // 2, after=sch.take())
    dhu = k_swiglu_bwd(f"{tag}_act_b", hu, da)
    grads[gu] = mm(f"{tag}_gu_dw", h[None], dhu, "tn", BF16, nbo=N_DEV)
    sch.emit(grads, (gu,))
    dh = mm(f"{tag}_gu_dx", dhu, p[gu], "nt", BF16, after=sch.take())[0]
    dx, grads[pre] = k_rms_bwd(f"{tag}_pre_b", x, p[pre], [dh], dy)
    return dx


def _to_chunks(a):
    t, h = a.shape
    return a.reshape(t // DN_CHUNK, DN_CHUNK, h).transpose(0, 2, 1).reshape(t // DN_CHUNK, h, 1, DN_CHUNK)


def _from_chunks(a):
    nc, h, _, c = a.shape
    return a.reshape(nc, h, c).transpose(0, 2, 1).reshape(nc * c, h)


def mix_fwd(tag, x, p, sch):
    h = k_rms_fwd(f"{tag}_pre", x, p["mix_norm_pre"], sch.take())
    proj = mm(f"{tag}_in", h[None], p["w_in_t"], "nt", F32, b_rows=(0, MAIN_W))[0]
    sch.mid(proj, late=False)
    small = mm(f"{tag}_in_s", h[None], p["w_in_t"], "nt", F32, after=sch.take(), b_rows=(MAIN_W, 2 * DN_HEADS))[0]
    braw, araw = _to_chunks(small[:, :DN_HEADS]), _to_chunks(small[:, DN_HEADS:2 * DN_HEADS])
    y = k_sgu_fwd(f"{tag}_sgu", proj, p)
    qkv = k_conv_fwd(f"{tag}_conv", proj, p["conv_w"])
    o, s_all = k_dn_fwd(f"{tag}_dn", qkv, braw, araw, p)
    y = k_ogate_fwd(f"{tag}_og", o, proj, p, y)
    m = mm(f"{tag}_out", y, p["w_out"], "nn", F32)[0]
    out = k_post_fwd(f"{tag}_post", x, m, p["mix_norm_post"], 1.0)
    return out, (x, h, proj, braw, araw, qkv, o, s_all, y, m)


def mix_bwd(tag, dy, saved, p, grads, sch):
    x, h, proj, braw, araw, qkv, o, s_all, y, m = saved
    dm, grads["mix_norm_post"] = k_post_bwd(f"{tag}_post_b", m, p["mix_norm_post"], dy, 1.0, sch.take())
    grads["w_out"] = mm(f"{tag}_out_dw", y, dm[None], "tn", BF16)
    sch.emit(grads, ("w_out",))
    dyy = mm(f"{tag}_out_dx", dm[None], p["w_out"], "nt", BF16, nbo=2, after=sch.take())
    do, dproj, grads["dn_norm_w"] = k_ogate_bwd(f"{tag}_og_b", o, proj, p, dyy)
    dq, dk, dv, db, da, grads["a_log"], grads["dt_bias"] = k_dn_bwd(f"{tag}_dn_b", qkv, braw, araw, p, s_all, do)
    dproj, grads["conv_w"] = k_conv_bwd(f"{tag}_conv_b", proj, p["conv_w"], dq, dk, dv, dproj)
    dproj, grads["sm_w"], grads["sm_b"], grads["sm_ln_g"], grads["sm_ln_b"] = k_sgu_bwd(f"{tag}_sgu_b", proj, p, dyy, dproj)
    dsmall = jnp.concatenate([_from_chunks(db), _from_chunks(da)], axis=-1)
    g_main = mm(f"{tag}_in_dw", dproj[None], h[None], "tn", BF16)[0]
    g_small = mm(f"{tag}_in_s_dw", dsmall[None], h[None], "tn", BF16)[0]
    g_in = jnp.concatenate([g_main, g_small], axis=0)
    grads["w_in"] = g_in.reshape(N_DEV, g_in.shape[0] // N_DEV, g_in.shape[1])
    sch.emit(grads, ("w_in",))
    dh = mm(f"{tag}_in_dx", dproj[None], p["w_in_t"], "nn", BF16, after=sch.take(), b_rows=(0, MAIN_W))[0]
    dh_s = mm(f"{tag}_in_s_dx", dsmall[None], p["w_in_t"], "nn", BF16, b_rows=(MAIN_W, 2 * DN_HEADS))[0]
    dx, grads["mix_norm_pre"] = k_rms_bwd(f"{tag}_pre_b", x, p["mix_norm_pre"], [dh, dh_s], dy)
    return dx


def xa_fwd(tag, x, mem, p, sch):
    after = sch.take()
    hx = k_rms_fwd(f"{tag}_pre", x, p["xa_norm_pre"], after)
    mh = k_rms_fwd(f"{tag}_mem", mem, p["mem_norm"], after)
    q = mm(f"{tag}_q", hx[None], p["w_xq"], "nn", BF16)[0]
    sch.mid(q, late=False)
    kv = mm(f"{tag}_kv", mh[None], p["w_xkv"], "nn", BF16, nbo=N_DEV, after=sch.take())
    o = k_xattn_fwd(f"{tag}_att", q, kv)
    sch.mid(o, late=True)
    c = mm(f"{tag}_o", o[None], p["w_xo"], "nn", F32, after=sch.take())[0]
    out = k_post_fwd(f"{tag}_post", x, c, p["xa_norm_post"], 1.0)
    return out, (x, mem, hx, mh, q, kv, o, c)


def xa_bwd(tag, dy, saved, p, grads, sch):
    x, mem, hx, mh, q, kv, o, c = saved
    dc, grads["xa_norm_post"] = k_post_bwd(f"{tag}_post_b", c, p["xa_norm_post"], dy, 1.0, sch.take())
    grads["w_xo"] = mm(f"{tag}_o_dw", o[None], dc[None], "tn", BF16)
    sch.emit(grads, ("w_xo",))
    do = mm(f"{tag}_o_dx", dc[None], p["w_xo"], "nt", BF16, after=sch.take())
    dq, dkv = k_xattn_bwd(f"{tag}_att_b", q, kv, do)
    dkv16 = dkv.astype(BF16)
    grads["w_xq"] = mm(f"{tag}_q_dw", hx[None], dq[None], "tn", BF16)
    grads["w_xkv"] = mm(f"{tag}_kv_dw", mh[None], dkv16, "tn", BF16, nbo=N_DEV)
    sch.emit(grads, ("w_xq", "w_xkv"))
    dhx = mm(f"{tag}_q_dx", dq[None], p["w_xq"], "nt", BF16, after=sch.take())[0]
    dmh = mm(f"{tag}_kv_dx", dkv16, p["w_xkv"], "nt", BF16)[0]
    _, grads["mem_norm"] = k_rms_bwd(f"{tag}_mem_b", mem, p["mem_norm"], [dmh], jnp.zeros_like(mem))
    dx, grads["xa_norm_pre"] = k_rms_bwd(f"{tag}_pre_b", x, p["xa_norm_pre"], [dhx], dy)
    return dx


def _place():
    return lax.axis_index("x"), lax.axis_index("y"), lax.axis_index("c")


def _other_chips(x, y):
    return [(1 - x, y), (x, 1 - y), (1 - x, 1 - y)]


_ANY = pl.BlockSpec(memory_space=pl.ANY)


def all_gather(name, shards, after=()):
    n, na = len(shards), len(after)

    def body(*refs):
        ins, outs = refs[:n], refs[n + na:2 * n + na]
        send_sems, recv_sems, local_sems = refs[2 * n + na:]
        x, y, c = _place()
        me, sibling = (x, y, c), (x, y, 1 - c)
        chips = _other_chips(x, y)
        idx = lambda px, py, pc: 4 * px + 2 * py + pc

        def copy(a, k, block, to, src=None):
            dst = outs[a].at[idx(*block)]
            return pltpu.make_async_remote_copy(src_ref=dst if src is None else src, dst_ref=dst,
                                                send_sem=send_sems.at[7 * a + k], recv_sem=recv_sems.at[7 * a + k],
                                                device_id=to, device_id_type=MESH)

        mine = [pltpu.make_async_copy(ins[a], outs[a].at[idx(*me)], local_sems.at[a]) for a in range(n)]
        for cp in mine:
            cp.start()
        first = []
        for a in range(n):
            first.append(copy(a, 0, me, sibling, src=ins[a]))
            first += [copy(a, 1 + j, me, (*chip, c), src=ins[a]) for j, chip in enumerate(chips)]
        for cp in first:
            cp.start()
        passed = []
        for a in range(n):
            for j, chip in enumerate(chips):
                copy(a, 1 + j, (*chip, c), me).wait_recv()
                passed.append(copy(a, 4 + j, (*chip, c), sibling))
                passed[-1].start()
        for a in range(n):
            copy(a, 0, sibling, me).wait_recv()
            for j, chip in enumerate(chips):
                copy(a, 4 + j, (*chip, 1 - c), me).wait_recv()
        for cp in first + passed:
            cp.wait_send()
        for cp in mine:
            cp.wait()

    return pl.pallas_call(
        body, name=name, in_specs=[_ANY] * (n + na), out_specs=[_ANY] * n,
        out_shape=[jax.ShapeDtypeStruct((N_DEV,) + s.shape, s.dtype) for s in shards],
        scratch_shapes=[pltpu.SemaphoreType.DMA((7 * n,)), pltpu.SemaphoreType.DMA((7 * n,)),
                        pltpu.SemaphoreType.DMA((n,))])(*shards, *after)


def k_cast_place(name, w, layer, me, after):
    _, r, cc = w.shape
    tr, tc = _tile2(r, cc)
    return stepk(name, lambda v: (v,), (r // tr, cc // tc), [(w, (None, tr, tc), lambda i, j, pre: (layer, i, j))],
                 [((N_DEV, r, cc), BF16, (None, tr, tc), lambda i, j, pre: (pre[0], i, j))], prefetch=me,
                 after=after)[0]


_HBM = pl.BlockSpec(memory_space=pltpu.HBM)
_SEM = pl.BlockSpec(memory_space=pltpu.SEMAPHORE)
_DATAFLOW = pltpu.SideEffectType.DATAFLOW_SIDE_EFFECTING


def xfer_start(name, plan, n_sems, srcs, lands, after=()):
    ns, nl, na = len(srcs), len(lands), len(after)
    bufs = list(srcs) + list(lands)

    def body(*refs):
        send_sems, recv_sems, token = refs[ns + nl + na], refs[ns + nl + na + 1], refs[-1]
        for cp in plan(refs[:ns], refs[ns:ns + nl], send_sems, recv_sems):
            cp.start()
        token[...] = jnp.zeros(token.shape, token.dtype)

    res = pl.pallas_call(
        body, name=name, in_specs=[_HBM] * (ns + nl) + [_ANY] * na,
        out_specs=(_SEM, _SEM, *[_HBM] * (ns + nl), pl.BlockSpec(memory_space=pltpu.VMEM)),
        out_shape=(pltpu.SemaphoreType.DMA((n_sems,)), pltpu.SemaphoreType.DMA((n_sems,)),
                   *[pltpu.HBM(b.shape, b.dtype) for b in bufs], jax.ShapeDtypeStruct((8, 128), F32)),
        input_output_aliases={i: 2 + i for i in range(ns + nl)},
        compiler_params=pltpu.CompilerParams(has_side_effects=_DATAFLOW),
    )(*[pltpu.with_memory_space_constraint(b, pltpu.HBM) for b in bufs], *after)
    return dict(send=res[0], recv=res[1], srcs=list(res[2:2 + ns]), lands=list(res[2 + ns:2 + ns + nl]), token=res[-1])


def xfer_wait(name, plan, started, after=()):
    ns, nl = len(started["srcs"]), len(started["lands"])
    bufs = started["srcs"] + started["lands"]

    def body(*refs):
        for cp in plan(refs[:ns], refs[ns:ns + nl], refs[ns + nl], refs[ns + nl + 1]):
            cp.wait_send()
            cp.wait_recv()

    res = pl.pallas_call(
        body, name=name, in_specs=[_HBM] * (ns + nl) + [_SEM, _SEM] + [_ANY] * len(after),
        out_specs=tuple([_HBM] * (ns + nl)), out_shape=tuple(pltpu.HBM(b.shape, b.dtype) for b in bufs),
        input_output_aliases={i: i for i in range(ns + nl)},
        compiler_params=pltpu.CompilerParams(has_side_effects=_DATAFLOW),
    )(*bufs, started["send"], started["recv"], *after)
    return list(res[:ns]), list(res[ns:])


def _remote(src, dst, send_sems, recv_sems, k, to):
    return pltpu.make_async_remote_copy(src_ref=src, dst_ref=dst, send_sem=send_sems.at[k], recv_sem=recv_sems.at[k],
                                        device_id=to, device_id_type=MESH)


def plan_gather_out(srcs, lands, send_sems, recv_sems):
    x, y, c = _place()
    me = 4 * x + 2 * y + c
    targets = [(x, y, 1 - c)] + [(px, py, c) for px, py in _other_chips(x, y)]
    return [_remote(lands[a].at[me], lands[a].at[me], send_sems, recv_sems, 4 * a + k, to)
            for a in range(len(lands)) for k, to in enumerate(targets)]


def plan_gather_pass(srcs, lands, send_sems, recv_sems):
    x, y, c = _place()
    return [_remote(lands[a].at[4 * px + 2 * py + c], lands[a].at[4 * px + 2 * py + c], send_sems, recv_sems,
                    3 * a + j, (x, y, 1 - c))
            for a in range(len(lands)) for j, (px, py) in enumerate(_other_chips(x, y))]


def plan_scatter_sibling(srcs, lands, send_sems, recv_sems):
    x, y, c = _place()
    return [_remote(srcs[a].at[j, 1 - c], lands[a].at[j], send_sems, recv_sems, 4 * a + j, (x, y, 1 - c))
            for a in range(len(srcs)) for j in range(4)]


def plan_scatter_chips(srcs, lands, send_sems, recv_sems):
    x, y, c = _place()
    return [_remote(srcs[a].at[2 * px + py], lands[a].at[j], send_sems, recv_sems, 3 * a + j, (px, py, c))
            for a in range(len(srcs)) for j, (px, py) in enumerate(_other_chips(x, y))]


ELEMWISE_BLOCK = 512 * 1024


def _tile2(r, cc, limit=ELEMWISE_BLOCK):
    if r % 16:
        tc = cc
        while r * tc > limit and tc % 256 == 0:
            tc //= 2
        return r, tc
    t = r
    while t * cc > limit and t % 32 == 0:
        t //= 2
    if t * cc > limit:
        for cand in range(t, 15, -16):
            if r % cand == 0 and cand * cc <= limit:
                return cand, cc
    return t, cc


def k_pair_add(name, g4, r1, core):
    _, _, r, cc = g4.shape
    tr, tc = _tile2(r, cc, 2 * ELEMWISE_BLOCK)
    fn = lambda av, bv: (av.astype(F32) + bv.astype(F32),)
    return stepk(name, fn, (4, r // tr, cc // tc),
                 [(g4, (None, None, tr, tc), lambda b, i, j, pre: (b, pre[0], i, j)),
                  (r1, (None, tr, tc), lambda b, i, j, pre: (b, i, j))],
                 [((4, r, cc), BF16, (None, tr, tc), lambda b, i, j, pre: (b, i, j))], prefetch=core)[0]


def _adamw(g, w, m, v):
    m2 = ADAM_B1 * m + (1.0 - ADAM_B1) * g
    v2 = ADAM_B2 * v + (1.0 - ADAM_B2) * jnp.square(g)
    m_hat = m2 / (1.0 - ADAM_B1 ** ADAM_STEP)
    v_hat = v2 / (1.0 - ADAM_B2 ** ADAM_STEP)
    delta = -ADAM_LR * (m_hat / (jnp.sqrt(v_hat) + ADAM_EPS) + ADAM_WD * w)
    return g, delta, m2, v2


def k_adamw_shard(name, part, others, w, m, v, layer, chip, fill):
    _, r, cc = part.shape
    tr, tc = _tile2(r, cc)

    def fn(pv, o0, o1, o2, wv, mv, vv, done):
        g = ((pv.astype(F32) + o0.astype(F32)) + o1.astype(F32)) + o2.astype(F32)
        return _adamw(g, wv, mv, vv) + (done,)

    other = lambda k: (others, (None, tr, tc), lambda i, j, pre: (k, i, j))
    state = lambda arr: (arr, (None, tr, tc), lambda i, j, pre: (layer, i, j))
    out = ((w.shape[0], r, cc), F32, (None, tr, tc), lambda i, j, pre: (layer, i, j))
    return stepk(name, fn, (r // tr, cc // tc),
                 [(part, (None, tr, tc), lambda i, j, pre: (pre[0], i, j)), other(0), other(1), other(2),
                  state(w), state(m), state(v)],
                 [out] * 4, carries=[((8, 128), F32)], prefetch=chip, fill=[(f, k) for k, f in enumerate(fill)])


def k_sum8(name, parts):
    _, rows, lanes = parts.shape

    def fn(pv):
        acc = pv[0]
        for d in range(1, N_DEV):
            acc = acc + pv[d]
        return (acc,)

    return stepk(name, fn, (1,), [(parts, parts.shape, lambda i: (0, 0, 0))],
                 [((rows, lanes), F32, (rows, lanes), lambda i: (0, 0))])[0]


def k_adamw_flat(name, g, w, m, v):
    whole = lambda arr: (arr, arr.shape, lambda i: (0, 0))
    out = (g.shape, F32, g.shape, lambda i: (0, 0))
    return stepk(name, _adamw, (1,), [whole(g), whole(w), whole(m), whole(v)], [out] * 4)


WEIGHTS = ("ffn1_norm_pre", "ffn1_w_gate_up", "ffn1_w_down", "ffn1_norm_post", "mix_norm_pre", "w_in", "conv_w", "a_log",
           "dt_bias", "sm_w", "sm_b", "sm_ln_g", "sm_ln_b", "dn_norm_w", "w_out", "mix_norm_post", "xa_norm_pre", "mem_norm",
           "w_xq", "w_xkv", "w_xo", "xa_norm_post", "ffn2_norm_pre", "ffn2_w_gate_up", "ffn2_w_down", "ffn2_norm_post")
BIG = ("ffn1_w_gate_up", "ffn1_w_down", "w_in", "w_out", "w_xq", "w_xkv", "w_xo", "ffn2_w_gate_up", "ffn2_w_down")
ROW_SHARDED = ("ffn1_w_down", "w_out", "w_xq", "w_xo", "ffn2_w_down")
SMALL = tuple(n for n in WEIGHTS if n not in BIG and n != "conv_w")
N_LAYERS = 2
IN_COLS = MAIN_W + 2 * DN_HEADS


BLOCKS = ("f1", "mx", "xa", "f2")
AG_BEFORE = {"f1": ("ffn1_w_gate_up", "ffn1_w_down"), "mx": ("w_in", "w_out"), "xa": ("w_xq", "w_xkv", "w_xo"),
             "f2": ("ffn2_w_gate_up", "ffn2_w_down")}
AG_LARGE = ("f1", "f2")
RS_IN_FLIGHT = 1
RS_LAST = ("ffn1_w_gate_up",)


def _fwd_block(l, sb, h, mem, p, sch):
    tag = f"l{l}{sb}"
    if sb == "f1":
        return ffn_fwd(tag, h, p, "ffn1_norm_pre", "ffn1_norm_post", "ffn1_w_gate_up", "ffn1_w_down", sch)
    if sb == "mx":
        return mix_fwd(tag, h, p, sch)
    if sb == "xa":
        return xa_fwd(tag, h, mem, p, sch)
    return ffn_fwd(tag, h, p, "ffn2_norm_pre", "ffn2_norm_post", "ffn2_w_gate_up", "ffn2_w_down", sch)


class WeightGather:
    def __init__(self, a, w_in_t, me1, chain):
        self.order = [(l, sb) for l in range(N_LAYERS) for sb in BLOCKS]
        self.going, self.passing, self.tie = {}, {}, []
        for l, sb in self.order:
            lands = [k_cast_place(f"place_l{l}{n}", *((w_in_t[""][l], 0) if n == "w_in" else (a[n], l)), me1, chain)
                     for n in AG_BEFORE[sb]]
            self.going[l, sb] = xfer_start(f"ag_out_l{l}{sb}", plan_gather_out, 4 * len(lands), [], lands)
            chain = [self.going[l, sb]["token"]]
        self.tie = chain
        self.block = None

    def take(self):
        tie, self.tie = self.tie, []
        return tie

    def _pass_on(self, key, after):
        l, sb = key
        _, got = xfer_wait(f"ag_outw_l{l}{sb}", plan_gather_out, self.going.pop(key), after=after)
        self.passing[key] = xfer_start(f"ag_pass_l{l}{sb}", plan_gather_pass, 3 * len(got), [], got)
        self.tie = [self.passing[key]["token"]]

    def mid(self, arr, late):
        later = self.order[self.order.index(self.block) + 1:]
        if later and later[0] in self.going and (later[0][1] in AG_LARGE) == late:
            self._pass_on(later[0], [arr])

    def weights(self, l, sb, after):
        self.block = (l, sb)
        after = list(after) + self.take()
        if (l, sb) in self.going:
            self._pass_on((l, sb), after)
        _, full = xfer_wait(f"ag_passw_l{l}{sb}", plan_gather_pass, self.passing.pop((l, sb)), after=after)
        self.tie = [full[0]]
        return _big_params(dict(zip(AG_BEFORE[sb], full)))


def _bwd_block(l, sb, dy, saved, p, grads, sch):
    tag = f"l{l}{sb}"
    if sb == "f1":
        return ffn_bwd(tag, dy, saved, p, "ffn1_norm_pre", "ffn1_norm_post", "ffn1_w_gate_up", "ffn1_w_down", grads, sch)
    if sb == "mx":
        return mix_bwd(tag, dy, saved, p, grads, sch)
    if sb == "xa":
        return xa_bwd(tag, dy, saved, p, grads, sch)
    return ffn_bwd(tag, dy, saved, p, "ffn2_norm_pre", "ffn2_norm_post", "ffn2_w_gate_up", "ffn2_w_down", grads, sch)


class GradExchange:
    def __init__(self, a, w_in_t, core, chip):
        self.a, self.w_in_t, self.core, self.chip = a, w_in_t, core, chip
        self.layer = None
        self.pending, self.tie, self.out = [], [], {}
        self.w_in_out = [None] * N_LAYERS

    def take(self):
        tie, self.tie = self.tie, []
        return tie

    def _advance(self, chain, drain, waits=True):
        a = self.a
        between_chips = [grp for grp in self.pending if grp["stage"] == "chips"]
        for grp in self.pending:
            if grp["stage"] == "sibling":
                l, names = grp["l"], grp["names"]
                g4, r1 = xfer_wait(f"rs_sibw_l{l}{names[0]}", plan_scatter_sibling, grp["going"], after=chain)
                parts = [k_pair_add(f"rs_add_l{l}{n}", gg, rr, self.core) for n, gg, rr in zip(names, g4, r1)]
                zones = [lax.empty((3,) + p.shape[1:], BF16) for p in parts]
                grp["going"] = xfer_start(f"rs_chip_l{l}{names[0]}", plan_scatter_chips, 3 * len(names), parts, zones)
                grp["stage"], grp["age"] = "chips", 0
                chain = [grp["going"]["token"]]
        for grp in between_chips if waits else ():
            l, names = grp["l"], grp["names"]
            if drain or grp["age"] >= grp["limit"]:
                parts, r2 = xfer_wait(f"rs_chipw_l{l}{names[0]}", plan_scatter_chips, grp["going"], after=chain)
                for n, part, others in zip(names, parts, r2):
                    if n == "w_in":
                        state = [self.w_in_t[pre][l] for pre in ("", "m_", "v_")]
                        *self.w_in_out[l], done = k_adamw_shard(f"adamw_{n}_l{l}", part, others, *state, 0, self.chip, ())
                    else:
                        *self.out[n], done = k_adamw_shard(f"adamw_{n}_l{l}", part, others, a[n], a["m_" + n],
                                                           a["v_" + n], l, self.chip, self.out.get(n, ()))
                    chain = [done]
                self.pending.remove(grp)
            else:
                grp["age"] += 1
        return chain

    def emit(self, grads, names):
        l = self.layer
        g4 = [_scatter_layout(n, grads[n]) for n in names]
        g4 = [t.reshape(4, 2, *t.shape[1:]) for t in g4]
        zones = [lax.empty((4,) + t.shape[2:], BF16) for t in g4]
        going = xfer_start(f"rs_sib_l{l}{names[0]}", plan_scatter_sibling, 4 * len(names), g4, zones)
        chain = self._advance([going["token"]], drain=False)
        self.pending.append(dict(l=l, names=names, stage="sibling", going=going, age=0, limit=RS_IN_FLIGHT))
        if l == 0 and names[0] in RS_LAST:
            chain = self._advance(chain, drain=False, waits=False)
        self.tie = chain

    def finish(self, last):
        chain = [last]
        while self.pending:
            chain = self._advance(chain, drain=True)
        return self.out


def _big_params(full):
    p = {}
    for n, w in full.items():
        if n in ROW_SHARDED:
            p[n] = w.reshape(1, w.shape[0] * w.shape[1], w.shape[2])
        elif n == "w_in":
            p["w_in_t"] = w.reshape(1, w.shape[0] * w.shape[1], w.shape[2])
        else:
            p[n] = w
    return p


def _small_params(l, conv_full, a):
    p = {"conv_w": conv_full[l]}
    for n in SMALL:
        w = a[n][l]
        if n in ("a_log", "dt_bias"):
            p[n] = w.reshape(DN_HEADS, 1, 1)
        elif n == "sm_b":
            p[n] = w[..., None]
        elif n == "sm_w":
            p[n] = w
        else:
            p[n] = w[None]
    return p


def _scatter_layout(n, g):
    if n in ROW_SHARDED:
        return g.reshape(N_DEV, g.shape[1] // N_DEV, g.shape[2])
    return g


def _seg_rows(size):
    return -(-size // 1024) * 8


def _pack(flat_parts):
    return jnp.concatenate([jnp.pad(f, (0, _seg_rows(f.shape[0]) * 128 - f.shape[0])).reshape(-1, 128) for f in flat_parts])


def kernel(x, mem, ffn1_norm_pre, ffn1_w_gate_up, ffn1_w_down, ffn1_norm_post, mix_norm_pre, w_in, conv_w, a_log, dt_bias, sm_w, sm_b, sm_ln_g, sm_ln_b, dn_norm_w, w_out, mix_norm_post, xa_norm_pre, mem_norm, w_xq, w_xkv, w_xo, xa_norm_post, ffn2_norm_pre, ffn2_w_gate_up, ffn2_w_down, ffn2_norm_post, loss_target, m_ffn1_norm_pre, m_ffn1_w_gate_up, m_ffn1_w_down, m_ffn1_norm_post, m_mix_norm_pre, m_w_in, m_conv_w, m_a_log, m_dt_bias, m_sm_w, m_sm_b, m_sm_ln_g, m_sm_ln_b, m_dn_norm_w, m_w_out, m_mix_norm_post, m_xa_norm_pre, m_mem_norm, m_w_xq, m_w_xkv, m_w_xo, m_xa_norm_post, m_ffn2_norm_pre, m_ffn2_w_gate_up, m_ffn2_w_down, m_ffn2_norm_post, v_ffn1_norm_pre, v_ffn1_w_gate_up, v_ffn1_w_down, v_ffn1_norm_post, v_mix_norm_pre, v_w_in, v_conv_w, v_a_log, v_dt_bias, v_sm_w, v_sm_b, v_sm_ln_g, v_sm_ln_b, v_dn_norm_w, v_w_out, v_mix_norm_post, v_xa_norm_pre, v_mem_norm, v_w_xq, v_w_xkv, v_w_xo, v_xa_norm_post, v_ffn2_norm_pre, v_ffn2_w_gate_up, v_ffn2_w_down, v_ffn2_norm_post):
    a = dict(locals())
    px, py, pc = _place()
    core = jnp.reshape(pc, (1,)).astype(jnp.int32)
    chip = jnp.reshape(2 * px + py, (1,)).astype(jnp.int32)
    me = 4 * px + 2 * py + pc
    xs, mems, tgt = x[0], mem[0], loss_target[0]
    conv_all = all_gather("ag_conv", [conv_w])[0]
    conv_full = conv_all.transpose(1, 2, 0, 3).reshape(N_LAYERS, CONV_W, QKV_W)
    params = [_small_params(l, conv_full, a) for l in range(N_LAYERS)]
    chain = [conv_all]

    w_in_t = {pre: [jnp.transpose(a[pre + "w_in"], (2, 0, 1))[:, l][None] for l in range(N_LAYERS)]
              for pre in ("", "m_", "v_")}
    gather = WeightGather(a, w_in_t, jnp.reshape(me, (1,)).astype(jnp.int32), chain)
    saved = [{} for _ in range(N_LAYERS)]
    h = xs
    for l in range(N_LAYERS):
        for sb in BLOCKS:
            params[l].update(gather.weights(l, sb, [h]))
            h, saved[l][sb] = _fwd_block(l, sb, h, mems, params[l], gather)
    dy, loss = k_loss("loss", h, tgt)
    loss = lax.psum(loss[0, 0], ("x", "y", "c"))

    grads = [{} for _ in range(N_LAYERS)]
    exchange = GradExchange(a, w_in_t, core, chip)
    for l in reversed(range(N_LAYERS)):
        exchange.layer = l
        for sb in reversed(BLOCKS):
            dy = _bwd_block(l, sb, dy, saved[l][sb], params[l], grads[l], exchange)
    big_out = exchange.finish(dy)

    flat = [jnp.concatenate([grads[l][n].reshape(-1) for l in range(N_LAYERS)]) for n in SMALL]
    flat.append(jnp.concatenate([grads[l]["conv_w"].reshape(-1) for l in range(N_LAYERS)]))
    gsum = k_sum8("small_sum", all_gather("ag_small", [_pack(flat)], after=[big_out["ffn1_w_gate_up"][0]])[0])
    rep_rows = sum(_seg_rows(a[n].size) for n in SMALL)
    conv_g = gsum[rep_rows:].reshape(-1)[:N_LAYERS * CONV_W * QKV_W].reshape(N_LAYERS, CONV_W, QKV_W)
    conv_g = lax.dynamic_slice_in_dim(conv_g, me * (QKV_W // N_DEV), QKV_W // N_DEV, axis=2)
    pack_state = lambda pre: _pack([a[pre + n].reshape(-1) for n in SMALL] + [a[pre + "conv_w"].reshape(-1)])
    small_g = jnp.concatenate([gsum[:rep_rows], _pack([conv_g.reshape(-1)])])
    small_out = k_adamw_flat("small_adamw", small_g, pack_state(""), pack_state("m_"), pack_state("v_"))
    outs = {}
    row = 0
    for n in SMALL + ("conv_w",):
        rows = _seg_rows(a[n].size)
        outs[n] = [o[row:row + rows].reshape(-1)[:a[n].size].reshape(a[n].shape) for o in small_out]
        row += rows
    for n in BIG:
        if n == "w_in":
            outs[n] = [jnp.transpose(jnp.concatenate([exchange.w_in_out[l][k] for l in range(N_LAYERS)]), (1, 0, 2))
                       for k in range(4)]
            outs[n] = [jnp.transpose(o, (1, 2, 0)) for o in outs[n]]
        else:
            outs[n] = list(big_out[n])

    return (loss, dy[None], *[outs[n][0] for n in WEIGHTS], *[outs[n][1] for n in WEIGHTS],
            *[outs[n][2] for n in WEIGHTS], *[outs[n][3] for n in WEIGHTS])
```

```python
import functools
import math

import jax
import jax.numpy as jnp
from jax import lax
from jax.experimental import pallas as pl
from jax.experimental.pallas import tpu as pltpu

F32, BF16 = jnp.float32, jnp.bfloat16
HI = lax.Precision.HIGHEST
MESH = pl.DeviceIdType.MESH

N_DEV = 8
NORM_EPS = 1e-6
GM_HEADS, GM_CHUNK = 8, 128
DN_HEADS, DN_CHUNK, DN_DIM = 8, 64, 128
XA_HEADS, XA_DIM = 4, 512
CONV_W = 4
ADAM_LR, ADAM_B1, ADAM_B2, ADAM_EPS, ADAM_WD, ADAM_STEP = 0.001, 0.9, 0.999, 1e-08, 0.01, 10

V7X_VMEM_LIMIT = 56 * 1024 * 1024
WHOLE_TILE_MAX = 1536
WHOLE_K_MAX = 2048
ROW_TILE = 256


def _cparams(n_grid):
    return pltpu.CompilerParams(dimension_semantics=("arbitrary",) * n_grid, vmem_limit_bytes=V7X_VMEM_LIMIT)


def stepk(name, fn, grid, ins, outs, carries=(), prefetch=None, fill=(), after=()):
    n_in, n_out, n_c, n_fill = len(ins), len(outs), len(carries), len(fill)
    n_pre = 0 if prefetch is None else 1

    def body(*refs):
        refs = refs[n_pre:]
        in_refs, refs = refs[:n_in], refs[n_in + n_fill + len(after):]
        out_refs, c_refs = refs[:n_out], refs[n_out:]
        if n_c:
            first = functools.reduce(jnp.logical_and, [pl.program_id(a) == 0 for a in range(len(grid))])

            @pl.when(first)
            def _():
                for r in c_refs:
                    r[...] = jnp.zeros(r.shape, r.dtype)
        res = fn(*[r[...] for r in in_refs], *[r[...] for r in c_refs])
        for r, v in zip(tuple(out_refs) + tuple(c_refs), res):
            r[...] = v.astype(r.dtype)

    in_specs = [pl.BlockSpec(bs, im) for _, bs, im in ins] + [pl.BlockSpec(memory_space=pl.ANY)] * (n_fill + len(after))
    aliases = {n_pre + n_in + i: k for i, (_, k) in enumerate(fill)}
    out_specs = [pl.BlockSpec(bs, im) for _, _, bs, im in outs]
    out_shape = [jax.ShapeDtypeStruct(s, d) for s, d, _, _ in outs]
    for s, d in carries:
        zeros = (0,) * len(s)
        out_specs.append(pl.BlockSpec(s, lambda *a, _z=zeros: _z))
        out_shape.append(jax.ShapeDtypeStruct(s, d))
    args = [a for a, _, _ in ins] + [a for a, _ in fill] + list(after)
    if prefetch is None:
        call = pl.pallas_call(body, name=name, grid=grid, in_specs=in_specs, out_specs=out_specs, out_shape=out_shape,
                              input_output_aliases=aliases, compiler_params=_cparams(len(grid)))
        return call(*args)
    spec = pltpu.PrefetchScalarGridSpec(num_scalar_prefetch=1, grid=grid, in_specs=in_specs, out_specs=out_specs)
    call = pl.pallas_call(body, name=name, grid_spec=spec, out_shape=out_shape, input_output_aliases=aliases,
                          compiler_params=_cparams(len(grid)))
    return call(prefetch, *args)


def _tile(n, pref, align=128):
    if n <= WHOLE_TILE_MAX:
        return n
    t = (pref // align) * align
    while t > align and n % t:
        t -= align
    assert n % t == 0, (n, pref)
    return t


def mm(name, a, b, mode, out_dtype, nbo=1, after=(), b_rows=None, out_rows=None, fill=None):
    nba, ra, ca = a.shape
    nbb, rb, cb = b.shape
    b_row0 = 0
    if b_rows is not None:
        assert nbb == 1 and mode != "tn"
        b_row0, rb = b_rows
    if mode == "nn":
        m, k, n = ra, nba * ca, nbb * cb
        assert rb == k
    elif mode == "nt":
        m, k, n = ra, nba * ca, rb
        assert nbb * cb == k
    else:
        k, m, n = ra, nba * ca, nbb * cb
        assert rb == k
    co = n // nbo
    whole_k = k <= WHOLE_K_MAX and (mode == "tn" or (nba == 1 and (mode == "nn" or nbb == 1)))
    k_pref = k if whole_k else 512
    if mode == "nn":
        tm, tk, tn = _tile(m, 1024, 8), _tile(ca, k_pref), _tile(math.gcd(cb, co), 1024)
        a_spec = pl.BlockSpec((None, tm, tk), lambda i, j, kk: (kk // (ca // tk), i, kk % (ca // tk)))
        assert b_row0 % tk == 0
        b_spec = pl.BlockSpec((None, tk, tn), lambda i, j, kk: (j // (cb // tn), kk + b_row0 // tk, j % (cb // tn)))
        dims = (((1,), (0,)), ((), ()))
    elif mode == "nt":
        tm, tk, tn = _tile(m, 1024, 8), _tile(math.gcd(ca, cb), k_pref), _tile(co, 1024)
        assert b_row0 % tn == 0
        a_spec = pl.BlockSpec((None, tm, tk), lambda i, j, kk: (kk // (ca // tk), i, kk % (ca // tk)))
        b_spec = pl.BlockSpec((None, tn, tk), lambda i, j, kk: (kk // (cb // tk), j + b_row0 // tn, kk % (cb // tk)))
        dims = (((1,), (1,)), ((), ()))
    else:
        tm, tk, tn = _tile(ca, 1024), _tile(k, k_pref), _tile(math.gcd(cb, co), 1024)
        a_spec = pl.BlockSpec((None, tk, tm), lambda i, j, kk: (i // (ca // tm), kk, i % (ca // tm)))
        b_spec = pl.BlockSpec((None, tk, tn), lambda i, j, kk: (j // (cb // tn), kk, j % (cb // tn)))
        dims = (((0,), (0,)), ((), ()))
    o_row0, o_rows = out_rows if out_rows is not None else (0, m)
    assert o_row0 % tm == 0
    o_spec = pl.BlockSpec((None, tm, tn), lambda i, j, kk: (j // (co // tn), i + o_row0 // tm, j % (co // tn)))
    nk = k // tk
    extra = list(after) + ([] if fill is None else [fill])

    def tile_product(a_ref, b_ref):
        return lax.dot_general(a_ref[...].astype(BF16), b_ref[...].astype(BF16), dims, preferred_element_type=F32)

    def body_one_step(a_ref, b_ref, *rest):
        o_ref = rest[-1]
        o_ref[...] = tile_product(a_ref, b_ref).astype(o_ref.dtype)

    def body(a_ref, b_ref, *rest):
        o_ref, acc_ref = rest[-2:]
        kk = pl.program_id(2)

        @pl.when(kk == 0)
        def _():
            acc_ref[...] = jnp.zeros(acc_ref.shape, F32)

        acc_ref[...] += tile_product(a_ref, b_ref)

        @pl.when(kk == nk - 1)
        def _():
            o_ref[...] = acc_ref[...].astype(o_ref.dtype)

    return pl.pallas_call(
        body_one_step if nk == 1 else body, name=name, grid=(m // tm, n // tn, nk),
        in_specs=[a_spec, b_spec] + [pl.BlockSpec(memory_space=pl.ANY)] * len(extra), out_specs=o_spec,
        out_shape=jax.ShapeDtypeStruct((nbo, o_rows, co), out_dtype),
        input_output_aliases={} if fill is None else {1 + len(extra): 0},
        scratch_shapes=[] if nk == 1 else [pltpu.VMEM((tm, tn), F32)], compiler_params=_cparams(3))(a, b, *extra)


def _rms(x, g):
    return x * lax.rsqrt(jnp.mean(x * x, axis=-1, keepdims=True) + NORM_EPS) * g


def _sigmoid(x):
    return 1.0 / (1.0 + jnp.exp(-x))


def _silu(x):
    return x * _sigmoid(x)


def _gelu(x):
    return 0.5 * x * (1.0 + lax.erf(x * 0.7071067811865476))


def _softplus(x):
    return jnp.maximum(x, 0.0) + jnp.log(1.0 + jnp.exp(-jnp.abs(x)))


def _split(x, n):
    w = x.shape[-1] // n
    return [x[..., h * w:(h + 1) * w] for h in range(n)]


def _sgu(ua, va, smw, smb, lng, lnb):
    c = ua[0].shape[0]
    width = len(va) * va[0].shape[-1]
    vf = [_gelu(v) for v in va]
    mu = sum(jnp.sum(v, axis=-1, keepdims=True) for v in vf) / width
    var = sum(jnp.sum(jnp.square(v - mu), axis=-1, keepdims=True) for v in vf) / width
    r = lax.rsqrt(var + NORM_EPS)
    causal = lax.broadcasted_iota(jnp.int32, (c, c), 0) >= lax.broadcasted_iota(jnp.int32, (c, c), 1)
    outs = []
    for h in range(len(ua)):
        vn = (vf[h] - mu) * r * lng[h] + lnb[h]
        w = jnp.where(causal, smw[h], 0.0)
        mixed = jnp.dot(w.astype(BF16), vn.astype(BF16), preferred_element_type=F32) + smb[h]
        outs.append(_gelu(ua[h]) * mixed)
    return outs


def _shift_rows(x, k, up):
    rows = x.shape[0]
    row = lax.broadcasted_iota(jnp.int32, x.shape, 0)
    if up:
        return jnp.where(row < rows - k, pltpu.roll(x, rows - k, 0), 0.0)
    return jnp.where(row >= k, pltpu.roll(x, k, 0), 0.0)


@functools.partial(jax.custom_vjp, nondiff_argnums=(1,))
def _delay(x, k):
    return _shift_rows(x, k, False)


def _delay_fwd(x, k):
    return _shift_rows(x, k, False), None


def _delay_bwd(k, _, g):
    return (_shift_rows(g, k, True),)


_delay.defvjp(_delay_fwd, _delay_bwd)


def _conv_silu(x, w0, w1, w2, w3):
    y = w3 * x + w2 * _delay(x, 1) + w1 * _delay(x, 2) + w0 * _delay(x, 3)
    return _silu(y)


_BDOT_DIMS = {"nn": (((2,), (1,)), ((0,), (0,))), "nt": (((2,), (2,)), ((0,), (0,))), "tn": (((1,), (1,)), ((0,), (0,)))}


def _bdot_passes(a, b, kind, passes):
    one = lambda x, y: lax.dot_general(x, y, _BDOT_DIMS[kind], preferred_element_type=F32)
    ah, bh = a.astype(BF16), b.astype(BF16)
    if passes == 1:
        return one(ah, bh)
    al, bl = (a - ah.astype(F32)).astype(BF16), (b - bh.astype(F32)).astype(BF16)
    return one(ah, bh) + (one(ah, bl) + one(al, bh))


@functools.partial(jax.custom_vjp, nondiff_argnums=(2, 3))
def bdot(a, b, kind, passes):
    return _bdot_passes(a, b, kind, passes)


def _bdot_fwd(a, b, kind, passes):
    return _bdot_passes(a, b, kind, passes), (a, b)


def _bdot_bwd(kind, passes, res, g):
    a, b = res
    if kind == "nn":
        return bdot(g, b, "nt", passes), bdot(a, g, "tn", passes)
    if kind == "nt":
        return bdot(g, b, "nn", passes), bdot(g, a, "tn", passes)
    return bdot(b, g, "nt", passes), bdot(a, g, "nn", passes)


bdot.defvjp(_bdot_fwd, _bdot_bwd)


def _dn_step(s, q, k, v, braw, araw, alog, dtb):
    nh, c, d = q.shape
    row = lax.broadcasted_iota(jnp.int32, (c, c), 0)
    col = lax.broadcasted_iota(jnp.int32, (c, c), 1)
    causal, strict = (row >= col)[None], (row > col)[None]
    lower = (row >= col).astype(F32)
    strict_f = jnp.broadcast_to((row > col).astype(F32)[None], (nh, c, c))
    eye = (row == col).astype(F32)[None]

    qn = q * lax.rsqrt(jnp.sum(q * q, axis=-1, keepdims=True) + NORM_EPS) * (d ** -0.5)
    kn = k * lax.rsqrt(jnp.sum(k * k, axis=-1, keepdims=True) + NORM_EPS)
    beta = _sigmoid(braw)
    g = -jnp.exp(alog) * _softplus(araw + dtb)
    lg = lower[None] * g
    gcum = jnp.sum(lg, axis=-1, keepdims=True)
    diff = bdot(lg, strict_f, "nn", 3)
    decay = jnp.where(causal, jnp.exp(diff), 0.0)
    bcol = jnp.sum(eye * beta, axis=-1, keepdims=True)
    kb = kn * bcol
    a = jnp.where(strict, bdot(kb, kn, "nt", 1) * decay, 0.0)
    inv = eye - a
    p = bdot(a, a, "nn", 3)
    n_fac = int(math.log2(c)) - 1
    for it in range(n_fac):
        inv = inv + bdot(inv, p, "nn", 3)
        if it < n_fac - 1:
            p = bdot(p, p, "nn", 3)
    eg = jnp.exp(gcum)
    u = bdot(inv, v * bcol, "nn", 1)
    w = bdot(inv, kb * eg, "nn", 1)
    qk = jnp.where(causal, bdot(qn, kn, "nt", 1) * decay, 0.0)
    v_new = u - bdot(w, s, "nn", 1)
    o = bdot(qn * eg, s, "nn", 1) + bdot(qk, v_new, "nn", 1)
    glast = jnp.sum(g, axis=-1, keepdims=True)
    kdec = kn * jnp.exp(glast - gcum)
    s_new = s * jnp.exp(glast) + bdot(kdec, v_new, "tn", 1)
    return s_new, o


def _ogate(o, z, w):
    return [_rms(oh, w) * _silu(zh) for oh, zh in zip(o, z)]


def _xattn(q, k, v):
    outs = []
    for qh, kh, vh in zip(q, k, v):
        s = lax.dot_general(qh.astype(BF16), kh.astype(BF16), (((1,), (1,)), ((), ())),
                            preferred_element_type=F32) * (qh.shape[-1] ** -0.5)
        s = s - jnp.max(s, axis=-1, keepdims=True)
        e = jnp.exp(s)
        p = e / jnp.sum(e, axis=-1, keepdims=True)
        outs.append(jnp.dot(p.astype(BF16), vh.astype(BF16), preferred_element_type=F32))
    return outs


def _rows(t):
    return min(ROW_TILE, t)


def k_rms_fwd(name, x, g, after=()):
    t, d = x.shape
    tm = _rows(t)
    return stepk(name, lambda xv, gv: (_rms(xv, gv),), (t // tm,),
                 [(x, (tm, d), lambda i: (i, 0)), (g, (1, d), lambda i: (0, 0))],
                 [((t, d), BF16, (tm, d), lambda i: (i, 0))], after=after)[0]


def k_rms_bwd(name, x, g, dhs, dx_res):
    t, d = x.shape
    tm = _rows(t)
    n = len(dhs)

    def fn(xv, gv, *rest):
        dh = sum(r.astype(F32) for r in rest[:n])
        dxr, dg_c = rest[n], rest[n + 1]
        _, vjp = jax.vjp(_rms, xv, gv)
        dx, dg = vjp(dh)
        return dx + dxr, dg_c + dg

    row = lambda arr: (arr, (tm, d), lambda i: (i, 0))
    return stepk(name, fn, (t // tm,), [row(x), (g, (1, d), lambda i: (0, 0))] + [row(h) for h in dhs] + [row(dx_res)],
                 [((t, d), F32, (tm, d), lambda i: (i, 0))], carries=[((1, d), F32)])


def k_post_fwd(name, x, f, g, scale, after=()):
    t, d = x.shape
    tm = _rows(t)
    row = lambda arr: (arr, (tm, d), lambda i: (i, 0))
    return stepk(name, lambda xv, fv, gv: (xv + scale * _rms(fv, gv),), (t // tm,),
                 [row(x), row(f), (g, (1, d), lambda i: (0, 0))], [((t, d), F32, (tm, d), lambda i: (i, 0))],
                 after=after)[0]


def k_post_bwd(name, f, g, dxo, scale, after=()):
    t, d = f.shape
    tm = _rows(t)

    def fn(fv, gv, dv, dg_c):
        _, vjp = jax.vjp(lambda a, b: scale * _rms(a, b), fv, gv)
        df, dg = vjp(dv)
        return df, dg_c + dg

    row = lambda arr: (arr, (tm, d), lambda i: (i, 0))
    return stepk(name, fn, (t // tm,), [row(f), (g, (1, d), lambda i: (0, 0)), row(dxo)],
                 [((t, d), BF16, (tm, d), lambda i: (i, 0))], carries=[((1, d), F32)], after=after)


def k_swiglu_fwd(name, hu, after=()):
    nb, t, c = hu.shape
    half = nb // 2
    tm = _rows(t)
    fn = lambda gv, uv: (_silu(gv.astype(F32)) * uv.astype(F32),)
    return stepk(name, fn, (half, t // tm),
                 [(hu, (None, tm, c), lambda b, i: (b, i, 0)), (hu, (None, tm, c), lambda b, i: (b + half, i, 0))],
                 [((half, t, c), BF16, (None, tm, c), lambda b, i: (b, i, 0))], after=after)[0]


def k_swiglu_bwd(name, hu, da):
    nb, t, c = hu.shape
    half = nb // 2
    tm = _rows(t)

    def fn(gv, uv, dv):
        gv, uv, dv = gv.astype(F32), uv.astype(F32), dv.astype(F32)
        sg = _sigmoid(gv)
        return (jnp.stack([dv * uv * sg * (1.0 + gv * (1.0 - sg)), dv * gv * sg]),)

    blk = lambda arr, off: (arr, (None, tm, c), lambda b, i: (b + off, i, 0))
    out = stepk(name, fn, (half, t // tm), [blk(hu, 0), blk(hu, half), blk(da, 0)],
                [((2, half, t, c), BF16, (2, None, tm, c), lambda b, i: (0, b, i, 0))])[0]
    return out.reshape(nb, t, c)


def k_loss(name, y, tgt):
    t, d = y.shape
    tm = _rows(t)

    def fn(yv, tv, acc):
        e = yv - tv
        part = jnp.sum(jnp.sum(e * e, axis=-1, keepdims=True), axis=0, keepdims=True)
        return e * (1.0 / d), acc + (0.5 / d) * part

    row = lambda arr: (arr, (tm, d), lambda i: (i, 0))
    return stepk(name, fn, (t // tm,), [row(y), row(tgt)], [((t, d), F32, (tm, d), lambda i: (i, 0))],
                 carries=[((1, 1), F32)])


GM_W = GM_HEADS * 128
QKV_COL0 = 2 * GM_W
QKV_W = 3 * DN_HEADS * DN_DIM
Z_COL0 = QKV_COL0 + QKV_W
MAIN_W = Z_COL0 + DN_HEADS * DN_DIM


def _sgu_ins(proj, p):
    c = GM_CHUNK
    return [(proj, (c, GM_W), lambda i: (i, 0)), (proj, (c, GM_W), lambda i: (i, 1)),
            (p["sm_w"], (GM_HEADS, c, c), lambda i: (0, 0, 0)), (p["sm_b"], (GM_HEADS, c, 1), lambda i: (0, 0, 0)),
            (p["sm_ln_g"], (1, GM_W), lambda i: (0, 0)), (p["sm_ln_b"], (1, GM_W), lambda i: (0, 0))]


def _sgu_lists(uv, vv, sw, sb, lg, lb):
    nh = GM_HEADS
    return (_split(uv, nh), _split(vv, nh), [sw[h] for h in range(nh)], [sb[h] for h in range(nh)],
            _split(lg, nh), _split(lb, nh))


def k_sgu_fwd(name, proj, p):
    t = proj.shape[0]

    def fn(*vals):
        return (jnp.concatenate(_sgu(*_sgu_lists(*vals)), axis=-1),)

    return stepk(name, fn, (t // GM_CHUNK,), _sgu_ins(proj, p),
                 [((2, t, GM_W), BF16, (None, GM_CHUNK, GM_W), lambda i: (0, i, 0))])[0]


def k_sgu_bwd(name, proj, p, dy, dproj):
    t = proj.shape[0]
    c = GM_CHUNK

    def fn(uv, vv, sw, sb, lg, lb, dv, dsw, dsb, dlg, dlb):
        _, vjp = jax.vjp(_sgu, *_sgu_lists(uv, vv, sw, sb, lg, lb))
        gu, gv, gsw, gsb, glg, glb = vjp(_split(dv.astype(F32), GM_HEADS))
        cat = lambda l: jnp.concatenate(l, axis=-1)
        return (cat(gu + gv), dsw + jnp.stack(gsw), dsb + jnp.stack(gsb), dlg + cat(glg), dlb + cat(glb))

    return stepk(name, fn, (t // c,), _sgu_ins(proj, p) + [(dy, (None, c, GM_W), lambda i: (0, i, 0))],
                 [((t, MAIN_W), BF16, (c, 2 * GM_W), lambda i: (i, 0))],
                 carries=[((GM_HEADS, c, c), F32), ((GM_HEADS, c, 1), F32), ((1, GM_W), F32), ((1, GM_W), F32)],
                 fill=[(dproj, 0)])


def _conv_ins(proj, conv_w):
    t = proj.shape[0]
    return [(proj, (t, 128), lambda j: (0, QKV_COL0 // 128 + j)), (conv_w, (CONV_W, 128), lambda j: (0, j))]


def k_conv_fwd(name, proj, conv_w):
    t = proj.shape[0]
    n = QKV_W // 128
    fn = lambda xv, wv: (_conv_silu(xv, *[wv[i:i + 1] for i in range(CONV_W)]),)
    return stepk(name, fn, (n,), _conv_ins(proj, conv_w), [((n, t, 128), F32, (None, t, 128), lambda j: (j, 0, 0))])[0]


def k_conv_bwd(name, proj, conv_w, dq, dk, dv, dproj):
    t = proj.shape[0]
    n = QKV_W // 128
    nh = dq.shape[0]

    def fn(xv, wv, dqv, dkv, dvv):
        part = pl.program_id(0) // nh
        d = jnp.where(part == 0, dqv, jnp.where(part == 1, dkv, dvv))
        _, vjp = jax.vjp(_conv_silu, xv, *[wv[i:i + 1] for i in range(CONV_W)])
        gx, *gw = vjp(d)
        return gx, jnp.concatenate(gw, axis=0)

    head = lambda arr, k: (arr, (None, t, 128), lambda j: (jnp.clip(j - k * nh, 0, nh - 1), 0, 0))
    return stepk(name, fn, (n,), _conv_ins(proj, conv_w) + [head(dq, 0), head(dk, 1), head(dv, 2)],
                 [((t, MAIN_W), BF16, (t, 128), lambda j: (0, QKV_COL0 // 128 + j)),
                  ((CONV_W, QKV_W), F32, (CONV_W, 128), lambda j: (0, j))], fill=[(dproj, 0)])


def _dn_ins(qkv, braw, araw, p, order):
    h, c, d = DN_HEADS, DN_CHUNK, DN_DIM
    qkv_in = lambda part: (qkv, (h, c, d), lambda n: (part, order(n), 0))
    gate_in = lambda arr: (arr, (None, h, 1, c), lambda n: (order(n), 0, 0, 0))
    par_in = lambda arr: (arr, (h, 1, 1), lambda n: (0, 0, 0))
    return [qkv_in(0), qkv_in(1), qkv_in(2), gate_in(braw), gate_in(araw), par_in(p["a_log"]), par_in(p["dt_bias"])]


def k_dn_fwd(name, qkv, braw, araw, p):
    t = qkv.shape[1]
    h, c, d = DN_HEADS, DN_CHUNK, DN_DIM
    nc = t // c

    def fn(q, k, v, b, a, al, dt, s):
        s_new, o = _dn_step(s, q, k, v, b, a, al, dt)
        return o, s, s_new

    o, s_all, _ = stepk(name, fn, (nc,), _dn_ins(qkv, braw, araw, p, lambda n: n),
                        [((h, t, d), F32, (h, c, d), lambda n: (0, n, 0)),
                         ((nc, h, d, d), F32, (None, h, d, d), lambda n: (n, 0, 0, 0))],
                        carries=[((h, d, d), F32)])
    return o, s_all


def k_dn_bwd(name, qkv, braw, araw, p, s_all, do):
    t = qkv.shape[1]
    h, c, d = DN_HEADS, DN_CHUNK, DN_DIM
    nc = t // c
    rev = lambda n: nc - 1 - n

    def fn(q, k, v, b, a, al, dt, s, dov, ds_c, dal_c, ddt_c):
        _, vjp = jax.vjp(_dn_step, s, q, k, v, b, a, al, dt)
        ds, dq, dk, dv, db, da, dal, ddt = vjp((ds_c, dov))
        return dq, dk, dv, db, da, ds, dal_c + dal, ddt_c + ddt

    ins = _dn_ins(qkv, braw, araw, p, rev) + [(s_all, (None, h, d, d), lambda n: (rev(n), 0, 0, 0)),
                                               (do, (h, c, d), lambda n: (0, rev(n), 0))]
    hd = ((h, t, d), F32, (h, c, d), lambda n: (0, rev(n), 0))
    gate = ((nc, h, 1, c), F32, (None, h, 1, c), lambda n: (rev(n), 0, 0, 0))
    dq, dk, dv, db, da, _, dal, ddt = stepk(name, fn, (nc,), ins, [hd, hd, hd, gate, gate],
                                            carries=[((h, d, d), F32), ((h, 1, 1), F32), ((h, 1, 1), F32)])
    return dq, dk, dv, db, da, dal, ddt


def _ogate_ins(o, proj, p):
    t = o.shape[1]
    tm = _rows(t)
    return tm, [(o, (DN_HEADS, tm, DN_DIM), lambda i: (0, i, 0)), (proj, (tm, GM_W), lambda i: (i, Z_COL0 // GM_W)),
                (p["dn_norm_w"], (1, DN_DIM), lambda i: (0, 0))]


def k_ogate_fwd(name, o, proj, p, y):
    t = o.shape[1]
    tm, ins = _ogate_ins(o, proj, p)

    def fn(ov, zv, wv):
        return (jnp.concatenate(_ogate([ov[h] for h in range(DN_HEADS)], _split(zv, DN_HEADS), wv), axis=-1),)

    return stepk(name, fn, (t // tm,), ins, [((2, t, GM_W), BF16, (None, tm, GM_W), lambda i: (1, i, 0))],
                 fill=[(y, 0)])[0]


def k_ogate_bwd(name, o, proj, p, dy):
    t = o.shape[1]
    tm, ins = _ogate_ins(o, proj, p)

    def fn(ov, zv, wv, dv, dw_c):
        _, vjp = jax.vjp(_ogate, [ov[h] for h in range(DN_HEADS)], _split(zv, DN_HEADS), wv)
        go, gz, gw = vjp(_split(dv.astype(F32), DN_HEADS))
        return jnp.stack(go), jnp.concatenate(gz, axis=-1), dw_c + gw

    return stepk(name, fn, (t // tm,), ins + [(dy, (None, tm, GM_W), lambda i: (1, i, 0))],
                 [((DN_HEADS, t, DN_DIM), F32, (DN_HEADS, tm, DN_DIM), lambda i: (0, i, 0)),
                  ((t, MAIN_W), BF16, (tm, GM_W), lambda i: (i, Z_COL0 // GM_W))], carries=[((1, DN_DIM), F32)])


def _xattn_lists(qv, kvv):
    nh = XA_HEADS
    return (_split(qv.astype(F32), nh), [kvv[h].astype(F32) for h in range(nh)],
            [kvv[nh + h].astype(F32) for h in range(nh)])


def k_xattn_fwd(name, q, kv):
    t, d = q.shape
    tm = _rows(t)
    fn = lambda qv, kvv: (jnp.concatenate(_xattn(*_xattn_lists(qv, kvv)), axis=-1),)
    return stepk(name, fn, (t // tm,), [(q, (tm, d), lambda i: (i, 0)), (kv, kv.shape, lambda i: (0, 0, 0))],
                 [((t, d), BF16, (tm, d), lambda i: (i, 0))])[0]


def k_xattn_bwd(name, q, kv, do):
    t, d = q.shape
    tm = _rows(t)

    def fn(qv, kvv, dv, dkv_c):
        _, vjp = jax.vjp(_xattn, *_xattn_lists(qv, kvv))
        gq, gk, gv = vjp(_split(dv.astype(F32), XA_HEADS))
        return jnp.concatenate(gq, axis=-1), dkv_c + jnp.stack(gk + gv)

    return stepk(name, fn, (t // tm,),
                 [(q, (tm, d), lambda i: (i, 0)), (kv, kv.shape, lambda i: (0, 0, 0)), (do, (None, tm, d), lambda i: (0, i, 0))],
                 [((t, d), BF16, (tm, d), lambda i: (i, 0))], carries=[(kv.shape, F32)])


def ffn_fwd(tag, x, p, pre, post, gu, dn, sch):
    h = k_rms_fwd(f"{tag}_pre", x, p[pre], sch.take())
    hu = mm(f"{tag}_gu", h[None], p[gu], "nn", BF16, nbo=N_DEV)
    sch.prefetch(p, dn, hu)
    a = k_swiglu_fwd(f"{tag}_act", hu, sch.take())
    w_down = sch.param(p, dn, [a])
    sch.mid(a, late=False)
    f = mm(f"{tag}_down", a, w_down, "nn", F32, after=sch.take())[0]
    sch.mid(f, late=True)
    y = k_post_fwd(f"{tag}_post", x, f, p[post], 0.5, sch.take())
    return y, (x, h, hu, a, f)


class NoExchange:
    def take(self):
        return ()

    def emit(self, grads, names):
        pass

    def mid(self, arr, late):
        pass

    def prefetch(self, p, name, arr):
        pass

    def param(self, p, name, after):
        return p[name]


def ffn_bwd(tag, dy, saved, p, pre, post, gu, dn, grads, sch):
    x, h, hu, a, f = saved
    df, grads[post] = k_post_bwd(f"{tag}_post_b", f, p[post], dy, 0.5, sch.take())
    grads[dn] = mm(f"{tag}_down_dw", a, df[None], "tn", BF16)
    sch.emit(grads, (dn,))
    da = mm(f"{tag}_down_dx", df[None], p[dn], "nt", BF16, nbo=N_DEV // 2, after=sch.take())
    dhu = k_swiglu_bwd(f"{tag}_act_b", hu, da)
    grads[gu] = mm(f"{tag}_gu_dw", h[None], dhu, "tn", BF16, nbo=N_DEV)
    sch.emit(grads, (gu,))
    dh = mm(f"{tag}_gu_dx", dhu, p[gu], "nt", BF16, after=sch.take())[0]
    dx, grads[pre] = k_rms_bwd(f"{tag}_pre_b", x, p[pre], [dh], dy)
    return dx


def _to_chunks(a):
    t, h = a.shape
    return a.reshape(t // DN_CHUNK, DN_CHUNK, h).transpose(0, 2, 1).reshape(t // DN_CHUNK, h, 1, DN_CHUNK)


def _from_chunks(a):
    nc, h, _, c = a.shape
    return a.reshape(nc, h, c).transpose(0, 2, 1).reshape(nc * c, h)


def mix_fwd(tag, x, p, sch):
    h = k_rms_fwd(f"{tag}_pre", x, p["mix_norm_pre"], sch.take())
    proj = mm(f"{tag}_in", h[None], p["w_in_t"], "nt", F32, b_rows=(0, MAIN_W))[0]
    sch.mid(proj, late=False)
    small = mm(f"{tag}_in_s", h[None], p["w_in_t"], "nt", F32, after=sch.take(), b_rows=(MAIN_W, 2 * DN_HEADS))[0]
    braw, araw = _to_chunks(small[:, :DN_HEADS]), _to_chunks(small[:, DN_HEADS:2 * DN_HEADS])
    y = k_sgu_fwd(f"{tag}_sgu", proj, p)
    qkv = k_conv_fwd(f"{tag}_conv", proj, p["conv_w"])
    o, s_all = k_dn_fwd(f"{tag}_dn", qkv, braw, araw, p)
    y = k_ogate_fwd(f"{tag}_og", o, proj, p, y)
    m = mm(f"{tag}_out", y, p["w_out"], "nn", F32)[0]
    out = k_post_fwd(f"{tag}_post", x, m, p["mix_norm_post"], 1.0)
    return out, (x, h, proj, braw, araw, qkv, o, s_all, y, m)


def mix_bwd(tag, dy, saved, p, grads, sch):
    x, h, proj, braw, araw, qkv, o, s_all, y, m = saved
    dm, grads["mix_norm_post"] = k_post_bwd(f"{tag}_post_b", m, p["mix_norm_post"], dy, 1.0, sch.take())
    grads["w_out"] = mm(f"{tag}_out_dw", y, dm[None], "tn", BF16)
    sch.emit(grads, ("w_out",))
    dyy = mm(f"{tag}_out_dx", dm[None], p["w_out"], "nt", BF16, nbo=2, after=sch.take())
    do, dproj, grads["dn_norm_w"] = k_ogate_bwd(f"{tag}_og_b", o, proj, p, dyy)
    dq, dk, dv, db, da, grads["a_log"], grads["dt_bias"] = k_dn_bwd(f"{tag}_dn_b", qkv, braw, araw, p, s_all, do)
    dproj, grads["conv_w"] = k_conv_bwd(f"{tag}_conv_b", proj, p["conv_w"], dq, dk, dv, dproj)
    dproj, grads["sm_w"], grads["sm_b"], grads["sm_ln_g"], grads["sm_ln_b"] = k_sgu_bwd(f"{tag}_sgu_b", proj, p, dyy, dproj)
    dsmall = jnp.concatenate([_from_chunks(db), _from_chunks(da)], axis=-1)
    g_in = mm(f"{tag}_in_dw", dproj[None], h[None], "tn", BF16, out_rows=(0, IN_COLS))
    g_in = mm(f"{tag}_in_s_dw", dsmall[None], h[None], "tn", BF16, out_rows=(MAIN_W, IN_COLS), fill=g_in)[0]
    grads["w_in"] = g_in.reshape(N_DEV, g_in.shape[0] // N_DEV, g_in.shape[1])
    sch.emit(grads, ("w_in",))
    dh = mm(f"{tag}_in_dx", dproj[None], p["w_in_t"], "nn", BF16, after=sch.take(), b_rows=(0, MAIN_W))[0]
    dh_s = mm(f"{tag}_in_s_dx", dsmall[None], p["w_in_t"], "nn", BF16, b_rows=(MAIN_W, 2 * DN_HEADS))[0]
    dx, grads["mix_norm_pre"] = k_rms_bwd(f"{tag}_pre_b", x, p["mix_norm_pre"], [dh, dh_s], dy)
    return dx


def xa_fwd(tag, x, mem, p, sch):
    after = sch.take()
    hx = k_rms_fwd(f"{tag}_pre", x, p["xa_norm_pre"], after)
    mh = k_rms_fwd(f"{tag}_mem", mem, p["mem_norm"], after)
    q = mm(f"{tag}_q", hx[None], p["w_xq"], "nn", BF16)[0]
    sch.mid(q, late=False)
    kv = mm(f"{tag}_kv", mh[None], p["w_xkv"], "nn", BF16, nbo=N_DEV, after=sch.take())
    o = k_xattn_fwd(f"{tag}_att", q, kv)
    sch.mid(o, late=True)
    c = mm(f"{tag}_o", o[None], p["w_xo"], "nn", F32, after=sch.take())[0]
    out = k_post_fwd(f"{tag}_post", x, c, p["xa_norm_post"], 1.0)
    return out, (x, mem, hx, mh, q, kv, o, c)


def xa_bwd(tag, dy, saved, p, grads, sch):
    x, mem, hx, mh, q, kv, o, c = saved
    dc, grads["xa_norm_post"] = k_post_bwd(f"{tag}_post_b", c, p["xa_norm_post"], dy, 1.0, sch.take())
    grads["w_xo"] = mm(f"{tag}_o_dw", o[None], dc[None], "tn", BF16)
    sch.emit(grads, ("w_xo",))
    do = mm(f"{tag}_o_dx", dc[None], p["w_xo"], "nt", BF16, after=sch.take())
    dq, dkv = k_xattn_bwd(f"{tag}_att_b", q, kv, do)
    dkv16 = dkv.astype(BF16)
    grads["w_xq"] = mm(f"{tag}_q_dw", hx[None], dq[None], "tn", BF16)
    grads["w_xkv"] = mm(f"{tag}_kv_dw", mh[None], dkv16, "tn", BF16, nbo=N_DEV)
    sch.emit(grads, ("w_xq", "w_xkv"))
    dhx = mm(f"{tag}_q_dx", dq[None], p["w_xq"], "nt", BF16, after=sch.take())[0]
    dmh = mm(f"{tag}_kv_dx", dkv16, p["w_xkv"], "nt", BF16)[0]
    _, grads["mem_norm"] = k_rms_bwd(f"{tag}_mem_b", mem, p["mem_norm"], [dmh], jnp.zeros_like(mem))
    dx, grads["xa_norm_pre"] = k_rms_bwd(f"{tag}_pre_b", x, p["xa_norm_pre"], [dhx], dy)
    return dx


def _place():
    return lax.axis_index("x"), lax.axis_index("y"), lax.axis_index("c")


def _other_chips(x, y):
    return [(1 - x, y), (x, 1 - y), (1 - x, 1 - y)]


_ANY = pl.BlockSpec(memory_space=pl.ANY)


def all_gather(name, shards, after=()):
    n, na = len(shards), len(after)

    def body(*refs):
        ins, outs = refs[:n], refs[n + na:2 * n + na]
        send_sems, recv_sems, local_sems = refs[2 * n + na:]
        x, y, c = _place()
        me, sibling = (x, y, c), (x, y, 1 - c)
        chips = _other_chips(x, y)
        idx = lambda px, py, pc: 4 * px + 2 * py + pc

        def copy(a, k, block, to, src=None):
            dst = outs[a].at[idx(*block)]
            return pltpu.make_async_remote_copy(src_ref=dst if src is None else src, dst_ref=dst,
                                                send_sem=send_sems.at[7 * a + k], recv_sem=recv_sems.at[7 * a + k],
                                                device_id=to, device_id_type=MESH)

        mine = [pltpu.make_async_copy(ins[a], outs[a].at[idx(*me)], local_sems.at[a]) for a in range(n)]
        for cp in mine:
            cp.start()
        first = []
        for a in range(n):
            first.append(copy(a, 0, me, sibling, src=ins[a]))
            first += [copy(a, 1 + j, me, (*chip, c), src=ins[a]) for j, chip in enumerate(chips)]
        for cp in first:
            cp.start()
        passed = []
        for a in range(n):
            for j, chip in enumerate(chips):
                copy(a, 1 + j, (*chip, c), me).wait_recv()
                passed.append(copy(a, 4 + j, (*chip, c), sibling))
                passed[-1].start()
        for a in range(n):
            copy(a, 0, sibling, me).wait_recv()
            for j, chip in enumerate(chips):
                copy(a, 4 + j, (*chip, 1 - c), me).wait_recv()
        for cp in first + passed:
            cp.wait_send()
        for cp in mine:
            cp.wait()

    return pl.pallas_call(
        body, name=name, in_specs=[_ANY] * (n + na), out_specs=[_ANY] * n,
        out_shape=[jax.ShapeDtypeStruct((N_DEV,) + s.shape, s.dtype) for s in shards],
        scratch_shapes=[pltpu.SemaphoreType.DMA((7 * n,)), pltpu.SemaphoreType.DMA((7 * n,)),
                        pltpu.SemaphoreType.DMA((n,))])(*shards, *after)


def k_cast_place(name, w, layer, me, after):
    _, r, cc = w.shape
    tr, tc = _tile2(r, cc)
    return stepk(name, lambda v: (v,), (r // tr, cc // tc), [(w, (None, tr, tc), lambda i, j, pre: (layer, i, j))],
                 [((N_DEV, r, cc), BF16, (None, tr, tc), lambda i, j, pre: (pre[0], i, j))], prefetch=me,
                 after=after)[0]


_HBM = pl.BlockSpec(memory_space=pltpu.HBM)
_SEM = pl.BlockSpec(memory_space=pltpu.SEMAPHORE)
_DATAFLOW = pltpu.SideEffectType.DATAFLOW_SIDE_EFFECTING


def xfer_start(name, plan, n_sems, srcs, lands, after=()):
    ns, nl, na = len(srcs), len(lands), len(after)
    bufs = list(srcs) + list(lands)

    def body(*refs):
        send_sems, recv_sems, token = refs[ns + nl + na], refs[ns + nl + na + 1], refs[-1]
        for cp in plan(refs[:ns], refs[ns:ns + nl], send_sems, recv_sems):
            cp.start()
        token[...] = jnp.zeros(token.shape, token.dtype)

    res = pl.pallas_call(
        body, name=name, in_specs=[_HBM] * (ns + nl) + [_ANY] * na,
        out_specs=(_SEM, _SEM, *[_HBM] * (ns + nl), pl.BlockSpec(memory_space=pltpu.VMEM)),
        out_shape=(pltpu.SemaphoreType.DMA((n_sems,)), pltpu.SemaphoreType.DMA((n_sems,)),
                   *[pltpu.HBM(b.shape, b.dtype) for b in bufs], jax.ShapeDtypeStruct((8, 128), F32)),
        input_output_aliases={i: 2 + i for i in range(ns + nl)},
        compiler_params=pltpu.CompilerParams(has_side_effects=_DATAFLOW),
    )(*[pltpu.with_memory_space_constraint(b, pltpu.HBM) for b in bufs], *after)
    return dict(send=res[0], recv=res[1], srcs=list(res[2:2 + ns]), lands=list(res[2 + ns:2 + ns + nl]), token=res[-1])


def xfer_wait(name, plan, started, after=()):
    ns, nl = len(started["srcs"]), len(started["lands"])
    bufs = started["srcs"] + started["lands"]

    def body(*refs):
        for cp in plan(refs[:ns], refs[ns:ns + nl], refs[ns + nl], refs[ns + nl + 1]):
            cp.wait_send()
            cp.wait_recv()

    res = pl.pallas_call(
        body, name=name, in_specs=[_HBM] * (ns + nl) + [_SEM, _SEM] + [_ANY] * len(after),
        out_specs=tuple([_HBM] * (ns + nl)), out_shape=tuple(pltpu.HBM(b.shape, b.dtype) for b in bufs),
        input_output_aliases={i: i for i in range(ns + nl)},
        compiler_params=pltpu.CompilerParams(has_side_effects=_DATAFLOW),
    )(*bufs, started["send"], started["recv"], *after)
    return list(res[:ns]), list(res[ns:])


def _remote(src, dst, send_sems, recv_sems, k, to):
    return pltpu.make_async_remote_copy(src_ref=src, dst_ref=dst, send_sem=send_sems.at[k], recv_sem=recv_sems.at[k],
                                        device_id=to, device_id_type=MESH)


def plan_gather_out(srcs, lands, send_sems, recv_sems):
    x, y, c = _place()
    me = 4 * x + 2 * y + c
    targets = [(x, y, 1 - c)] + [(px, py, c) for px, py in _other_chips(x, y)]
    return [_remote(lands[a].at[me], lands[a].at[me], send_sems, recv_sems, 4 * a + k, to)
            for a in range(len(lands)) for k, to in enumerate(targets)]


def plan_gather_pass(srcs, lands, send_sems, recv_sems):
    x, y, c = _place()
    return [_remote(lands[a].at[4 * px + 2 * py + c], lands[a].at[4 * px + 2 * py + c], send_sems, recv_sems,
                    3 * a + j, (x, y, 1 - c))
            for a in range(len(lands)) for j, (px, py) in enumerate(_other_chips(x, y))]


def plan_scatter_sibling(srcs, lands, send_sems, recv_sems):
    x, y, c = _place()
    return [_remote(srcs[a].at[j, 1 - c], lands[a].at[j], send_sems, recv_sems, 4 * a + j, (x, y, 1 - c))
            for a in range(len(srcs)) for j in range(4)]


def plan_scatter_chips(srcs, lands, send_sems, recv_sems):
    x, y, c = _place()
    return [_remote(srcs[a].at[2 * px + py], lands[a].at[j], send_sems, recv_sems, 3 * a + j, (px, py, c))
            for a in range(len(srcs)) for j, (px, py) in enumerate(_other_chips(x, y))]


ELEMWISE_BLOCK = 512 * 1024


def _tile2(r, cc, limit=ELEMWISE_BLOCK):
    if r % 16:
        tc = cc
        while r * tc > limit and tc % 256 == 0:
            tc //= 2
        return r, tc
    t = r
    while t * cc > limit and t % 32 == 0:
        t //= 2
    if t * cc > limit:
        for cand in range(t, 15, -16):
            if r % cand == 0 and cand * cc <= limit:
                return cand, cc
    return t, cc


def k_pair_add(name, g4, r1, core):
    _, _, r, cc = g4.shape
    tr, tc = _tile2(r, cc, 2 * ELEMWISE_BLOCK)
    fn = lambda av, bv: (av.astype(F32) + bv.astype(F32),)
    return stepk(name, fn, (4, r // tr, cc // tc),
                 [(g4, (None, None, tr, tc), lambda b, i, j, pre: (b, pre[0], i, j)),
                  (r1, (None, tr, tc), lambda b, i, j, pre: (b, i, j))],
                 [((4, r, cc), BF16, (None, tr, tc), lambda b, i, j, pre: (b, i, j))], prefetch=core)[0]


def _adamw(g, w, m, v):
    m2 = ADAM_B1 * m + (1.0 - ADAM_B1) * g
    v2 = ADAM_B2 * v + (1.0 - ADAM_B2) * jnp.square(g)
    m_hat = m2 / (1.0 - ADAM_B1 ** ADAM_STEP)
    v_hat = v2 / (1.0 - ADAM_B2 ** ADAM_STEP)
    delta = -ADAM_LR * (m_hat / (jnp.sqrt(v_hat) + ADAM_EPS) + ADAM_WD * w)
    return g, delta, m2, v2


def k_adamw_shard(name, part, others, w, m, v, layer, chip, fill):
    _, r, cc = part.shape
    tr, tc = _tile2(r, cc)

    def fn(pv, o0, o1, o2, wv, mv, vv, done):
        g = ((pv.astype(F32) + o0.astype(F32)) + o1.astype(F32)) + o2.astype(F32)
        return _adamw(g, wv, mv, vv) + (done,)

    other = lambda k: (others, (None, tr, tc), lambda i, j, pre: (k, i, j))
    state = lambda arr: (arr, (None, tr, tc), lambda i, j, pre: (layer, i, j))
    out = ((w.shape[0], r, cc), F32, (None, tr, tc), lambda i, j, pre: (layer, i, j))
    return stepk(name, fn, (r // tr, cc // tc),
                 [(part, (None, tr, tc), lambda i, j, pre: (pre[0], i, j)), other(0), other(1), other(2),
                  state(w), state(m), state(v)],
                 [out] * 4, carries=[((8, 128), F32)], prefetch=chip, fill=[(f, k) for k, f in enumerate(fill)])


def k_sum8(name, parts):
    _, rows, lanes = parts.shape

    def fn(pv):
        acc = pv[0]
        for d in range(1, N_DEV):
            acc = acc + pv[d]
        return (acc,)

    return stepk(name, fn, (1,), [(parts, parts.shape, lambda i: (0, 0, 0))],
                 [((rows, lanes), F32, (rows, lanes), lambda i: (0, 0))])[0]


def k_adamw_flat(name, g, w, m, v):
    whole = lambda arr: (arr, arr.shape, lambda i: (0, 0))
    out = (g.shape, F32, g.shape, lambda i: (0, 0))
    return stepk(name, _adamw, (1,), [whole(g), whole(w), whole(m), whole(v)], [out] * 4)


WEIGHTS = ("ffn1_norm_pre", "ffn1_w_gate_up", "ffn1_w_down", "ffn1_norm_post", "mix_norm_pre", "w_in", "conv_w", "a_log",
           "dt_bias", "sm_w", "sm_b", "sm_ln_g", "sm_ln_b", "dn_norm_w", "w_out", "mix_norm_post", "xa_norm_pre", "mem_norm",
           "w_xq", "w_xkv", "w_xo", "xa_norm_post", "ffn2_norm_pre", "ffn2_w_gate_up", "ffn2_w_down", "ffn2_norm_post")
BIG = ("ffn1_w_gate_up", "ffn1_w_down", "w_in", "w_out", "w_xq", "w_xkv", "w_xo", "ffn2_w_gate_up", "ffn2_w_down")
ROW_SHARDED = ("ffn1_w_down", "w_out", "w_xq", "w_xo", "ffn2_w_down")
SMALL = tuple(n for n in WEIGHTS if n not in BIG and n != "conv_w")
N_LAYERS = 2
IN_COLS = MAIN_W + 2 * DN_HEADS


BLOCKS = ("f1", "mx", "xa", "f2")
AG_BEFORE = {"f1": ("ffn1_w_gate_up", "ffn1_w_down"), "mx": ("w_in", "w_out"), "xa": ("w_xq", "w_xkv", "w_xo"),
             "f2": ("ffn2_w_gate_up", "ffn2_w_down")}
AG_LARGE = ("f1", "f2")
RS_IN_FLIGHT = 1
RS_LAST = ("ffn1_w_gate_up",)


def _fwd_block(l, sb, h, mem, p, sch):
    tag = f"l{l}{sb}"
    if sb == "f1":
        return ffn_fwd(tag, h, p, "ffn1_norm_pre", "ffn1_norm_post", "ffn1_w_gate_up", "ffn1_w_down", sch)
    if sb == "mx":
        return mix_fwd(tag, h, p, sch)
    if sb == "xa":
        return xa_fwd(tag, h, mem, p, sch)
    return ffn_fwd(tag, h, p, "ffn2_norm_pre", "ffn2_norm_post", "ffn2_w_gate_up", "ffn2_w_down", sch)


class WeightGather:
    def __init__(self, a, w_in_t, me1, chain):
        self.blocks = [(l, sb) for l in range(N_LAYERS) for sb in BLOCKS]
        self.names = {(l, sb): AG_BEFORE[sb] for l, sb in self.blocks}
        first, second = AG_BEFORE[BLOCKS[0]][:1], AG_BEFORE[BLOCKS[0]][1:]
        self.names[0, BLOCKS[0]] = first
        self.names[0, BLOCKS[0] + "b"] = second
        order = [(0, BLOCKS[0]), (0, BLOCKS[0] + "b")] + self.blocks[1:]
        self.going, self.passing, self.tie = {}, {}, []
        for l, sb in order:
            lands = [k_cast_place(f"place_l{l}{n}", *((w_in_t[""][l], 0) if n == "w_in" else (a[n], l)), me1, chain)
                     for n in self.names[l, sb]]
            self.going[l, sb] = xfer_start(f"ag_out_l{l}{sb}", plan_gather_out, 4 * len(lands), [], lands)
            chain = [self.going[l, sb]["token"]]
        self.tie = chain
        self.block = None

    def take(self):
        tie, self.tie = self.tie, []
        return tie

    def _pass_on(self, key, after):
        l, sb = key
        _, got = xfer_wait(f"ag_outw_l{l}{sb}", plan_gather_out, self.going.pop(key), after=after)
        self.passing[key] = xfer_start(f"ag_pass_l{l}{sb}", plan_gather_pass, 3 * len(got), [], got)
        self.tie = [self.passing[key]["token"]]

    def _arrived(self, key, after):
        l, sb = key
        _, full = xfer_wait(f"ag_passw_l{l}{sb}", plan_gather_pass, self.passing.pop(key), after=after)
        self.tie = [full[0]]
        return _big_params(dict(zip(self.names[key], full)))

    def _holding(self, name):
        return next(k for k in list(self.going) + list(self.passing)
                    if k[0] == self.block[0] and name in self.names[k])

    def mid(self, arr, late):
        later = self.blocks[self.blocks.index(self.block) + 1:]
        if later and later[0] in self.going and (later[0][1] in AG_LARGE) == late:
            self._pass_on(later[0], [arr])

    def prefetch(self, p, name, arr):
        if name not in p and self._holding(name) in self.going:
            self._pass_on(self._holding(name), [arr])

    def param(self, p, name, after):
        if name not in p:
            key = self._holding(name)
            if key in self.going:
                self._pass_on(key, list(after))
            p.update(self._arrived(key, list(after) + self.take()))
        return p[name]

    def weights(self, l, sb, after):
        self.block = (l, sb)
        after = list(after) + self.take()
        if (l, sb) in self.going:
            self._pass_on((l, sb), after)
        return self._arrived((l, sb), after)


def _bwd_block(l, sb, dy, saved, p, grads, sch):
    tag = f"l{l}{sb}"
    if sb == "f1":
        return ffn_bwd(tag, dy, saved, p, "ffn1_norm_pre", "ffn1_norm_post", "ffn1_w_gate_up", "ffn1_w_down", grads, sch)
    if sb == "mx":
        return mix_bwd(tag, dy, saved, p, grads, sch)
    if sb == "xa":
        return xa_bwd(tag, dy, saved, p, grads, sch)
    return ffn_bwd(tag, dy, saved, p, "ffn2_norm_pre", "ffn2_norm_post", "ffn2_w_gate_up", "ffn2_w_down", grads, sch)


class GradExchange:
    def __init__(self, a, w_in_t, core, chip):
        self.a, self.w_in_t, self.core, self.chip = a, w_in_t, core, chip
        self.layer = None
        self.pending, self.tie, self.out = [], [], {}
        self.w_in_out = [None] * N_LAYERS

    def take(self):
        tie, self.tie = self.tie, []
        return tie

    def _advance(self, chain, drain, waits=True):
        a = self.a
        between_chips = [grp for grp in self.pending if grp["stage"] == "chips"]
        for grp in self.pending:
            if grp["stage"] == "sibling":
                l, names = grp["l"], grp["names"]
                g4, r1 = xfer_wait(f"rs_sibw_l{l}{names[0]}", plan_scatter_sibling, grp["going"], after=chain)
                parts = [k_pair_add(f"rs_add_l{l}{n}", gg, rr, self.core) for n, gg, rr in zip(names, g4, r1)]
                zones = [lax.empty((3,) + p.shape[1:], BF16) for p in parts]
                grp["going"] = xfer_start(f"rs_chip_l{l}{names[0]}", plan_scatter_chips, 3 * len(names), parts, zones)
                grp["stage"], grp["age"] = "chips", 0
                chain = [grp["going"]["token"]]
        for grp in between_chips if waits else ():
            l, names = grp["l"], grp["names"]
            if drain or grp["age"] >= grp["limit"]:
                parts, r2 = xfer_wait(f"rs_chipw_l{l}{names[0]}", plan_scatter_chips, grp["going"], after=chain)
                for n, part, others in zip(names, parts, r2):
                    if n == "w_in":
                        state = [self.w_in_t[pre][l] for pre in ("", "m_", "v_")]
                        *self.w_in_out[l], done = k_adamw_shard(f"adamw_{n}_l{l}", part, others, *state, 0, self.chip, ())
                    else:
                        *self.out[n], done = k_adamw_shard(f"adamw_{n}_l{l}", part, others, a[n], a["m_" + n],
                                                           a["v_" + n], l, self.chip, self.out.get(n, ()))
                    chain = [done]
                self.pending.remove(grp)
            else:
                grp["age"] += 1
        return chain

    def emit(self, grads, names):
        l = self.layer
        g4 = [_scatter_layout(n, grads[n]) for n in names]
        g4 = [t.reshape(4, 2, *t.shape[1:]) for t in g4]
        zones = [lax.empty((4,) + t.shape[2:], BF16) for t in g4]
        going = xfer_start(f"rs_sib_l{l}{names[0]}", plan_scatter_sibling, 4 * len(names), g4, zones)
        chain = self._advance([going["token"]], drain=False)
        self.pending.append(dict(l=l, names=names, stage="sibling", going=going, age=0, limit=RS_IN_FLIGHT))
        if l == 0 and names[0] in RS_LAST:
            chain = self._advance(chain, drain=False, waits=False)
        self.tie = chain

    def finish(self, last):
        chain = [last]
        while self.pending:
            chain = self._advance(chain, drain=True)
        return self.out


def _big_params(full):
    p = {}
    for n, w in full.items():
        if n in ROW_SHARDED:
            p[n] = w.reshape(1, w.shape[0] * w.shape[1], w.shape[2])
        elif n == "w_in":
            p["w_in_t"] = w.reshape(1, w.shape[0] * w.shape[1], w.shape[2])
        else:
            p[n] = w
    return p


def _small_params(l, conv_full, a):
    p = {"conv_w": conv_full[l]}
    for n in SMALL:
        w = a[n][l]
        if n in ("a_log", "dt_bias"):
            p[n] = w.reshape(DN_HEADS, 1, 1)
        elif n == "sm_b":
            p[n] = w[..., None]
        elif n == "sm_w":
            p[n] = w
        else:
            p[n] = w[None]
    return p


def _scatter_layout(n, g):
    if n in ROW_SHARDED:
        return g.reshape(N_DEV, g.shape[1] // N_DEV, g.shape[2])
    return g


def _seg_rows(size):
    return -(-size // 1024) * 8


def _pack(flat_parts):
    return jnp.concatenate([jnp.pad(f, (0, _seg_rows(f.shape[0]) * 128 - f.shape[0])).reshape(-1, 128) for f in flat_parts])


def kernel(x, mem, ffn1_norm_pre, ffn1_w_gate_up, ffn1_w_down, ffn1_norm_post, mix_norm_pre, w_in, conv_w, a_log, dt_bias, sm_w, sm_b, sm_ln_g, sm_ln_b, dn_norm_w, w_out, mix_norm_post, xa_norm_pre, mem_norm, w_xq, w_xkv, w_xo, xa_norm_post, ffn2_norm_pre, ffn2_w_gate_up, ffn2_w_down, ffn2_norm_post, loss_target, m_ffn1_norm_pre, m_ffn1_w_gate_up, m_ffn1_w_down, m_ffn1_norm_post, m_mix_norm_pre, m_w_in, m_conv_w, m_a_log, m_dt_bias, m_sm_w, m_sm_b, m_sm_ln_g, m_sm_ln_b, m_dn_norm_w, m_w_out, m_mix_norm_post, m_xa_norm_pre, m_mem_norm, m_w_xq, m_w_xkv, m_w_xo, m_xa_norm_post, m_ffn2_norm_pre, m_ffn2_w_gate_up, m_ffn2_w_down, m_ffn2_norm_post, v_ffn1_norm_pre, v_ffn1_w_gate_up, v_ffn1_w_down, v_ffn1_norm_post, v_mix_norm_pre, v_w_in, v_conv_w, v_a_log, v_dt_bias, v_sm_w, v_sm_b, v_sm_ln_g, v_sm_ln_b, v_dn_norm_w, v_w_out, v_mix_norm_post, v_xa_norm_pre, v_mem_norm, v_w_xq, v_w_xkv, v_w_xo, v_xa_norm_post, v_ffn2_norm_pre, v_ffn2_w_gate_up, v_ffn2_w_down, v_ffn2_norm_post):
    a = dict(locals())
    px, py, pc = _place()
    core = jnp.reshape(pc, (1,)).astype(jnp.int32)
    chip = jnp.reshape(2 * px + py, (1,)).astype(jnp.int32)
    me = 4 * px + 2 * py + pc
    xs, mems, tgt = x[0], mem[0], loss_target[0]
    conv_all = all_gather("ag_conv", [conv_w])[0]
    conv_full = conv_all.transpose(1, 2, 0, 3).reshape(N_LAYERS, CONV_W, QKV_W)
    params = [_small_params(l, conv_full, a) for l in range(N_LAYERS)]
    chain = [conv_all]

    w_in_t = {pre: [jnp.transpose(a[pre + "w_in"], (2, 0, 1))[:, l][None] for l in range(N_LAYERS)]
              for pre in ("", "m_", "v_")}
    gather = WeightGather(a, w_in_t, jnp.reshape(me, (1,)).astype(jnp.int32), chain)
    saved = [{} for _ in range(N_LAYERS)]
    h = xs
    for l in range(N_LAYERS):
        for sb in BLOCKS:
            params[l].update(gather.weights(l, sb, [h]))
            h, saved[l][sb] = _fwd_block(l, sb, h, mems, params[l], gather)
    dy, loss = k_loss("loss", h, tgt)
    loss = lax.psum(loss[0, 0], ("x", "y", "c"))

    grads = [{} for _ in range(N_LAYERS)]
    exchange = GradExchange(a, w_in_t, core, chip)
    for l in reversed(range(N_LAYERS)):
        exchange.layer = l
        for sb in reversed(BLOCKS):
            dy = _bwd_block(l, sb, dy, saved[l][sb], params[l], grads[l], exchange)
    big_out = exchange.finish(dy)

    flat = [jnp.concatenate([grads[l][n].reshape(-1) for l in range(N_LAYERS)]) for n in SMALL]
    flat.append(jnp.concatenate([grads[l]["conv_w"].reshape(-1) for l in range(N_LAYERS)]))
    gsum = k_sum8("small_sum", all_gather("ag_small", [_pack(flat)], after=[big_out["ffn1_w_gate_up"][0]])[0])
    rep_rows = sum(_seg_rows(a[n].size) for n in SMALL)
    conv_g = gsum[rep_rows:].reshape(-1)[:N_LAYERS * CONV_W * QKV_W].reshape(N_LAYERS, CONV_W, QKV_W)
    conv_g = lax.dynamic_slice_in_dim(conv_g, me * (QKV_W // N_DEV), QKV_W // N_DEV, axis=2)
    pack_state = lambda pre: _pack([a[pre + n].reshape(-1) for n in SMALL] + [a[pre + "conv_w"].reshape(-1)])
    small_g = jnp.concatenate([gsum[:rep_rows], _pack([conv_g.reshape(-1)])])
    small_out = k_adamw_flat("small_adamw", small_g, pack_state(""), pack_state("m_"), pack_state("v_"))
    outs = {}
    row = 0
    for n in SMALL + ("conv_w",):
        rows = _seg_rows(a[n].size)
        outs[n] = [o[row:row + rows].reshape(-1)[:a[n].size].reshape(a[n].shape) for o in small_out]
        row += rows
    for n in BIG:
        if n == "w_in":
            outs[n] = [jnp.transpose(jnp.concatenate([exchange.w_in_out[l][k] for l in range(N_LAYERS)]), (1, 0, 2))
                       for k in range(4)]
            outs[n] = [jnp.transpose(o, (1, 2, 0)) for o in outs[n]]
        else:
            outs[n] = list(big_out[n])

    return (loss, dy[None], *[outs[n][0] for n in WEIGHTS], *[outs[n][1] for n in WEIGHTS],
            *[outs[n][2] for n in WEIGHTS], *[outs[n][3] for n in WEIGHTS])
```

```python
import functools
import math

import jax
import jax.numpy as jnp
from jax import lax
from jax.experimental import pallas as pl
from jax.experimental.pallas import tpu as pltpu

F32, BF16 = jnp.float32, jnp.bfloat16
HI = lax.Precision.HIGHEST
MESH = pl.DeviceIdType.MESH

N_DEV = 8
NORM_EPS = 1e-6
GM_HEADS, GM_CHUNK = 8, 128
DN_HEADS, DN_CHUNK, DN_DIM = 8, 64, 128
XA_HEADS, XA_DIM = 4, 512
CONV_W = 4
ADAM_LR, ADAM_B1, ADAM_B2, ADAM_EPS, ADAM_WD, ADAM_STEP = 0.001, 0.9, 0.999, 1e-08, 0.01, 10

V7X_VMEM_LIMIT = 56 * 1024 * 1024
WHOLE_TILE_MAX = 1536
WHOLE_K_MAX = 2048
ROW_TILE = 512


def _cparams(n_grid):
    return pltpu.CompilerParams(dimension_semantics=("arbitrary",) * n_grid, vmem_limit_bytes=V7X_VMEM_LIMIT)


def stepk(name, fn, grid, ins, outs, carries=(), prefetch=None, fill=(), after=()):
    n_in, n_out, n_c, n_fill = len(ins), len(outs), len(carries), len(fill)
    n_pre = 0 if prefetch is None else 1

    def body(*refs):
        refs = refs[n_pre:]
        in_refs, refs = refs[:n_in], refs[n_in + n_fill + len(after):]
        out_refs, c_refs = refs[:n_out], refs[n_out:]
        if n_c:
            first = functools.reduce(jnp.logical_and, [pl.program_id(a) == 0 for a in range(len(grid))])

            @pl.when(first)
            def _():
                for r in c_refs:
                    r[...] = jnp.zeros(r.shape, r.dtype)
        res = fn(*[r[...] for r in in_refs], *[r[...] for r in c_refs])
        for r, v in zip(tuple(out_refs) + tuple(c_refs), res):
            r[...] = v.astype(r.dtype)

    in_specs = [pl.BlockSpec(bs, im) for _, bs, im in ins] + [pl.BlockSpec(memory_space=pl.ANY)] * (n_fill + len(after))
    aliases = {n_pre + n_in + i: k for i, (_, k) in enumerate(fill)}
    out_specs = [pl.BlockSpec(bs, im) for _, _, bs, im in outs]
    out_shape = [jax.ShapeDtypeStruct(s, d) for s, d, _, _ in outs]
    for s, d in carries:
        zeros = (0,) * len(s)
        out_specs.append(pl.BlockSpec(s, lambda *a, _z=zeros: _z))
        out_shape.append(jax.ShapeDtypeStruct(s, d))
    args = [a for a, _, _ in ins] + [a for a, _ in fill] + list(after)
    if prefetch is None:
        call = pl.pallas_call(body, name=name, grid=grid, in_specs=in_specs, out_specs=out_specs, out_shape=out_shape,
                              input_output_aliases=aliases, compiler_params=_cparams(len(grid)))
        return call(*args)
    spec = pltpu.PrefetchScalarGridSpec(num_scalar_prefetch=1, grid=grid, in_specs=in_specs, out_specs=out_specs)
    call = pl.pallas_call(body, name=name, grid_spec=spec, out_shape=out_shape, input_output_aliases=aliases,
                          compiler_params=_cparams(len(grid)))
    return call(prefetch, *args)


def _tile(n, pref, align=128):
    if n <= WHOLE_TILE_MAX:
        return n
    t = (pref // align) * align
    while t > align and n % t:
        t -= align
    assert n % t == 0, (n, pref)
    return t


def mm(name, a, b, mode, out_dtype, nbo=1, after=(), b_rows=None, out_rows=None, fill=None):
    nba, ra, ca = a.shape
    nbb, rb, cb = b.shape
    b_row0 = 0
    if b_rows is not None:
        assert nbb == 1 and mode != "tn"
        b_row0, rb = b_rows
    if mode == "nn":
        m, k, n = ra, nba * ca, nbb * cb
        assert rb == k
    elif mode == "nt":
        m, k, n = ra, nba * ca, rb
        assert nbb * cb == k
    else:
        k, m, n = ra, nba * ca, nbb * cb
        assert rb == k
    co = n // nbo
    whole_k = k <= WHOLE_K_MAX and (mode == "tn" or (nba == 1 and (mode == "nn" or nbb == 1)))
    k_pref = k if whole_k else 512
    if mode == "nn":
        tm, tk, tn = _tile(m, 1024, 8), _tile(ca, k_pref), _tile(math.gcd(cb, co), 1024)
        a_spec = pl.BlockSpec((None, tm, tk), lambda i, j, kk: (kk // (ca // tk), i, kk % (ca // tk)))
        assert b_row0 % tk == 0
        b_spec = pl.BlockSpec((None, tk, tn), lambda i, j, kk: (j // (cb // tn), kk + b_row0 // tk, j % (cb // tn)))
        dims = (((1,), (0,)), ((), ()))
    elif mode == "nt":
        tm, tk, tn = _tile(m, 1024, 8), _tile(math.gcd(ca, cb), k_pref), _tile(co, 1024)
        assert b_row0 % tn == 0
        a_spec = pl.BlockSpec((None, tm, tk), lambda i, j, kk: (kk // (ca // tk), i, kk % (ca // tk)))
        b_spec = pl.BlockSpec((None, tn, tk), lambda i, j, kk: (kk // (cb // tk), j + b_row0 // tn, kk % (cb // tk)))
        dims = (((1,), (1,)), ((), ()))
    else:
        tm, tk, tn = _tile(ca, 1024), _tile(k, k_pref), _tile(math.gcd(cb, co), 1024)
        a_spec = pl.BlockSpec((None, tk, tm), lambda i, j, kk: (i // (ca // tm), kk, i % (ca // tm)))
        b_spec = pl.BlockSpec((None, tk, tn), lambda i, j, kk: (j // (cb // tn), kk, j % (cb // tn)))
        dims = (((0,), (0,)), ((), ()))
    o_row0, o_rows = out_rows if out_rows is not None else (0, m)
    assert o_row0 % tm == 0
    o_spec = pl.BlockSpec((None, tm, tn), lambda i, j, kk: (j // (co // tn), i + o_row0 // tm, j % (co // tn)))
    nk = k // tk
    extra = list(after) + ([] if fill is None else [fill])

    def tile_product(a_ref, b_ref):
        return lax.dot_general(a_ref[...].astype(BF16), b_ref[...].astype(BF16), dims, preferred_element_type=F32)

    def body_one_step(a_ref, b_ref, *rest):
        o_ref = rest[-1]
        o_ref[...] = tile_product(a_ref, b_ref).astype(o_ref.dtype)

    def body(a_ref, b_ref, *rest):
        o_ref, acc_ref = rest[-2:]
        kk = pl.program_id(2)

        @pl.when(kk == 0)
        def _():
            acc_ref[...] = jnp.zeros(acc_ref.shape, F32)

        acc_ref[...] += tile_product(a_ref, b_ref)

        @pl.when(kk == nk - 1)
        def _():
            o_ref[...] = acc_ref[...].astype(o_ref.dtype)

    return pl.pallas_call(
        body_one_step if nk == 1 else body, name=name, grid=(m // tm, n // tn, nk),
        in_specs=[a_spec, b_spec] + [pl.BlockSpec(memory_space=pl.ANY)] * len(extra), out_specs=o_spec,
        out_shape=jax.ShapeDtypeStruct((nbo, o_rows, co), out_dtype),
        input_output_aliases={} if fill is None else {1 + len(extra): 0},
        scratch_shapes=[] if nk == 1 else [pltpu.VMEM((tm, tn), F32)], compiler_params=_cparams(3))(a, b, *extra)


def _rms(x, g):
    return x * lax.rsqrt(jnp.mean(x * x, axis=-1, keepdims=True) + NORM_EPS) * g


def _sigmoid(x):
    return 1.0 / (1.0 + jnp.exp(-x))


def _silu(x):
    return x * _sigmoid(x)


def _gelu(x):
    return 0.5 * x * (1.0 + lax.erf(x * 0.7071067811865476))


def _softplus(x):
    return jnp.maximum(x, 0.0) + jnp.log(1.0 + jnp.exp(-jnp.abs(x)))


def _split(x, n):
    w = x.shape[-1] // n
    return [x[..., h * w:(h + 1) * w] for h in range(n)]


def _sgu(ua, va, smw, smb, lng, lnb):
    c = ua[0].shape[0]
    width = len(va) * va[0].shape[-1]
    vf = [_gelu(v) for v in va]
    mu = sum(jnp.sum(v, axis=-1, keepdims=True) for v in vf) / width
    var = sum(jnp.sum(jnp.square(v - mu), axis=-1, keepdims=True) for v in vf) / width
    r = lax.rsqrt(var + NORM_EPS)
    causal = lax.broadcasted_iota(jnp.int32, (c, c), 0) >= lax.broadcasted_iota(jnp.int32, (c, c), 1)
    outs = []
    for h in range(len(ua)):
        vn = (vf[h] - mu) * r * lng[h] + lnb[h]
        w = jnp.where(causal, smw[h], 0.0)
        mixed = jnp.dot(w.astype(BF16), vn.astype(BF16), preferred_element_type=F32) + smb[h]
        outs.append(_gelu(ua[h]) * mixed)
    return outs


def _shift_rows(x, k, up):
    rows = x.shape[0]
    row = lax.broadcasted_iota(jnp.int32, x.shape, 0)
    if up:
        return jnp.where(row < rows - k, pltpu.roll(x, rows - k, 0), 0.0)
    return jnp.where(row >= k, pltpu.roll(x, k, 0), 0.0)


@functools.partial(jax.custom_vjp, nondiff_argnums=(1,))
def _delay(x, k):
    return _shift_rows(x, k, False)


def _delay_fwd(x, k):
    return _shift_rows(x, k, False), None


def _delay_bwd(k, _, g):
    return (_shift_rows(g, k, True),)


_delay.defvjp(_delay_fwd, _delay_bwd)


def _conv_silu(x, w0, w1, w2, w3):
    y = w3 * x + w2 * _delay(x, 1) + w1 * _delay(x, 2) + w0 * _delay(x, 3)
    return _silu(y)


_BDOT_DIMS = {"nn": (((2,), (1,)), ((0,), (0,))), "nt": (((2,), (2,)), ((0,), (0,))), "tn": (((1,), (1,)), ((0,), (0,)))}


def _bdot_passes(a, b, kind, passes):
    one = lambda x, y: lax.dot_general(x, y, _BDOT_DIMS[kind], preferred_element_type=F32)
    ah, bh = a.astype(BF16), b.astype(BF16)
    if passes == 1:
        return one(ah, bh)
    al, bl = (a - ah.astype(F32)).astype(BF16), (b - bh.astype(F32)).astype(BF16)
    return one(ah, bh) + (one(ah, bl) + one(al, bh))


@functools.partial(jax.custom_vjp, nondiff_argnums=(2, 3))
def bdot(a, b, kind, passes):
    return _bdot_passes(a, b, kind, passes)


def _bdot_fwd(a, b, kind, passes):
    return _bdot_passes(a, b, kind, passes), (a, b)


def _bdot_bwd(kind, passes, res, g):
    a, b = res
    if kind == "nn":
        return bdot(g, b, "nt", passes), bdot(a, g, "tn", passes)
    if kind == "nt":
        return bdot(g, b, "nn", passes), bdot(g, a, "tn", passes)
    return bdot(b, g, "nt", passes), bdot(a, g, "nn", passes)


bdot.defvjp(_bdot_fwd, _bdot_bwd)


def _tri_inv(a, eye):
    c = a.shape[-1]
    inv = eye - a
    p = bdot(a, a, "nn", 3)
    n_fac = int(math.log2(c)) - 1
    for it in range(n_fac):
        inv = inv + bdot(inv, p, "nn", 3)
        if it < n_fac - 1:
            p = bdot(p, p, "nn", 3)
    return inv


@jax.custom_vjp
def _known_inv(a, x):
    return x


def _known_inv_fwd(a, x):
    return x, x


def _known_inv_bwd(x, g):
    return -bdot(bdot(x, g, "tn", 3), x, "nt", 3), jnp.zeros_like(x)


_known_inv.defvjp(_known_inv_fwd, _known_inv_bwd)


def _dn_step(s, q, k, v, braw, araw, alog, dtb, inv_known=None):
    nh, c, d = q.shape
    row = lax.broadcasted_iota(jnp.int32, (c, c), 0)
    col = lax.broadcasted_iota(jnp.int32, (c, c), 1)
    causal, strict = (row >= col)[None], (row > col)[None]
    lower = (row >= col).astype(F32)
    strict_f = jnp.broadcast_to((row > col).astype(F32)[None], (nh, c, c))
    eye = (row == col).astype(F32)[None]

    qn = q * lax.rsqrt(jnp.sum(q * q, axis=-1, keepdims=True) + NORM_EPS) * (d ** -0.5)
    kn = k * lax.rsqrt(jnp.sum(k * k, axis=-1, keepdims=True) + NORM_EPS)
    beta = _sigmoid(braw)
    g = -jnp.exp(alog) * _softplus(araw + dtb)
    lg = lower[None] * g
    gcum = jnp.sum(lg, axis=-1, keepdims=True)
    diff = bdot(lg, strict_f, "nn", 3)
    decay = jnp.where(causal, jnp.exp(diff), 0.0)
    bcol = jnp.sum(eye * beta, axis=-1, keepdims=True)
    kb = kn * bcol
    a = jnp.where(strict, bdot(kb, kn, "nt", 1) * decay, 0.0)
    inv = _tri_inv(a, eye) if inv_known is None else _known_inv(a, inv_known)
    eg = jnp.exp(gcum)
    u = bdot(inv, v * bcol, "nn", 1)
    w = bdot(inv, kb * eg, "nn", 1)
    qk = jnp.where(causal, bdot(qn, kn, "nt", 1) * decay, 0.0)
    v_new = u - bdot(w, s, "nn", 1)
    o = bdot(qn * eg, s, "nn", 1) + bdot(qk, v_new, "nn", 1)
    glast = jnp.sum(g, axis=-1, keepdims=True)
    kdec = kn * jnp.exp(glast - gcum)
    s_new = s * jnp.exp(glast) + bdot(kdec, v_new, "tn", 1)
    return s_new, o, inv


def _ogate(o, z, w):
    return [_rms(oh, w) * _silu(zh) for oh, zh in zip(o, z)]


def _xattn(q, k, v):
    outs = []
    for qh, kh, vh in zip(q, k, v):
        s = lax.dot_general(qh.astype(BF16), kh.astype(BF16), (((1,), (1,)), ((), ())),
                            preferred_element_type=F32) * (qh.shape[-1] ** -0.5)
        s = s - jnp.max(s, axis=-1, keepdims=True)
        e = jnp.exp(s)
        p = e / jnp.sum(e, axis=-1, keepdims=True)
        outs.append(jnp.dot(p.astype(BF16), vh.astype(BF16), preferred_element_type=F32))
    return outs


def _rows(t):
    return min(ROW_TILE, t)


def k_rms_fwd(name, x, g, after=()):
    t, d = x.shape
    tm = _rows(t)
    return stepk(name, lambda xv, gv: (_rms(xv, gv),), (t // tm,),
                 [(x, (tm, d), lambda i: (i, 0)), (g, (1, d), lambda i: (0, 0))],
                 [((t, d), BF16, (tm, d), lambda i: (i, 0))], after=after)[0]


def k_rms_bwd(name, x, g, dhs, dx_res):
    t, d = x.shape
    tm = _rows(t)
    n = len(dhs)

    def fn(xv, gv, *rest):
        dh = sum(r.astype(F32) for r in rest[:n])
        dxr, dg_c = rest[n], rest[n + 1]
        _, vjp = jax.vjp(_rms, xv, gv)
        dx, dg = vjp(dh)
        return dx + dxr, dg_c + dg

    row = lambda arr: (arr, (tm, d), lambda i: (i, 0))
    return stepk(name, fn, (t // tm,), [row(x), (g, (1, d), lambda i: (0, 0))] + [row(h) for h in dhs] + [row(dx_res)],
                 [((t, d), F32, (tm, d), lambda i: (i, 0))], carries=[((1, d), F32)])


def k_post_fwd(name, x, f, g, scale, after=()):
    t, d = x.shape
    tm = _rows(t)
    row = lambda arr: (arr, (tm, d), lambda i: (i, 0))
    return stepk(name, lambda xv, fv, gv: (xv + scale * _rms(fv, gv),), (t // tm,),
                 [row(x), row(f), (g, (1, d), lambda i: (0, 0))], [((t, d), F32, (tm, d), lambda i: (i, 0))],
                 after=after)[0]


def k_post_bwd(name, f, g, dxo, scale, after=()):
    t, d = f.shape
    tm = _rows(t)

    def fn(fv, gv, dv, dg_c):
        _, vjp = jax.vjp(lambda a, b: scale * _rms(a, b), fv, gv)
        df, dg = vjp(dv)
        return df, dg_c + dg

    row = lambda arr: (arr, (tm, d), lambda i: (i, 0))
    return stepk(name, fn, (t // tm,), [row(f), (g, (1, d), lambda i: (0, 0)), row(dxo)],
                 [((t, d), BF16, (tm, d), lambda i: (i, 0))], carries=[((1, d), F32)], after=after)


def k_swiglu_fwd(name, hu, after=()):
    nb, t, c = hu.shape
    half = nb // 2
    tm = _rows(t)
    fn = lambda gv, uv: (_silu(gv.astype(F32)) * uv.astype(F32),)
    return stepk(name, fn, (half, t // tm),
                 [(hu, (None, tm, c), lambda b, i: (b, i, 0)), (hu, (None, tm, c), lambda b, i: (b + half, i, 0))],
                 [((half, t, c), BF16, (None, tm, c), lambda b, i: (b, i, 0))], after=after)[0]


def k_swiglu_bwd(name, hu, da):
    nb, t, c = hu.shape
    half = nb // 2
    tm = _rows(t)

    def fn(gv, uv, dv):
        gv, uv, dv = gv.astype(F32), uv.astype(F32), dv.astype(F32)
        sg = _sigmoid(gv)
        return (jnp.stack([dv * uv * sg * (1.0 + gv * (1.0 - sg)), dv * gv * sg]),)

    blk = lambda arr, off: (arr, (None, tm, c), lambda b, i: (b + off, i, 0))
    out = stepk(name, fn, (half, t // tm), [blk(hu, 0), blk(hu, half), blk(da, 0)],
                [((2, half, t, c), BF16, (2, None, tm, c), lambda b, i: (0, b, i, 0))])[0]
    return out.reshape(nb, t, c)


def k_loss(name, y, tgt):
    t, d = y.shape
    tm = _rows(t)

    def fn(yv, tv, acc):
        e = yv - tv
        part = jnp.sum(jnp.sum(e * e, axis=-1, keepdims=True), axis=0, keepdims=True)
        return e * (1.0 / d), acc + (0.5 / d) * part

    row = lambda arr: (arr, (tm, d), lambda i: (i, 0))
    return stepk(name, fn, (t // tm,), [row(y), row(tgt)], [((t, d), F32, (tm, d), lambda i: (i, 0))],
                 carries=[((1, 1), F32)])


GM_W = GM_HEADS * 128
QKV_COL0 = 2 * GM_W
QKV_W = 3 * DN_HEADS * DN_DIM
Z_COL0 = QKV_COL0 + QKV_W
MAIN_W = Z_COL0 + DN_HEADS * DN_DIM


def _sgu_ins(proj, p):
    c = GM_CHUNK
    return [(proj, (c, GM_W), lambda i: (i, 0)), (proj, (c, GM_W), lambda i: (i, 1)),
            (p["sm_w"], (GM_HEADS, c, c), lambda i: (0, 0, 0)), (p["sm_b"], (GM_HEADS, c, 1), lambda i: (0, 0, 0)),
            (p["sm_ln_g"], (1, GM_W), lambda i: (0, 0)), (p["sm_ln_b"], (1, GM_W), lambda i: (0, 0))]


def _sgu_lists(uv, vv, sw, sb, lg, lb):
    nh = GM_HEADS
    return (_split(uv, nh), _split(vv, nh), [sw[h] for h in range(nh)], [sb[h] for h in range(nh)],
            _split(lg, nh), _split(lb, nh))


def k_sgu_fwd(name, proj, p):
    t = proj.shape[0]

    def fn(*vals):
        return (jnp.concatenate(_sgu(*_sgu_lists(*vals)), axis=-1),)

    return stepk(name, fn, (t // GM_CHUNK,), _sgu_ins(proj, p),
                 [((2, t, GM_W), BF16, (None, GM_CHUNK, GM_W), lambda i: (0, i, 0))])[0]


def k_sgu_bwd(name, proj, p, dy, dproj):
    t = proj.shape[0]
    c = GM_CHUNK

    def fn(uv, vv, sw, sb, lg, lb, dv, dsw, dsb, dlg, dlb):
        _, vjp = jax.vjp(_sgu, *_sgu_lists(uv, vv, sw, sb, lg, lb))
        gu, gv, gsw, gsb, glg, glb = vjp(_split(dv.astype(F32), GM_HEADS))
        cat = lambda l: jnp.concatenate(l, axis=-1)
        return (cat(gu + gv), dsw + jnp.stack(gsw), dsb + jnp.stack(gsb), dlg + cat(glg), dlb + cat(glb))

    return stepk(name, fn, (t // c,), _sgu_ins(proj, p) + [(dy, (None, c, GM_W), lambda i: (0, i, 0))],
                 [((t, MAIN_W), BF16, (c, 2 * GM_W), lambda i: (i, 0))],
                 carries=[((GM_HEADS, c, c), F32), ((GM_HEADS, c, 1), F32), ((1, GM_W), F32), ((1, GM_W), F32)],
                 fill=[(dproj, 0)])


def _conv_ins(proj, conv_w):
    t = proj.shape[0]
    return [(proj, (t, 128), lambda j: (0, QKV_COL0 // 128 + j)), (conv_w, (CONV_W, 128), lambda j: (0, j))]


def k_conv_fwd(name, proj, conv_w):
    t = proj.shape[0]
    n = QKV_W // 128
    fn = lambda xv, wv: (_conv_silu(xv, *[wv[i:i + 1] for i in range(CONV_W)]),)
    return stepk(name, fn, (n,), _conv_ins(proj, conv_w), [((n, t, 128), F32, (None, t, 128), lambda j: (j, 0, 0))])[0]


def k_conv_bwd(name, proj, conv_w, dq, dk, dv, dproj):
    t = proj.shape[0]
    n = QKV_W // 128
    nh = dq.shape[0]

    def fn(xv, wv, dqv, dkv, dvv):
        part = pl.program_id(0) // nh
        d = jnp.where(part == 0, dqv, jnp.where(part == 1, dkv, dvv))
        _, vjp = jax.vjp(_conv_silu, xv, *[wv[i:i + 1] for i in range(CONV_W)])
        gx, *gw = vjp(d)
        return gx, jnp.concatenate(gw, axis=0)

    head = lambda arr, k: (arr, (None, t, 128), lambda j: (jnp.clip(j - k * nh, 0, nh - 1), 0, 0))
    return stepk(name, fn, (n,), _conv_ins(proj, conv_w) + [head(dq, 0), head(dk, 1), head(dv, 2)],
                 [((t, MAIN_W), BF16, (t, 128), lambda j: (0, QKV_COL0 // 128 + j)),
                  ((CONV_W, QKV_W), F32, (CONV_W, 128), lambda j: (0, j))], fill=[(dproj, 0)])


def _dn_ins(qkv, braw, araw, p, order):
    h, c, d = DN_HEADS, DN_CHUNK, DN_DIM
    qkv_in = lambda part: (qkv, (h, c, d), lambda n: (part, order(n), 0))
    gate_in = lambda arr: (arr, (None, h, 1, c), lambda n: (order(n), 0, 0, 0))
    par_in = lambda arr: (arr, (h, 1, 1), lambda n: (0, 0, 0))
    return [qkv_in(0), qkv_in(1), qkv_in(2), gate_in(braw), gate_in(araw), par_in(p["a_log"]), par_in(p["dt_bias"])]


def k_dn_fwd(name, qkv, braw, araw, p):
    t = qkv.shape[1]
    h, c, d = DN_HEADS, DN_CHUNK, DN_DIM
    nc = t // c

    def fn(q, k, v, b, a, al, dt, s):
        s_new, o, inv = _dn_step(s, q, k, v, b, a, al, dt)
        return o, s, inv, s_new

    o, s_all, inv_all, _ = stepk(name, fn, (nc,), _dn_ins(qkv, braw, araw, p, lambda n: n),
                                 [((h, t, d), F32, (h, c, d), lambda n: (0, n, 0)),
                                  ((nc, h, d, d), F32, (None, h, d, d), lambda n: (n, 0, 0, 0)),
                                  ((nc, h, c, c), F32, (None, h, c, c), lambda n: (n, 0, 0, 0))],
                                 carries=[((h, d, d), F32)])
    return o, s_all, inv_all


def k_dn_bwd(name, qkv, braw, araw, p, s_all, inv_all, do):
    t = qkv.shape[1]
    h, c, d = DN_HEADS, DN_CHUNK, DN_DIM
    nc = t // c
    rev = lambda n: nc - 1 - n

    def fn(q, k, v, b, a, al, dt, s, inv, dov, ds_c, dal_c, ddt_c):
        _, vjp = jax.vjp(lambda *args: _dn_step(*args, inv_known=inv)[:2], s, q, k, v, b, a, al, dt)
        ds, dq, dk, dv, db, da, dal, ddt = vjp((ds_c, dov))
        return dq, dk, dv, db, da, ds, dal_c + dal, ddt_c + ddt

    ins = _dn_ins(qkv, braw, araw, p, rev) + [(s_all, (None, h, d, d), lambda n: (rev(n), 0, 0, 0)),
                                               (inv_all, (None, h, c, c), lambda n: (rev(n), 0, 0, 0)),
                                               (do, (h, c, d), lambda n: (0, rev(n), 0))]
    hd = ((h, t, d), F32, (h, c, d), lambda n: (0, rev(n), 0))
    gate = ((nc, h, 1, c), F32, (None, h, 1, c), lambda n: (rev(n), 0, 0, 0))
    dq, dk, dv, db, da, _, dal, ddt = stepk(name, fn, (nc,), ins, [hd, hd, hd, gate, gate],
                                            carries=[((h, d, d), F32), ((h, 1, 1), F32), ((h, 1, 1), F32)])
    return dq, dk, dv, db, da, dal, ddt


def _ogate_ins(o, proj, p):
    t = o.shape[1]
    tm = _rows(t)
    return tm, [(o, (DN_HEADS, tm, DN_DIM), lambda i: (0, i, 0)), (proj, (tm, GM_W), lambda i: (i, Z_COL0 // GM_W)),
                (p["dn_norm_w"], (1, DN_DIM), lambda i: (0, 0))]


def k_ogate_fwd(name, o, proj, p, y):
    t = o.shape[1]
    tm, ins = _ogate_ins(o, proj, p)

    def fn(ov, zv, wv):
        return (jnp.concatenate(_ogate([ov[h] for h in range(DN_HEADS)], _split(zv, DN_HEADS), wv), axis=-1),)

    return stepk(name, fn, (t // tm,), ins, [((2, t, GM_W), BF16, (None, tm, GM_W), lambda i: (1, i, 0))],
                 fill=[(y, 0)])[0]


def k_ogate_bwd(name, o, proj, p, dy):
    t = o.shape[1]
    tm, ins = _ogate_ins(o, proj, p)

    def fn(ov, zv, wv, dv, dw_c):
        _, vjp = jax.vjp(_ogate, [ov[h] for h in range(DN_HEADS)], _split(zv, DN_HEADS), wv)
        go, gz, gw = vjp(_split(dv.astype(F32), DN_HEADS))
        return jnp.stack(go), jnp.concatenate(gz, axis=-1), dw_c + gw

    return stepk(name, fn, (t // tm,), ins + [(dy, (None, tm, GM_W), lambda i: (1, i, 0))],
                 [((DN_HEADS, t, DN_DIM), F32, (DN_HEADS, tm, DN_DIM), lambda i: (0, i, 0)),
                  ((t, MAIN_W), BF16, (tm, GM_W), lambda i: (i, Z_COL0 // GM_W))], carries=[((1, DN_DIM), F32)])


def _xattn_lists(qv, kvv):
    nh = XA_HEADS
    return (_split(qv.astype(F32), nh), [kvv[h].astype(F32) for h in range(nh)],
            [kvv[nh + h].astype(F32) for h in range(nh)])


def k_xattn_fwd(name, q, kv):
    t, d = q.shape
    tm = _rows(t)
    fn = lambda qv, kvv: (jnp.concatenate(_xattn(*_xattn_lists(qv, kvv)), axis=-1),)
    return stepk(name, fn, (t // tm,), [(q, (tm, d), lambda i: (i, 0)), (kv, kv.shape, lambda i: (0, 0, 0))],
                 [((t, d), BF16, (tm, d), lambda i: (i, 0))])[0]


def k_xattn_bwd(name, q, kv, do):
    t, d = q.shape
    tm = _rows(t)

    def fn(qv, kvv, dv, dkv_c):
        _, vjp = jax.vjp(_xattn, *_xattn_lists(qv, kvv))
        gq, gk, gv = vjp(_split(dv.astype(F32), XA_HEADS))
        return jnp.concatenate(gq, axis=-1), dkv_c + jnp.stack(gk + gv)

    return stepk(name, fn, (t // tm,),
                 [(q, (tm, d), lambda i: (i, 0)), (kv, kv.shape, lambda i: (0, 0, 0)), (do, (None, tm, d), lambda i: (0, i, 0))],
                 [((t, d), BF16, (tm, d), lambda i: (i, 0))], carries=[(kv.shape, F32)])


def ffn_fwd(tag, x, p, pre, post, gu, dn, sch):
    h = k_rms_fwd(f"{tag}_pre", x, p[pre], sch.take())
    hu = mm(f"{tag}_gu", h[None], p[gu], "nn", BF16, nbo=N_DEV)
    sch.prefetch(p, dn, hu)
    a = k_swiglu_fwd(f"{tag}_act", hu, sch.take())
    w_down = sch.param(p, dn, [a])
    sch.mid(a, late=False)
    f = mm(f"{tag}_down", a, w_down, "nn", F32, after=sch.take())[0]
    sch.mid(f, late=True)
    y = k_post_fwd(f"{tag}_post", x, f, p[post], 0.5, sch.take())
    return y, (x, h, hu, a, f)


class NoExchange:
    def take(self):
        return ()

    def emit(self, grads, names):
        pass

    def mid(self, arr, late):
        pass

    def prefetch(self, p, name, arr):
        pass

    def param(self, p, name, after):
        return p[name]


def ffn_bwd(tag, dy, saved, p, pre, post, gu, dn, grads, sch):
    x, h, hu, a, f = saved
    df, grads[post] = k_post_bwd(f"{tag}_post_b", f, p[post], dy, 0.5, sch.take())
    grads[dn] = mm(f"{tag}_down_dw", a, df[None], "tn", BF16)
    sch.emit(grads, (dn,))
    da = mm(f"{tag}_down_dx", df[None], p[dn], "nt", BF16, nbo=N_DEV // 2, after=sch.take())
    dhu = k_swiglu_bwd(f"{tag}_act_b", hu, da)
    grads[gu] = mm(f"{tag}_gu_dw", h[None], dhu, "tn", BF16, nbo=N_DEV)
    sch.emit(grads, (gu,))
    dh = mm(f"{tag}_gu_dx", dhu, p[gu], "nt", BF16, after=sch.take())[0]
    dx, grads[pre] = k_rms_bwd(f"{tag}_pre_b", x, p[pre], [dh], dy)
    return dx


def _to_chunks(a):
    t, h = a.shape
    return a.reshape(t // DN_CHUNK, DN_CHUNK, h).transpose(0, 2, 1).reshape(t // DN_CHUNK, h, 1, DN_CHUNK)


def _from_chunks(a):
    nc, h, _, c = a.shape
    return a.reshape(nc, h, c).transpose(0, 2, 1).reshape(nc * c, h)


def mix_fwd(tag, x, p, sch):
    h = k_rms_fwd(f"{tag}_pre", x, p["mix_norm_pre"], sch.take())
    proj = mm(f"{tag}_in", h[None], p["w_in_t"], "nt", F32, b_rows=(0, MAIN_W))[0]
    sch.mid(proj, late=False)
    small = mm(f"{tag}_in_s", h[None], p["w_in_t"], "nt", F32, after=sch.take(), b_rows=(MAIN_W, 2 * DN_HEADS))[0]
    braw, araw = _to_chunks(small[:, :DN_HEADS]), _to_chunks(small[:, DN_HEADS:2 * DN_HEADS])
    y = k_sgu_fwd(f"{tag}_sgu", proj, p)
    qkv = k_conv_fwd(f"{tag}_conv", proj, p["conv_w"])
    o, s_all, inv_all = k_dn_fwd(f"{tag}_dn", qkv, braw, araw, p)
    y = k_ogate_fwd(f"{tag}_og", o, proj, p, y)
    m = mm(f"{tag}_out", y, p["w_out"], "nn", F32)[0]
    out = k_post_fwd(f"{tag}_post", x, m, p["mix_norm_post"], 1.0)
    return out, (x, h, proj, braw, araw, qkv, o, s_all, inv_all, y, m)


def mix_bwd(tag, dy, saved, p, grads, sch):
    x, h, proj, braw, araw, qkv, o, s_all, inv_all, y, m = saved
    dm, grads["mix_norm_post"] = k_post_bwd(f"{tag}_post_b", m, p["mix_norm_post"], dy, 1.0, sch.take())
    grads["w_out"] = mm(f"{tag}_out_dw", y, dm[None], "tn", BF16)
    sch.emit(grads, ("w_out",))
    dyy = mm(f"{tag}_out_dx", dm[None], p["w_out"], "nt", BF16, nbo=2, after=sch.take())
    do, dproj, grads["dn_norm_w"] = k_ogate_bwd(f"{tag}_og_b", o, proj, p, dyy)
    dq, dk, dv, db, da, grads["a_log"], grads["dt_bias"] = k_dn_bwd(f"{tag}_dn_b", qkv, braw, araw, p, s_all, inv_all, do)
    dproj, grads["conv_w"] = k_conv_bwd(f"{tag}_conv_b", proj, p["conv_w"], dq, dk, dv, dproj)
    dproj, grads["sm_w"], grads["sm_b"], grads["sm_ln_g"], grads["sm_ln_b"] = k_sgu_bwd(f"{tag}_sgu_b", proj, p, dyy, dproj)
    dsmall = jnp.concatenate([_from_chunks(db), _from_chunks(da)], axis=-1)
    g_in = mm(f"{tag}_in_dw", dproj[None], h[None], "tn", BF16, out_rows=(0, IN_COLS))
    g_in = mm(f"{tag}_in_s_dw", dsmall[None], h[None], "tn", BF16, out_rows=(MAIN_W, IN_COLS), fill=g_in)[0]
    grads["w_in"] = g_in.reshape(N_DEV, g_in.shape[0] // N_DEV, g_in.shape[1])
    sch.emit(grads, ("w_in",))
    dh = mm(f"{tag}_in_dx", dproj[None], p["w_in_t"], "nn", BF16, after=sch.take(), b_rows=(0, MAIN_W))[0]
    dh_s = mm(f"{tag}_in_s_dx", dsmall[None], p["w_in_t"], "nn", BF16, b_rows=(MAIN_W, 2 * DN_HEADS))[0]
    dx, grads["mix_norm_pre"] = k_rms_bwd(f"{tag}_pre_b", x, p["mix_norm_pre"], [dh, dh_s], dy)
    return dx


def xa_fwd(tag, x, mem, p, sch):
    after = sch.take()
    hx = k_rms_fwd(f"{tag}_pre", x, p["xa_norm_pre"], after)
    mh = k_rms_fwd(f"{tag}_mem", mem, p["mem_norm"], after)
    q = mm(f"{tag}_q", hx[None], p["w_xq"], "nn", BF16)[0]
    sch.mid(q, late=False)
    kv = mm(f"{tag}_kv", mh[None], p["w_xkv"], "nn", BF16, nbo=N_DEV, after=sch.take())
    o = k_xattn_fwd(f"{tag}_att", q, kv)
    sch.mid(o, late=True)
    c = mm(f"{tag}_o", o[None], p["w_xo"], "nn", F32, after=sch.take())[0]
    out = k_post_fwd(f"{tag}_post", x, c, p["xa_norm_post"], 1.0)
    return out, (x, mem, hx, mh, q, kv, o, c)


def xa_bwd(tag, dy, saved, p, grads, sch):
    x, mem, hx, mh, q, kv, o, c = saved
    dc, grads["xa_norm_post"] = k_post_bwd(f"{tag}_post_b", c, p["xa_norm_post"], dy, 1.0, sch.take())
    grads["w_xo"] = mm(f"{tag}_o_dw", o[None], dc[None], "tn", BF16)
    sch.emit(grads, ("w_xo",))
    do = mm(f"{tag}_o_dx", dc[None], p["w_xo"], "nt", BF16, after=sch.take())
    dq, dkv = k_xattn_bwd(f"{tag}_att_b", q, kv, do)
    dkv16 = dkv.astype(BF16)
    grads["w_xq"] = mm(f"{tag}_q_dw", hx[None], dq[None], "tn", BF16)
    grads["w_xkv"] = mm(f"{tag}_kv_dw", mh[None], dkv16, "tn", BF16, nbo=N_DEV)
    sch.emit(grads, ("w_xq", "w_xkv"))
    dhx = mm(f"{tag}_q_dx", dq[None], p["w_xq"], "nt", BF16, after=sch.take())[0]
    dmh = mm(f"{tag}_kv_dx", dkv16, p["w_xkv"], "nt", BF16)[0]
    _, grads["mem_norm"] = k_rms_bwd(f"{tag}_mem_b", mem, p["mem_norm"], [dmh], jnp.zeros_like(mem))
    dx, grads["xa_norm_pre"] = k_rms_bwd(f"{tag}_pre_b", x, p["xa_norm_pre"], [dhx], dy)
    return dx


def _place():
    return lax.axis_index("x"), lax.axis_index("y"), lax.axis_index("c")


def _other_chips(x, y):
    return [(1 - x, y), (x, 1 - y), (1 - x, 1 - y)]


_ANY = pl.BlockSpec(memory_space=pl.ANY)


def all_gather(name, shards, after=()):
    n, na = len(shards), len(after)

    def body(*refs):
        ins, outs = refs[:n], refs[n + na:2 * n + na]
        send_sems, recv_sems, local_sems = refs[2 * n + na:]
        x, y, c = _place()
        me, sibling = (x, y, c), (x, y, 1 - c)
        chips = _other_chips(x, y)
        idx = lambda px, py, pc: 4 * px + 2 * py + pc

        def copy(a, k, block, to, src=None):
            dst = outs[a].at[idx(*block)]
            return pltpu.make_async_remote_copy(src_ref=dst if src is None else src, dst_ref=dst,
                                                send_sem=send_sems.at[7 * a + k], recv_sem=recv_sems.at[7 * a + k],
                                                device_id=to, device_id_type=MESH)

        mine = [pltpu.make_async_copy(ins[a], outs[a].at[idx(*me)], local_sems.at[a]) for a in range(n)]
        for cp in mine:
            cp.start()
        first = []
        for a in range(n):
            first.append(copy(a, 0, me, sibling, src=ins[a]))
            first += [copy(a, 1 + j, me, (*chip, c), src=ins[a]) for j, chip in enumerate(chips)]
        for cp in first:
            cp.start()
        passed = []
        for a in range(n):
            for j, chip in enumerate(chips):
                copy(a, 1 + j, (*chip, c), me).wait_recv()
                passed.append(copy(a, 4 + j, (*chip, c), sibling))
                passed[-1].start()
        for a in range(n):
            copy(a, 0, sibling, me).wait_recv()
            for j, chip in enumerate(chips):
                copy(a, 4 + j, (*chip, 1 - c), me).wait_recv()
        for cp in first + passed:
            cp.wait_send()
        for cp in mine:
            cp.wait()

    return pl.pallas_call(
        body, name=name, in_specs=[_ANY] * (n + na), out_specs=[_ANY] * n,
        out_shape=[jax.ShapeDtypeStruct((N_DEV,) + s.shape, s.dtype) for s in shards],
        scratch_shapes=[pltpu.SemaphoreType.DMA((7 * n,)), pltpu.SemaphoreType.DMA((7 * n,)),
                        pltpu.SemaphoreType.DMA((n,))])(*shards, *after)


def k_cast_place(name, w, layer, me, after):
    _, r, cc = w.shape
    tr, tc = _tile2(r, cc)
    return stepk(name, lambda v: (v,), (r // tr, cc // tc), [(w, (None, tr, tc), lambda i, j, pre: (layer, i, j))],
                 [((N_DEV, r, cc), BF16, (None, tr, tc), lambda i, j, pre: (pre[0], i, j))], prefetch=me,
                 after=after)[0]


_HBM = pl.BlockSpec(memory_space=pltpu.HBM)
_SEM = pl.BlockSpec(memory_space=pltpu.SEMAPHORE)
_DATAFLOW = pltpu.SideEffectType.DATAFLOW_SIDE_EFFECTING


def xfer_start(name, plan, n_sems, srcs, lands, after=()):
    ns, nl, na = len(srcs), len(lands), len(after)
    bufs = list(srcs) + list(lands)

    def body(*refs):
        send_sems, recv_sems, token = refs[ns + nl + na], refs[ns + nl + na + 1], refs[-1]
        for cp in plan(refs[:ns], refs[ns:ns + nl], send_sems, recv_sems):
            cp.start()
        token[...] = jnp.zeros(token.shape, token.dtype)

    res = pl.pallas_call(
        body, name=name, in_specs=[_HBM] * (ns + nl) + [_ANY] * na,
        out_specs=(_SEM, _SEM, *[_HBM] * (ns + nl), pl.BlockSpec(memory_space=pltpu.VMEM)),
        out_shape=(pltpu.SemaphoreType.DMA((n_sems,)), pltpu.SemaphoreType.DMA((n_sems,)),
                   *[pltpu.HBM(b.shape, b.dtype) for b in bufs], jax.ShapeDtypeStruct((8, 128), F32)),
        input_output_aliases={i: 2 + i for i in range(ns + nl)},
        compiler_params=pltpu.CompilerParams(has_side_effects=_DATAFLOW),
    )(*[pltpu.with_memory_space_constraint(b, pltpu.HBM) for b in bufs], *after)
    return dict(send=res[0], recv=res[1], srcs=list(res[2:2 + ns]), lands=list(res[2 + ns:2 + ns + nl]), token=res[-1])


def xfer_wait(name, plan, started, after=()):
    ns, nl = len(started["srcs"]), len(started["lands"])
    bufs = started["srcs"] + started["lands"]

    def body(*refs):
        for cp in plan(refs[:ns], refs[ns:ns + nl], refs[ns + nl], refs[ns + nl + 1]):
            cp.wait_send()
            cp.wait_recv()

    res = pl.pallas_call(
        body, name=name, in_specs=[_HBM] * (ns + nl) + [_SEM, _SEM] + [_ANY] * len(after),
        out_specs=tuple([_HBM] * (ns + nl)), out_shape=tuple(pltpu.HBM(b.shape, b.dtype) for b in bufs),
        input_output_aliases={i: i for i in range(ns + nl)},
        compiler_params=pltpu.CompilerParams(has_side_effects=_DATAFLOW),
    )(*bufs, started["send"], started["recv"], *after)
    return list(res[:ns]), list(res[ns:])


def _remote(src, dst, send_sems, recv_sems, k, to):
    return pltpu.make_async_remote_copy(src_ref=src, dst_ref=dst, send_sem=send_sems.at[k], recv_sem=recv_sems.at[k],
                                        device_id=to, device_id_type=MESH)


def plan_gather_out(srcs, lands, send_sems, recv_sems):
    x, y, c = _place()
    me = 4 * x + 2 * y + c
    targets = [(x, y, 1 - c)] + [(px, py, c) for px, py in _other_chips(x, y)]
    return [_remote(lands[a].at[me], lands[a].at[me], send_sems, recv_sems, 4 * a + k, to)
            for a in range(len(lands)) for k, to in enumerate(targets)]


def plan_gather_pass(srcs, lands, send_sems, recv_sems):
    x, y, c = _place()
    return [_remote(lands[a].at[4 * px + 2 * py + c], lands[a].at[4 * px + 2 * py + c], send_sems, recv_sems,
                    3 * a + j, (x, y, 1 - c))
            for a in range(len(lands)) for j, (px, py) in enumerate(_other_chips(x, y))]


def plan_scatter_sibling(srcs, lands, send_sems, recv_sems):
    x, y, c = _place()
    return [_remote(srcs[a].at[j, 1 - c], lands[a].at[j], send_sems, recv_sems, 4 * a + j, (x, y, 1 - c))
            for a in range(len(srcs)) for j in range(4)]


def plan_scatter_chips(srcs, lands, send_sems, recv_sems):
    x, y, c = _place()
    return [_remote(srcs[a].at[2 * px + py], lands[a].at[j], send_sems, recv_sems, 3 * a + j, (px, py, c))
            for a in range(len(srcs)) for j, (px, py) in enumerate(_other_chips(x, y))]


ELEMWISE_BLOCK = 512 * 1024


def _tile2(r, cc, limit=ELEMWISE_BLOCK):
    if r % 16:
        tc = cc
        while r * tc > limit and tc % 256 == 0:
            tc //= 2
        return r, tc
    t = r
    while t * cc > limit and t % 32 == 0:
        t //= 2
    if t * cc > limit:
        for cand in range(t, 15, -16):
            if r % cand == 0 and cand * cc <= limit:
                return cand, cc
    return t, cc


def k_pair_add(name, g4, r1, core):
    _, _, r, cc = g4.shape
    tr, tc = _tile2(r, cc, 2 * ELEMWISE_BLOCK)
    fn = lambda av, bv: (av.astype(F32) + bv.astype(F32),)
    return stepk(name, fn, (4, r // tr, cc // tc),
                 [(g4, (None, None, tr, tc), lambda b, i, j, pre: (b, pre[0], i, j)),
                  (r1, (None, tr, tc), lambda b, i, j, pre: (b, i, j))],
                 [((4, r, cc), BF16, (None, tr, tc), lambda b, i, j, pre: (b, i, j))], prefetch=core)[0]


def _adamw(g, w, m, v):
    m2 = ADAM_B1 * m + (1.0 - ADAM_B1) * g
    v2 = ADAM_B2 * v + (1.0 - ADAM_B2) * jnp.square(g)
    m_hat = m2 / (1.0 - ADAM_B1 ** ADAM_STEP)
    v_hat = v2 / (1.0 - ADAM_B2 ** ADAM_STEP)
    delta = -ADAM_LR * (m_hat / (jnp.sqrt(v_hat) + ADAM_EPS) + ADAM_WD * w)
    return g, delta, m2, v2


def k_adamw_shard(name, part, others, w, m, v, layer, chip, fill):
    _, r, cc = part.shape
    tr, tc = _tile2(r, cc)

    def fn(pv, o0, o1, o2, wv, mv, vv, done):
        g = ((pv.astype(F32) + o0.astype(F32)) + o1.astype(F32)) + o2.astype(F32)
        return _adamw(g, wv, mv, vv) + (done,)

    other = lambda k: (others, (None, tr, tc), lambda i, j, pre: (k, i, j))
    state = lambda arr: (arr, (None, tr, tc), lambda i, j, pre: (layer, i, j))
    out = ((w.shape[0], r, cc), F32, (None, tr, tc), lambda i, j, pre: (layer, i, j))
    return stepk(name, fn, (r // tr, cc // tc),
                 [(part, (None, tr, tc), lambda i, j, pre: (pre[0], i, j)), other(0), other(1), other(2),
                  state(w), state(m), state(v)],
                 [out] * 4, carries=[((8, 128), F32)], prefetch=chip, fill=[(f, k) for k, f in enumerate(fill)])


def k_sum8(name, parts):
    _, rows, lanes = parts.shape

    def fn(pv):
        acc = pv[0]
        for d in range(1, N_DEV):
            acc = acc + pv[d]
        return (acc,)

    return stepk(name, fn, (1,), [(parts, parts.shape, lambda i: (0, 0, 0))],
                 [((rows, lanes), F32, (rows, lanes), lambda i: (0, 0))])[0]


def k_adamw_flat(name, g, w, m, v):
    whole = lambda arr: (arr, arr.shape, lambda i: (0, 0))
    out = (g.shape, F32, g.shape, lambda i: (0, 0))
    return stepk(name, _adamw, (1,), [whole(g), whole(w), whole(m), whole(v)], [out] * 4)


WEIGHTS = ("ffn1_norm_pre", "ffn1_w_gate_up", "ffn1_w_down", "ffn1_norm_post", "mix_norm_pre", "w_in", "conv_w", "a_log",
           "dt_bias", "sm_w", "sm_b", "sm_ln_g", "sm_ln_b", "dn_norm_w", "w_out", "mix_norm_post", "xa_norm_pre", "mem_norm",
           "w_xq", "w_xkv", "w_xo", "xa_norm_post", "ffn2_norm_pre", "ffn2_w_gate_up", "ffn2_w_down", "ffn2_norm_post")
BIG = ("ffn1_w_gate_up", "ffn1_w_down", "w_in", "w_out", "w_xq", "w_xkv", "w_xo", "ffn2_w_gate_up", "ffn2_w_down")
ROW_SHARDED = ("ffn1_w_down", "w_out", "w_xq", "w_xo", "ffn2_w_down")
SMALL = tuple(n for n in WEIGHTS if n not in BIG and n != "conv_w")
N_LAYERS = 2
IN_COLS = MAIN_W + 2 * DN_HEADS


BLOCKS = ("f1", "mx", "xa", "f2")
AG_BEFORE = {"f1": ("ffn1_w_gate_up", "ffn1_w_down"), "mx": ("w_in", "w_out"), "xa": ("w_xq", "w_xkv", "w_xo"),
             "f2": ("ffn2_w_gate_up", "ffn2_w_down")}
AG_LARGE = ("f1", "f2")
RS_IN_FLIGHT = 1
RS_LAST = ("ffn1_w_gate_up",)


def _fwd_block(l, sb, h, mem, p, sch):
    tag = f"l{l}{sb}"
    if sb == "f1":
        return ffn_fwd(tag, h, p, "ffn1_norm_pre", "ffn1_norm_post", "ffn1_w_gate_up", "ffn1_w_down", sch)
    if sb == "mx":
        return mix_fwd(tag, h, p, sch)
    if sb == "xa":
        return xa_fwd(tag, h, mem, p, sch)
    return ffn_fwd(tag, h, p, "ffn2_norm_pre", "ffn2_norm_post", "ffn2_w_gate_up", "ffn2_w_down", sch)


class WeightGather:
    def __init__(self, a, w_in_t, me1, chain):
        self.blocks = [(l, sb) for l in range(N_LAYERS) for sb in BLOCKS]
        self.names = {(l, sb): AG_BEFORE[sb] for l, sb in self.blocks}
        first, second = AG_BEFORE[BLOCKS[0]][:1], AG_BEFORE[BLOCKS[0]][1:]
        self.names[0, BLOCKS[0]] = first
        self.names[0, BLOCKS[0] + "b"] = second
        order = [(0, BLOCKS[0]), (0, BLOCKS[0] + "b")] + self.blocks[1:]
        self.going, self.passing, self.tie = {}, {}, []
        for l, sb in order:
            lands = [k_cast_place(f"place_l{l}{n}", *((w_in_t[""][l], 0) if n == "w_in" else (a[n], l)), me1, chain)
                     for n in self.names[l, sb]]
            self.going[l, sb] = xfer_start(f"ag_out_l{l}{sb}", plan_gather_out, 4 * len(lands), [], lands)
            chain = [self.going[l, sb]["token"]]
        self.tie = chain
        self.block = None

    def take(self):
        tie, self.tie = self.tie, []
        return tie

    def _pass_on(self, key, after):
        l, sb = key
        _, got = xfer_wait(f"ag_outw_l{l}{sb}", plan_gather_out, self.going.pop(key), after=after)
        self.passing[key] = xfer_start(f"ag_pass_l{l}{sb}", plan_gather_pass, 3 * len(got), [], got)
        self.tie = [self.passing[key]["token"]]

    def _arrived(self, key, after):
        l, sb = key
        _, full = xfer_wait(f"ag_passw_l{l}{sb}", plan_gather_pass, self.passing.pop(key), after=after)
        self.tie = [full[0]]
        return _big_params(dict(zip(self.names[key], full)))

    def _holding(self, name):
        return next(k for k in list(self.going) + list(self.passing)
                    if k[0] == self.block[0] and name in self.names[k])

    def mid(self, arr, late):
        later = self.blocks[self.blocks.index(self.block) + 1:]
        if later and later[0] in self.going and (later[0][1] in AG_LARGE) == late:
            self._pass_on(later[0], [arr])

    def prefetch(self, p, name, arr):
        if name not in p and self._holding(name) in self.going:
            self._pass_on(self._holding(name), [arr])

    def param(self, p, name, after):
        if name not in p:
            key = self._holding(name)
            if key in self.going:
                self._pass_on(key, list(after))
            p.update(self._arrived(key, list(after) + self.take()))
        return p[name]

    def weights(self, l, sb, after):
        self.block = (l, sb)
        after = list(after) + self.take()
        if (l, sb) in self.going:
            self._pass_on((l, sb), after)
        return self._arrived((l, sb), after)


def _bwd_block(l, sb, dy, saved, p, grads, sch):
    tag = f"l{l}{sb}"
    if sb == "f1":
        return ffn_bwd(tag, dy, saved, p, "ffn1_norm_pre", "ffn1_norm_post", "ffn1_w_gate_up", "ffn1_w_down", grads, sch)
    if sb == "mx":
        return mix_bwd(tag, dy, saved, p, grads, sch)
    if sb == "xa":
        return xa_bwd(tag, dy, saved, p, grads, sch)
    return ffn_bwd(tag, dy, saved, p, "ffn2_norm_pre", "ffn2_norm_post", "ffn2_w_gate_up", "ffn2_w_down", grads, sch)


class GradExchange:
    def __init__(self, a, w_in_t, core, chip):
        self.a, self.w_in_t, self.core, self.chip = a, w_in_t, core, chip
        self.layer = None
        self.pending, self.tie, self.out = [], [], {}
        self.w_in_out = [None] * N_LAYERS

    def take(self):
        tie, self.tie = self.tie, []
        return tie

    def _advance(self, chain, drain, waits=True):
        a = self.a
        between_chips = [grp for grp in self.pending if grp["stage"] == "chips"]
        for grp in self.pending:
            if grp["stage"] == "sibling":
                l, names = grp["l"], grp["names"]
                g4, r1 = xfer_wait(f"rs_sibw_l{l}{names[0]}", plan_scatter_sibling, grp["going"], after=chain)
                parts = [k_pair_add(f"rs_add_l{l}{n}", gg, rr, self.core) for n, gg, rr in zip(names, g4, r1)]
                zones = [lax.empty((3,) + p.shape[1:], BF16) for p in parts]
                grp["going"] = xfer_start(f"rs_chip_l{l}{names[0]}", plan_scatter_chips, 3 * len(names), parts, zones)
                grp["stage"], grp["age"] = "chips", 0
                chain = [grp["going"]["token"]]
        for grp in between_chips if waits else ():
            l, names = grp["l"], grp["names"]
            if drain or grp["age"] >= grp["limit"]:
                parts, r2 = xfer_wait(f"rs_chipw_l{l}{names[0]}", plan_scatter_chips, grp["going"], after=chain)
                for n, part, others in zip(names, parts, r2):
                    if n == "w_in":
                        state = [self.w_in_t[pre][l] for pre in ("", "m_", "v_")]
                        *self.w_in_out[l], done = k_adamw_shard(f"adamw_{n}_l{l}", part, others, *state, 0, self.chip, ())
                    else:
                        *self.out[n], done = k_adamw_shard(f"adamw_{n}_l{l}", part, others, a[n], a["m_" + n],
                                                           a["v_" + n], l, self.chip, self.out.get(n, ()))
                    chain = [done]
                self.pending.remove(grp)
            else:
                grp["age"] += 1
        return chain

    def emit(self, grads, names):
        l = self.layer
        g4 = [_scatter_layout(n, grads[n]) for n in names]
        g4 = [t.reshape(4, 2, *t.shape[1:]) for t in g4]
        zones = [lax.empty((4,) + t.shape[2:], BF16) for t in g4]
        going = xfer_start(f"rs_sib_l{l}{names[0]}", plan_scatter_sibling, 4 * len(names), g4, zones)
        chain = self._advance([going["token"]], drain=False)
        self.pending.append(dict(l=l, names=names, stage="sibling", going=going, age=0, limit=RS_IN_FLIGHT))
        if l == 0 and names[0] in RS_LAST:
            chain = self._advance(chain, drain=False, waits=False)
        self.tie = chain

    def finish(self, last):
        chain = [last]
        while self.pending:
            chain = self._advance(chain, drain=True)
        return self.out


def _big_params(full):
    p = {}
    for n, w in full.items():
        if n in ROW_SHARDED:
            p[n] = w.reshape(1, w.shape[0] * w.shape[1], w.shape[2])
        elif n == "w_in":
            p["w_in_t"] = w.reshape(1, w.shape[0] * w.shape[1], w.shape[2])
        else:
            p[n] = w
    return p


def _small_params(l, conv_full, a):
    p = {"conv_w": conv_full[l]}
    for n in SMALL:
        w = a[n][l]
        if n in ("a_log", "dt_bias"):
            p[n] = w.reshape(DN_HEADS, 1, 1)
        elif n == "sm_b":
            p[n] = w[..., None]
        elif n == "sm_w":
            p[n] = w
        else:
            p[n] = w[None]
    return p


def _scatter_layout(n, g):
    if n in ROW_SHARDED:
        return g.reshape(N_DEV, g.shape[1] // N_DEV, g.shape[2])
    return g


def _seg_rows(size):
    return -(-size // 1024) * 8


def _pack(flat_parts):
    return jnp.concatenate([jnp.pad(f, (0, _seg_rows(f.shape[0]) * 128 - f.shape[0])).reshape(-1, 128) for f in flat_parts])


def kernel(x, mem, ffn1_norm_pre, ffn1_w_gate_up, ffn1_w_down, ffn1_norm_post, mix_norm_pre, w_in, conv_w, a_log, dt_bias, sm_w, sm_b, sm_ln_g, sm_ln_b, dn_norm_w, w_out, mix_norm_post, xa_norm_pre, mem_norm, w_xq, w_xkv, w_xo, xa_norm_post, ffn2_norm_pre, ffn2_w_gate_up, ffn2_w_down, ffn2_norm_post, loss_target, m_ffn1_norm_pre, m_ffn1_w_gate_up, m_ffn1_w_down, m_ffn1_norm_post, m_mix_norm_pre, m_w_in, m_conv_w, m_a_log, m_dt_bias, m_sm_w, m_sm_b, m_sm_ln_g, m_sm_ln_b, m_dn_norm_w, m_w_out, m_mix_norm_post, m_xa_norm_pre, m_mem_norm, m_w_xq, m_w_xkv, m_w_xo, m_xa_norm_post, m_ffn2_norm_pre, m_ffn2_w_gate_up, m_ffn2_w_down, m_ffn2_norm_post, v_ffn1_norm_pre, v_ffn1_w_gate_up, v_ffn1_w_down, v_ffn1_norm_post, v_mix_norm_pre, v_w_in, v_conv_w, v_a_log, v_dt_bias, v_sm_w, v_sm_b, v_sm_ln_g, v_sm_ln_b, v_dn_norm_w, v_w_out, v_mix_norm_post, v_xa_norm_pre, v_mem_norm, v_w_xq, v_w_xkv, v_w_xo, v_xa_norm_post, v_ffn2_norm_pre, v_ffn2_w_gate_up, v_ffn2_w_down, v_ffn2_norm_post):
    a = dict(locals())
    px, py, pc = _place()
    core = jnp.reshape(pc, (1,)).astype(jnp.int32)
    chip = jnp.reshape(2 * px + py, (1,)).astype(jnp.int32)
    me = 4 * px + 2 * py + pc
    xs, mems, tgt = x[0], mem[0], loss_target[0]
    conv_all = all_gather("ag_conv", [conv_w])[0]
    conv_full = conv_all.transpose(1, 2, 0, 3).reshape(N_LAYERS, CONV_W, QKV_W)
    params = [_small_params(l, conv_full, a) for l in range(N_LAYERS)]
    chain = [conv_all]

    w_in_t = {pre: [jnp.transpose(a[pre + "w_in"], (2, 0, 1))[:, l][None] for l in range(N_LAYERS)]
              for pre in ("", "m_", "v_")}
    gather = WeightGather(a, w_in_t, jnp.reshape(me, (1,)).astype(jnp.int32), chain)
    saved = [{} for _ in range(N_LAYERS)]
    h = xs
    for l in range(N_LAYERS):
        for sb in BLOCKS:
            params[l].update(gather.weights(l, sb, [h]))
            h, saved[l][sb] = _fwd_block(l, sb, h, mems, params[l], gather)
    dy, loss = k_loss("loss", h, tgt)
    loss = lax.psum(loss[0, 0], ("x", "y", "c"))

    grads = [{} for _ in range(N_LAYERS)]
    exchange = GradExchange(a, w_in_t, core, chip)
    for l in reversed(range(N_LAYERS)):
        exchange.layer = l
        for sb in reversed(BLOCKS):
            dy = _bwd_block(l, sb, dy, saved[l][sb], params[l], grads[l], exchange)
    big_out = exchange.finish(dy)

    flat = [jnp.concatenate([grads[l][n].reshape(-1) for l in range(N_LAYERS)]) for n in SMALL]
    flat.append(jnp.concatenate([grads[l]["conv_w"].reshape(-1) for l in range(N_LAYERS)]))
    gsum = k_sum8("small_sum", all_gather("ag_small", [_pack(flat)], after=[big_out["ffn1_w_gate_up"][0]])[0])
    rep_rows = sum(_seg_rows(a[n].size) for n in SMALL)
    conv_g = gsum[rep_rows:].reshape(-1)[:N_LAYERS * CONV_W * QKV_W].reshape(N_LAYERS, CONV_W, QKV_W)
    conv_g = lax.dynamic_slice_in_dim(conv_g, me * (QKV_W // N_DEV), QKV_W // N_DEV, axis=2)
    pack_state = lambda pre: _pack([a[pre + n].reshape(-1) for n in SMALL] + [a[pre + "conv_w"].reshape(-1)])
    small_g = jnp.concatenate([gsum[:rep_rows], _pack([conv_g.reshape(-1)])])
    small_out = k_adamw_flat("small_adamw", small_g, pack_state(""), pack_state("m_"), pack_state("v_"))
    outs = {}
    row = 0
    for n in SMALL + ("conv_w",):
        rows = _seg_rows(a[n].size)
        outs[n] = [o[row:row + rows].reshape(-1)[:a[n].size].reshape(a[n].shape) for o in small_out]
        row += rows
    for n in BIG:
        if n == "w_in":
            outs[n] = [jnp.transpose(jnp.concatenate([exchange.w_in_out[l][k] for l in range(N_LAYERS)]), (1, 0, 2))
                       for k in range(4)]
            outs[n] = [jnp.transpose(o, (1, 2, 0)) for o in outs[n]]
        else:
            outs[n] = list(big_out[n])

    return (loss, dy[None], *[outs[n][0] for n in WEIGHTS], *[outs[n][1] for n in WEIGHTS],
            *[outs[n][2] for n in WEIGHTS], *[outs[n][3] for n in WEIGHTS])
```

```python
import functools
import math

import jax
import jax.numpy as jnp
from jax import lax
from jax.experimental import pallas as pl
from jax.experimental.pallas import tpu as pltpu

F32, BF16 = jnp.float32, jnp.bfloat16
HI = lax.Precision.HIGHEST
MESH = pl.DeviceIdType.MESH

N_DEV = 8
NORM_EPS = 1e-6
GM_HEADS, GM_CHUNK = 8, 128
DN_HEADS, DN_CHUNK, DN_DIM = 8, 64, 128
XA_HEADS, XA_DIM = 4, 512
CONV_W = 4
ADAM_LR, ADAM_B1, ADAM_B2, ADAM_EPS, ADAM_WD, ADAM_STEP = 0.001, 0.9, 0.999, 1e-08, 0.01, 10

V7X_VMEM_LIMIT = 56 * 1024 * 1024
WHOLE_TILE_MAX = 1536
WHOLE_K_MAX = 2048
ROW_TILE = 512


def _cparams(n_grid):
    return pltpu.CompilerParams(dimension_semantics=("arbitrary",) * n_grid, vmem_limit_bytes=V7X_VMEM_LIMIT)


def stepk(name, fn, grid, ins, outs, carries=(), prefetch=None, fill=(), after=()):
    n_in, n_out, n_c, n_fill = len(ins), len(outs), len(carries), len(fill)
    n_pre = 0 if prefetch is None else 1

    def body(*refs):
        refs = refs[n_pre:]
        in_refs, refs = refs[:n_in], refs[n_in + n_fill + len(after):]
        out_refs, c_refs = refs[:n_out], refs[n_out:]
        if n_c:
            first = functools.reduce(jnp.logical_and, [pl.program_id(a) == 0 for a in range(len(grid))])

            @pl.when(first)
            def _():
                for r in c_refs:
                    r[...] = jnp.zeros(r.shape, r.dtype)
        res = fn(*[r[...] for r in in_refs], *[r[...] for r in c_refs])
        for r, v in zip(tuple(out_refs) + tuple(c_refs), res):
            r[...] = v.astype(r.dtype)

    in_specs = [pl.BlockSpec(bs, im) for _, bs, im in ins] + [pl.BlockSpec(memory_space=pl.ANY)] * (n_fill + len(after))
    aliases = {n_pre + n_in + i: k for i, (_, k) in enumerate(fill)}
    out_specs = [pl.BlockSpec(bs, im) for _, _, bs, im in outs]
    out_shape = [jax.ShapeDtypeStruct(s, d) for s, d, _, _ in outs]
    for s, d in carries:
        zeros = (0,) * len(s)
        out_specs.append(pl.BlockSpec(s, lambda *a, _z=zeros: _z))
        out_shape.append(jax.ShapeDtypeStruct(s, d))
    args = [a for a, _, _ in ins] + [a for a, _ in fill] + list(after)
    if prefetch is None:
        call = pl.pallas_call(body, name=name, grid=grid, in_specs=in_specs, out_specs=out_specs, out_shape=out_shape,
                              input_output_aliases=aliases, compiler_params=_cparams(len(grid)))
        return call(*args)
    spec = pltpu.PrefetchScalarGridSpec(num_scalar_prefetch=1, grid=grid, in_specs=in_specs, out_specs=out_specs)
    call = pl.pallas_call(body, name=name, grid_spec=spec, out_shape=out_shape, input_output_aliases=aliases,
                          compiler_params=_cparams(len(grid)))
    return call(prefetch, *args)


def _tile(n, pref, align=128):
    if n <= WHOLE_TILE_MAX:
        return n
    t = (pref // align) * align
    while t > align and n % t:
        t -= align
    assert n % t == 0, (n, pref)
    return t


def mm(name, a, b, mode, out_dtype, nbo=1, after=(), b_rows=None, out_rows=None, fill=None):
    nba, ra, ca = a.shape
    nbb, rb, cb = b.shape
    b_row0 = 0
    if b_rows is not None:
        assert nbb == 1 and mode != "tn"
        b_row0, rb = b_rows
    if mode == "nn":
        m, k, n = ra, nba * ca, nbb * cb
        assert rb == k
    elif mode == "nt":
        m, k, n = ra, nba * ca, rb
        assert nbb * cb == k
    else:
        k, m, n = ra, nba * ca, nbb * cb
        assert rb == k
    co = n // nbo
    whole_k = k <= WHOLE_K_MAX and (mode == "tn" or (nba == 1 and (mode == "nn" or nbb == 1)))
    k_pref = k if whole_k else 512
    m_pref = 1024 if whole_k else 2048
    if mode == "nn":
        tm, tk, tn = _tile(m, m_pref, 8), _tile(ca, k_pref), _tile(math.gcd(cb, co), 1024)
        a_spec = pl.BlockSpec((None, tm, tk), lambda i, j, kk: (kk // (ca // tk), i, kk % (ca // tk)))
        assert b_row0 % tk == 0
        b_spec = pl.BlockSpec((None, tk, tn), lambda i, j, kk: (j // (cb // tn), kk + b_row0 // tk, j % (cb // tn)))
        dims = (((1,), (0,)), ((), ()))
    elif mode == "nt":
        tm, tk, tn = _tile(m, m_pref, 8), _tile(math.gcd(ca, cb), k_pref), _tile(co, 1024)
        assert b_row0 % tn == 0
        a_spec = pl.BlockSpec((None, tm, tk), lambda i, j, kk: (kk // (ca // tk), i, kk % (ca // tk)))
        b_spec = pl.BlockSpec((None, tn, tk), lambda i, j, kk: (kk // (cb // tk), j + b_row0 // tn, kk % (cb // tk)))
        dims = (((1,), (1,)), ((), ()))
    else:
        tm, tk, tn = _tile(ca, 1024), _tile(k, k_pref), _tile(math.gcd(cb, co), 1024)
        a_spec = pl.BlockSpec((None, tk, tm), lambda i, j, kk: (i // (ca // tm), kk, i % (ca // tm)))
        b_spec = pl.BlockSpec((None, tk, tn), lambda i, j, kk: (j // (cb // tn), kk, j % (cb // tn)))
        dims = (((0,), (0,)), ((), ()))
    o_row0, o_rows = out_rows if out_rows is not None else (0, m)
    assert o_row0 % tm == 0
    o_spec = pl.BlockSpec((None, tm, tn), lambda i, j, kk: (j // (co // tn), i + o_row0 // tm, j % (co // tn)))
    nk = k // tk
    extra = list(after) + ([] if fill is None else [fill])

    def tile_product(a_ref, b_ref):
        return lax.dot_general(a_ref[...].astype(BF16), b_ref[...].astype(BF16), dims, preferred_element_type=F32)

    def body_one_step(a_ref, b_ref, *rest):
        o_ref = rest[-1]
        o_ref[...] = tile_product(a_ref, b_ref).astype(o_ref.dtype)

    def body(a_ref, b_ref, *rest):
        o_ref, acc_ref = rest[-2:]
        kk = pl.program_id(2)

        @pl.when(kk == 0)
        def _():
            acc_ref[...] = jnp.zeros(acc_ref.shape, F32)

        acc_ref[...] += tile_product(a_ref, b_ref)

        @pl.when(kk == nk - 1)
        def _():
            o_ref[...] = acc_ref[...].astype(o_ref.dtype)

    return pl.pallas_call(
        body_one_step if nk == 1 else body, name=name, grid=(m // tm, n // tn, nk),
        in_specs=[a_spec, b_spec] + [pl.BlockSpec(memory_space=pl.ANY)] * len(extra), out_specs=o_spec,
        out_shape=jax.ShapeDtypeStruct((nbo, o_rows, co), out_dtype),
        input_output_aliases={} if fill is None else {1 + len(extra): 0},
        scratch_shapes=[] if nk == 1 else [pltpu.VMEM((tm, tn), F32)], compiler_params=_cparams(3))(a, b, *extra)


def _rms(x, g):
    return x * lax.rsqrt(jnp.mean(x * x, axis=-1, keepdims=True) + NORM_EPS) * g


def _sigmoid(x):
    return 1.0 / (1.0 + jnp.exp(-x))


def _silu(x):
    return x * _sigmoid(x)


def _gelu(x):
    return 0.5 * x * (1.0 + lax.erf(x * 0.7071067811865476))


def _softplus(x):
    return jnp.maximum(x, 0.0) + jnp.log(1.0 + jnp.exp(-jnp.abs(x)))


def _split(x, n):
    w = x.shape[-1] // n
    return [x[..., h * w:(h + 1) * w] for h in range(n)]


def _sgu(ua, va, smw, smb, lng, lnb):
    c = ua[0].shape[0]
    width = len(va) * va[0].shape[-1]
    vf = [_gelu(v) for v in va]
    mu = sum(jnp.sum(v, axis=-1, keepdims=True) for v in vf) / width
    var = sum(jnp.sum(jnp.square(v - mu), axis=-1, keepdims=True) for v in vf) / width
    r = lax.rsqrt(var + NORM_EPS)
    causal = lax.broadcasted_iota(jnp.int32, (c, c), 0) >= lax.broadcasted_iota(jnp.int32, (c, c), 1)
    outs = []
    for h in range(len(ua)):
        vn = (vf[h] - mu) * r * lng[h] + lnb[h]
        w = jnp.where(causal, smw[h], 0.0)
        mixed = jnp.dot(w.astype(BF16), vn.astype(BF16), preferred_element_type=F32) + smb[h]
        outs.append(_gelu(ua[h]) * mixed)
    return outs


def _shift_rows(x, k, up):
    rows = x.shape[0]
    row = lax.broadcasted_iota(jnp.int32, x.shape, 0)
    if up:
        return jnp.where(row < rows - k, pltpu.roll(x, rows - k, 0), 0.0)
    return jnp.where(row >= k, pltpu.roll(x, k, 0), 0.0)


@functools.partial(jax.custom_vjp, nondiff_argnums=(1,))
def _delay(x, k):
    return _shift_rows(x, k, False)


def _delay_fwd(x, k):
    return _shift_rows(x, k, False), None


def _delay_bwd(k, _, g):
    return (_shift_rows(g, k, True),)


_delay.defvjp(_delay_fwd, _delay_bwd)


def _conv_silu(x, w0, w1, w2, w3):
    y = w3 * x + w2 * _delay(x, 1) + w1 * _delay(x, 2) + w0 * _delay(x, 3)
    return _silu(y)


_BDOT_DIMS = {"nn": (((2,), (1,)), ((0,), (0,))), "nt": (((2,), (2,)), ((0,), (0,))), "tn": (((1,), (1,)), ((0,), (0,)))}


def _bdot_passes(a, b, kind, passes):
    one = lambda x, y: lax.dot_general(x, y, _BDOT_DIMS[kind], preferred_element_type=F32)
    ah, bh = a.astype(BF16), b.astype(BF16)
    if passes == 1:
        return one(ah, bh)
    al, bl = (a - ah.astype(F32)).astype(BF16), (b - bh.astype(F32)).astype(BF16)
    return one(ah, bh) + (one(ah, bl) + one(al, bh))


@functools.partial(jax.custom_vjp, nondiff_argnums=(2, 3))
def bdot(a, b, kind, passes):
    return _bdot_passes(a, b, kind, passes)


def _bdot_fwd(a, b, kind, passes):
    return _bdot_passes(a, b, kind, passes), (a, b)


def _bdot_bwd(kind, passes, res, g):
    a, b = res
    if kind == "nn":
        return bdot(g, b, "nt", passes), bdot(a, g, "tn", passes)
    if kind == "nt":
        return bdot(g, b, "nn", passes), bdot(g, a, "tn", passes)
    return bdot(b, g, "nt", passes), bdot(a, g, "nn", passes)


bdot.defvjp(_bdot_fwd, _bdot_bwd)


def _tri_inv(a, eye):
    c = a.shape[-1]
    inv = eye - a
    p = bdot(a, a, "nn", 3)
    n_fac = int(math.log2(c)) - 1
    for it in range(n_fac):
        inv = inv + bdot(inv, p, "nn", 3)
        if it < n_fac - 1:
            p = bdot(p, p, "nn", 3)
    return inv


@jax.custom_vjp
def _known_inv(a, x):
    return x


def _known_inv_fwd(a, x):
    return x, x


def _known_inv_bwd(x, g):
    return -bdot(bdot(x, g, "tn", 3), x, "nt", 3), jnp.zeros_like(x)


_known_inv.defvjp(_known_inv_fwd, _known_inv_bwd)


def _dn_step(s, q, k, v, braw, araw, alog, dtb, inv_known=None):
    nh, c, d = q.shape
    row = lax.broadcasted_iota(jnp.int32, (c, c), 0)
    col = lax.broadcasted_iota(jnp.int32, (c, c), 1)
    causal, strict = (row >= col)[None], (row > col)[None]
    lower = (row >= col).astype(F32)
    strict_f = jnp.broadcast_to((row > col).astype(F32)[None], (nh, c, c))
    eye = (row == col).astype(F32)[None]

    qn = q * lax.rsqrt(jnp.sum(q * q, axis=-1, keepdims=True) + NORM_EPS) * (d ** -0.5)
    kn = k * lax.rsqrt(jnp.sum(k * k, axis=-1, keepdims=True) + NORM_EPS)
    beta = _sigmoid(braw)
    g = -jnp.exp(alog) * _softplus(araw + dtb)
    lg = lower[None] * g
    gcum = jnp.sum(lg, axis=-1, keepdims=True)
    diff = bdot(lg, strict_f, "nn", 3)
    decay = jnp.where(causal, jnp.exp(diff), 0.0)
    bcol = jnp.sum(eye * beta, axis=-1, keepdims=True)
    kb = kn * bcol
    a = jnp.where(strict, bdot(kb, kn, "nt", 1) * decay, 0.0)
    inv = _tri_inv(a, eye) if inv_known is None else _known_inv(a, inv_known)
    eg = jnp.exp(gcum)
    u = bdot(inv, v * bcol, "nn", 1)
    w = bdot(inv, kb * eg, "nn", 1)
    qk = jnp.where(causal, bdot(qn, kn, "nt", 1) * decay, 0.0)
    v_new = u - bdot(w, s, "nn", 1)
    o = bdot(qn * eg, s, "nn", 1) + bdot(qk, v_new, "nn", 1)
    glast = jnp.sum(g, axis=-1, keepdims=True)
    kdec = kn * jnp.exp(glast - gcum)
    s_new = s * jnp.exp(glast) + bdot(kdec, v_new, "tn", 1)
    return s_new, o, inv


def _ogate(o, z, w):
    return [_rms(oh, w) * _silu(zh) for oh, zh in zip(o, z)]


def _xattn(q, k, v):
    outs = []
    for qh, kh, vh in zip(q, k, v):
        s = lax.dot_general(qh.astype(BF16), kh.astype(BF16), (((1,), (1,)), ((), ())),
                            preferred_element_type=F32) * (qh.shape[-1] ** -0.5)
        s = s - jnp.max(s, axis=-1, keepdims=True)
        e = jnp.exp(s)
        p = e / jnp.sum(e, axis=-1, keepdims=True)
        outs.append(jnp.dot(p.astype(BF16), vh.astype(BF16), preferred_element_type=F32))
    return outs


def _rows(t):
    return min(ROW_TILE, t)


def k_rms_fwd(name, x, g, after=()):
    t, d = x.shape
    tm = _rows(t)
    return stepk(name, lambda xv, gv: (_rms(xv, gv),), (t // tm,),
                 [(x, (tm, d), lambda i: (i, 0)), (g, (1, d), lambda i: (0, 0))],
                 [((t, d), BF16, (tm, d), lambda i: (i, 0))], after=after)[0]


def k_rms_bwd(name, x, g, dhs, dx_res):
    t, d = x.shape
    tm = _rows(t)
    n = len(dhs)

    def fn(xv, gv, *rest):
        dh = sum(r.astype(F32) for r in rest[:n])
        dxr, dg_c = rest[n], rest[n + 1]
        _, vjp = jax.vjp(_rms, xv, gv)
        dx, dg = vjp(dh)
        return dx + dxr, dg_c + dg

    row = lambda arr: (arr, (tm, d), lambda i: (i, 0))
    return stepk(name, fn, (t // tm,), [row(x), (g, (1, d), lambda i: (0, 0))] + [row(h) for h in dhs] + [row(dx_res)],
                 [((t, d), F32, (tm, d), lambda i: (i, 0))], carries=[((1, d), F32)])


def k_post_fwd(name, x, f, g, scale, after=()):
    t, d = x.shape
    tm = _rows(t)
    row = lambda arr: (arr, (tm, d), lambda i: (i, 0))
    return stepk(name, lambda xv, fv, gv: (xv + scale * _rms(fv, gv),), (t // tm,),
                 [row(x), row(f), (g, (1, d), lambda i: (0, 0))], [((t, d), F32, (tm, d), lambda i: (i, 0))],
                 after=after)[0]


def k_post_bwd(name, f, g, dxo, scale, after=()):
    t, d = f.shape
    tm = _rows(t)

    def fn(fv, gv, dv, dg_c):
        _, vjp = jax.vjp(lambda a, b: scale * _rms(a, b), fv, gv)
        df, dg = vjp(dv)
        return df, dg_c + dg

    row = lambda arr: (arr, (tm, d), lambda i: (i, 0))
    return stepk(name, fn, (t // tm,), [row(f), (g, (1, d), lambda i: (0, 0)), row(dxo)],
                 [((t, d), BF16, (tm, d), lambda i: (i, 0))], carries=[((1, d), F32)], after=after)


def k_swiglu_fwd(name, hu, after=()):
    nb, t, c = hu.shape
    half = nb // 2
    tm = _rows(t)
    fn = lambda gv, uv: (_silu(gv.astype(F32)) * uv.astype(F32),)
    return stepk(name, fn, (half, t // tm),
                 [(hu, (None, tm, c), lambda b, i: (b, i, 0)), (hu, (None, tm, c), lambda b, i: (b + half, i, 0))],
                 [((half, t, c), BF16, (None, tm, c), lambda b, i: (b, i, 0))], after=after)[0]


def k_swiglu_bwd(name, hu, da):
    nb, t, c = hu.shape
    half = nb // 2
    tm = _rows(t)

    def fn(gv, uv, dv):
        gv, uv, dv = gv.astype(F32), uv.astype(F32), dv.astype(F32)
        sg = _sigmoid(gv)
        return (jnp.stack([dv * uv * sg * (1.0 + gv * (1.0 - sg)), dv * gv * sg]),)

    blk = lambda arr, off: (arr, (None, tm, c), lambda b, i: (b + off, i, 0))
    out = stepk(name, fn, (half, t // tm), [blk(hu, 0), blk(hu, half), blk(da, 0)],
                [((2, half, t, c), BF16, (2, None, tm, c), lambda b, i: (0, b, i, 0))])[0]
    return out.reshape(nb, t, c)


def k_loss(name, y, tgt):
    t, d = y.shape
    tm = _rows(t)

    def fn(yv, tv, acc):
        e = yv - tv
        part = jnp.sum(jnp.sum(e * e, axis=-1, keepdims=True), axis=0, keepdims=True)
        return e * (1.0 / d), acc + (0.5 / d) * part

    row = lambda arr: (arr, (tm, d), lambda i: (i, 0))
    return stepk(name, fn, (t // tm,), [row(y), row(tgt)], [((t, d), F32, (tm, d), lambda i: (i, 0))],
                 carries=[((1, 1), F32)])


GM_W = GM_HEADS * 128
QKV_COL0 = 2 * GM_W
QKV_W = 3 * DN_HEADS * DN_DIM
Z_COL0 = QKV_COL0 + QKV_W
MAIN_W = Z_COL0 + DN_HEADS * DN_DIM


def _sgu_ins(proj, p):
    c = GM_CHUNK
    return [(proj, (c, GM_W), lambda i: (i, 0)), (proj, (c, GM_W), lambda i: (i, 1)),
            (p["sm_w"], (GM_HEADS, c, c), lambda i: (0, 0, 0)), (p["sm_b"], (GM_HEADS, c, 1), lambda i: (0, 0, 0)),
            (p["sm_ln_g"], (1, GM_W), lambda i: (0, 0)), (p["sm_ln_b"], (1, GM_W), lambda i: (0, 0))]


def _sgu_lists(uv, vv, sw, sb, lg, lb):
    nh = GM_HEADS
    return (_split(uv, nh), _split(vv, nh), [sw[h] for h in range(nh)], [sb[h] for h in range(nh)],
            _split(lg, nh), _split(lb, nh))


def k_sgu_fwd(name, proj, p):
    t = proj.shape[0]

    def fn(*vals):
        return (jnp.concatenate(_sgu(*_sgu_lists(*vals)), axis=-1),)

    return stepk(name, fn, (t // GM_CHUNK,), _sgu_ins(proj, p),
                 [((2, t, GM_W), BF16, (None, GM_CHUNK, GM_W), lambda i: (0, i, 0))])[0]


def k_sgu_bwd(name, proj, p, dy, dproj):
    t = proj.shape[0]
    c = GM_CHUNK

    def fn(uv, vv, sw, sb, lg, lb, dv, dsw, dsb, dlg, dlb):
        _, vjp = jax.vjp(_sgu, *_sgu_lists(uv, vv, sw, sb, lg, lb))
        gu, gv, gsw, gsb, glg, glb = vjp(_split(dv.astype(F32), GM_HEADS))
        cat = lambda l: jnp.concatenate(l, axis=-1)
        return (cat(gu + gv), dsw + jnp.stack(gsw), dsb + jnp.stack(gsb), dlg + cat(glg), dlb + cat(glb))

    return stepk(name, fn, (t // c,), _sgu_ins(proj, p) + [(dy, (None, c, GM_W), lambda i: (0, i, 0))],
                 [((t, MAIN_W), BF16, (c, 2 * GM_W), lambda i: (i, 0))],
                 carries=[((GM_HEADS, c, c), F32), ((GM_HEADS, c, 1), F32), ((1, GM_W), F32), ((1, GM_W), F32)],
                 fill=[(dproj, 0)])


def _conv_ins(proj, conv_w):
    t = proj.shape[0]
    return [(proj, (t, 128), lambda j: (0, QKV_COL0 // 128 + j)), (conv_w, (CONV_W, 128), lambda j: (0, j))]


def k_conv_fwd(name, proj, conv_w):
    t = proj.shape[0]
    n = QKV_W // 128
    fn = lambda xv, wv: (_conv_silu(xv, *[wv[i:i + 1] for i in range(CONV_W)]),)
    return stepk(name, fn, (n,), _conv_ins(proj, conv_w), [((n, t, 128), F32, (None, t, 128), lambda j: (j, 0, 0))])[0]


def k_conv_bwd(name, proj, conv_w, dq, dk, dv, dproj):
    t = proj.shape[0]
    n = QKV_W // 128
    nh = dq.shape[0]

    def fn(xv, wv, dqv, dkv, dvv):
        part = pl.program_id(0) // nh
        d = jnp.where(part == 0, dqv, jnp.where(part == 1, dkv, dvv))
        _, vjp = jax.vjp(_conv_silu, xv, *[wv[i:i + 1] for i in range(CONV_W)])
        gx, *gw = vjp(d)
        return gx, jnp.concatenate(gw, axis=0)

    head = lambda arr, k: (arr, (None, t, 128), lambda j: (jnp.clip(j - k * nh, 0, nh - 1), 0, 0))
    return stepk(name, fn, (n,), _conv_ins(proj, conv_w) + [head(dq, 0), head(dk, 1), head(dv, 2)],
                 [((t, MAIN_W), BF16, (t, 128), lambda j: (0, QKV_COL0 // 128 + j)),
                  ((CONV_W, QKV_W), F32, (CONV_W, 128), lambda j: (0, j))], fill=[(dproj, 0)])


def _dn_ins(qkv, braw, araw, p, order):
    h, c, d = DN_HEADS, DN_CHUNK, DN_DIM
    qkv_in = lambda part: (qkv, (h, c, d), lambda n: (part, order(n), 0))
    gate_in = lambda arr: (arr, (None, h, 1, c), lambda n: (order(n), 0, 0, 0))
    par_in = lambda arr: (arr, (h, 1, 1), lambda n: (0, 0, 0))
    return [qkv_in(0), qkv_in(1), qkv_in(2), gate_in(braw), gate_in(araw), par_in(p["a_log"]), par_in(p["dt_bias"])]


def k_dn_fwd(name, qkv, braw, araw, p):
    t = qkv.shape[1]
    h, c, d = DN_HEADS, DN_CHUNK, DN_DIM
    nc = t // c

    def fn(q, k, v, b, a, al, dt, s):
        s_new, o, inv = _dn_step(s, q, k, v, b, a, al, dt)
        return o, s, inv, s_new

    o, s_all, inv_all, _ = stepk(name, fn, (nc,), _dn_ins(qkv, braw, araw, p, lambda n: n),
                                 [((h, t, d), F32, (h, c, d), lambda n: (0, n, 0)),
                                  ((nc, h, d, d), F32, (None, h, d, d), lambda n: (n, 0, 0, 0)),
                                  ((nc, h, c, c), F32, (None, h, c, c), lambda n: (n, 0, 0, 0))],
                                 carries=[((h, d, d), F32)])
    return o, s_all, inv_all


def k_dn_bwd(name, qkv, braw, araw, p, s_all, inv_all, do):
    t = qkv.shape[1]
    h, c, d = DN_HEADS, DN_CHUNK, DN_DIM
    nc = t // c
    rev = lambda n: nc - 1 - n

    def fn(q, k, v, b, a, al, dt, s, inv, dov, ds_c, dal_c, ddt_c):
        _, vjp = jax.vjp(lambda *args: _dn_step(*args, inv_known=inv)[:2], s, q, k, v, b, a, al, dt)
        ds, dq, dk, dv, db, da, dal, ddt = vjp((ds_c, dov))
        return dq, dk, dv, db, da, ds, dal_c + dal, ddt_c + ddt

    ins = _dn_ins(qkv, braw, araw, p, rev) + [(s_all, (None, h, d, d), lambda n: (rev(n), 0, 0, 0)),
                                               (inv_all, (None, h, c, c), lambda n: (rev(n), 0, 0, 0)),
                                               (do, (h, c, d), lambda n: (0, rev(n), 0))]
    hd = ((h, t, d), F32, (h, c, d), lambda n: (0, rev(n), 0))
    gate = ((nc, h, 1, c), F32, (None, h, 1, c), lambda n: (rev(n), 0, 0, 0))
    dq, dk, dv, db, da, _, dal, ddt = stepk(name, fn, (nc,), ins, [hd, hd, hd, gate, gate],
                                            carries=[((h, d, d), F32), ((h, 1, 1), F32), ((h, 1, 1), F32)])
    return dq, dk, dv, db, da, dal, ddt


def _ogate_ins(o, proj, p):
    t = o.shape[1]
    tm = _rows(t)
    return tm, [(o, (DN_HEADS, tm, DN_DIM), lambda i: (0, i, 0)), (proj, (tm, GM_W), lambda i: (i, Z_COL0 // GM_W)),
                (p["dn_norm_w"], (1, DN_DIM), lambda i: (0, 0))]


def k_ogate_fwd(name, o, proj, p, y):
    t = o.shape[1]
    tm, ins = _ogate_ins(o, proj, p)

    def fn(ov, zv, wv):
        return (jnp.concatenate(_ogate([ov[h] for h in range(DN_HEADS)], _split(zv, DN_HEADS), wv), axis=-1),)

    return stepk(name, fn, (t // tm,), ins, [((2, t, GM_W), BF16, (None, tm, GM_W), lambda i: (1, i, 0))],
                 fill=[(y, 0)])[0]


def k_ogate_bwd(name, o, proj, p, dy):
    t = o.shape[1]
    tm, ins = _ogate_ins(o, proj, p)

    def fn(ov, zv, wv, dv, dw_c):
        _, vjp = jax.vjp(_ogate, [ov[h] for h in range(DN_HEADS)], _split(zv, DN_HEADS), wv)
        go, gz, gw = vjp(_split(dv.astype(F32), DN_HEADS))
        return jnp.stack(go), jnp.concatenate(gz, axis=-1), dw_c + gw

    return stepk(name, fn, (t // tm,), ins + [(dy, (None, tm, GM_W), lambda i: (1, i, 0))],
                 [((DN_HEADS, t, DN_DIM), F32, (DN_HEADS, tm, DN_DIM), lambda i: (0, i, 0)),
                  ((t, MAIN_W), BF16, (tm, GM_W), lambda i: (i, Z_COL0 // GM_W))], carries=[((1, DN_DIM), F32)])


def _xattn_lists(qv, kvv):
    nh = XA_HEADS
    return (_split(qv.astype(F32), nh), [kvv[h].astype(F32) for h in range(nh)],
            [kvv[nh + h].astype(F32) for h in range(nh)])


def k_xattn_fwd(name, q, kv):
    t, d = q.shape
    tm = _rows(t)
    fn = lambda qv, kvv: (jnp.concatenate(_xattn(*_xattn_lists(qv, kvv)), axis=-1),)
    return stepk(name, fn, (t // tm,), [(q, (tm, d), lambda i: (i, 0)), (kv, kv.shape, lambda i: (0, 0, 0))],
                 [((t, d), BF16, (tm, d), lambda i: (i, 0))])[0]


def k_xattn_bwd(name, q, kv, do):
    t, d = q.shape
    tm = _rows(t)

    def fn(qv, kvv, dv, dkv_c):
        _, vjp = jax.vjp(_xattn, *_xattn_lists(qv, kvv))
        gq, gk, gv = vjp(_split(dv.astype(F32), XA_HEADS))
        return jnp.concatenate(gq, axis=-1), dkv_c + jnp.stack(gk + gv)

    return stepk(name, fn, (t // tm,),
                 [(q, (tm, d), lambda i: (i, 0)), (kv, kv.shape, lambda i: (0, 0, 0)), (do, (None, tm, d), lambda i: (0, i, 0))],
                 [((t, d), BF16, (tm, d), lambda i: (i, 0))], carries=[(kv.shape, F32)])


def ffn_fwd(tag, x, p, pre, post, gu, dn, sch):
    h = k_rms_fwd(f"{tag}_pre", x, p[pre], sch.take())
    hu = mm(f"{tag}_gu", h[None], p[gu], "nn", BF16, nbo=N_DEV)
    sch.prefetch(p, dn, hu)
    a = k_swiglu_fwd(f"{tag}_act", hu, sch.take())
    w_down = sch.param(p, dn, [a])
    sch.mid(a, late=False)
    f = mm(f"{tag}_down", a, w_down, "nn", F32, after=sch.take())[0]
    sch.mid(f, late=True)
    y = k_post_fwd(f"{tag}_post", x, f, p[post], 0.5, sch.take())
    return y, (x, h, hu, a, f)


class NoExchange:
    def take(self):
        return ()

    def emit(self, grads, names):
        pass

    def mid(self, arr, late):
        pass

    def prefetch(self, p, name, arr):
        pass

    def param(self, p, name, after):
        return p[name]


def ffn_bwd(tag, dy, saved, p, pre, post, gu, dn, grads, sch):
    x, h, hu, a, f = saved
    df, grads[post] = k_post_bwd(f"{tag}_post_b", f, p[post], dy, 0.5, sch.take())
    grads[dn] = mm(f"{tag}_down_dw", a, df[None], "tn", BF16)
    sch.emit(grads, (dn,))
    da = mm(f"{tag}_down_dx", df[None], p[dn], "nt", BF16, nbo=N_DEV // 2, after=sch.take())
    dhu = k_swiglu_bwd(f"{tag}_act_b", hu, da)
    grads[gu] = mm(f"{tag}_gu_dw", h[None], dhu, "tn", BF16, nbo=N_DEV)
    sch.emit(grads, (gu,))
    dh = mm(f"{tag}_gu_dx", dhu, p[gu], "nt", BF16, after=sch.take())[0]
    dx, grads[pre] = k_rms_bwd(f"{tag}_pre_b", x, p[pre], [dh], dy)
    return dx


def _to_chunks(a):
    t, h = a.shape
    return a.reshape(t // DN_CHUNK, DN_CHUNK, h).transpose(0, 2, 1).reshape(t // DN_CHUNK, h, 1, DN_CHUNK)


def _from_chunks(a):
    nc, h, _, c = a.shape
    return a.reshape(nc, h, c).transpose(0, 2, 1).reshape(nc * c, h)


def mix_fwd(tag, x, p, sch):
    h = k_rms_fwd(f"{tag}_pre", x, p["mix_norm_pre"], sch.take())
    proj = mm(f"{tag}_in", h[None], p["w_in_t"], "nt", F32, b_rows=(0, MAIN_W))[0]
    sch.mid(proj, late=False)
    small = mm(f"{tag}_in_s", h[None], p["w_in_t"], "nt", F32, after=sch.take(), b_rows=(MAIN_W, 2 * DN_HEADS))[0]
    braw, araw = _to_chunks(small[:, :DN_HEADS]), _to_chunks(small[:, DN_HEADS:2 * DN_HEADS])
    y = k_sgu_fwd(f"{tag}_sgu", proj, p)
    qkv = k_conv_fwd(f"{tag}_conv", proj, p["conv_w"])
    o, s_all, inv_all = k_dn_fwd(f"{tag}_dn", qkv, braw, araw, p)
    y = k_ogate_fwd(f"{tag}_og", o, proj, p, y)
    m = mm(f"{tag}_out", y, p["w_out"], "nn", F32)[0]
    out = k_post_fwd(f"{tag}_post", x, m, p["mix_norm_post"], 1.0)
    return out, (x, h, proj, braw, araw, qkv, o, s_all, inv_all, y, m)


def mix_bwd(tag, dy, saved, p, grads, sch):
    x, h, proj, braw, araw, qkv, o, s_all, inv_all, y, m = saved
    dm, grads["mix_norm_post"] = k_post_bwd(f"{tag}_post_b", m, p["mix_norm_post"], dy, 1.0, sch.take())
    grads["w_out"] = mm(f"{tag}_out_dw", y, dm[None], "tn", BF16)
    sch.emit(grads, ("w_out",))
    dyy = mm(f"{tag}_out_dx", dm[None], p["w_out"], "nt", BF16, nbo=2, after=sch.take())
    do, dproj, grads["dn_norm_w"] = k_ogate_bwd(f"{tag}_og_b", o, proj, p, dyy)
    dq, dk, dv, db, da, grads["a_log"], grads["dt_bias"] = k_dn_bwd(f"{tag}_dn_b", qkv, braw, araw, p, s_all, inv_all, do)
    dproj, grads["conv_w"] = k_conv_bwd(f"{tag}_conv_b", proj, p["conv_w"], dq, dk, dv, dproj)
    dproj, grads["sm_w"], grads["sm_b"], grads["sm_ln_g"], grads["sm_ln_b"] = k_sgu_bwd(f"{tag}_sgu_b", proj, p, dyy, dproj)
    dsmall = jnp.concatenate([_from_chunks(db), _from_chunks(da)], axis=-1)
    g_in = mm(f"{tag}_in_dw", dproj[None], h[None], "tn", BF16, out_rows=(0, IN_COLS))
    g_in = mm(f"{tag}_in_s_dw", dsmall[None], h[None], "tn", BF16, out_rows=(MAIN_W, IN_COLS), fill=g_in)[0]
    grads["w_in"] = g_in.reshape(N_DEV, g_in.shape[0] // N_DEV, g_in.shape[1])
    sch.emit(grads, ("w_in",))
    dh = mm(f"{tag}_in_dx", dproj[None], p["w_in_t"], "nn", BF16, after=sch.take(), b_rows=(0, MAIN_W))[0]
    dh_s = mm(f"{tag}_in_s_dx", dsmall[None], p["w_in_t"], "nn", BF16, b_rows=(MAIN_W, 2 * DN_HEADS))[0]
    dx, grads["mix_norm_pre"] = k_rms_bwd(f"{tag}_pre_b", x, p["mix_norm_pre"], [dh, dh_s], dy)
    return dx


def xa_fwd(tag, x, mem, p, sch):
    after = sch.take()
    hx = k_rms_fwd(f"{tag}_pre", x, p["xa_norm_pre"], after)
    mh = k_rms_fwd(f"{tag}_mem", mem, p["mem_norm"], after)
    q = mm(f"{tag}_q", hx[None], p["w_xq"], "nn", BF16)[0]
    sch.mid(q, late=False)
    kv = mm(f"{tag}_kv", mh[None], p["w_xkv"], "nn", BF16, nbo=N_DEV, after=sch.take())
    o = k_xattn_fwd(f"{tag}_att", q, kv)
    sch.mid(o, late=True)
    c = mm(f"{tag}_o", o[None], p["w_xo"], "nn", F32, after=sch.take())[0]
    out = k_post_fwd(f"{tag}_post", x, c, p["xa_norm_post"], 1.0)
    return out, (x, mem, hx, mh, q, kv, o, c)


def xa_bwd(tag, dy, saved, p, grads, sch):
    x, mem, hx, mh, q, kv, o, c = saved
    dc, grads["xa_norm_post"] = k_post_bwd(f"{tag}_post_b", c, p["xa_norm_post"], dy, 1.0, sch.take())
    grads["w_xo"] = mm(f"{tag}_o_dw", o[None], dc[None], "tn", BF16)
    sch.emit(grads, ("w_xo",))
    do = mm(f"{tag}_o_dx", dc[None], p["w_xo"], "nt", BF16, after=sch.take())
    dq, dkv = k_xattn_bwd(f"{tag}_att_b", q, kv, do)
    dkv16 = dkv.astype(BF16)
    grads["w_xq"] = mm(f"{tag}_q_dw", hx[None], dq[None], "tn", BF16)
    grads["w_xkv"] = mm(f"{tag}_kv_dw", mh[None], dkv16, "tn", BF16, nbo=N_DEV)
    sch.emit(grads, ("w_xq", "w_xkv"))
    dhx = mm(f"{tag}_q_dx", dq[None], p["w_xq"], "nt", BF16, after=sch.take())[0]
    dmh = mm(f"{tag}_kv_dx", dkv16, p["w_xkv"], "nt", BF16)[0]
    _, grads["mem_norm"] = k_rms_bwd(f"{tag}_mem_b", mem, p["mem_norm"], [dmh], jnp.zeros_like(mem))
    dx, grads["xa_norm_pre"] = k_rms_bwd(f"{tag}_pre_b", x, p["xa_norm_pre"], [dhx], dy)
    return dx


def _place():
    return lax.axis_index("x"), lax.axis_index("y"), lax.axis_index("c")


def _other_chips(x, y):
    return [(1 - x, y), (x, 1 - y), (1 - x, 1 - y)]


_ANY = pl.BlockSpec(memory_space=pl.ANY)


def all_gather(name, shards, after=()):
    n, na = len(shards), len(after)

    def body(*refs):
        ins, outs = refs[:n], refs[n + na:2 * n + na]
        send_sems, recv_sems, local_sems = refs[2 * n + na:]
        x, y, c = _place()
        me, sibling = (x, y, c), (x, y, 1 - c)
        chips = _other_chips(x, y)
        idx = lambda px, py, pc: 4 * px + 2 * py + pc

        def copy(a, k, block, to, src=None):
            dst = outs[a].at[idx(*block)]
            return pltpu.make_async_remote_copy(src_ref=dst if src is None else src, dst_ref=dst,
                                                send_sem=send_sems.at[7 * a + k], recv_sem=recv_sems.at[7 * a + k],
                                                device_id=to, device_id_type=MESH)

        mine = [pltpu.make_async_copy(ins[a], outs[a].at[idx(*me)], local_sems.at[a]) for a in range(n)]
        for cp in mine:
            cp.start()
        first = []
        for a in range(n):
            first.append(copy(a, 0, me, sibling, src=ins[a]))
            first += [copy(a, 1 + j, me, (*chip, c), src=ins[a]) for j, chip in enumerate(chips)]
        for cp in first:
            cp.start()
        passed = []
        for a in range(n):
            for j, chip in enumerate(chips):
                copy(a, 1 + j, (*chip, c), me).wait_recv()
                passed.append(copy(a, 4 + j, (*chip, c), sibling))
                passed[-1].start()
        for a in range(n):
            copy(a, 0, sibling, me).wait_recv()
            for j, chip in enumerate(chips):
                copy(a, 4 + j, (*chip, 1 - c), me).wait_recv()
        for cp in first + passed:
            cp.wait_send()
        for cp in mine:
            cp.wait()

    return pl.pallas_call(
        body, name=name, in_specs=[_ANY] * (n + na), out_specs=[_ANY] * n,
        out_shape=[jax.ShapeDtypeStruct((N_DEV,) + s.shape, s.dtype) for s in shards],
        scratch_shapes=[pltpu.SemaphoreType.DMA((7 * n,)), pltpu.SemaphoreType.DMA((7 * n,)),
                        pltpu.SemaphoreType.DMA((n,))])(*shards, *after)


def k_cast_place(name, w, layer, me, after):
    _, r, cc = w.shape
    tr, tc = _tile2(r, cc)
    return stepk(name, lambda v: (v,), (r // tr, cc // tc), [(w, (None, tr, tc), lambda i, j, pre: (layer, i, j))],
                 [((N_DEV, r, cc), BF16, (None, tr, tc), lambda i, j, pre: (pre[0], i, j))], prefetch=me,
                 after=after)[0]


_HBM = pl.BlockSpec(memory_space=pltpu.HBM)
_SEM = pl.BlockSpec(memory_space=pltpu.SEMAPHORE)
_DATAFLOW = pltpu.SideEffectType.DATAFLOW_SIDE_EFFECTING


def xfer_start(name, plan, n_sems, srcs, lands, after=()):
    ns, nl, na = len(srcs), len(lands), len(after)
    bufs = list(srcs) + list(lands)

    def body(*refs):
        send_sems, recv_sems, token = refs[ns + nl + na], refs[ns + nl + na + 1], refs[-1]
        for cp in plan(refs[:ns], refs[ns:ns + nl], send_sems, recv_sems):
            cp.start()
        token[...] = jnp.zeros(token.shape, token.dtype)

    res = pl.pallas_call(
        body, name=name, in_specs=[_HBM] * (ns + nl) + [_ANY] * na,
        out_specs=(_SEM, _SEM, *[_HBM] * (ns + nl), pl.BlockSpec(memory_space=pltpu.VMEM)),
        out_shape=(pltpu.SemaphoreType.DMA((n_sems,)), pltpu.SemaphoreType.DMA((n_sems,)),
                   *[pltpu.HBM(b.shape, b.dtype) for b in bufs], jax.ShapeDtypeStruct((8, 128), F32)),
        input_output_aliases={i: 2 + i for i in range(ns + nl)},
        compiler_params=pltpu.CompilerParams(has_side_effects=_DATAFLOW),
    )(*[pltpu.with_memory_space_constraint(b, pltpu.HBM) for b in bufs], *after)
    return dict(send=res[0], recv=res[1], srcs=list(res[2:2 + ns]), lands=list(res[2 + ns:2 + ns + nl]), token=res[-1])


def xfer_wait(name, plan, started, after=()):
    ns, nl = len(started["srcs"]), len(started["lands"])
    bufs = started["srcs"] + started["lands"]

    def body(*refs):
        for cp in plan(refs[:ns], refs[ns:ns + nl], refs[ns + nl], refs[ns + nl + 1]):
            cp.wait_send()
            cp.wait_recv()

    res = pl.pallas_call(
        body, name=name, in_specs=[_HBM] * (ns + nl) + [_SEM, _SEM] + [_ANY] * len(after),
        out_specs=tuple([_HBM] * (ns + nl)), out_shape=tuple(pltpu.HBM(b.shape, b.dtype) for b in bufs),
        input_output_aliases={i: i for i in range(ns + nl)},
        compiler_params=pltpu.CompilerParams(has_side_effects=_DATAFLOW),
    )(*bufs, started["send"], started["recv"], *after)
    return list(res[:ns]), list(res[ns:])


def _remote(src, dst, send_sems, recv_sems, k, to):
    return pltpu.make_async_remote_copy(src_ref=src, dst_ref=dst, send_sem=send_sems.at[k], recv_sem=recv_sems.at[k],
                                        device_id=to, device_id_type=MESH)


def plan_gather_out(srcs, lands, send_sems, recv_sems):
    x, y, c = _place()
    me = 4 * x + 2 * y + c
    targets = [(x, y, 1 - c)] + [(px, py, c) for px, py in _other_chips(x, y)]
    return [_remote(lands[a].at[me], lands[a].at[me], send_sems, recv_sems, 4 * a + k, to)
            for a in range(len(lands)) for k, to in enumerate(targets)]


def plan_gather_pass(srcs, lands, send_sems, recv_sems):
    x, y, c = _place()
    return [_remote(lands[a].at[4 * px + 2 * py + c], lands[a].at[4 * px + 2 * py + c], send_sems, recv_sems,
                    3 * a + j, (x, y, 1 - c))
            for a in range(len(lands)) for j, (px, py) in enumerate(_other_chips(x, y))]


def plan_scatter_sibling(srcs, lands, send_sems, recv_sems):
    x, y, c = _place()
    return [_remote(srcs[a].at[j, 1 - c], lands[a].at[j], send_sems, recv_sems, 4 * a + j, (x, y, 1 - c))
            for a in range(len(srcs)) for j in range(4)]


def plan_scatter_chips(srcs, lands, send_sems, recv_sems):
    x, y, c = _place()
    return [_remote(srcs[a].at[2 * px + py], lands[a].at[j], send_sems, recv_sems, 3 * a + j, (px, py, c))
            for a in range(len(srcs)) for j, (px, py) in enumerate(_other_chips(x, y))]


ELEMWISE_BLOCK = 512 * 1024


def _tile2(r, cc, limit=ELEMWISE_BLOCK):
    if r % 16:
        tc = cc
        while r * tc > limit and tc % 256 == 0:
            tc //= 2
        return r, tc
    t = r
    while t * cc > limit and t % 32 == 0:
        t //= 2
    if t * cc > limit:
        for cand in range(t, 15, -16):
            if r % cand == 0 and cand * cc <= limit:
                return cand, cc
    return t, cc


def k_pair_add(name, g4, r1, core):
    _, _, r, cc = g4.shape
    tr, tc = _tile2(r, cc, 2 * ELEMWISE_BLOCK)
    fn = lambda av, bv: (av.astype(F32) + bv.astype(F32),)
    return stepk(name, fn, (4, r // tr, cc // tc),
                 [(g4, (None, None, tr, tc), lambda b, i, j, pre: (b, pre[0], i, j)),
                  (r1, (None, tr, tc), lambda b, i, j, pre: (b, i, j))],
                 [((4, r, cc), BF16, (None, tr, tc), lambda b, i, j, pre: (b, i, j))], prefetch=core)[0]


def _adamw(g, w, m, v):
    m2 = ADAM_B1 * m + (1.0 - ADAM_B1) * g
    v2 = ADAM_B2 * v + (1.0 - ADAM_B2) * jnp.square(g)
    m_hat = m2 / (1.0 - ADAM_B1 ** ADAM_STEP)
    v_hat = v2 / (1.0 - ADAM_B2 ** ADAM_STEP)
    delta = -ADAM_LR * (m_hat / (jnp.sqrt(v_hat) + ADAM_EPS) + ADAM_WD * w)
    return g, delta, m2, v2


def k_adamw_shard(name, part, others, w, m, v, layer, chip, fill):
    _, r, cc = part.shape
    tr, tc = _tile2(r, cc)

    def fn(pv, o0, o1, o2, wv, mv, vv, done):
        g = ((pv.astype(F32) + o0.astype(F32)) + o1.astype(F32)) + o2.astype(F32)
        return _adamw(g, wv, mv, vv) + (done,)

    other = lambda k: (others, (None, tr, tc), lambda i, j, pre: (k, i, j))
    state = lambda arr: (arr, (None, tr, tc), lambda i, j, pre: (layer, i, j))
    out = ((w.shape[0], r, cc), F32, (None, tr, tc), lambda i, j, pre: (layer, i, j))
    return stepk(name, fn, (r // tr, cc // tc),
                 [(part, (None, tr, tc), lambda i, j, pre: (pre[0], i, j)), other(0), other(1), other(2),
                  state(w), state(m), state(v)],
                 [out] * 4, carries=[((8, 128), F32)], prefetch=chip, fill=[(f, k) for k, f in enumerate(fill)])


def k_sum8(name, parts):
    _, rows, lanes = parts.shape

    def fn(pv):
        acc = pv[0]
        for d in range(1, N_DEV):
            acc = acc + pv[d]
        return (acc,)

    return stepk(name, fn, (1,), [(parts, parts.shape, lambda i: (0, 0, 0))],
                 [((rows, lanes), F32, (rows, lanes), lambda i: (0, 0))])[0]


def k_adamw_flat(name, g, w, m, v):
    whole = lambda arr: (arr, arr.shape, lambda i: (0, 0))
    out = (g.shape, F32, g.shape, lambda i: (0, 0))
    return stepk(name, _adamw, (1,), [whole(g), whole(w), whole(m), whole(v)], [out] * 4)


WEIGHTS = ("ffn1_norm_pre", "ffn1_w_gate_up", "ffn1_w_down", "ffn1_norm_post", "mix_norm_pre", "w_in", "conv_w", "a_log",
           "dt_bias", "sm_w", "sm_b", "sm_ln_g", "sm_ln_b", "dn_norm_w", "w_out", "mix_norm_post", "xa_norm_pre", "mem_norm",
           "w_xq", "w_xkv", "w_xo", "xa_norm_post", "ffn2_norm_pre", "ffn2_w_gate_up", "ffn2_w_down", "ffn2_norm_post")
BIG = ("ffn1_w_gate_up", "ffn1_w_down", "w_in", "w_out", "w_xq", "w_xkv", "w_xo", "ffn2_w_gate_up", "ffn2_w_down")
ROW_SHARDED = ("ffn1_w_down", "w_out", "w_xq", "w_xo", "ffn2_w_down")
SMALL = tuple(n for n in WEIGHTS if n not in BIG and n != "conv_w")
N_LAYERS = 2
IN_COLS = MAIN_W + 2 * DN_HEADS


BLOCKS = ("f1", "mx", "xa", "f2")
AG_BEFORE = {"f1": ("ffn1_w_gate_up", "ffn1_w_down"), "mx": ("w_in", "w_out"), "xa": ("w_xq", "w_xkv", "w_xo"),
             "f2": ("ffn2_w_gate_up", "ffn2_w_down")}
AG_LARGE = ("f1", "f2")
RS_IN_FLIGHT = 1
RS_LAST = ("ffn1_w_gate_up",)


def _fwd_block(l, sb, h, mem, p, sch):
    tag = f"l{l}{sb}"
    if sb == "f1":
        return ffn_fwd(tag, h, p, "ffn1_norm_pre", "ffn1_norm_post", "ffn1_w_gate_up", "ffn1_w_down", sch)
    if sb == "mx":
        return mix_fwd(tag, h, p, sch)
    if sb == "xa":
        return xa_fwd(tag, h, mem, p, sch)
    return ffn_fwd(tag, h, p, "ffn2_norm_pre", "ffn2_norm_post", "ffn2_w_gate_up", "ffn2_w_down", sch)


class WeightGather:
    def __init__(self, a, w_in_t, me1, chain):
        self.blocks = [(l, sb) for l in range(N_LAYERS) for sb in BLOCKS]
        self.names = {(l, sb): AG_BEFORE[sb] for l, sb in self.blocks}
        first, second = AG_BEFORE[BLOCKS[0]][:1], AG_BEFORE[BLOCKS[0]][1:]
        self.names[0, BLOCKS[0]] = first
        self.names[0, BLOCKS[0] + "b"] = second
        order = [(0, BLOCKS[0]), (0, BLOCKS[0] + "b")] + self.blocks[1:]
        self.going, self.passing, self.tie = {}, {}, []
        for l, sb in order:
            lands = [k_cast_place(f"place_l{l}{n}", *((w_in_t[""][l], 0) if n == "w_in" else (a[n], l)), me1, chain)
                     for n in self.names[l, sb]]
            self.going[l, sb] = xfer_start(f"ag_out_l{l}{sb}", plan_gather_out, 4 * len(lands), [], lands)
            chain = [self.going[l, sb]["token"]]
        self.tie = chain
        self.block = None

    def take(self):
        tie, self.tie = self.tie, []
        return tie

    def _pass_on(self, key, after):
        l, sb = key
        _, got = xfer_wait(f"ag_outw_l{l}{sb}", plan_gather_out, self.going.pop(key), after=after)
        self.passing[key] = xfer_start(f"ag_pass_l{l}{sb}", plan_gather_pass, 3 * len(got), [], got)
        self.tie = [self.passing[key]["token"]]

    def _arrived(self, key, after):
        l, sb = key
        _, full = xfer_wait(f"ag_passw_l{l}{sb}", plan_gather_pass, self.passing.pop(key), after=after)
        self.tie = [full[0]]
        return _big_params(dict(zip(self.names[key], full)))

    def _holding(self, name):
        return next(k for k in list(self.going) + list(self.passing)
                    if k[0] == self.block[0] and name in self.names[k])

    def mid(self, arr, late):
        later = self.blocks[self.blocks.index(self.block) + 1:]
        if later and later[0] in self.going and (later[0][1] in AG_LARGE) == late:
            self._pass_on(later[0], [arr])

    def prefetch(self, p, name, arr):
        if name not in p and self._holding(name) in self.going:
            self._pass_on(self._holding(name), [arr])

    def param(self, p, name, after):
        if name not in p:
            key = self._holding(name)
            if key in self.going:
                self._pass_on(key, list(after))
            p.update(self._arrived(key, list(after) + self.take()))
        return p[name]

    def weights(self, l, sb, after):
        self.block = (l, sb)
        after = list(after) + self.take()
        if (l, sb) in self.going:
            self._pass_on((l, sb), after)
        return self._arrived((l, sb), after)


def _bwd_block(l, sb, dy, saved, p, grads, sch):
    tag = f"l{l}{sb}"
    if sb == "f1":
        return ffn_bwd(tag, dy, saved, p, "ffn1_norm_pre", "ffn1_norm_post", "ffn1_w_gate_up", "ffn1_w_down", grads, sch)
    if sb == "mx":
        return mix_bwd(tag, dy, saved, p, grads, sch)
    if sb == "xa":
        return xa_bwd(tag, dy, saved, p, grads, sch)
    return ffn_bwd(tag, dy, saved, p, "ffn2_norm_pre", "ffn2_norm_post", "ffn2_w_gate_up", "ffn2_w_down", grads, sch)


class GradExchange:
    def __init__(self, a, w_in_t, core, chip):
        self.a, self.w_in_t, self.core, self.chip = a, w_in_t, core, chip
        self.layer = None
        self.pending, self.tie, self.out = [], [], {}
        self.w_in_out = [None] * N_LAYERS

    def take(self):
        tie, self.tie = self.tie, []
        return tie

    def _advance(self, chain, drain, waits=True):
        a = self.a
        between_chips = [grp for grp in self.pending if grp["stage"] == "chips"]
        for grp in self.pending:
            if grp["stage"] == "sibling":
                l, names = grp["l"], grp["names"]
                g4, r1 = xfer_wait(f"rs_sibw_l{l}{names[0]}", plan_scatter_sibling, grp["going"], after=chain)
                parts = [k_pair_add(f"rs_add_l{l}{n}", gg, rr, self.core) for n, gg, rr in zip(names, g4, r1)]
                zones = [lax.empty((3,) + p.shape[1:], BF16) for p in parts]
                grp["going"] = xfer_start(f"rs_chip_l{l}{names[0]}", plan_scatter_chips, 3 * len(names), parts, zones)
                grp["stage"], grp["age"] = "chips", 0
                chain = [grp["going"]["token"]]
        for grp in between_chips if waits else ():
            l, names = grp["l"], grp["names"]
            if drain or grp["age"] >= grp["limit"]:
                parts, r2 = xfer_wait(f"rs_chipw_l{l}{names[0]}", plan_scatter_chips, grp["going"], after=chain)
                for n, part, others in zip(names, parts, r2):
                    if n == "w_in":
                        state = [self.w_in_t[pre][l] for pre in ("", "m_", "v_")]
                        *self.w_in_out[l], done = k_adamw_shard(f"adamw_{n}_l{l}", part, others, *state, 0, self.chip, ())
                    else:
                        *self.out[n], done = k_adamw_shard(f"adamw_{n}_l{l}", part, others, a[n], a["m_" + n],
                                                           a["v_" + n], l, self.chip, self.out.get(n, ()))
                    chain = [done]
                self.pending.remove(grp)
            else:
                grp["age"] += 1
        return chain

    def emit(self, grads, names):
        l = self.layer
        g4 = [_scatter_layout(n, grads[n]) for n in names]
        g4 = [t.reshape(4, 2, *t.shape[1:]) for t in g4]
        zones = [lax.empty((4,) + t.shape[2:], BF16) for t in g4]
        going = xfer_start(f"rs_sib_l{l}{names[0]}", plan_scatter_sibling, 4 * len(names), g4, zones)
        chain = self._advance([going["token"]], drain=False)
        self.pending.append(dict(l=l, names=names, stage="sibling", going=going, age=0, limit=RS_IN_FLIGHT))
        if l == 0 and names[0] in RS_LAST:
            chain = self._advance(chain, drain=False, waits=False)
        self.tie = chain

    def finish(self, last):
        chain = [last]
        while self.pending:
            chain = self._advance(chain, drain=True)
        return self.out


def _big_params(full):
    p = {}
    for n, w in full.items():
        if n in ROW_SHARDED:
            p[n] = w.reshape(1, w.shape[0] * w.shape[1], w.shape[2])
        elif n == "w_in":
            p["w_in_t"] = w.reshape(1, w.shape[0] * w.shape[1], w.shape[2])
        else:
            p[n] = w
    return p


def _small_params(l, conv_full, a):
    p = {"conv_w": conv_full[l]}
    for n in SMALL:
        w = a[n][l]
        if n in ("a_log", "dt_bias"):
            p[n] = w.reshape(DN_HEADS, 1, 1)
        elif n == "sm_b":
            p[n] = w[..., None]
        elif n == "sm_w":
            p[n] = w
        else:
            p[n] = w[None]
    return p


def _scatter_layout(n, g):
    if n in ROW_SHARDED:
        return g.reshape(N_DEV, g.shape[1] // N_DEV, g.shape[2])
    return g


def _seg_rows(size):
    return -(-size // 1024) * 8


def _pack(flat_parts):
    return jnp.concatenate([jnp.pad(f, (0, _seg_rows(f.shape[0]) * 128 - f.shape[0])).reshape(-1, 128) for f in flat_parts])


def kernel(x, mem, ffn1_norm_pre, ffn1_w_gate_up, ffn1_w_down, ffn1_norm_post, mix_norm_pre, w_in, conv_w, a_log, dt_bias, sm_w, sm_b, sm_ln_g, sm_ln_b, dn_norm_w, w_out, mix_norm_post, xa_norm_pre, mem_norm, w_xq, w_xkv, w_xo, xa_norm_post, ffn2_norm_pre, ffn2_w_gate_up, ffn2_w_down, ffn2_norm_post, loss_target, m_ffn1_norm_pre, m_ffn1_w_gate_up, m_ffn1_w_down, m_ffn1_norm_post, m_mix_norm_pre, m_w_in, m_conv_w, m_a_log, m_dt_bias, m_sm_w, m_sm_b, m_sm_ln_g, m_sm_ln_b, m_dn_norm_w, m_w_out, m_mix_norm_post, m_xa_norm_pre, m_mem_norm, m_w_xq, m_w_xkv, m_w_xo, m_xa_norm_post, m_ffn2_norm_pre, m_ffn2_w_gate_up, m_ffn2_w_down, m_ffn2_norm_post, v_ffn1_norm_pre, v_ffn1_w_gate_up, v_ffn1_w_down, v_ffn1_norm_post, v_mix_norm_pre, v_w_in, v_conv_w, v_a_log, v_dt_bias, v_sm_w, v_sm_b, v_sm_ln_g, v_sm_ln_b, v_dn_norm_w, v_w_out, v_mix_norm_post, v_xa_norm_pre, v_mem_norm, v_w_xq, v_w_xkv, v_w_xo, v_xa_norm_post, v_ffn2_norm_pre, v_ffn2_w_gate_up, v_ffn2_w_down, v_ffn2_norm_post):
    a = dict(locals())
    px, py, pc = _place()
    core = jnp.reshape(pc, (1,)).astype(jnp.int32)
    chip = jnp.reshape(2 * px + py, (1,)).astype(jnp.int32)
    me = 4 * px + 2 * py + pc
    xs, mems, tgt = x[0], mem[0], loss_target[0]
    conv_all = all_gather("ag_conv", [conv_w])[0]
    conv_full = conv_all.transpose(1, 2, 0, 3).reshape(N_LAYERS, CONV_W, QKV_W)
    params = [_small_params(l, conv_full, a) for l in range(N_LAYERS)]
    chain = [conv_all]

    w_in_t = {pre: [jnp.transpose(a[pre + "w_in"], (2, 0, 1))[:, l][None] for l in range(N_LAYERS)]
              for pre in ("", "m_", "v_")}
    gather = WeightGather(a, w_in_t, jnp.reshape(me, (1,)).astype(jnp.int32), chain)
    saved = [{} for _ in range(N_LAYERS)]
    h = xs
    for l in range(N_LAYERS):
        for sb in BLOCKS:
            params[l].update(gather.weights(l, sb, [h]))
            h, saved[l][sb] = _fwd_block(l, sb, h, mems, params[l], gather)
    dy, loss = k_loss("loss", h, tgt)
    loss = lax.psum(loss[0, 0], ("x", "y", "c"))

    grads = [{} for _ in range(N_LAYERS)]
    exchange = GradExchange(a, w_in_t, core, chip)
    for l in reversed(range(N_LAYERS)):
        exchange.layer = l
        for sb in reversed(BLOCKS):
            dy = _bwd_block(l, sb, dy, saved[l][sb], params[l], grads[l], exchange)
    big_out = exchange.finish(dy)

    flat = [jnp.concatenate([grads[l][n].reshape(-1) for l in range(N_LAYERS)]) for n in SMALL]
    flat.append(jnp.concatenate([grads[l]["conv_w"].reshape(-1) for l in range(N_LAYERS)]))
    gsum = k_sum8("small_sum", all_gather("ag_small", [_pack(flat)], after=[big_out["ffn1_w_gate_up"][0]])[0])
    rep_rows = sum(_seg_rows(a[n].size) for n in SMALL)
    conv_g = gsum[rep_rows:].reshape(-1)[:N_LAYERS * CONV_W * QKV_W].reshape(N_LAYERS, CONV_W, QKV_W)
    conv_g = lax.dynamic_slice_in_dim(conv_g, me * (QKV_W // N_DEV), QKV_W // N_DEV, axis=2)
    pack_state = lambda pre: _pack([a[pre + n].reshape(-1) for n in SMALL] + [a[pre + "conv_w"].reshape(-1)])
    small_g = jnp.concatenate([gsum[:rep_rows], _pack([conv_g.reshape(-1)])])
    small_out = k_adamw_flat("small_adamw", small_g, pack_state(""), pack_state("m_"), pack_state("v_"))
    outs = {}
    row = 0
    for n in SMALL + ("conv_w",):
        rows = _seg_rows(a[n].size)
        outs[n] = [o[row:row + rows].reshape(-1)[:a[n].size].reshape(a[n].shape) for o in small_out]
        row += rows
    for n in BIG:
        if n == "w_in":
            outs[n] = [jnp.transpose(jnp.concatenate([exchange.w_in_out[l][k] for l in range(N_LAYERS)]), (1, 0, 2))
                       for k in range(4)]
            outs[n] = [jnp.transpose(o, (1, 2, 0)) for o in outs[n]]
        else:
            outs[n] = list(big_out[n])

    return (loss, dy[None], *[outs[n][0] for n in WEIGHTS], *[outs[n][1] for n in WEIGHTS],
            *[outs[n][2] for n in WEIGHTS], *[outs[n][3] for n in WEIGHTS])
```

```python
import functools
import math

import jax
import jax.numpy as jnp
from jax import lax
from jax.experimental import pallas as pl
from jax.experimental.pallas import tpu as pltpu

F32, BF16 = jnp.float32, jnp.bfloat16
HI = lax.Precision.HIGHEST
MESH = pl.DeviceIdType.MESH

N_DEV = 8
NORM_EPS = 1e-6
GM_HEADS, GM_CHUNK = 8, 128
DN_HEADS, DN_CHUNK, DN_DIM = 8, 64, 128
XA_HEADS, XA_DIM = 4, 512
CONV_W = 4
ADAM_LR, ADAM_B1, ADAM_B2, ADAM_EPS, ADAM_WD, ADAM_STEP = 0.001, 0.9, 0.999, 1e-08, 0.01, 10

V7X_VMEM_LIMIT = 56 * 1024 * 1024
WHOLE_TILE_MAX = 1536
WHOLE_K_MAX = 2048
ROW_TILE = 512


def _cparams(n_grid):
    return pltpu.CompilerParams(dimension_semantics=("arbitrary",) * n_grid, vmem_limit_bytes=V7X_VMEM_LIMIT)


def stepk(name, fn, grid, ins, outs, carries=(), prefetch=None, fill=(), after=()):
    n_in, n_out, n_c, n_fill = len(ins), len(outs), len(carries), len(fill)
    n_pre = 0 if prefetch is None else 1

    def body(*refs):
        refs = refs[n_pre:]
        in_refs, refs = refs[:n_in], refs[n_in + n_fill + len(after):]
        out_refs, c_refs = refs[:n_out], refs[n_out:]
        if n_c:
            first = functools.reduce(jnp.logical_and, [pl.program_id(a) == 0 for a in range(len(grid))])

            @pl.when(first)
            def _():
                for r in c_refs:
                    r[...] = jnp.zeros(r.shape, r.dtype)
        res = fn(*[r[...] for r in in_refs], *[r[...] for r in c_refs])
        for r, v in zip(tuple(out_refs) + tuple(c_refs), res):
            r[...] = v.astype(r.dtype)

    in_specs = [pl.BlockSpec(bs, im) for _, bs, im in ins] + [pl.BlockSpec(memory_space=pl.ANY)] * (n_fill + len(after))
    aliases = {n_pre + n_in + i: k for i, (_, k) in enumerate(fill)}
    out_specs = [pl.BlockSpec(bs, im) for _, _, bs, im in outs]
    out_shape = [jax.ShapeDtypeStruct(s, d) for s, d, _, _ in outs]
    for s, d in carries:
        zeros = (0,) * len(s)
        out_specs.append(pl.BlockSpec(s, lambda *a, _z=zeros: _z))
        out_shape.append(jax.ShapeDtypeStruct(s, d))
    args = [a for a, _, _ in ins] + [a for a, _ in fill] + list(after)
    if prefetch is None:
        call = pl.pallas_call(body, name=name, grid=grid, in_specs=in_specs, out_specs=out_specs, out_shape=out_shape,
                              input_output_aliases=aliases, compiler_params=_cparams(len(grid)))
        return call(*args)
    spec = pltpu.PrefetchScalarGridSpec(num_scalar_prefetch=1, grid=grid, in_specs=in_specs, out_specs=out_specs)
    call = pl.pallas_call(body, name=name, grid_spec=spec, out_shape=out_shape, input_output_aliases=aliases,
                          compiler_params=_cparams(len(grid)))
    return call(prefetch, *args)


def _tile(n, pref, align=128):
    if n <= WHOLE_TILE_MAX:
        return n
    t = (pref // align) * align
    while t > align and n % t:
        t -= align
    assert n % t == 0, (n, pref)
    return t


def mm(name, a, b, mode, out_dtype, nbo=1, after=(), b_rows=None, out_rows=None, fill=None):
    nba, ra, ca = a.shape
    nbb, rb, cb = b.shape
    b_row0 = 0
    if b_rows is not None:
        assert nbb == 1 and mode != "tn"
        b_row0, rb = b_rows
    if mode == "nn":
        m, k, n = ra, nba * ca, nbb * cb
        assert rb == k
    elif mode == "nt":
        m, k, n = ra, nba * ca, rb
        assert nbb * cb == k
    else:
        k, m, n = ra, nba * ca, nbb * cb
        assert rb == k
    co = n // nbo
    whole_k = k <= WHOLE_K_MAX and (mode == "tn" or (nba == 1 and (mode == "nn" or nbb == 1)))
    k_pref = k if whole_k else 512
    m_pref = 1024 if whole_k else 2048
    if mode == "nn":
        tm, tk, tn = _tile(m, m_pref, 8), _tile(ca, k_pref), _tile(math.gcd(cb, co), 1024)
        a_spec = pl.BlockSpec((None, tm, tk), lambda i, j, kk: (kk // (ca // tk), i, kk % (ca // tk)))
        assert b_row0 % tk == 0
        b_spec = pl.BlockSpec((None, tk, tn), lambda i, j, kk: (j // (cb // tn), kk + b_row0 // tk, j % (cb // tn)))
        dims = (((1,), (0,)), ((), ()))
    elif mode == "nt":
        tm, tk, tn = _tile(m, m_pref, 8), _tile(math.gcd(ca, cb), k_pref), _tile(co, 1024)
        assert b_row0 % tn == 0
        a_spec = pl.BlockSpec((None, tm, tk), lambda i, j, kk: (kk // (ca // tk), i, kk % (ca // tk)))
        b_spec = pl.BlockSpec((None, tn, tk), lambda i, j, kk: (kk // (cb // tk), j + b_row0 // tn, kk % (cb // tk)))
        dims = (((1,), (1,)), ((), ()))
    else:
        tm, tk, tn = _tile(ca, 1024), _tile(k, k_pref), _tile(math.gcd(cb, co), 1024)
        a_spec = pl.BlockSpec((None, tk, tm), lambda i, j, kk: (i // (ca // tm), kk, i % (ca // tm)))
        b_spec = pl.BlockSpec((None, tk, tn), lambda i, j, kk: (j // (cb // tn), kk, j % (cb // tn)))
        dims = (((0,), (0,)), ((), ()))
    o_row0, o_rows = out_rows if out_rows is not None else (0, m)
    assert o_row0 % tm == 0
    o_spec = pl.BlockSpec((None, tm, tn), lambda i, j, kk: (j // (co // tn), i + o_row0 // tm, j % (co // tn)))
    nk = k // tk
    extra = list(after) + ([] if fill is None else [fill])

    def tile_product(a_ref, b_ref):
        return lax.dot_general(a_ref[...].astype(BF16), b_ref[...].astype(BF16), dims, preferred_element_type=F32)

    def body_one_step(a_ref, b_ref, *rest):
        o_ref = rest[-1]
        o_ref[...] = tile_product(a_ref, b_ref).astype(o_ref.dtype)

    def body(a_ref, b_ref, *rest):
        o_ref, acc_ref = rest[-2:]
        kk = pl.program_id(2)

        @pl.when(kk == 0)
        def _():
            acc_ref[...] = jnp.zeros(acc_ref.shape, F32)

        acc_ref[...] += tile_product(a_ref, b_ref)

        @pl.when(kk == nk - 1)
        def _():
            o_ref[...] = acc_ref[...].astype(o_ref.dtype)

    return pl.pallas_call(
        body_one_step if nk == 1 else body, name=name, grid=(m // tm, n // tn, nk),
        in_specs=[a_spec, b_spec] + [pl.BlockSpec(memory_space=pl.ANY)] * len(extra), out_specs=o_spec,
        out_shape=jax.ShapeDtypeStruct((nbo, o_rows, co), out_dtype),
        input_output_aliases={} if fill is None else {1 + len(extra): 0},
        scratch_shapes=[] if nk == 1 else [pltpu.VMEM((tm, tn), F32)], compiler_params=_cparams(3))(a, b, *extra)


def _rms(x, g):
    return x * lax.rsqrt(jnp.mean(x * x, axis=-1, keepdims=True) + NORM_EPS) * g


def _sigmoid(x):
    return 1.0 / (1.0 + jnp.exp(-x))


def _silu(x):
    return x * _sigmoid(x)


def _gelu(x):
    return 0.5 * x * (1.0 + lax.erf(x * 0.7071067811865476))


def _softplus(x):
    return jnp.maximum(x, 0.0) + jnp.log(1.0 + jnp.exp(-jnp.abs(x)))


def _split(x, n):
    w = x.shape[-1] // n
    return [x[..., h * w:(h + 1) * w] for h in range(n)]


def _sgu(ua, va, smw, smb, lng, lnb):
    c = ua[0].shape[0]
    width = len(va) * va[0].shape[-1]
    vf = [_gelu(v) for v in va]
    mu = sum(jnp.sum(v, axis=-1, keepdims=True) for v in vf) / width
    var = sum(jnp.sum(jnp.square(v - mu), axis=-1, keepdims=True) for v in vf) / width
    r = lax.rsqrt(var + NORM_EPS)
    causal = lax.broadcasted_iota(jnp.int32, (c, c), 0) >= lax.broadcasted_iota(jnp.int32, (c, c), 1)
    outs = []
    for h in range(len(ua)):
        vn = (vf[h] - mu) * r * lng[h] + lnb[h]
        w = jnp.where(causal, smw[h], 0.0)
        mixed = bdot(w[None], vn[None], "nn", 1)[0] + smb[h]
        outs.append(_gelu(ua[h]) * mixed)
    return outs


def _shift_rows(x, k, up):
    rows = x.shape[0]
    row = lax.broadcasted_iota(jnp.int32, x.shape, 0)
    if up:
        return jnp.where(row < rows - k, pltpu.roll(x, rows - k, 0), 0.0)
    return jnp.where(row >= k, pltpu.roll(x, k, 0), 0.0)


@functools.partial(jax.custom_vjp, nondiff_argnums=(1,))
def _delay(x, k):
    return _shift_rows(x, k, False)


def _delay_fwd(x, k):
    return _shift_rows(x, k, False), None


def _delay_bwd(k, _, g):
    return (_shift_rows(g, k, True),)


_delay.defvjp(_delay_fwd, _delay_bwd)


def _conv_silu(x, w0, w1, w2, w3):
    y = w3 * x + w2 * _delay(x, 1) + w1 * _delay(x, 2) + w0 * _delay(x, 3)
    return _silu(y)


_BDOT_DIMS = {"nn": (((2,), (1,)), ((0,), (0,))), "nt": (((2,), (2,)), ((0,), (0,))), "tn": (((1,), (1,)), ((0,), (0,)))}


def _bdot_passes(a, b, kind, passes):
    one = lambda x, y: lax.dot_general(x, y, _BDOT_DIMS[kind], preferred_element_type=F32)
    ah, bh = a.astype(BF16), b.astype(BF16)
    if passes == 1:
        return one(ah, bh)
    al, bl = (a - ah.astype(F32)).astype(BF16), (b - bh.astype(F32)).astype(BF16)
    return one(ah, bh) + (one(ah, bl) + one(al, bh))


@functools.partial(jax.custom_vjp, nondiff_argnums=(2, 3))
def bdot(a, b, kind, passes):
    return _bdot_passes(a, b, kind, passes)


def _bdot_fwd(a, b, kind, passes):
    return _bdot_passes(a, b, kind, passes), (a, b)


def _bdot_bwd(kind, passes, res, g):
    a, b = res
    if kind == "nn":
        return bdot(g, b, "nt", passes), bdot(a, g, "tn", passes)
    if kind == "nt":
        return bdot(g, b, "nn", passes), bdot(g, a, "tn", passes)
    return bdot(b, g, "nt", passes), bdot(a, g, "nn", passes)


bdot.defvjp(_bdot_fwd, _bdot_bwd)


def _tri_inv(a, eye):
    c = a.shape[-1]
    inv = eye - a
    p = bdot(a, a, "nn", 3)
    n_fac = int(math.log2(c)) - 1
    for it in range(n_fac):
        inv = inv + bdot(inv, p, "nn", 3)
        if it < n_fac - 1:
            p = bdot(p, p, "nn", 3)
    return inv


@jax.custom_vjp
def _known_inv(a, x):
    return x


def _known_inv_fwd(a, x):
    return x, x


def _known_inv_bwd(x, g):
    return -bdot(bdot(x, g, "tn", 3), x, "nt", 3), jnp.zeros_like(x)


_known_inv.defvjp(_known_inv_fwd, _known_inv_bwd)


def _dn_step(s, q, k, v, braw, araw, alog, dtb, inv_known=None):
    nh, c, d = q.shape
    row = lax.broadcasted_iota(jnp.int32, (c, c), 0)
    col = lax.broadcasted_iota(jnp.int32, (c, c), 1)
    causal, strict = (row >= col)[None], (row > col)[None]
    lower = (row >= col).astype(F32)
    strict_f = jnp.broadcast_to((row > col).astype(F32)[None], (nh, c, c))
    eye = (row == col).astype(F32)[None]

    qn = q * lax.rsqrt(jnp.sum(q * q, axis=-1, keepdims=True) + NORM_EPS) * (d ** -0.5)
    kn = k * lax.rsqrt(jnp.sum(k * k, axis=-1, keepdims=True) + NORM_EPS)
    beta = _sigmoid(braw)
    g = -jnp.exp(alog) * _softplus(araw + dtb)
    lg = lower[None] * g
    gcum = jnp.sum(lg, axis=-1, keepdims=True)
    diff = bdot(lg, strict_f, "nn", 3)
    decay = jnp.where(causal, jnp.exp(diff), 0.0)
    bcol = jnp.sum(eye * beta, axis=-1, keepdims=True)
    kb = kn * bcol
    a = jnp.where(strict, bdot(kb, kn, "nt", 1) * decay, 0.0)
    inv = _tri_inv(a, eye) if inv_known is None else _known_inv(a, inv_known)
    eg = jnp.exp(gcum)
    u = bdot(inv, v * bcol, "nn", 1)
    w = bdot(inv, kb * eg, "nn", 1)
    qk = jnp.where(causal, bdot(qn, kn, "nt", 1) * decay, 0.0)
    v_new = u - bdot(w, s, "nn", 1)
    o = bdot(qn * eg, s, "nn", 1) + bdot(qk, v_new, "nn", 1)
    glast = jnp.sum(g, axis=-1, keepdims=True)
    kdec = kn * jnp.exp(glast - gcum)
    s_new = s * jnp.exp(glast) + bdot(kdec, v_new, "tn", 1)
    return s_new, o, inv


def _ogate(o, z, w):
    return [_rms(oh, w) * _silu(zh) for oh, zh in zip(o, z)]


def _xattn(q, k, v):
    outs = []
    for qh, kh, vh in zip(q, k, v):
        s = bdot(qh[None], kh[None], "nt", 1)[0] * (qh.shape[-1] ** -0.5)
        s = s - jnp.max(s, axis=-1, keepdims=True)
        e = jnp.exp(s)
        p = e / jnp.sum(e, axis=-1, keepdims=True)
        outs.append(bdot(p[None], vh[None], "nn", 1)[0])
    return outs


def _rows(t):
    return min(ROW_TILE, t)


def k_rms_fwd(name, x, g, after=()):
    t, d = x.shape
    tm = _rows(t)
    return stepk(name, lambda xv, gv: (_rms(xv, gv),), (t // tm,),
                 [(x, (tm, d), lambda i: (i, 0)), (g, (1, d), lambda i: (0, 0))],
                 [((t, d), BF16, (tm, d), lambda i: (i, 0))], after=after)[0]


def k_rms_bwd(name, x, g, dhs, dx_res):
    t, d = x.shape
    tm = _rows(t)
    n = len(dhs)

    def fn(xv, gv, *rest):
        dh = sum(r.astype(F32) for r in rest[:n])
        dxr, dg_c = rest[n], rest[n + 1]
        _, vjp = jax.vjp(_rms, xv, gv)
        dx, dg = vjp(dh)
        return dx + dxr, dg_c + dg

    row = lambda arr: (arr, (tm, d), lambda i: (i, 0))
    return stepk(name, fn, (t // tm,), [row(x), (g, (1, d), lambda i: (0, 0))] + [row(h) for h in dhs] + [row(dx_res)],
                 [((t, d), F32, (tm, d), lambda i: (i, 0))], carries=[((1, d), F32)])


def k_post_fwd(name, x, f, g, scale, after=()):
    t, d = x.shape
    tm = _rows(t)
    row = lambda arr: (arr, (tm, d), lambda i: (i, 0))
    return stepk(name, lambda xv, fv, gv: (xv + scale * _rms(fv, gv),), (t // tm,),
                 [row(x), row(f), (g, (1, d), lambda i: (0, 0))], [((t, d), F32, (tm, d), lambda i: (i, 0))],
                 after=after)[0]


def k_post_bwd(name, f, g, dxo, scale, after=()):
    t, d = f.shape
    tm = _rows(t)

    def fn(fv, gv, dv, dg_c):
        _, vjp = jax.vjp(lambda a, b: scale * _rms(a, b), fv, gv)
        df, dg = vjp(dv)
        return df, dg_c + dg

    row = lambda arr: (arr, (tm, d), lambda i: (i, 0))
    return stepk(name, fn, (t // tm,), [row(f), (g, (1, d), lambda i: (0, 0)), row(dxo)],
                 [((t, d), BF16, (tm, d), lambda i: (i, 0))], carries=[((1, d), F32)], after=after)


def k_swiglu_fwd(name, hu, after=()):
    nb, t, c = hu.shape
    half = nb // 2
    tm = _rows(t)
    fn = lambda gv, uv: (_silu(gv.astype(F32)) * uv.astype(F32),)
    return stepk(name, fn, (half, t // tm),
                 [(hu, (None, tm, c), lambda b, i: (b, i, 0)), (hu, (None, tm, c), lambda b, i: (b + half, i, 0))],
                 [((half, t, c), BF16, (None, tm, c), lambda b, i: (b, i, 0))], after=after)[0]


def k_swiglu_bwd(name, hu, da):
    nb, t, c = hu.shape
    half = nb // 2
    tm = _rows(t)

    def fn(gv, uv, dv):
        gv, uv, dv = gv.astype(F32), uv.astype(F32), dv.astype(F32)
        sg = _sigmoid(gv)
        return (jnp.stack([dv * uv * sg * (1.0 + gv * (1.0 - sg)), dv * gv * sg]),)

    blk = lambda arr, off: (arr, (None, tm, c), lambda b, i: (b + off, i, 0))
    out = stepk(name, fn, (half, t // tm), [blk(hu, 0), blk(hu, half), blk(da, 0)],
                [((2, half, t, c), BF16, (2, None, tm, c), lambda b, i: (0, b, i, 0))])[0]
    return out.reshape(nb, t, c)


def k_loss(name, y, tgt):
    t, d = y.shape
    tm = _rows(t)

    def fn(yv, tv, acc):
        e = yv - tv
        part = jnp.sum(jnp.sum(e * e, axis=-1, keepdims=True), axis=0, keepdims=True)
        return e * (1.0 / d), acc + (0.5 / d) * part

    row = lambda arr: (arr, (tm, d), lambda i: (i, 0))
    return stepk(name, fn, (t // tm,), [row(y), row(tgt)], [((t, d), F32, (tm, d), lambda i: (i, 0))],
                 carries=[((1, 1), F32)])


GM_W = GM_HEADS * 128
QKV_COL0 = 2 * GM_W
QKV_W = 3 * DN_HEADS * DN_DIM
Z_COL0 = QKV_COL0 + QKV_W
MAIN_W = Z_COL0 + DN_HEADS * DN_DIM


def _sgu_ins(proj, p):
    c = GM_CHUNK
    return [(proj, (c, GM_W), lambda i: (i, 0)), (proj, (c, GM_W), lambda i: (i, 1)),
            (p["sm_w"], (GM_HEADS, c, c), lambda i: (0, 0, 0)), (p["sm_b"], (GM_HEADS, c, 1), lambda i: (0, 0, 0)),
            (p["sm_ln_g"], (1, GM_W), lambda i: (0, 0)), (p["sm_ln_b"], (1, GM_W), lambda i: (0, 0))]


def _sgu_lists(uv, vv, sw, sb, lg, lb):
    nh = GM_HEADS
    return (_split(uv, nh), _split(vv, nh), [sw[h] for h in range(nh)], [sb[h] for h in range(nh)],
            _split(lg, nh), _split(lb, nh))


def k_sgu_fwd(name, proj, p):
    t = proj.shape[0]

    def fn(*vals):
        return (jnp.concatenate(_sgu(*_sgu_lists(*vals)), axis=-1),)

    return stepk(name, fn, (t // GM_CHUNK,), _sgu_ins(proj, p),
                 [((2, t, GM_W), BF16, (None, GM_CHUNK, GM_W), lambda i: (0, i, 0))])[0]


def k_sgu_bwd(name, proj, p, dy, dproj):
    t = proj.shape[0]
    c = GM_CHUNK

    def fn(uv, vv, sw, sb, lg, lb, dv, dsw, dsb, dlg, dlb):
        _, vjp = jax.vjp(_sgu, *_sgu_lists(uv, vv, sw, sb, lg, lb))
        gu, gv, gsw, gsb, glg, glb = vjp(_split(dv.astype(F32), GM_HEADS))
        cat = lambda l: jnp.concatenate(l, axis=-1)
        return (cat(gu + gv), dsw + jnp.stack(gsw), dsb + jnp.stack(gsb), dlg + cat(glg), dlb + cat(glb))

    return stepk(name, fn, (t // c,), _sgu_ins(proj, p) + [(dy, (None, c, GM_W), lambda i: (0, i, 0))],
                 [((t, MAIN_W), BF16, (c, 2 * GM_W), lambda i: (i, 0))],
                 carries=[((GM_HEADS, c, c), F32), ((GM_HEADS, c, 1), F32), ((1, GM_W), F32), ((1, GM_W), F32)],
                 fill=[(dproj, 0)])


def _conv_ins(proj, conv_w):
    t = proj.shape[0]
    return [(proj, (t, 128), lambda j: (0, QKV_COL0 // 128 + j)), (conv_w, (CONV_W, 128), lambda j: (0, j))]


def k_conv_fwd(name, proj, conv_w):
    t = proj.shape[0]
    n = QKV_W // 128
    fn = lambda xv, wv: (_conv_silu(xv, *[wv[i:i + 1] for i in range(CONV_W)]),)
    return stepk(name, fn, (n,), _conv_ins(proj, conv_w), [((n, t, 128), F32, (None, t, 128), lambda j: (j, 0, 0))])[0]


def k_conv_bwd(name, proj, conv_w, dq, dk, dv, dproj):
    t = proj.shape[0]
    n = QKV_W // 128
    nh = dq.shape[0]

    def fn(xv, wv, dqv, dkv, dvv):
        part = pl.program_id(0) // nh
        d = jnp.where(part == 0, dqv, jnp.where(part == 1, dkv, dvv))
        _, vjp = jax.vjp(_conv_silu, xv, *[wv[i:i + 1] for i in range(CONV_W)])
        gx, *gw = vjp(d)
        return gx, jnp.concatenate(gw, axis=0)

    head = lambda arr, k: (arr, (None, t, 128), lambda j: (jnp.clip(j - k * nh, 0, nh - 1), 0, 0))
    return stepk(name, fn, (n,), _conv_ins(proj, conv_w) + [head(dq, 0), head(dk, 1), head(dv, 2)],
                 [((t, MAIN_W), BF16, (t, 128), lambda j: (0, QKV_COL0 // 128 + j)),
                  ((CONV_W, QKV_W), F32, (CONV_W, 128), lambda j: (0, j))], fill=[(dproj, 0)])


def _dn_ins(qkv, braw, araw, p, order):
    h, c, d = DN_HEADS, DN_CHUNK, DN_DIM
    qkv_in = lambda part: (qkv, (h, c, d), lambda n: (part, order(n), 0))
    gate_in = lambda arr: (arr, (None, h, 1, c), lambda n: (order(n), 0, 0, 0))
    par_in = lambda arr: (arr, (h, 1, 1), lambda n: (0, 0, 0))
    return [qkv_in(0), qkv_in(1), qkv_in(2), gate_in(braw), gate_in(araw), par_in(p["a_log"]), par_in(p["dt_bias"])]


def k_dn_fwd(name, qkv, braw, araw, p):
    t = qkv.shape[1]
    h, c, d = DN_HEADS, DN_CHUNK, DN_DIM
    nc = t // c

    def fn(q, k, v, b, a, al, dt, s):
        s_new, o, inv = _dn_step(s, q, k, v, b, a, al, dt)
        return o, s, inv, s_new

    o, s_all, inv_all, _ = stepk(name, fn, (nc,), _dn_ins(qkv, braw, araw, p, lambda n: n),
                                 [((h, t, d), F32, (h, c, d), lambda n: (0, n, 0)),
                                  ((nc, h, d, d), F32, (None, h, d, d), lambda n: (n, 0, 0, 0)),
                                  ((nc, h, c, c), F32, (None, h, c, c), lambda n: (n, 0, 0, 0))],
                                 carries=[((h, d, d), F32)])
    return o, s_all, inv_all


def k_dn_bwd(name, qkv, braw, araw, p, s_all, inv_all, do):
    t = qkv.shape[1]
    h, c, d = DN_HEADS, DN_CHUNK, DN_DIM
    nc = t // c
    rev = lambda n: nc - 1 - n

    def fn(q, k, v, b, a, al, dt, s, inv, dov, ds_c, dal_c, ddt_c):
        _, vjp = jax.vjp(lambda *args: _dn_step(*args, inv_known=inv)[:2], s, q, k, v, b, a, al, dt)
        ds, dq, dk, dv, db, da, dal, ddt = vjp((ds_c, dov))
        return dq, dk, dv, db, da, ds, dal_c + dal, ddt_c + ddt

    ins = _dn_ins(qkv, braw, araw, p, rev) + [(s_all, (None, h, d, d), lambda n: (rev(n), 0, 0, 0)),
                                               (inv_all, (None, h, c, c), lambda n: (rev(n), 0, 0, 0)),
                                               (do, (h, c, d), lambda n: (0, rev(n), 0))]
    hd = ((h, t, d), F32, (h, c, d), lambda n: (0, rev(n), 0))
    gate = ((nc, h, 1, c), F32, (None, h, 1, c), lambda n: (rev(n), 0, 0, 0))
    dq, dk, dv, db, da, _, dal, ddt = stepk(name, fn, (nc,), ins, [hd, hd, hd, gate, gate],
                                            carries=[((h, d, d), F32), ((h, 1, 1), F32), ((h, 1, 1), F32)])
    return dq, dk, dv, db, da, dal, ddt


def _ogate_ins(o, proj, p):
    t = o.shape[1]
    tm = _rows(t)
    return tm, [(o, (DN_HEADS, tm, DN_DIM), lambda i: (0, i, 0)), (proj, (tm, GM_W), lambda i: (i, Z_COL0 // GM_W)),
                (p["dn_norm_w"], (1, DN_DIM), lambda i: (0, 0))]


def k_ogate_fwd(name, o, proj, p, y):
    t = o.shape[1]
    tm, ins = _ogate_ins(o, proj, p)

    def fn(ov, zv, wv):
        return (jnp.concatenate(_ogate([ov[h] for h in range(DN_HEADS)], _split(zv, DN_HEADS), wv), axis=-1),)

    return stepk(name, fn, (t // tm,), ins, [((2, t, GM_W), BF16, (None, tm, GM_W), lambda i: (1, i, 0))],
                 fill=[(y, 0)])[0]


def k_ogate_bwd(name, o, proj, p, dy):
    t = o.shape[1]
    tm, ins = _ogate_ins(o, proj, p)

    def fn(ov, zv, wv, dv, dw_c):
        _, vjp = jax.vjp(_ogate, [ov[h] for h in range(DN_HEADS)], _split(zv, DN_HEADS), wv)
        go, gz, gw = vjp(_split(dv.astype(F32), DN_HEADS))
        return jnp.stack(go), jnp.concatenate(gz, axis=-1), dw_c + gw

    return stepk(name, fn, (t // tm,), ins + [(dy, (None, tm, GM_W), lambda i: (1, i, 0))],
                 [((DN_HEADS, t, DN_DIM), F32, (DN_HEADS, tm, DN_DIM), lambda i: (0, i, 0)),
                  ((t, MAIN_W), BF16, (tm, GM_W), lambda i: (i, Z_COL0 // GM_W))], carries=[((1, DN_DIM), F32)])


def _xattn_lists(qv, kvv):
    nh = XA_HEADS
    return (_split(qv.astype(F32), nh), [kvv[h].astype(F32) for h in range(nh)],
            [kvv[nh + h].astype(F32) for h in range(nh)])


def k_xattn_fwd(name, q, kv):
    t, d = q.shape
    tm = _rows(t)
    fn = lambda qv, kvv: (jnp.concatenate(_xattn(*_xattn_lists(qv, kvv)), axis=-1),)
    return stepk(name, fn, (t // tm,), [(q, (tm, d), lambda i: (i, 0)), (kv, kv.shape, lambda i: (0, 0, 0))],
                 [((t, d), BF16, (tm, d), lambda i: (i, 0))])[0]


def k_xattn_bwd(name, q, kv, do):
    t, d = q.shape
    tm = _rows(t)

    def fn(qv, kvv, dv, dkv_c):
        _, vjp = jax.vjp(_xattn, *_xattn_lists(qv, kvv))
        gq, gk, gv = vjp(_split(dv.astype(F32), XA_HEADS))
        return jnp.concatenate(gq, axis=-1), dkv_c + jnp.stack(gk + gv)

    return stepk(name, fn, (t // tm,),
                 [(q, (tm, d), lambda i: (i, 0)), (kv, kv.shape, lambda i: (0, 0, 0)), (do, (None, tm, d), lambda i: (0, i, 0))],
                 [((t, d), BF16, (tm, d), lambda i: (i, 0))], carries=[(kv.shape, F32)])


def ffn_fwd(tag, x, p, pre, post, gu, dn, sch):
    h = k_rms_fwd(f"{tag}_pre", x, p[pre], sch.take())
    hu = mm(f"{tag}_gu", h[None], p[gu], "nn", BF16, nbo=N_DEV)
    sch.prefetch(p, dn, hu)
    a = k_swiglu_fwd(f"{tag}_act", hu, sch.take())
    w_down = sch.param(p, dn, [a])
    sch.mid(a, late=False)
    f = mm(f"{tag}_down", a, w_down, "nn", F32, after=sch.take())[0]
    sch.mid(f, late=True)
    y = k_post_fwd(f"{tag}_post", x, f, p[post], 0.5, sch.take())
    return y, (x, h, hu, a, f)


class NoExchange:
    def take(self):
        return ()

    def emit(self, grads, names):
        pass

    def mid(self, arr, late):
        pass

    def prefetch(self, p, name, arr):
        pass

    def param(self, p, name, after):
        return p[name]


def ffn_bwd(tag, dy, saved, p, pre, post, gu, dn, grads, sch):
    x, h, hu, a, f = saved
    df, grads[post] = k_post_bwd(f"{tag}_post_b", f, p[post], dy, 0.5, sch.take())
    grads[dn] = mm(f"{tag}_down_dw", a, df[None], "tn", BF16)
    sch.emit(grads, (dn,))
    da = mm(f"{tag}_down_dx", df[None], p[dn], "nt", BF16, nbo=N_DEV // 2, after=sch.take())
    dhu = k_swiglu_bwd(f"{tag}_act_b", hu, da)
    grads[gu] = mm(f"{tag}_gu_dw", h[None], dhu, "tn", BF16, nbo=N_DEV)
    sch.emit(grads, (gu,))
    dh = mm(f"{tag}_gu_dx", dhu, p[gu], "nt", BF16, after=sch.take())[0]
    dx, grads[pre] = k_rms_bwd(f"{tag}_pre_b", x, p[pre], [dh], dy)
    return dx


def _to_chunks(a):
    t, h = a.shape
    return a.reshape(t // DN_CHUNK, DN_CHUNK, h).transpose(0, 2, 1).reshape(t // DN_CHUNK, h, 1, DN_CHUNK)


def _from_chunks(a):
    nc, h, _, c = a.shape
    return a.reshape(nc, h, c).transpose(0, 2, 1).reshape(nc * c, h)


def mix_fwd(tag, x, p, sch):
    h = k_rms_fwd(f"{tag}_pre", x, p["mix_norm_pre"], sch.take())
    proj = mm(f"{tag}_in", h[None], p["w_in_t"], "nt", F32, b_rows=(0, MAIN_W))[0]
    sch.mid(proj, late=False)
    small = mm(f"{tag}_in_s", h[None], p["w_in_t"], "nt", F32, after=sch.take(), b_rows=(MAIN_W, 2 * DN_HEADS))[0]
    braw, araw = _to_chunks(small[:, :DN_HEADS]), _to_chunks(small[:, DN_HEADS:2 * DN_HEADS])
    y = k_sgu_fwd(f"{tag}_sgu", proj, p)
    qkv = k_conv_fwd(f"{tag}_conv", proj, p["conv_w"])
    o, s_all, inv_all = k_dn_fwd(f"{tag}_dn", qkv, braw, araw, p)
    y = k_ogate_fwd(f"{tag}_og", o, proj, p, y)
    m = mm(f"{tag}_out", y, p["w_out"], "nn", F32)[0]
    out = k_post_fwd(f"{tag}_post", x, m, p["mix_norm_post"], 1.0)
    return out, (x, h, proj, braw, araw, qkv, o, s_all, inv_all, y, m)


def mix_bwd(tag, dy, saved, p, grads, sch):
    x, h, proj, braw, araw, qkv, o, s_all, inv_all, y, m = saved
    dm, grads["mix_norm_post"] = k_post_bwd(f"{tag}_post_b", m, p["mix_norm_post"], dy, 1.0, sch.take())
    grads["w_out"] = mm(f"{tag}_out_dw", y, dm[None], "tn", BF16)
    sch.emit(grads, ("w_out",))
    dyy = mm(f"{tag}_out_dx", dm[None], p["w_out"], "nt", BF16, nbo=2, after=sch.take())
    do, dproj, grads["dn_norm_w"] = k_ogate_bwd(f"{tag}_og_b", o, proj, p, dyy)
    dq, dk, dv, db, da, grads["a_log"], grads["dt_bias"] = k_dn_bwd(f"{tag}_dn_b", qkv, braw, araw, p, s_all, inv_all, do)
    dproj, grads["conv_w"] = k_conv_bwd(f"{tag}_conv_b", proj, p["conv_w"], dq, dk, dv, dproj)
    dproj, grads["sm_w"], grads["sm_b"], grads["sm_ln_g"], grads["sm_ln_b"] = k_sgu_bwd(f"{tag}_sgu_b", proj, p, dyy, dproj)
    dsmall = jnp.concatenate([_from_chunks(db), _from_chunks(da)], axis=-1)
    g_in = mm(f"{tag}_in_dw", dproj[None], h[None], "tn", BF16, out_rows=(0, IN_COLS))
    g_in = mm(f"{tag}_in_s_dw", dsmall[None], h[None], "tn", BF16, out_rows=(MAIN_W, IN_COLS), fill=g_in)[0]
    grads["w_in"] = g_in.reshape(N_DEV, g_in.shape[0] // N_DEV, g_in.shape[1])
    sch.emit(grads, ("w_in",))
    dh = mm(f"{tag}_in_dx", dproj[None], p["w_in_t"], "nn", BF16, after=sch.take(), b_rows=(0, MAIN_W))[0]
    dh_s = mm(f"{tag}_in_s_dx", dsmall[None], p["w_in_t"], "nn", BF16, b_rows=(MAIN_W, 2 * DN_HEADS))[0]
    dx, grads["mix_norm_pre"] = k_rms_bwd(f"{tag}_pre_b", x, p["mix_norm_pre"], [dh, dh_s], dy)
    return dx


def xa_fwd(tag, x, mem, p, sch):
    after = sch.take()
    hx = k_rms_fwd(f"{tag}_pre", x, p["xa_norm_pre"], after)
    mh = k_rms_fwd(f"{tag}_mem", mem, p["mem_norm"], after)
    q = mm(f"{tag}_q", hx[None], p["w_xq"], "nn", BF16)[0]
    sch.mid(q, late=False)
    kv = mm(f"{tag}_kv", mh[None], p["w_xkv"], "nn", BF16, nbo=N_DEV, after=sch.take())
    o = k_xattn_fwd(f"{tag}_att", q, kv)
    sch.mid(o, late=True)
    c = mm(f"{tag}_o", o[None], p["w_xo"], "nn", F32, after=sch.take())[0]
    out = k_post_fwd(f"{tag}_post", x, c, p["xa_norm_post"], 1.0)
    return out, (x, mem, hx, mh, q, kv, o, c)


def xa_bwd(tag, dy, saved, p, grads, sch):
    x, mem, hx, mh, q, kv, o, c = saved
    dc, grads["xa_norm_post"] = k_post_bwd(f"{tag}_post_b", c, p["xa_norm_post"], dy, 1.0, sch.take())
    grads["w_xo"] = mm(f"{tag}_o_dw", o[None], dc[None], "tn", BF16)
    sch.emit(grads, ("w_xo",))
    do = mm(f"{tag}_o_dx", dc[None], p["w_xo"], "nt", BF16, after=sch.take())
    dq, dkv = k_xattn_bwd(f"{tag}_att_b", q, kv, do)
    dkv16 = dkv.astype(BF16)
    grads["w_xq"] = mm(f"{tag}_q_dw", hx[None], dq[None], "tn", BF16)
    grads["w_xkv"] = mm(f"{tag}_kv_dw", mh[None], dkv16, "tn", BF16, nbo=N_DEV)
    sch.emit(grads, ("w_xq", "w_xkv"))
    dhx = mm(f"{tag}_q_dx", dq[None], p["w_xq"], "nt", BF16, after=sch.take())[0]
    dmh = mm(f"{tag}_kv_dx", dkv16, p["w_xkv"], "nt", BF16)[0]
    _, grads["mem_norm"] = k_rms_bwd(f"{tag}_mem_b", mem, p["mem_norm"], [dmh], jnp.zeros_like(mem))
    dx, grads["xa_norm_pre"] = k_rms_bwd(f"{tag}_pre_b", x, p["xa_norm_pre"], [dhx], dy)
    return dx


def _place():
    return lax.axis_index("x"), lax.axis_index("y"), lax.axis_index("c")


def _other_chips(x, y):
    return [(1 - x, y), (x, 1 - y), (1 - x, 1 - y)]


_ANY = pl.BlockSpec(memory_space=pl.ANY)


def all_gather(name, shards, after=()):
    n, na = len(shards), len(after)

    def body(*refs):
        ins, outs = refs[:n], refs[n + na:2 * n + na]
        send_sems, recv_sems, local_sems = refs[2 * n + na:]
        x, y, c = _place()
        me, sibling = (x, y, c), (x, y, 1 - c)
        chips = _other_chips(x, y)
        idx = lambda px, py, pc: 4 * px + 2 * py + pc

        def copy(a, k, block, to, src=None):
            dst = outs[a].at[idx(*block)]
            return pltpu.make_async_remote_copy(src_ref=dst if src is None else src, dst_ref=dst,
                                                send_sem=send_sems.at[7 * a + k], recv_sem=recv_sems.at[7 * a + k],
                                                device_id=to, device_id_type=MESH)

        mine = [pltpu.make_async_copy(ins[a], outs[a].at[idx(*me)], local_sems.at[a]) for a in range(n)]
        for cp in mine:
            cp.start()
        first = []
        for a in range(n):
            first.append(copy(a, 0, me, sibling, src=ins[a]))
            first += [copy(a, 1 + j, me, (*chip, c), src=ins[a]) for j, chip in enumerate(chips)]
        for cp in first:
            cp.start()
        passed = []
        for a in range(n):
            for j, chip in enumerate(chips):
                copy(a, 1 + j, (*chip, c), me).wait_recv()
                passed.append(copy(a, 4 + j, (*chip, c), sibling))
                passed[-1].start()
        for a in range(n):
            copy(a, 0, sibling, me).wait_recv()
            for j, chip in enumerate(chips):
                copy(a, 4 + j, (*chip, 1 - c), me).wait_recv()
        for cp in first + passed:
            cp.wait_send()
        for cp in mine:
            cp.wait()

    return pl.pallas_call(
        body, name=name, in_specs=[_ANY] * (n + na), out_specs=[_ANY] * n,
        out_shape=[jax.ShapeDtypeStruct((N_DEV,) + s.shape, s.dtype) for s in shards],
        scratch_shapes=[pltpu.SemaphoreType.DMA((7 * n,)), pltpu.SemaphoreType.DMA((7 * n,)),
                        pltpu.SemaphoreType.DMA((n,))])(*shards, *after)


def k_cast_place(name, w, layer, me, after):
    _, r, cc = w.shape
    tr, tc = _tile2(r, cc)
    return stepk(name, lambda v: (v,), (r // tr, cc // tc), [(w, (None, tr, tc), lambda i, j, pre: (layer, i, j))],
                 [((N_DEV, r, cc), BF16, (None, tr, tc), lambda i, j, pre: (pre[0], i, j))], prefetch=me,
                 after=after)[0]


_HBM = pl.BlockSpec(memory_space=pltpu.HBM)
_SEM = pl.BlockSpec(memory_space=pltpu.SEMAPHORE)
_DATAFLOW = pltpu.SideEffectType.DATAFLOW_SIDE_EFFECTING


def xfer_start(name, plan, n_sems, srcs, lands, after=()):
    ns, nl, na = len(srcs), len(lands), len(after)
    bufs = list(srcs) + list(lands)

    def body(*refs):
        send_sems, recv_sems, token = refs[ns + nl + na], refs[ns + nl + na + 1], refs[-1]
        for cp in plan(refs[:ns], refs[ns:ns + nl], send_sems, recv_sems):
            cp.start()
        token[...] = jnp.zeros(token.shape, token.dtype)

    res = pl.pallas_call(
        body, name=name, in_specs=[_HBM] * (ns + nl) + [_ANY] * na,
        out_specs=(_SEM, _SEM, *[_HBM] * (ns + nl), pl.BlockSpec(memory_space=pltpu.VMEM)),
        out_shape=(pltpu.SemaphoreType.DMA((n_sems,)), pltpu.SemaphoreType.DMA((n_sems,)),
                   *[pltpu.HBM(b.shape, b.dtype) for b in bufs], jax.ShapeDtypeStruct((8, 128), F32)),
        input_output_aliases={i: 2 + i for i in range(ns + nl)},
        compiler_params=pltpu.CompilerParams(has_side_effects=_DATAFLOW),
    )(*[pltpu.with_memory_space_constraint(b, pltpu.HBM) for b in bufs], *after)
    return dict(send=res[0], recv=res[1], srcs=list(res[2:2 + ns]), lands=list(res[2 + ns:2 + ns + nl]), token=res[-1])


def xfer_wait(name, plan, started, after=()):
    ns, nl = len(started["srcs"]), len(started["lands"])
    bufs = started["srcs"] + started["lands"]

    def body(*refs):
        for cp in plan(refs[:ns], refs[ns:ns + nl], refs[ns + nl], refs[ns + nl + 1]):
            cp.wait_send()
            cp.wait_recv()

    res = pl.pallas_call(
        body, name=name, in_specs=[_HBM] * (ns + nl) + [_SEM, _SEM] + [_ANY] * len(after),
        out_specs=tuple([_HBM] * (ns + nl)), out_shape=tuple(pltpu.HBM(b.shape, b.dtype) for b in bufs),
        input_output_aliases={i: i for i in range(ns + nl)},
        compiler_params=pltpu.CompilerParams(has_side_effects=_DATAFLOW),
    )(*bufs, started["send"], started["recv"], *after)
    return list(res[:ns]), list(res[ns:])


def _remote(src, dst, send_sems, recv_sems, k, to):
    return pltpu.make_async_remote_copy(src_ref=src, dst_ref=dst, send_sem=send_sems.at[k], recv_sem=recv_sems.at[k],
                                        device_id=to, device_id_type=MESH)


def plan_gather_out(srcs, lands, send_sems, recv_sems):
    x, y, c = _place()
    me = 4 * x + 2 * y + c
    targets = [(x, y, 1 - c)] + [(px, py, c) for px, py in _other_chips(x, y)]
    return [_remote(lands[a].at[me], lands[a].at[me], send_sems, recv_sems, 4 * a + k, to)
            for a in range(len(lands)) for k, to in enumerate(targets)]


def plan_gather_pass(srcs, lands, send_sems, recv_sems):
    x, y, c = _place()
    return [_remote(lands[a].at[4 * px + 2 * py + c], lands[a].at[4 * px + 2 * py + c], send_sems, recv_sems,
                    3 * a + j, (x, y, 1 - c))
            for a in range(len(lands)) for j, (px, py) in enumerate(_other_chips(x, y))]


def plan_scatter_sibling(srcs, lands, send_sems, recv_sems):
    x, y, c = _place()
    return [_remote(srcs[a].at[j, 1 - c], lands[a].at[j], send_sems, recv_sems, 4 * a + j, (x, y, 1 - c))
            for a in range(len(srcs)) for j in range(4)]


def plan_scatter_chips(srcs, lands, send_sems, recv_sems):
    x, y, c = _place()
    return [_remote(srcs[a].at[2 * px + py], lands[a].at[j], send_sems, recv_sems, 3 * a + j, (px, py, c))
            for a in range(len(srcs)) for j, (px, py) in enumerate(_other_chips(x, y))]


ELEMWISE_BLOCK = 512 * 1024


def _tile2(r, cc, limit=ELEMWISE_BLOCK):
    if r % 16:
        tc = cc
        while r * tc > limit and tc % 256 == 0:
            tc //= 2
        return r, tc
    t = r
    while t * cc > limit and t % 32 == 0:
        t //= 2
    if t * cc > limit:
        for cand in range(t, 15, -16):
            if r % cand == 0 and cand * cc <= limit:
                return cand, cc
    return t, cc


def k_pair_add(name, g4, r1, core):
    _, _, r, cc = g4.shape
    tr, tc = _tile2(r, cc, 2 * ELEMWISE_BLOCK)
    fn = lambda av, bv: (av.astype(F32) + bv.astype(F32),)
    return stepk(name, fn, (4, r // tr, cc // tc),
                 [(g4, (None, None, tr, tc), lambda b, i, j, pre: (b, pre[0], i, j)),
                  (r1, (None, tr, tc), lambda b, i, j, pre: (b, i, j))],
                 [((4, r, cc), BF16, (None, tr, tc), lambda b, i, j, pre: (b, i, j))], prefetch=core)[0]


def _adamw(g, w, m, v):
    m2 = ADAM_B1 * m + (1.0 - ADAM_B1) * g
    v2 = ADAM_B2 * v + (1.0 - ADAM_B2) * jnp.square(g)
    m_hat = m2 / (1.0 - ADAM_B1 ** ADAM_STEP)
    v_hat = v2 / (1.0 - ADAM_B2 ** ADAM_STEP)
    delta = -ADAM_LR * (m_hat / (jnp.sqrt(v_hat) + ADAM_EPS) + ADAM_WD * w)
    return g, delta, m2, v2


def k_adamw_shard(name, part, others, w, m, v, layer, chip, fill):
    _, r, cc = part.shape
    tr, tc = _tile2(r, cc)

    def fn(pv, o0, o1, o2, wv, mv, vv, done):
        g = ((pv.astype(F32) + o0.astype(F32)) + o1.astype(F32)) + o2.astype(F32)
        return _adamw(g, wv, mv, vv) + (done,)

    other = lambda k: (others, (None, tr, tc), lambda i, j, pre: (k, i, j))
    state = lambda arr: (arr, (None, tr, tc), lambda i, j, pre: (layer, i, j))
    out = ((w.shape[0], r, cc), F32, (None, tr, tc), lambda i, j, pre: (layer, i, j))
    return stepk(name, fn, (r // tr, cc // tc),
                 [(part, (None, tr, tc), lambda i, j, pre: (pre[0], i, j)), other(0), other(1), other(2),
                  state(w), state(m), state(v)],
                 [out] * 4, carries=[((8, 128), F32)], prefetch=chip, fill=[(f, k) for k, f in enumerate(fill)])


def k_sum8(name, parts):
    _, rows, lanes = parts.shape

    def fn(pv):
        acc = pv[0]
        for d in range(1, N_DEV):
            acc = acc + pv[d]
        return (acc,)

    return stepk(name, fn, (1,), [(parts, parts.shape, lambda i: (0, 0, 0))],
                 [((rows, lanes), F32, (rows, lanes), lambda i: (0, 0))])[0]


def k_adamw_flat(name, g, w, m, v):
    whole = lambda arr: (arr, arr.shape, lambda i: (0, 0))
    out = (g.shape, F32, g.shape, lambda i: (0, 0))
    return stepk(name, _adamw, (1,), [whole(g), whole(w), whole(m), whole(v)], [out] * 4)


WEIGHTS = ("ffn1_norm_pre", "ffn1_w_gate_up", "ffn1_w_down", "ffn1_norm_post", "mix_norm_pre", "w_in", "conv_w", "a_log",
           "dt_bias", "sm_w", "sm_b", "sm_ln_g", "sm_ln_b", "dn_norm_w", "w_out", "mix_norm_post", "xa_norm_pre", "mem_norm",
           "w_xq", "w_xkv", "w_xo", "xa_norm_post", "ffn2_norm_pre", "ffn2_w_gate_up", "ffn2_w_down", "ffn2_norm_post")
BIG = ("ffn1_w_gate_up", "ffn1_w_down", "w_in", "w_out", "w_xq", "w_xkv", "w_xo", "ffn2_w_gate_up", "ffn2_w_down")
ROW_SHARDED = ("ffn1_w_down", "w_out", "w_xq", "w_xo", "ffn2_w_down")
SMALL = tuple(n for n in WEIGHTS if n not in BIG and n != "conv_w")
N_LAYERS = 2
IN_COLS = MAIN_W + 2 * DN_HEADS


BLOCKS = ("f1", "mx", "xa", "f2")
AG_BEFORE = {"f1": ("ffn1_w_gate_up", "ffn1_w_down"), "mx": ("w_in", "w_out"), "xa": ("w_xq", "w_xkv", "w_xo"),
             "f2": ("ffn2_w_gate_up", "ffn2_w_down")}
AG_LARGE = ("f1", "f2")
RS_IN_FLIGHT = 1
RS_LAST = ("ffn1_w_gate_up",)


def _fwd_block(l, sb, h, mem, p, sch):
    tag = f"l{l}{sb}"
    if sb == "f1":
        return ffn_fwd(tag, h, p, "ffn1_norm_pre", "ffn1_norm_post", "ffn1_w_gate_up", "ffn1_w_down", sch)
    if sb == "mx":
        return mix_fwd(tag, h, p, sch)
    if sb == "xa":
        return xa_fwd(tag, h, mem, p, sch)
    return ffn_fwd(tag, h, p, "ffn2_norm_pre", "ffn2_norm_post", "ffn2_w_gate_up", "ffn2_w_down", sch)


class WeightGather:
    def __init__(self, a, w_in_t, me1, chain):
        self.blocks = [(l, sb) for l in range(N_LAYERS) for sb in BLOCKS]
        self.names = {(l, sb): AG_BEFORE[sb] for l, sb in self.blocks}
        first, second = AG_BEFORE[BLOCKS[0]][:1], AG_BEFORE[BLOCKS[0]][1:]
        self.names[0, BLOCKS[0]] = first
        self.names[0, BLOCKS[0] + "b"] = second
        order = [(0, BLOCKS[0]), (0, BLOCKS[0] + "b")] + self.blocks[1:]
        self.going, self.passing, self.tie = {}, {}, []
        for l, sb in order:
            lands = [k_cast_place(f"place_l{l}{n}", *((w_in_t[""][l], 0) if n == "w_in" else (a[n], l)), me1, chain)
                     for n in self.names[l, sb]]
            self.going[l, sb] = xfer_start(f"ag_out_l{l}{sb}", plan_gather_out, 4 * len(lands), [], lands)
            chain = [self.going[l, sb]["token"]]
        self.tie = chain
        self.block = None

    def take(self):
        tie, self.tie = self.tie, []
        return tie

    def _pass_on(self, key, after):
        l, sb = key
        _, got = xfer_wait(f"ag_outw_l{l}{sb}", plan_gather_out, self.going.pop(key), after=after)
        self.passing[key] = xfer_start(f"ag_pass_l{l}{sb}", plan_gather_pass, 3 * len(got), [], got)
        self.tie = [self.passing[key]["token"]]

    def _arrived(self, key, after):
        l, sb = key
        _, full = xfer_wait(f"ag_passw_l{l}{sb}", plan_gather_pass, self.passing.pop(key), after=after)
        self.tie = [full[0]]
        return _big_params(dict(zip(self.names[key], full)))

    def _holding(self, name):
        return next(k for k in list(self.going) + list(self.passing)
                    if k[0] == self.block[0] and name in self.names[k])

    def mid(self, arr, late):
        later = self.blocks[self.blocks.index(self.block) + 1:]
        if later and later[0] in self.going and (later[0][1] in AG_LARGE) == late:
            self._pass_on(later[0], [arr])

    def prefetch(self, p, name, arr):
        if name not in p and self._holding(name) in self.going:
            self._pass_on(self._holding(name), [arr])

    def param(self, p, name, after):
        if name not in p:
            key = self._holding(name)
            if key in self.going:
                self._pass_on(key, list(after))
            p.update(self._arrived(key, list(after) + self.take()))
        return p[name]

    def weights(self, l, sb, after):
        self.block = (l, sb)
        after = list(after) + self.take()
        if (l, sb) in self.going:
            self._pass_on((l, sb), after)
        return self._arrived((l, sb), after)


def _bwd_block(l, sb, dy, saved, p, grads, sch):
    tag = f"l{l}{sb}"
    if sb == "f1":
        return ffn_bwd(tag, dy, saved, p, "ffn1_norm_pre", "ffn1_norm_post", "ffn1_w_gate_up", "ffn1_w_down", grads, sch)
    if sb == "mx":
        return mix_bwd(tag, dy, saved, p, grads, sch)
    if sb == "xa":
        return xa_bwd(tag, dy, saved, p, grads, sch)
    return ffn_bwd(tag, dy, saved, p, "ffn2_norm_pre", "ffn2_norm_post", "ffn2_w_gate_up", "ffn2_w_down", grads, sch)


class GradExchange:
    def __init__(self, a, w_in_t, core, chip):
        self.a, self.w_in_t, self.core, self.chip = a, w_in_t, core, chip
        self.layer = None
        self.pending, self.tie, self.out = [], [], {}
        self.w_in_out = [None] * N_LAYERS

    def take(self):
        tie, self.tie = self.tie, []
        return tie

    def _advance(self, chain, drain, waits=True):
        a = self.a
        between_chips = [grp for grp in self.pending if grp["stage"] == "chips"]
        for grp in self.pending:
            if grp["stage"] == "sibling":
                l, names = grp["l"], grp["names"]
                g4, r1 = xfer_wait(f"rs_sibw_l{l}{names[0]}", plan_scatter_sibling, grp["going"], after=chain)
                parts = [k_pair_add(f"rs_add_l{l}{n}", gg, rr, self.core) for n, gg, rr in zip(names, g4, r1)]
                zones = [lax.empty((3,) + p.shape[1:], BF16) for p in parts]
                grp["going"] = xfer_start(f"rs_chip_l{l}{names[0]}", plan_scatter_chips, 3 * len(names), parts, zones)
                grp["stage"], grp["age"] = "chips", 0
                chain = [grp["going"]["token"]]
        for grp in between_chips if waits else ():
            l, names = grp["l"], grp["names"]
            if drain or grp["age"] >= grp["limit"]:
                parts, r2 = xfer_wait(f"rs_chipw_l{l}{names[0]}", plan_scatter_chips, grp["going"], after=chain)
                for n, part, others in zip(names, parts, r2):
                    if n == "w_in":
                        state = [self.w_in_t[pre][l] for pre in ("", "m_", "v_")]
                        *self.w_in_out[l], done = k_adamw_shard(f"adamw_{n}_l{l}", part, others, *state, 0, self.chip, ())
                    else:
                        *self.out[n], done = k_adamw_shard(f"adamw_{n}_l{l}", part, others, a[n], a["m_" + n],
                                                           a["v_" + n], l, self.chip, self.out.get(n, ()))
                    chain = [done]
                self.pending.remove(grp)
            else:
                grp["age"] += 1
        return chain

    def emit(self, grads, names):
        l = self.layer
        g4 = [_scatter_layout(n, grads[n]) for n in names]
        g4 = [t.reshape(4, 2, *t.shape[1:]) for t in g4]
        zones = [lax.empty((4,) + t.shape[2:], BF16) for t in g4]
        going = xfer_start(f"rs_sib_l{l}{names[0]}", plan_scatter_sibling, 4 * len(names), g4, zones)
        chain = self._advance([going["token"]], drain=False)
        self.pending.append(dict(l=l, names=names, stage="sibling", going=going, age=0, limit=RS_IN_FLIGHT))
        if l == 0 and names[0] in RS_LAST:
            chain = self._advance(chain, drain=False, waits=False)
        self.tie = chain

    def finish(self, last):
        chain = [last]
        while self.pending:
            chain = self._advance(chain, drain=True)
        return self.out


def _big_params(full):
    p = {}
    for n, w in full.items():
        if n in ROW_SHARDED:
            p[n] = w.reshape(1, w.shape[0] * w.shape[1], w.shape[2])
        elif n == "w_in":
            p["w_in_t"] = w.reshape(1, w.shape[0] * w.shape[1], w.shape[2])
        else:
            p[n] = w
    return p


def _small_params(l, conv_full, a):
    p = {"conv_w": conv_full[l]}
    for n in SMALL:
        w = a[n][l]
        if n in ("a_log", "dt_bias"):
            p[n] = w.reshape(DN_HEADS, 1, 1)
        elif n == "sm_b":
            p[n] = w[..., None]
        elif n == "sm_w":
            p[n] = w
        else:
            p[n] = w[None]
    return p


def _scatter_layout(n, g):
    if n in ROW_SHARDED:
        return g.reshape(N_DEV, g.shape[1] // N_DEV, g.shape[2])
    return g


def _seg_rows(size):
    return -(-size // 1024) * 8


def _pack(flat_parts):
    return jnp.concatenate([jnp.pad(f, (0, _seg_rows(f.shape[0]) * 128 - f.shape[0])).reshape(-1, 128) for f in flat_parts])


def kernel(x, mem, ffn1_norm_pre, ffn1_w_gate_up, ffn1_w_down, ffn1_norm_post, mix_norm_pre, w_in, conv_w, a_log, dt_bias, sm_w, sm_b, sm_ln_g, sm_ln_b, dn_norm_w, w_out, mix_norm_post, xa_norm_pre, mem_norm, w_xq, w_xkv, w_xo, xa_norm_post, ffn2_norm_pre, ffn2_w_gate_up, ffn2_w_down, ffn2_norm_post, loss_target, m_ffn1_norm_pre, m_ffn1_w_gate_up, m_ffn1_w_down, m_ffn1_norm_post, m_mix_norm_pre, m_w_in, m_conv_w, m_a_log, m_dt_bias, m_sm_w, m_sm_b, m_sm_ln_g, m_sm_ln_b, m_dn_norm_w, m_w_out, m_mix_norm_post, m_xa_norm_pre, m_mem_norm, m_w_xq, m_w_xkv, m_w_xo, m_xa_norm_post, m_ffn2_norm_pre, m_ffn2_w_gate_up, m_ffn2_w_down, m_ffn2_norm_post, v_ffn1_norm_pre, v_ffn1_w_gate_up, v_ffn1_w_down, v_ffn1_norm_post, v_mix_norm_pre, v_w_in, v_conv_w, v_a_log, v_dt_bias, v_sm_w, v_sm_b, v_sm_ln_g, v_sm_ln_b, v_dn_norm_w, v_w_out, v_mix_norm_post, v_xa_norm_pre, v_mem_norm, v_w_xq, v_w_xkv, v_w_xo, v_xa_norm_post, v_ffn2_norm_pre, v_ffn2_w_gate_up, v_ffn2_w_down, v_ffn2_norm_post):
    a = dict(locals())
    px, py, pc = _place()
    core = jnp.reshape(pc, (1,)).astype(jnp.int32)
    chip = jnp.reshape(2 * px + py, (1,)).astype(jnp.int32)
    me = 4 * px + 2 * py + pc
    xs, mems, tgt = x[0], mem[0], loss_target[0]
    conv_all = all_gather("ag_conv", [conv_w])[0]
    conv_full = conv_all.transpose(1, 2, 0, 3).reshape(N_LAYERS, CONV_W, QKV_W)
    params = [_small_params(l, conv_full, a) for l in range(N_LAYERS)]
    chain = [conv_all]

    w_in_t = {pre: [jnp.transpose(a[pre + "w_in"], (2, 0, 1))[:, l][None] for l in range(N_LAYERS)]
              for pre in ("", "m_", "v_")}
    gather = WeightGather(a, w_in_t, jnp.reshape(me, (1,)).astype(jnp.int32), chain)
    saved = [{} for _ in range(N_LAYERS)]
    h = xs
    for l in range(N_LAYERS):
        for sb in BLOCKS:
            params[l].update(gather.weights(l, sb, [h]))
            h, saved[l][sb] = _fwd_block(l, sb, h, mems, params[l], gather)
    dy, loss = k_loss("loss", h, tgt)
    loss = lax.psum(loss[0, 0], ("x", "y", "c"))

    grads = [{} for _ in range(N_LAYERS)]
    exchange = GradExchange(a, w_in_t, core, chip)
    for l in reversed(range(N_LAYERS)):
        exchange.layer = l
        for sb in reversed(BLOCKS):
            dy = _bwd_block(l, sb, dy, saved[l][sb], params[l], grads[l], exchange)
    big_out = exchange.finish(dy)

    flat = [jnp.concatenate([grads[l][n].reshape(-1) for l in range(N_LAYERS)]) for n in SMALL]
    flat.append(jnp.concatenate([grads[l]["conv_w"].reshape(-1) for l in range(N_LAYERS)]))
    gsum = k_sum8("small_sum", all_gather("ag_small", [_pack(flat)], after=[big_out["ffn1_w_gate_up"][0]])[0])
    rep_rows = sum(_seg_rows(a[n].size) for n in SMALL)
    conv_g = gsum[rep_rows:].reshape(-1)[:N_LAYERS * CONV_W * QKV_W].reshape(N_LAYERS, CONV_W, QKV_W)
    conv_g = lax.dynamic_slice_in_dim(conv_g, me * (QKV_W // N_DEV), QKV_W // N_DEV, axis=2)
    pack_state = lambda pre: _pack([a[pre + n].reshape(-1) for n in SMALL] + [a[pre + "conv_w"].reshape(-1)])
    small_g = jnp.concatenate([gsum[:rep_rows], _pack([conv_g.reshape(-1)])])
    small_out = k_adamw_flat("small_adamw", small_g, pack_state(""), pack_state("m_"), pack_state("v_"))
    outs = {}
    row = 0
    for n in SMALL + ("conv_w",):
        rows = _seg_rows(a[n].size)
        outs[n] = [o[row:row + rows].reshape(-1)[:a[n].size].reshape(a[n].shape) for o in small_out]
        row += rows
    for n in BIG:
        if n == "w_in":
            outs[n] = [jnp.transpose(jnp.concatenate([exchange.w_in_out[l][k] for l in range(N_LAYERS)]), (1, 0, 2))
                       for k in range(4)]
            outs[n] = [jnp.transpose(o, (1, 2, 0)) for o in outs[n]]
        else:
            outs[n] = list(big_out[n])

    return (loss, dy[None], *[outs[n][0] for n in WEIGHTS], *[outs[n][1] for n in WEIGHTS],
            *[outs[n][2] for n in WEIGHTS], *[outs[n][3] for n in WEIGHTS])
```

```python
import functools
import math

import jax
import jax.numpy as jnp
from jax import lax
from jax.experimental import pallas as pl
from jax.experimental.pallas import tpu as pltpu

F32, BF16 = jnp.float32, jnp.bfloat16
HI = lax.Precision.HIGHEST
MESH = pl.DeviceIdType.MESH

N_DEV = 8
NORM_EPS = 1e-6
GM_HEADS, GM_CHUNK = 8, 128
DN_HEADS, DN_CHUNK, DN_DIM = 8, 64, 128
XA_HEADS, XA_DIM = 4, 512
CONV_W = 4
ADAM_LR, ADAM_B1, ADAM_B2, ADAM_EPS, ADAM_WD, ADAM_STEP = 0.001, 0.9, 0.999, 1e-08, 0.01, 10

V7X_VMEM_LIMIT = 56 * 1024 * 1024
WHOLE_TILE_MAX = 1536
WHOLE_K_MAX = 2048
ROW_TILE = 512


def _cparams(n_grid):
    return pltpu.CompilerParams(dimension_semantics=("arbitrary",) * n_grid, vmem_limit_bytes=V7X_VMEM_LIMIT)


def stepk(name, fn, grid, ins, outs, carries=(), prefetch=None, fill=(), after=()):
    n_in, n_out, n_c, n_fill = len(ins), len(outs), len(carries), len(fill)
    n_pre = 0 if prefetch is None else 1

    def body(*refs):
        refs = refs[n_pre:]
        in_refs, refs = refs[:n_in], refs[n_in + n_fill + len(after):]
        out_refs, c_refs = refs[:n_out], refs[n_out:]
        if n_c:
            first = functools.reduce(jnp.logical_and, [pl.program_id(a) == 0 for a in range(len(grid))])

            @pl.when(first)
            def _():
                for r in c_refs:
                    r[...] = jnp.zeros(r.shape, r.dtype)
        res = fn(*[r[...] for r in in_refs], *[r[...] for r in c_refs])
        for r, v in zip(tuple(out_refs) + tuple(c_refs), res):
            r[...] = v.astype(r.dtype)

    in_specs = [pl.BlockSpec(bs, im) for _, bs, im in ins] + [pl.BlockSpec(memory_space=pl.ANY)] * (n_fill + len(after))
    aliases = {n_pre + n_in + i: k for i, (_, k) in enumerate(fill)}
    out_specs = [pl.BlockSpec(bs, im) for _, _, bs, im in outs]
    out_shape = [jax.ShapeDtypeStruct(s, d) for s, d, _, _ in outs]
    for s, d in carries:
        zeros = (0,) * len(s)
        out_specs.append(pl.BlockSpec(s, lambda *a, _z=zeros: _z))
        out_shape.append(jax.ShapeDtypeStruct(s, d))
    args = [a for a, _, _ in ins] + [a for a, _ in fill] + list(after)
    if prefetch is None:
        call = pl.pallas_call(body, name=name, grid=grid, in_specs=in_specs, out_specs=out_specs, out_shape=out_shape,
                              input_output_aliases=aliases, compiler_params=_cparams(len(grid)))
        return call(*args)
    spec = pltpu.PrefetchScalarGridSpec(num_scalar_prefetch=1, grid=grid, in_specs=in_specs, out_specs=out_specs)
    call = pl.pallas_call(body, name=name, grid_spec=spec, out_shape=out_shape, input_output_aliases=aliases,
                          compiler_params=_cparams(len(grid)))
    return call(prefetch, *args)


def _tile(n, pref, align=128):
    if n <= WHOLE_TILE_MAX:
        return n
    t = (pref // align) * align
    while t > align and n % t:
        t -= align
    assert n % t == 0, (n, pref)
    return t


def mm(name, a, b, mode, out_dtype, nbo=1, after=(), b_rows=None, out_rows=None, fill=None):
    nba, ra, ca = a.shape
    nbb, rb, cb = b.shape
    b_row0 = 0
    if b_rows is not None:
        assert nbb == 1 and mode != "tn"
        b_row0, rb = b_rows
    if mode == "nn":
        m, k, n = ra, nba * ca, nbb * cb
        assert rb == k
    elif mode == "nt":
        m, k, n = ra, nba * ca, rb
        assert nbb * cb == k
    else:
        k, m, n = ra, nba * ca, nbb * cb
        assert rb == k
    co = n // nbo
    whole_k = k <= WHOLE_K_MAX and (mode == "tn" or (nba == 1 and (mode == "nn" or nbb == 1)))
    k_pref = k if whole_k else 512
    m_pref = 1024 if whole_k else 2048
    if mode == "nn":
        tm, tk, tn = _tile(m, m_pref, 8), _tile(ca, k_pref), _tile(math.gcd(cb, co), 1024)
        a_spec = pl.BlockSpec((None, tm, tk), lambda i, j, kk: (kk // (ca // tk), i, kk % (ca // tk)))
        assert b_row0 % tk == 0
        b_spec = pl.BlockSpec((None, tk, tn), lambda i, j, kk: (j // (cb // tn), kk + b_row0 // tk, j % (cb // tn)))
        dims = (((1,), (0,)), ((), ()))
    elif mode == "nt":
        tm, tk, tn = _tile(m, m_pref, 8), _tile(math.gcd(ca, cb), k_pref), _tile(co, 1024)
        assert b_row0 % tn == 0
        a_spec = pl.BlockSpec((None, tm, tk), lambda i, j, kk: (kk // (ca // tk), i, kk % (ca // tk)))
        b_spec = pl.BlockSpec((None, tn, tk), lambda i, j, kk: (kk // (cb // tk), j + b_row0 // tn, kk % (cb // tk)))
        dims = (((1,), (1,)), ((), ()))
    else:
        tm, tk, tn = _tile(ca, 1024), _tile(k, k_pref), _tile(math.gcd(cb, co), 1024)
        a_spec = pl.BlockSpec((None, tk, tm), lambda i, j, kk: (i // (ca // tm), kk, i % (ca // tm)))
        b_spec = pl.BlockSpec((None, tk, tn), lambda i, j, kk: (j // (cb // tn), kk, j % (cb // tn)))
        dims = (((0,), (0,)), ((), ()))
    o_row0, o_rows = out_rows if out_rows is not None else (0, m)
    assert o_row0 % tm == 0
    o_spec = pl.BlockSpec((None, tm, tn), lambda i, j, kk: (j // (co // tn), i + o_row0 // tm, j % (co // tn)))
    nk = k // tk
    extra = list(after) + ([] if fill is None else [fill])

    def tile_product(a_ref, b_ref):
        return lax.dot_general(a_ref[...].astype(BF16), b_ref[...].astype(BF16), dims, preferred_element_type=F32)

    def body_one_step(a_ref, b_ref, *rest):
        o_ref = rest[-1]
        o_ref[...] = tile_product(a_ref, b_ref).astype(o_ref.dtype)

    def body(a_ref, b_ref, *rest):
        o_ref, acc_ref = rest[-2:]
        kk = pl.program_id(2)

        @pl.when(kk == 0)
        def _():
            acc_ref[...] = jnp.zeros(acc_ref.shape, F32)

        acc_ref[...] += tile_product(a_ref, b_ref)

        @pl.when(kk == nk - 1)
        def _():
            o_ref[...] = acc_ref[...].astype(o_ref.dtype)

    return pl.pallas_call(
        body_one_step if nk == 1 else body, name=name, grid=(m // tm, n // tn, nk),
        in_specs=[a_spec, b_spec] + [pl.BlockSpec(memory_space=pl.ANY)] * len(extra), out_specs=o_spec,
        out_shape=jax.ShapeDtypeStruct((nbo, o_rows, co), out_dtype),
        input_output_aliases={} if fill is None else {1 + len(extra): 0},
        scratch_shapes=[] if nk == 1 else [pltpu.VMEM((tm, tn), F32)], compiler_params=_cparams(3))(a, b, *extra)


def _rms(x, g):
    return x * lax.rsqrt(jnp.mean(x * x, axis=-1, keepdims=True) + NORM_EPS) * g


def _sigmoid(x):
    return 1.0 / (1.0 + jnp.exp(-x))


def _silu(x):
    return x * _sigmoid(x)


def _gelu(x):
    return 0.5 * x * (1.0 + lax.erf(x * 0.7071067811865476))


def _softplus(x):
    return jnp.maximum(x, 0.0) + jnp.log(1.0 + jnp.exp(-jnp.abs(x)))


def _split(x, n):
    w = x.shape[-1] // n
    return [x[..., h * w:(h + 1) * w] for h in range(n)]


def _sgu(ua, va, smw, smb, lng, lnb):
    c = ua[0].shape[0]
    width = len(va) * va[0].shape[-1]
    vf = [_gelu(v) for v in va]
    mu = sum(jnp.sum(v, axis=-1, keepdims=True) for v in vf) / width
    var = sum(jnp.sum(jnp.square(v - mu), axis=-1, keepdims=True) for v in vf) / width
    r = lax.rsqrt(var + NORM_EPS)
    causal = lax.broadcasted_iota(jnp.int32, (c, c), 0) >= lax.broadcasted_iota(jnp.int32, (c, c), 1)
    outs = []
    for h in range(len(ua)):
        vn = (vf[h] - mu) * r * lng[h] + lnb[h]
        w = jnp.where(causal, smw[h], 0.0)
        mixed = bdot(w[None], vn[None], "nn", 1)[0] + smb[h]
        outs.append(_gelu(ua[h]) * mixed)
    return outs


def _shift_rows(x, k, up):
    rows = x.shape[0]
    row = lax.broadcasted_iota(jnp.int32, x.shape, 0)
    if up:
        return jnp.where(row < rows - k, pltpu.roll(x, rows - k, 0), 0.0)
    return jnp.where(row >= k, pltpu.roll(x, k, 0), 0.0)


@functools.partial(jax.custom_vjp, nondiff_argnums=(1,))
def _delay(x, k):
    return _shift_rows(x, k, False)


def _delay_fwd(x, k):
    return _shift_rows(x, k, False), None


def _delay_bwd(k, _, g):
    return (_shift_rows(g, k, True),)


_delay.defvjp(_delay_fwd, _delay_bwd)


def _conv_silu(x, w0, w1, w2, w3):
    y = w3 * x + w2 * _delay(x, 1) + w1 * _delay(x, 2) + w0 * _delay(x, 3)
    return _silu(y)


_BDOT_DIMS = {"nn": (((2,), (1,)), ((0,), (0,))), "nt": (((2,), (2,)), ((0,), (0,))), "tn": (((1,), (1,)), ((0,), (0,)))}


def _bdot_passes(a, b, kind, passes):
    one = lambda x, y: lax.dot_general(x, y, _BDOT_DIMS[kind], preferred_element_type=F32)
    ah, bh = a.astype(BF16), b.astype(BF16)
    if passes == 1:
        return one(ah, bh)
    al, bl = (a - ah.astype(F32)).astype(BF16), (b - bh.astype(F32)).astype(BF16)
    return one(ah, bh) + (one(ah, bl) + one(al, bh))


@functools.partial(jax.custom_vjp, nondiff_argnums=(2, 3))
def bdot(a, b, kind, passes):
    return _bdot_passes(a, b, kind, passes)


def _bdot_fwd(a, b, kind, passes):
    return _bdot_passes(a, b, kind, passes), (a, b)


def _bdot_bwd(kind, passes, res, g):
    a, b = res
    if kind == "nn":
        return bdot(g, b, "nt", passes), bdot(a, g, "tn", passes)
    if kind == "nt":
        return bdot(g, b, "nn", passes), bdot(g, a, "tn", passes)
    return bdot(b, g, "nt", passes), bdot(a, g, "nn", passes)


bdot.defvjp(_bdot_fwd, _bdot_bwd)


def _tri_inv(a, eye):
    c = a.shape[-1]
    inv = eye - a
    p = bdot(a, a, "nn", 3)
    n_fac = int(math.log2(c)) - 1
    for it in range(n_fac):
        inv = inv + bdot(inv, p, "nn", 3)
        if it < n_fac - 1:
            p = bdot(p, p, "nn", 3)
    return inv


@jax.custom_vjp
def _known_inv(a, x):
    return x


def _known_inv_fwd(a, x):
    return x, x


def _known_inv_bwd(x, g):
    return -bdot(bdot(x, g, "tn", 3), x, "nt", 3), jnp.zeros_like(x)


_known_inv.defvjp(_known_inv_fwd, _known_inv_bwd)


def _dn_step(s, q, k, v, braw, araw, alog, dtb, inv_known=None):
    nh, c, d = q.shape
    row = lax.broadcasted_iota(jnp.int32, (c, c), 0)
    col = lax.broadcasted_iota(jnp.int32, (c, c), 1)
    causal, strict = (row >= col)[None], (row > col)[None]
    lower = (row >= col).astype(F32)
    strict_f = jnp.broadcast_to((row > col).astype(F32)[None], (nh, c, c))
    eye = (row == col).astype(F32)[None]

    qn = q * lax.rsqrt(jnp.sum(q * q, axis=-1, keepdims=True) + NORM_EPS) * (d ** -0.5)
    kn = k * lax.rsqrt(jnp.sum(k * k, axis=-1, keepdims=True) + NORM_EPS)
    beta = _sigmoid(braw)
    g = -jnp.exp(alog) * _softplus(araw + dtb)
    lg = lower[None] * g
    gcum = jnp.sum(lg, axis=-1, keepdims=True)
    diff = bdot(lg, strict_f, "nn", 3)
    decay = jnp.where(causal, jnp.exp(diff), 0.0)
    bcol = jnp.sum(eye * beta, axis=-1, keepdims=True)
    kb = kn * bcol
    a = jnp.where(strict, bdot(kb, kn, "nt", 1) * decay, 0.0)
    inv = _tri_inv(a, eye) if inv_known is None else _known_inv(a, inv_known)
    eg = jnp.exp(gcum)
    u = bdot(inv, v * bcol, "nn", 1)
    w = bdot(inv, kb * eg, "nn", 1)
    qk = jnp.where(causal, bdot(qn, kn, "nt", 1) * decay, 0.0)
    v_new = u - bdot(w, s, "nn", 1)
    o = bdot(qn * eg, s, "nn", 1) + bdot(qk, v_new, "nn", 1)
    glast = jnp.sum(g, axis=-1, keepdims=True)
    kdec = kn * jnp.exp(glast - gcum)
    s_new = s * jnp.exp(glast) + bdot(kdec, v_new, "tn", 1)
    return s_new, o, inv


def _ogate(o, z, w):
    return [_rms(oh, w) * _silu(zh) for oh, zh in zip(o, z)]


def _xattn(q, k, v):
    outs = []
    for qh, kh, vh in zip(q, k, v):
        s = bdot(qh[None], kh[None], "nt", 1)[0] * (qh.shape[-1] ** -0.5)
        s = s - jnp.max(s, axis=-1, keepdims=True)
        e = jnp.exp(s)
        p = e / jnp.sum(e, axis=-1, keepdims=True)
        outs.append(bdot(p[None], vh[None], "nn", 1)[0])
    return outs


def _rows(t):
    return min(ROW_TILE, t)


def k_rms_fwd(name, x, g, after=()):
    t, d = x.shape
    tm = _rows(t)
    return stepk(name, lambda xv, gv: (_rms(xv, gv),), (t // tm,),
                 [(x, (tm, d), lambda i: (i, 0)), (g, (1, d), lambda i: (0, 0))],
                 [((t, d), BF16, (tm, d), lambda i: (i, 0))], after=after)[0]


def k_rms_bwd(name, x, g, dhs, dx_res):
    t, d = x.shape
    tm = _rows(t)
    n = len(dhs)

    def fn(xv, gv, *rest):
        dh = sum(r.astype(F32) for r in rest[:n])
        dxr, dg_c = rest[n], rest[n + 1]
        _, vjp = jax.vjp(_rms, xv, gv)
        dx, dg = vjp(dh)
        return dx + dxr, dg_c + dg

    row = lambda arr: (arr, (tm, d), lambda i: (i, 0))
    return stepk(name, fn, (t // tm,), [row(x), (g, (1, d), lambda i: (0, 0))] + [row(h) for h in dhs] + [row(dx_res)],
                 [((t, d), F32, (tm, d), lambda i: (i, 0))], carries=[((1, d), F32)])


def k_post_fwd(name, x, f, g, scale, after=()):
    t, d = x.shape
    tm = _rows(t)
    row = lambda arr: (arr, (tm, d), lambda i: (i, 0))
    return stepk(name, lambda xv, fv, gv: (xv + scale * _rms(fv, gv),), (t // tm,),
                 [row(x), row(f), (g, (1, d), lambda i: (0, 0))], [((t, d), F32, (tm, d), lambda i: (i, 0))],
                 after=after)[0]


def k_post_bwd(name, f, g, dxo, scale, after=()):
    t, d = f.shape
    tm = _rows(t)

    def fn(fv, gv, dv, dg_c):
        _, vjp = jax.vjp(lambda a, b: scale * _rms(a, b), fv, gv)
        df, dg = vjp(dv)
        return df, dg_c + dg

    row = lambda arr: (arr, (tm, d), lambda i: (i, 0))
    return stepk(name, fn, (t // tm,), [row(f), (g, (1, d), lambda i: (0, 0)), row(dxo)],
                 [((t, d), BF16, (tm, d), lambda i: (i, 0))], carries=[((1, d), F32)], after=after)


def k_swiglu_fwd(name, hu, after=()):
    nb, t, c = hu.shape
    half = nb // 2
    tm = _rows(t)
    fn = lambda gv, uv: (_silu(gv.astype(F32)) * uv.astype(F32),)
    return stepk(name, fn, (half, t // tm),
                 [(hu, (None, tm, c), lambda b, i: (b, i, 0)), (hu, (None, tm, c), lambda b, i: (b + half, i, 0))],
                 [((half, t, c), BF16, (None, tm, c), lambda b, i: (b, i, 0))], after=after)[0]


def k_down_dx_act(name, df, w_down, hu, after=()):
    _, t, d = df.shape
    nb, _, c = hu.shape
    half = nb // 2
    tm = min(ROW_TILE, t)

    def body(df_ref, w_ref, g_ref, u_ref, *rest):
        o_ref = rest[-1]
        da = lax.dot_general(df_ref[...].astype(BF16), w_ref[...].astype(BF16), (((1,), (1,)), ((), ())),
                             preferred_element_type=F32)
        gv, uv = g_ref[...].astype(F32), u_ref[...].astype(F32)
        sg = _sigmoid(gv)
        o_ref[0] = (da * uv * sg * (1.0 + gv * (1.0 - sg))).astype(o_ref.dtype)
        o_ref[1] = (da * gv * sg).astype(o_ref.dtype)

    blk = lambda off: pl.BlockSpec((None, tm, c), lambda i, b: (b + off, i, 0))
    out = pl.pallas_call(
        body, name=name, grid=(t // tm, half),
        in_specs=[pl.BlockSpec((None, tm, d), lambda i, b: (0, i, 0)), pl.BlockSpec((None, c, d), lambda i, b: (0, b, 0)),
                  blk(0), blk(half)] + [pl.BlockSpec(memory_space=pl.ANY)] * len(after),
        out_specs=pl.BlockSpec((2, None, tm, c), lambda i, b: (0, b, i, 0)),
        out_shape=jax.ShapeDtypeStruct((2, half, t, c), BF16), compiler_params=_cparams(2))(df, w_down, hu, hu, *after)
    return out.reshape(nb, t, c)


def k_swiglu_bwd(name, hu, da):
    nb, t, c = hu.shape
    half = nb // 2
    tm = _rows(t)

    def fn(gv, uv, dv):
        gv, uv, dv = gv.astype(F32), uv.astype(F32), dv.astype(F32)
        sg = _sigmoid(gv)
        return (jnp.stack([dv * uv * sg * (1.0 + gv * (1.0 - sg)), dv * gv * sg]),)

    blk = lambda arr, off: (arr, (None, tm, c), lambda b, i: (b + off, i, 0))
    out = stepk(name, fn, (half, t // tm), [blk(hu, 0), blk(hu, half), blk(da, 0)],
                [((2, half, t, c), BF16, (2, None, tm, c), lambda b, i: (0, b, i, 0))])[0]
    return out.reshape(nb, t, c)


def k_loss(name, y, tgt):
    t, d = y.shape
    tm = _rows(t)

    def fn(yv, tv, acc):
        e = yv - tv
        part = jnp.sum(jnp.sum(e * e, axis=-1, keepdims=True), axis=0, keepdims=True)
        return e * (1.0 / d), acc + (0.5 / d) * part

    row = lambda arr: (arr, (tm, d), lambda i: (i, 0))
    return stepk(name, fn, (t // tm,), [row(y), row(tgt)], [((t, d), F32, (tm, d), lambda i: (i, 0))],
                 carries=[((1, 1), F32)])


GM_W = GM_HEADS * 128
QKV_COL0 = 2 * GM_W
QKV_W = 3 * DN_HEADS * DN_DIM
Z_COL0 = QKV_COL0 + QKV_W
MAIN_W = Z_COL0 + DN_HEADS * DN_DIM


def _sgu_ins(proj, p):
    c = GM_CHUNK
    return [(proj, (c, GM_W), lambda i: (i, 0)), (proj, (c, GM_W), lambda i: (i, 1)),
            (p["sm_w"], (GM_HEADS, c, c), lambda i: (0, 0, 0)), (p["sm_b"], (GM_HEADS, c, 1), lambda i: (0, 0, 0)),
            (p["sm_ln_g"], (1, GM_W), lambda i: (0, 0)), (p["sm_ln_b"], (1, GM_W), lambda i: (0, 0))]


def _sgu_lists(uv, vv, sw, sb, lg, lb):
    nh = GM_HEADS
    return (_split(uv, nh), _split(vv, nh), [sw[h] for h in range(nh)], [sb[h] for h in range(nh)],
            _split(lg, nh), _split(lb, nh))


def k_sgu_fwd(name, proj, p):
    t = proj.shape[0]

    def fn(*vals):
        return (jnp.concatenate(_sgu(*_sgu_lists(*vals)), axis=-1),)

    return stepk(name, fn, (t // GM_CHUNK,), _sgu_ins(proj, p),
                 [((2, t, GM_W), BF16, (None, GM_CHUNK, GM_W), lambda i: (0, i, 0))])[0]


def k_sgu_bwd(name, proj, p, dy, dproj):
    t = proj.shape[0]
    c = GM_CHUNK

    def fn(uv, vv, sw, sb, lg, lb, dv, dsw, dsb, dlg, dlb):
        _, vjp = jax.vjp(_sgu, *_sgu_lists(uv, vv, sw, sb, lg, lb))
        gu, gv, gsw, gsb, glg, glb = vjp(_split(dv.astype(F32), GM_HEADS))
        cat = lambda l: jnp.concatenate(l, axis=-1)
        return (cat(gu + gv), dsw + jnp.stack(gsw), dsb + jnp.stack(gsb), dlg + cat(glg), dlb + cat(glb))

    return stepk(name, fn, (t // c,), _sgu_ins(proj, p) + [(dy, (None, c, GM_W), lambda i: (0, i, 0))],
                 [((t, MAIN_W), BF16, (c, 2 * GM_W), lambda i: (i, 0))],
                 carries=[((GM_HEADS, c, c), F32), ((GM_HEADS, c, 1), F32), ((1, GM_W), F32), ((1, GM_W), F32)],
                 fill=[(dproj, 0)])


def _conv_ins(proj, conv_w):
    t = proj.shape[0]
    return [(proj, (t, 128), lambda j: (0, QKV_COL0 // 128 + j)), (conv_w, (CONV_W, 128), lambda j: (0, j))]


def k_conv_fwd(name, proj, conv_w):
    t = proj.shape[0]
    n = QKV_W // 128
    fn = lambda xv, wv: (_conv_silu(xv, *[wv[i:i + 1] for i in range(CONV_W)]),)
    return stepk(name, fn, (n,), _conv_ins(proj, conv_w), [((n, t, 128), F32, (None, t, 128), lambda j: (j, 0, 0))])[0]


def k_conv_bwd(name, proj, conv_w, dq, dk, dv, dproj):
    t = proj.shape[0]
    n = QKV_W // 128
    nh = dq.shape[0]

    def fn(xv, wv, dqv, dkv, dvv):
        part = pl.program_id(0) // nh
        d = jnp.where(part == 0, dqv, jnp.where(part == 1, dkv, dvv))
        _, vjp = jax.vjp(_conv_silu, xv, *[wv[i:i + 1] for i in range(CONV_W)])
        gx, *gw = vjp(d)
        return gx, jnp.concatenate(gw, axis=0)

    head = lambda arr, k: (arr, (None, t, 128), lambda j: (jnp.clip(j - k * nh, 0, nh - 1), 0, 0))
    return stepk(name, fn, (n,), _conv_ins(proj, conv_w) + [head(dq, 0), head(dk, 1), head(dv, 2)],
                 [((t, MAIN_W), BF16, (t, 128), lambda j: (0, QKV_COL0 // 128 + j)),
                  ((CONV_W, QKV_W), F32, (CONV_W, 128), lambda j: (0, j))], fill=[(dproj, 0)])


def _dn_ins(qkv, braw, araw, p, order):
    h, c, d = DN_HEADS, DN_CHUNK, DN_DIM
    qkv_in = lambda part: (qkv, (h, c, d), lambda n: (part, order(n), 0))
    gate_in = lambda arr: (arr, (None, h, 1, c), lambda n: (order(n), 0, 0, 0))
    par_in = lambda arr: (arr, (h, 1, 1), lambda n: (0, 0, 0))
    return [qkv_in(0), qkv_in(1), qkv_in(2), gate_in(braw), gate_in(araw), par_in(p["a_log"]), par_in(p["dt_bias"])]


def k_dn_fwd(name, qkv, braw, araw, p):
    t = qkv.shape[1]
    h, c, d = DN_HEADS, DN_CHUNK, DN_DIM
    nc = t // c

    def fn(q, k, v, b, a, al, dt, s):
        s_new, o, inv = _dn_step(s, q, k, v, b, a, al, dt)
        return o, s, inv, s_new

    o, s_all, inv_all, _ = stepk(name, fn, (nc,), _dn_ins(qkv, braw, araw, p, lambda n: n),
                                 [((h, t, d), F32, (h, c, d), lambda n: (0, n, 0)),
                                  ((nc, h, d, d), F32, (None, h, d, d), lambda n: (n, 0, 0, 0)),
                                  ((nc, h, c, c), F32, (None, h, c, c), lambda n: (n, 0, 0, 0))],
                                 carries=[((h, d, d), F32)])
    return o, s_all, inv_all


def k_dn_bwd(name, qkv, braw, araw, p, s_all, inv_all, do):
    t = qkv.shape[1]
    h, c, d = DN_HEADS, DN_CHUNK, DN_DIM
    nc = t // c
    rev = lambda n: nc - 1 - n

    def fn(q, k, v, b, a, al, dt, s, inv, dov, ds_c, dal_c, ddt_c):
        _, vjp = jax.vjp(lambda *args: _dn_step(*args, inv_known=inv)[:2], s, q, k, v, b, a, al, dt)
        ds, dq, dk, dv, db, da, dal, ddt = vjp((ds_c, dov))
        return dq, dk, dv, db, da, ds, dal_c + dal, ddt_c + ddt

    ins = _dn_ins(qkv, braw, araw, p, rev) + [(s_all, (None, h, d, d), lambda n: (rev(n), 0, 0, 0)),
                                               (inv_all, (None, h, c, c), lambda n: (rev(n), 0, 0, 0)),
                                               (do, (h, c, d), lambda n: (0, rev(n), 0))]
    hd = ((h, t, d), F32, (h, c, d), lambda n: (0, rev(n), 0))
    gate = ((nc, h, 1, c), F32, (None, h, 1, c), lambda n: (rev(n), 0, 0, 0))
    dq, dk, dv, db, da, _, dal, ddt = stepk(name, fn, (nc,), ins, [hd, hd, hd, gate, gate],
                                            carries=[((h, d, d), F32), ((h, 1, 1), F32), ((h, 1, 1), F32)])
    return dq, dk, dv, db, da, dal, ddt


def _ogate_ins(o, proj, p):
    t = o.shape[1]
    tm = _rows(t)
    return tm, [(o, (DN_HEADS, tm, DN_DIM), lambda i: (0, i, 0)), (proj, (tm, GM_W), lambda i: (i, Z_COL0 // GM_W)),
                (p["dn_norm_w"], (1, DN_DIM), lambda i: (0, 0))]


def k_ogate_fwd(name, o, proj, p, y):
    t = o.shape[1]
    tm, ins = _ogate_ins(o, proj, p)

    def fn(ov, zv, wv):
        return (jnp.concatenate(_ogate([ov[h] for h in range(DN_HEADS)], _split(zv, DN_HEADS), wv), axis=-1),)

    return stepk(name, fn, (t // tm,), ins, [((2, t, GM_W), BF16, (None, tm, GM_W), lambda i: (1, i, 0))],
                 fill=[(y, 0)])[0]


def k_ogate_bwd(name, o, proj, p, dy):
    t = o.shape[1]
    tm, ins = _ogate_ins(o, proj, p)

    def fn(ov, zv, wv, dv, dw_c):
        _, vjp = jax.vjp(_ogate, [ov[h] for h in range(DN_HEADS)], _split(zv, DN_HEADS), wv)
        go, gz, gw = vjp(_split(dv.astype(F32), DN_HEADS))
        return jnp.stack(go), jnp.concatenate(gz, axis=-1), dw_c + gw

    return stepk(name, fn, (t // tm,), ins + [(dy, (None, tm, GM_W), lambda i: (1, i, 0))],
                 [((DN_HEADS, t, DN_DIM), F32, (DN_HEADS, tm, DN_DIM), lambda i: (0, i, 0)),
                  ((t, MAIN_W), BF16, (tm, GM_W), lambda i: (i, Z_COL0 // GM_W))], carries=[((1, DN_DIM), F32)])


def _xattn_lists(qv, kvv):
    nh = XA_HEADS
    return (_split(qv.astype(F32), nh), [kvv[h].astype(F32) for h in range(nh)],
            [kvv[nh + h].astype(F32) for h in range(nh)])


def k_xattn_fwd(name, q, kv):
    t, d = q.shape
    tm = _rows(t)
    fn = lambda qv, kvv: (jnp.concatenate(_xattn(*_xattn_lists(qv, kvv)), axis=-1),)
    return stepk(name, fn, (t // tm,), [(q, (tm, d), lambda i: (i, 0)), (kv, kv.shape, lambda i: (0, 0, 0))],
                 [((t, d), BF16, (tm, d), lambda i: (i, 0))])[0]


def k_xattn_bwd(name, q, kv, do):
    t, d = q.shape
    tm = _rows(t)

    def fn(qv, kvv, dv, dkv_c):
        _, vjp = jax.vjp(_xattn, *_xattn_lists(qv, kvv))
        gq, gk, gv = vjp(_split(dv.astype(F32), XA_HEADS))
        return jnp.concatenate(gq, axis=-1), dkv_c + jnp.stack(gk + gv)

    return stepk(name, fn, (t // tm,),
                 [(q, (tm, d), lambda i: (i, 0)), (kv, kv.shape, lambda i: (0, 0, 0)), (do, (None, tm, d), lambda i: (0, i, 0))],
                 [((t, d), BF16, (tm, d), lambda i: (i, 0))], carries=[(kv.shape, F32)])


def ffn_fwd(tag, x, p, pre, post, gu, dn, sch):
    h = k_rms_fwd(f"{tag}_pre", x, p[pre], sch.take())
    hu = mm(f"{tag}_gu", h[None], p[gu], "nn", BF16, nbo=N_DEV)
    sch.prefetch(p, dn, hu)
    a = k_swiglu_fwd(f"{tag}_act", hu, sch.take())
    w_down = sch.param(p, dn, [a])
    sch.mid(a, late=False)
    f = mm(f"{tag}_down", a, w_down, "nn", F32, after=sch.take())[0]
    sch.mid(f, late=True)
    y = k_post_fwd(f"{tag}_post", x, f, p[post], 0.5, sch.take())
    return y, (x, h, hu, a, f)


class NoExchange:
    def take(self):
        return ()

    def emit(self, grads, names):
        pass

    def mid(self, arr, late):
        pass

    def prefetch(self, p, name, arr):
        pass

    def param(self, p, name, after):
        return p[name]


def ffn_bwd(tag, dy, saved, p, pre, post, gu, dn, grads, sch):
    x, h, hu, a, f = saved
    df, grads[post] = k_post_bwd(f"{tag}_post_b", f, p[post], dy, 0.5, sch.take())
    grads[dn] = mm(f"{tag}_down_dw", a, df[None], "tn", BF16)
    sch.emit(grads, (dn,))
    dhu = k_down_dx_act(f"{tag}_down_dx", df[None], p[dn], hu, sch.take())
    grads[gu] = mm(f"{tag}_gu_dw", h[None], dhu, "tn", BF16, nbo=N_DEV)
    sch.emit(grads, (gu,))
    dh = mm(f"{tag}_gu_dx", dhu, p[gu], "nt", BF16, after=sch.take())[0]
    dx, grads[pre] = k_rms_bwd(f"{tag}_pre_b", x, p[pre], [dh], dy)
    return dx


def _to_chunks(a):
    t, h = a.shape
    return a.reshape(t // DN_CHUNK, DN_CHUNK, h).transpose(0, 2, 1).reshape(t // DN_CHUNK, h, 1, DN_CHUNK)


def _from_chunks(a):
    nc, h, _, c = a.shape
    return a.reshape(nc, h, c).transpose(0, 2, 1).reshape(nc * c, h)


def mix_fwd(tag, x, p, sch):
    h = k_rms_fwd(f"{tag}_pre", x, p["mix_norm_pre"], sch.take())
    proj = mm(f"{tag}_in", h[None], p["w_in_t"], "nt", F32, b_rows=(0, MAIN_W))[0]
    sch.mid(proj, late=False)
    small = mm(f"{tag}_in_s", h[None], p["w_in_t"], "nt", F32, after=sch.take(), b_rows=(MAIN_W, 2 * DN_HEADS))[0]
    braw, araw = _to_chunks(small[:, :DN_HEADS]), _to_chunks(small[:, DN_HEADS:2 * DN_HEADS])
    y = k_sgu_fwd(f"{tag}_sgu", proj, p)
    qkv = k_conv_fwd(f"{tag}_conv", proj, p["conv_w"])
    o, s_all, inv_all = k_dn_fwd(f"{tag}_dn", qkv, braw, araw, p)
    y = k_ogate_fwd(f"{tag}_og", o, proj, p, y)
    m = mm(f"{tag}_out", y, p["w_out"], "nn", F32)[0]
    out = k_post_fwd(f"{tag}_post", x, m, p["mix_norm_post"], 1.0)
    return out, (x, h, proj, braw, araw, qkv, o, s_all, inv_all, y, m)


def mix_bwd(tag, dy, saved, p, grads, sch):
    x, h, proj, braw, araw, qkv, o, s_all, inv_all, y, m = saved
    dm, grads["mix_norm_post"] = k_post_bwd(f"{tag}_post_b", m, p["mix_norm_post"], dy, 1.0, sch.take())
    grads["w_out"] = mm(f"{tag}_out_dw", y, dm[None], "tn", BF16)
    sch.emit(grads, ("w_out",))
    dyy = mm(f"{tag}_out_dx", dm[None], p["w_out"], "nt", BF16, nbo=2, after=sch.take())
    do, dproj, grads["dn_norm_w"] = k_ogate_bwd(f"{tag}_og_b", o, proj, p, dyy)
    dq, dk, dv, db, da, grads["a_log"], grads["dt_bias"] = k_dn_bwd(f"{tag}_dn_b", qkv, braw, araw, p, s_all, inv_all, do)
    dproj, grads["conv_w"] = k_conv_bwd(f"{tag}_conv_b", proj, p["conv_w"], dq, dk, dv, dproj)
    dproj, grads["sm_w"], grads["sm_b"], grads["sm_ln_g"], grads["sm_ln_b"] = k_sgu_bwd(f"{tag}_sgu_b", proj, p, dyy, dproj)
    dsmall = jnp.concatenate([_from_chunks(db), _from_chunks(da)], axis=-1)
    g_in = mm(f"{tag}_in_dw", dproj[None], h[None], "tn", BF16, out_rows=(0, IN_COLS))
    g_in = mm(f"{tag}_in_s_dw", dsmall[None], h[None], "tn", BF16, out_rows=(MAIN_W, IN_COLS), fill=g_in)[0]
    grads["w_in"] = g_in.reshape(N_DEV, g_in.shape[0] // N_DEV, g_in.shape[1])
    sch.emit(grads, ("w_in",))
    dh = mm(f"{tag}_in_dx", dproj[None], p["w_in_t"], "nn", BF16, after=sch.take(), b_rows=(0, MAIN_W))[0]
    dh_s = mm(f"{tag}_in_s_dx", dsmall[None], p["w_in_t"], "nn", BF16, b_rows=(MAIN_W, 2 * DN_HEADS))[0]
    dx, grads["mix_norm_pre"] = k_rms_bwd(f"{tag}_pre_b", x, p["mix_norm_pre"], [dh, dh_s], dy)
    return dx


def xa_fwd(tag, x, mem, p, sch):
    after = sch.take()
    hx = k_rms_fwd(f"{tag}_pre", x, p["xa_norm_pre"], after)
    mh = k_rms_fwd(f"{tag}_mem", mem, p["mem_norm"], after)
    q = mm(f"{tag}_q", hx[None], p["w_xq"], "nn", BF16)[0]
    sch.mid(q, late=False)
    kv = mm(f"{tag}_kv", mh[None], p["w_xkv"], "nn", BF16, nbo=N_DEV, after=sch.take())
    o = k_xattn_fwd(f"{tag}_att", q, kv)
    sch.mid(o, late=True)
    c = mm(f"{tag}_o", o[None], p["w_xo"], "nn", F32, after=sch.take())[0]
    out = k_post_fwd(f"{tag}_post", x, c, p["xa_norm_post"], 1.0)
    return out, (x, mem, hx, mh, q, kv, o, c)


def xa_bwd(tag, dy, saved, p, grads, sch):
    x, mem, hx, mh, q, kv, o, c = saved
    dc, grads["xa_norm_post"] = k_post_bwd(f"{tag}_post_b", c, p["xa_norm_post"], dy, 1.0, sch.take())
    grads["w_xo"] = mm(f"{tag}_o_dw", o[None], dc[None], "tn", BF16)
    sch.emit(grads, ("w_xo",))
    do = mm(f"{tag}_o_dx", dc[None], p["w_xo"], "nt", BF16, after=sch.take())
    dq, dkv = k_xattn_bwd(f"{tag}_att_b", q, kv, do)
    dkv16 = dkv.astype(BF16)
    grads["w_xq"] = mm(f"{tag}_q_dw", hx[None], dq[None], "tn", BF16)
    grads["w_xkv"] = mm(f"{tag}_kv_dw", mh[None], dkv16, "tn", BF16, nbo=N_DEV)
    sch.emit(grads, ("w_xq", "w_xkv"))
    dhx = mm(f"{tag}_q_dx", dq[None], p["w_xq"], "nt", BF16, after=sch.take())[0]
    dmh = mm(f"{tag}_kv_dx", dkv16, p["w_xkv"], "nt", BF16)[0]
    _, grads["mem_norm"] = k_rms_bwd(f"{tag}_mem_b", mem, p["mem_norm"], [dmh], jnp.zeros_like(mem))
    dx, grads["xa_norm_pre"] = k_rms_bwd(f"{tag}_pre_b", x, p["xa_norm_pre"], [dhx], dy)
    return dx


def _place():
    return lax.axis_index("x"), lax.axis_index("y"), lax.axis_index("c")


def _other_chips(x, y):
    return [(1 - x, y), (x, 1 - y), (1 - x, 1 - y)]


_ANY = pl.BlockSpec(memory_space=pl.ANY)


def all_gather(name, shards, after=()):
    n, na = len(shards), len(after)

    def body(*refs):
        ins, outs = refs[:n], refs[n + na:2 * n + na]
        send_sems, recv_sems, local_sems = refs[2 * n + na:]
        x, y, c = _place()
        me, sibling = (x, y, c), (x, y, 1 - c)
        chips = _other_chips(x, y)
        idx = lambda px, py, pc: 4 * px + 2 * py + pc

        def copy(a, k, block, to, src=None):
            dst = outs[a].at[idx(*block)]
            return pltpu.make_async_remote_copy(src_ref=dst if src is None else src, dst_ref=dst,
                                                send_sem=send_sems.at[7 * a + k], recv_sem=recv_sems.at[7 * a + k],
                                                device_id=to, device_id_type=MESH)

        mine = [pltpu.make_async_copy(ins[a], outs[a].at[idx(*me)], local_sems.at[a]) for a in range(n)]
        for cp in mine:
            cp.start()
        first = []
        for a in range(n):
            first.append(copy(a, 0, me, sibling, src=ins[a]))
            first += [copy(a, 1 + j, me, (*chip, c), src=ins[a]) for j, chip in enumerate(chips)]
        for cp in first:
            cp.start()
        passed = []
        for a in range(n):
            for j, chip in enumerate(chips):
                copy(a, 1 + j, (*chip, c), me).wait_recv()
                passed.append(copy(a, 4 + j, (*chip, c), sibling))
                passed[-1].start()
        for a in range(n):
            copy(a, 0, sibling, me).wait_recv()
            for j, chip in enumerate(chips):
                copy(a, 4 + j, (*chip, 1 - c), me).wait_recv()
        for cp in first + passed:
            cp.wait_send()
        for cp in mine:
            cp.wait()

    return pl.pallas_call(
        body, name=name, in_specs=[_ANY] * (n + na), out_specs=[_ANY] * n,
        out_shape=[jax.ShapeDtypeStruct((N_DEV,) + s.shape, s.dtype) for s in shards],
        scratch_shapes=[pltpu.SemaphoreType.DMA((7 * n,)), pltpu.SemaphoreType.DMA((7 * n,)),
                        pltpu.SemaphoreType.DMA((n,))])(*shards, *after)


def k_cast_place(name, w, layer, me, after):
    _, r, cc = w.shape
    tr, tc = _tile2(r, cc)
    return stepk(name, lambda v: (v,), (r // tr, cc // tc), [(w, (None, tr, tc), lambda i, j, pre: (layer, i, j))],
                 [((N_DEV, r, cc), BF16, (None, tr, tc), lambda i, j, pre: (pre[0], i, j))], prefetch=me,
                 after=after)[0]


_HBM = pl.BlockSpec(memory_space=pltpu.HBM)
_SEM = pl.BlockSpec(memory_space=pltpu.SEMAPHORE)
_DATAFLOW = pltpu.SideEffectType.DATAFLOW_SIDE_EFFECTING


def xfer_start(name, plan, n_sems, srcs, lands, after=()):
    ns, nl, na = len(srcs), len(lands), len(after)
    bufs = list(srcs) + list(lands)

    def body(*refs):
        send_sems, recv_sems, token = refs[ns + nl + na], refs[ns + nl + na + 1], refs[-1]
        for cp in plan(refs[:ns], refs[ns:ns + nl], send_sems, recv_sems):
            cp.start()
        token[...] = jnp.zeros(token.shape, token.dtype)

    res = pl.pallas_call(
        body, name=name, in_specs=[_HBM] * (ns + nl) + [_ANY] * na,
        out_specs=(_SEM, _SEM, *[_HBM] * (ns + nl), pl.BlockSpec(memory_space=pltpu.VMEM)),
        out_shape=(pltpu.SemaphoreType.DMA((n_sems,)), pltpu.SemaphoreType.DMA((n_sems,)),
                   *[pltpu.HBM(b.shape, b.dtype) for b in bufs], jax.ShapeDtypeStruct((8, 128), F32)),
        input_output_aliases={i: 2 + i for i in range(ns + nl)},
        compiler_params=pltpu.CompilerParams(has_side_effects=_DATAFLOW),
    )(*[pltpu.with_memory_space_constraint(b, pltpu.HBM) for b in bufs], *after)
    return dict(send=res[0], recv=res[1], srcs=list(res[2:2 + ns]), lands=list(res[2 + ns:2 + ns + nl]), token=res[-1])


def xfer_wait(name, plan, started, after=()):
    ns, nl = len(started["srcs"]), len(started["lands"])
    bufs = started["srcs"] + started["lands"]

    def body(*refs):
        for cp in plan(refs[:ns], refs[ns:ns + nl], refs[ns + nl], refs[ns + nl + 1]):
            cp.wait_send()
            cp.wait_recv()

    res = pl.pallas_call(
        body, name=name, in_specs=[_HBM] * (ns + nl) + [_SEM, _SEM] + [_ANY] * len(after),
        out_specs=tuple([_HBM] * (ns + nl)), out_shape=tuple(pltpu.HBM(b.shape, b.dtype) for b in bufs),
        input_output_aliases={i: i for i in range(ns + nl)},
        compiler_params=pltpu.CompilerParams(has_side_effects=_DATAFLOW),
    )(*bufs, started["send"], started["recv"], *after)
    return list(res[:ns]), list(res[ns:])


def _remote(src, dst, send_sems, recv_sems, k, to):
    return pltpu.make_async_remote_copy(src_ref=src, dst_ref=dst, send_sem=send_sems.at[k], recv_sem=recv_sems.at[k],
                                        device_id=to, device_id_type=MESH)


def plan_gather_out(srcs, lands, send_sems, recv_sems):
    x, y, c = _place()
    me = 4 * x + 2 * y + c
    targets = [(x, y, 1 - c)] + [(px, py, c) for px, py in _other_chips(x, y)]
    return [_remote(lands[a].at[me], lands[a].at[me], send_sems, recv_sems, 4 * a + k, to)
            for a in range(len(lands)) for k, to in enumerate(targets)]


def plan_gather_pass(srcs, lands, send_sems, recv_sems):
    x, y, c = _place()
    return [_remote(lands[a].at[4 * px + 2 * py + c], lands[a].at[4 * px + 2 * py + c], send_sems, recv_sems,
                    3 * a + j, (x, y, 1 - c))
            for a in range(len(lands)) for j, (px, py) in enumerate(_other_chips(x, y))]


def plan_scatter_sibling(srcs, lands, send_sems, recv_sems):
    x, y, c = _place()
    return [_remote(srcs[a].at[j, 1 - c], lands[a].at[j], send_sems, recv_sems, 4 * a + j, (x, y, 1 - c))
            for a in range(len(srcs)) for j in range(4)]


def plan_scatter_chips(srcs, lands, send_sems, recv_sems):
    x, y, c = _place()
    return [_remote(srcs[a].at[2 * px + py], lands[a].at[j], send_sems, recv_sems, 3 * a + j, (px, py, c))
            for a in range(len(srcs)) for j, (px, py) in enumerate(_other_chips(x, y))]


ELEMWISE_BLOCK = 512 * 1024


def _tile2(r, cc, limit=ELEMWISE_BLOCK):
    if r % 16:
        tc = cc
        while r * tc > limit and tc % 256 == 0:
            tc //= 2
        return r, tc
    t = r
    while t * cc > limit and t % 32 == 0:
        t //= 2
    if t * cc > limit:
        for cand in range(t, 15, -16):
            if r % cand == 0 and cand * cc <= limit:
                return cand, cc
    return t, cc


def k_pair_add(name, g4, r1, core):
    _, _, r, cc = g4.shape
    tr, tc = _tile2(r, cc, 2 * ELEMWISE_BLOCK)
    fn = lambda av, bv: (av.astype(F32) + bv.astype(F32),)
    return stepk(name, fn, (4, r // tr, cc // tc),
                 [(g4, (None, None, tr, tc), lambda b, i, j, pre: (b, pre[0], i, j)),
                  (r1, (None, tr, tc), lambda b, i, j, pre: (b, i, j))],
                 [((4, r, cc), BF16, (None, tr, tc), lambda b, i, j, pre: (b, i, j))], prefetch=core)[0]


def _adamw(g, w, m, v):
    m2 = ADAM_B1 * m + (1.0 - ADAM_B1) * g
    v2 = ADAM_B2 * v + (1.0 - ADAM_B2) * jnp.square(g)
    m_hat = m2 / (1.0 - ADAM_B1 ** ADAM_STEP)
    v_hat = v2 / (1.0 - ADAM_B2 ** ADAM_STEP)
    delta = -ADAM_LR * (m_hat / (jnp.sqrt(v_hat) + ADAM_EPS) + ADAM_WD * w)
    return g, delta, m2, v2


def k_adamw_shard(name, part, others, w, m, v, layer, chip, fill):
    _, r, cc = part.shape
    tr, tc = _tile2(r, cc)

    def fn(pv, o0, o1, o2, wv, mv, vv, done):
        g = ((pv.astype(F32) + o0.astype(F32)) + o1.astype(F32)) + o2.astype(F32)
        return _adamw(g, wv, mv, vv) + (done,)

    other = lambda k: (others, (None, tr, tc), lambda i, j, pre: (k, i, j))
    state = lambda arr: (arr, (None, tr, tc), lambda i, j, pre: (layer, i, j))
    out = ((w.shape[0], r, cc), F32, (None, tr, tc), lambda i, j, pre: (layer, i, j))
    return stepk(name, fn, (r // tr, cc // tc),
                 [(part, (None, tr, tc), lambda i, j, pre: (pre[0], i, j)), other(0), other(1), other(2),
                  state(w), state(m), state(v)],
                 [out] * 4, carries=[((8, 128), F32)], prefetch=chip, fill=[(f, k) for k, f in enumerate(fill)])


def k_sum8(name, parts):
    _, rows, lanes = parts.shape

    def fn(pv):
        acc = pv[0]
        for d in range(1, N_DEV):
            acc = acc + pv[d]
        return (acc,)

    return stepk(name, fn, (1,), [(parts, parts.shape, lambda i: (0, 0, 0))],
                 [((rows, lanes), F32, (rows, lanes), lambda i: (0, 0))])[0]


def k_adamw_flat(name, g, w, m, v):
    whole = lambda arr: (arr, arr.shape, lambda i: (0, 0))
    out = (g.shape, F32, g.shape, lambda i: (0, 0))
    return stepk(name, _adamw, (1,), [whole(g), whole(w), whole(m), whole(v)], [out] * 4)


WEIGHTS = ("ffn1_norm_pre", "ffn1_w_gate_up", "ffn1_w_down", "ffn1_norm_post", "mix_norm_pre", "w_in", "conv_w", "a_log",
           "dt_bias", "sm_w", "sm_b", "sm_ln_g", "sm_ln_b", "dn_norm_w", "w_out", "mix_norm_post", "xa_norm_pre", "mem_norm",
           "w_xq", "w_xkv", "w_xo", "xa_norm_post", "ffn2_norm_pre", "ffn2_w_gate_up", "ffn2_w_down", "ffn2_norm_post")
BIG = ("ffn1_w_gate_up", "ffn1_w_down", "w_in", "w_out", "w_xq", "w_xkv", "w_xo", "ffn2_w_gate_up", "ffn2_w_down")
ROW_SHARDED = ("ffn1_w_down", "w_out", "w_xq", "w_xo", "ffn2_w_down")
SMALL = tuple(n for n in WEIGHTS if n not in BIG and n != "conv_w")
N_LAYERS = 2
IN_COLS = MAIN_W + 2 * DN_HEADS


BLOCKS = ("f1", "mx", "xa", "f2")
AG_BEFORE = {"f1": ("ffn1_w_gate_up", "ffn1_w_down"), "mx": ("w_in", "w_out"), "xa": ("w_xq", "w_xkv", "w_xo"),
             "f2": ("ffn2_w_gate_up", "ffn2_w_down")}
AG_LARGE = ("f1", "f2")
RS_IN_FLIGHT = 1
RS_LAST = ("ffn1_w_gate_up",)


def _fwd_block(l, sb, h, mem, p, sch):
    tag = f"l{l}{sb}"
    if sb == "f1":
        return ffn_fwd(tag, h, p, "ffn1_norm_pre", "ffn1_norm_post", "ffn1_w_gate_up", "ffn1_w_down", sch)
    if sb == "mx":
        return mix_fwd(tag, h, p, sch)
    if sb == "xa":
        return xa_fwd(tag, h, mem, p, sch)
    return ffn_fwd(tag, h, p, "ffn2_norm_pre", "ffn2_norm_post", "ffn2_w_gate_up", "ffn2_w_down", sch)


class WeightGather:
    def __init__(self, a, w_in_t, me1, chain):
        self.blocks = [(l, sb) for l in range(N_LAYERS) for sb in BLOCKS]
        self.names = {(l, sb): AG_BEFORE[sb] for l, sb in self.blocks}
        first, second = AG_BEFORE[BLOCKS[0]][:1], AG_BEFORE[BLOCKS[0]][1:]
        self.names[0, BLOCKS[0]] = first
        self.names[0, BLOCKS[0] + "b"] = second
        order = [(0, BLOCKS[0]), (0, BLOCKS[0] + "b")] + self.blocks[1:]
        self.going, self.passing, self.tie = {}, {}, []
        for l, sb in order:
            lands = [k_cast_place(f"place_l{l}{n}", *((w_in_t[""][l], 0) if n == "w_in" else (a[n], l)), me1, chain)
                     for n in self.names[l, sb]]
            self.going[l, sb] = xfer_start(f"ag_out_l{l}{sb}", plan_gather_out, 4 * len(lands), [], lands)
            chain = [self.going[l, sb]["token"]]
        self.tie = chain
        self.block = None

    def take(self):
        tie, self.tie = self.tie, []
        return tie

    def _pass_on(self, key, after):
        l, sb = key
        _, got = xfer_wait(f"ag_outw_l{l}{sb}", plan_gather_out, self.going.pop(key), after=after)
        self.passing[key] = xfer_start(f"ag_pass_l{l}{sb}", plan_gather_pass, 3 * len(got), [], got)
        self.tie = [self.passing[key]["token"]]

    def _arrived(self, key, after):
        l, sb = key
        _, full = xfer_wait(f"ag_passw_l{l}{sb}", plan_gather_pass, self.passing.pop(key), after=after)
        self.tie = [full[0]]
        return _big_params(dict(zip(self.names[key], full)))

    def _holding(self, name):
        return next(k for k in list(self.going) + list(self.passing)
                    if k[0] == self.block[0] and name in self.names[k])

    def mid(self, arr, late):
        later = self.blocks[self.blocks.index(self.block) + 1:]
        if later and later[0] in self.going and (later[0][1] in AG_LARGE) == late:
            self._pass_on(later[0], [arr])

    def prefetch(self, p, name, arr):
        if name not in p and self._holding(name) in self.going:
            self._pass_on(self._holding(name), [arr])

    def param(self, p, name, after):
        if name not in p:
            key = self._holding(name)
            if key in self.going:
                self._pass_on(key, list(after))
            p.update(self._arrived(key, list(after) + self.take()))
        return p[name]

    def weights(self, l, sb, after):
        self.block = (l, sb)
        after = list(after) + self.take()
        if (l, sb) in self.going:
            self._pass_on((l, sb), after)
        return self._arrived((l, sb), after)


def _bwd_block(l, sb, dy, saved, p, grads, sch):
    tag = f"l{l}{sb}"
    if sb == "f1":
        return ffn_bwd(tag, dy, saved, p, "ffn1_norm_pre", "ffn1_norm_post", "ffn1_w_gate_up", "ffn1_w_down", grads, sch)
    if sb == "mx":
        return mix_bwd(tag, dy, saved, p, grads, sch)
    if sb == "xa":
        return xa_bwd(tag, dy, saved, p, grads, sch)
    return ffn_bwd(tag, dy, saved, p, "ffn2_norm_pre", "ffn2_norm_post", "ffn2_w_gate_up", "ffn2_w_down", grads, sch)


class GradExchange:
    def __init__(self, a, w_in_t, core, chip):
        self.a, self.w_in_t, self.core, self.chip = a, w_in_t, core, chip
        self.layer = None
        self.pending, self.tie, self.out = [], [], {}
        self.w_in_out = [None] * N_LAYERS

    def take(self):
        tie, self.tie = self.tie, []
        return tie

    def _advance(self, chain, drain, waits=True):
        a = self.a
        between_chips = [grp for grp in self.pending if grp["stage"] == "chips"]
        for grp in self.pending:
            if grp["stage"] == "sibling":
                l, names = grp["l"], grp["names"]
                g4, r1 = xfer_wait(f"rs_sibw_l{l}{names[0]}", plan_scatter_sibling, grp["going"], after=chain)
                parts = [k_pair_add(f"rs_add_l{l}{n}", gg, rr, self.core) for n, gg, rr in zip(names, g4, r1)]
                zones = [lax.empty((3,) + p.shape[1:], BF16) for p in parts]
                grp["going"] = xfer_start(f"rs_chip_l{l}{names[0]}", plan_scatter_chips, 3 * len(names), parts, zones)
                grp["stage"], grp["age"] = "chips", 0
                chain = [grp["going"]["token"]]
        for grp in between_chips if waits else ():
            l, names = grp["l"], grp["names"]
            if drain or grp["age"] >= grp["limit"]:
                parts, r2 = xfer_wait(f"rs_chipw_l{l}{names[0]}", plan_scatter_chips, grp["going"], after=chain)
                for n, part, others in zip(names, parts, r2):
                    if n == "w_in":
                        state = [self.w_in_t[pre][l] for pre in ("", "m_", "v_")]
                        *self.w_in_out[l], done = k_adamw_shard(f"adamw_{n}_l{l}", part, others, *state, 0, self.chip, ())
                    else:
                        *self.out[n], done = k_adamw_shard(f"adamw_{n}_l{l}", part, others, a[n], a["m_" + n],
                                                           a["v_" + n], l, self.chip, self.out.get(n, ()))
                    chain = [done]
                self.pending.remove(grp)
            else:
                grp["age"] += 1
        return chain

    def emit(self, grads, names):
        l = self.layer
        g4 = [_scatter_layout(n, grads[n]) for n in names]
        g4 = [t.reshape(4, 2, *t.shape[1:]) for t in g4]
        zones = [lax.empty((4,) + t.shape[2:], BF16) for t in g4]
        going = xfer_start(f"rs_sib_l{l}{names[0]}", plan_scatter_sibling, 4 * len(names), g4, zones)
        chain = self._advance([going["token"]], drain=False)
        self.pending.append(dict(l=l, names=names, stage="sibling", going=going, age=0, limit=RS_IN_FLIGHT))
        if l == 0 and names[0] in RS_LAST:
            chain = self._advance(chain, drain=False, waits=False)
        self.tie = chain

    def finish(self, last):
        chain = [last]
        while self.pending:
            chain = self._advance(chain, drain=True)
        return self.out


def _big_params(full):
    p = {}
    for n, w in full.items():
        if n in ROW_SHARDED:
            p[n] = w.reshape(1, w.shape[0] * w.shape[1], w.shape[2])
        elif n == "w_in":
            p["w_in_t"] = w.reshape(1, w.shape[0] * w.shape[1], w.shape[2])
        else:
            p[n] = w
    return p


def _small_params(l, conv_full, a):
    p = {"conv_w": conv_full[l]}
    for n in SMALL:
        w = a[n][l]
        if n in ("a_log", "dt_bias"):
            p[n] = w.reshape(DN_HEADS, 1, 1)
        elif n == "sm_b":
            p[n] = w[..., None]
        elif n == "sm_w":
            p[n] = w
        else:
            p[n] = w[None]
    return p


def _scatter_layout(n, g):
    if n in ROW_SHARDED:
        return g.reshape(N_DEV, g.shape[1] // N_DEV, g.shape[2])
    return g


def _seg_rows(size):
    return -(-size // 1024) * 8


def _pack(flat_parts):
    return jnp.concatenate([jnp.pad(f, (0, _seg_rows(f.shape[0]) * 128 - f.shape[0])).reshape(-1, 128) for f in flat_parts])


def kernel(x, mem, ffn1_norm_pre, ffn1_w_gate_up, ffn1_w_down, ffn1_norm_post, mix_norm_pre, w_in, conv_w, a_log, dt_bias, sm_w, sm_b, sm_ln_g, sm_ln_b, dn_norm_w, w_out, mix_norm_post, xa_norm_pre, mem_norm, w_xq, w_xkv, w_xo, xa_norm_post, ffn2_norm_pre, ffn2_w_gate_up, ffn2_w_down, ffn2_norm_post, loss_target, m_ffn1_norm_pre, m_ffn1_w_gate_up, m_ffn1_w_down, m_ffn1_norm_post, m_mix_norm_pre, m_w_in, m_conv_w, m_a_log, m_dt_bias, m_sm_w, m_sm_b, m_sm_ln_g, m_sm_ln_b, m_dn_norm_w, m_w_out, m_mix_norm_post, m_xa_norm_pre, m_mem_norm, m_w_xq, m_w_xkv, m_w_xo, m_xa_norm_post, m_ffn2_norm_pre, m_ffn2_w_gate_up, m_ffn2_w_down, m_ffn2_norm_post, v_ffn1_norm_pre, v_ffn1_w_gate_up, v_ffn1_w_down, v_ffn1_norm_post, v_mix_norm_pre, v_w_in, v_conv_w, v_a_log, v_dt_bias, v_sm_w, v_sm_b, v_sm_ln_g, v_sm_ln_b, v_dn_norm_w, v_w_out, v_mix_norm_post, v_xa_norm_pre, v_mem_norm, v_w_xq, v_w_xkv, v_w_xo, v_xa_norm_post, v_ffn2_norm_pre, v_ffn2_w_gate_up, v_ffn2_w_down, v_ffn2_norm_post):
    a = dict(locals())
    px, py, pc = _place()
    core = jnp.reshape(pc, (1,)).astype(jnp.int32)
    chip = jnp.reshape(2 * px + py, (1,)).astype(jnp.int32)
    me = 4 * px + 2 * py + pc
    xs, mems, tgt = x[0], mem[0], loss_target[0]
    conv_all = all_gather("ag_conv", [conv_w])[0]
    conv_full = conv_all.transpose(1, 2, 0, 3).reshape(N_LAYERS, CONV_W, QKV_W)
    params = [_small_params(l, conv_full, a) for l in range(N_LAYERS)]
    chain = [conv_all]

    w_in_t = {pre: [jnp.transpose(a[pre + "w_in"], (2, 0, 1))[:, l][None] for l in range(N_LAYERS)]
              for pre in ("", "m_", "v_")}
    gather = WeightGather(a, w_in_t, jnp.reshape(me, (1,)).astype(jnp.int32), chain)
    saved = [{} for _ in range(N_LAYERS)]
    h = xs
    for l in range(N_LAYERS):
        for sb in BLOCKS:
            params[l].update(gather.weights(l, sb, [h]))
            h, saved[l][sb] = _fwd_block(l, sb, h, mems, params[l], gather)
    dy, loss = k_loss("loss", h, tgt)
    loss = lax.psum(loss[0, 0], ("x", "y", "c"))

    grads = [{} for _ in range(N_LAYERS)]
    exchange = GradExchange(a, w_in_t, core, chip)
    for l in reversed(range(N_LAYERS)):
        exchange.layer = l
        for sb in reversed(BLOCKS):
            dy = _bwd_block(l, sb, dy, saved[l][sb], params[l], grads[l], exchange)
    big_out = exchange.finish(dy)

    flat = [jnp.concatenate([grads[l][n].reshape(-1) for l in range(N_LAYERS)]) for n in SMALL]
    flat.append(jnp.concatenate([grads[l]["conv_w"].reshape(-1) for l in range(N_LAYERS)]))
    gsum = k_sum8("small_sum", all_gather("ag_small", [_pack(flat)], after=[big_out["ffn1_w_gate_up"][0]])[0])
    rep_rows = sum(_seg_rows(a[n].size) for n in SMALL)
    conv_g = gsum[rep_rows:].reshape(-1)[:N_LAYERS * CONV_W * QKV_W].reshape(N_LAYERS, CONV_W, QKV_W)
    conv_g = lax.dynamic_slice_in_dim(conv_g, me * (QKV_W // N_DEV), QKV_W // N_DEV, axis=2)
    pack_state = lambda pre: _pack([a[pre + n].reshape(-1) for n in SMALL] + [a[pre + "conv_w"].reshape(-1)])
    small_g = jnp.concatenate([gsum[:rep_rows], _pack([conv_g.reshape(-1)])])
    small_out = k_adamw_flat("small_adamw", small_g, pack_state(""), pack_state("m_"), pack_state("v_"))
    outs = {}
    row = 0
    for n in SMALL + ("conv_w",):
        rows = _seg_rows(a[n].size)
        outs[n] = [o[row:row + rows].reshape(-1)[:a[n].size].reshape(a[n].shape) for o in small_out]
        row += rows
    for n in BIG:
        if n == "w_in":
            outs[n] = [jnp.transpose(jnp.concatenate([exchange.w_in_out[l][k] for l in range(N_LAYERS)]), (1, 0, 2))
                       for k in range(4)]
            outs[n] = [jnp.transpose(o, (1, 2, 0)) for o in outs[n]]
        else:
            outs[n] = list(big_out[n])

    return (loss, dy[None], *[outs[n][0] for n in WEIGHTS], *[outs[n][1] for n in WEIGHTS],
            *[outs[n][2] for n in WEIGHTS], *[outs[n][3] for n in WEIGHTS])
```
